```python
import jax, jax.numpy as jnp
from jax import lax
import numpy as np

D_MODEL = 1024
BATCH = 32
SEQ = 2048
DEPTH = 2

N_MIXERS = 2
N_A = (DEPTH + 1) // 2
N_B = DEPTH // 2
D_RNN = D_MODEL
LRU_BLOCKS = 16
LRU_BW = D_RNN // LRU_BLOCKS
LRU_CONV = 4
LRU_C = 8.0
RWKV_N = 64
RWKV_H = D_MODEL // RWKV_N
R_DECAY = 64
R_AAA = 64
R_GATE = 160
GN_EPS = 64e-5
D_FF = 3 * D_MODEL
FFN_CONV = 3
RMS_EPS = 1e-6

kernel_name = 'hybrid_rglru_rwkv7_convffn'


def _rmsnorm(x, g):
    xf = x.astype(jnp.float32)
    y = xf * lax.rsqrt(jnp.mean(xf * xf, axis=-1, keepdims=True) + RMS_EPS)
    return (y * g.astype(jnp.float32)).astype(x.dtype)


def _causal_dwconv(x, w, b):
    k_width, seq = w.shape[0], x.shape[1]
    xp = jnp.pad(x, ((0, 0), (k_width - 1, 0), (0, 0)))
    out = b
    for j in range(k_width):
        out = out + xp[:, j:j + seq] * w[j]
    return out


def _lru_combine(c1, c2):
    a1, b1 = c1
    a2, b2 = c2
    return a1 * a2, a2 * b1 + b2


def _rglru_block(x, norm, w_in, b_in, conv_w, conv_b, gate_w, gate_b, lam, w_out, b_out):
    bsz, seq, _ = x.shape
    h = _rmsnorm(x, norm)
    u = jnp.einsum('btd,de->bte', h, w_in) + b_in
    y_branch = jax.nn.gelu(u[..., :D_RNN], approximate=True)
    xr = _causal_dwconv(u[..., D_RNN:], conv_w, conv_b)
    xb = xr.reshape(bsz, seq, LRU_BLOCKS, LRU_BW)
    gates = jax.nn.sigmoid(jnp.einsum('btnc,gncd->gbtnd', xb, gate_w) + gate_b[:, None, None])
    r_gate = gates[0].reshape(bsz, seq, D_RNN).astype(jnp.float32)
    i_gate = gates[1].reshape(bsz, seq, D_RNN).astype(jnp.float32)
    log_a = -LRU_C * r_gate * jax.nn.softplus(-lam.astype(jnp.float32))
    a = jnp.exp(log_a)
    mult = jnp.sqrt(-jnp.expm1(2.0 * log_a))
    bterm = mult * (i_gate * xr.astype(jnp.float32))
    _, hs = lax.associative_scan(_lru_combine, (a, bterm), axis=1)
    out = hs.astype(x.dtype) * y_branch
    return jnp.einsum('bte,ed->btd', out, w_out) + b_out


def _rwkv7_scan(r, w, k, v, aa, bb):
    bsz, _, nh, n = r.shape

    def step(S, inp):
        r_t, w_t, k_t, v_t, a_t, b_t = inp
        sa = jnp.einsum('bhij,bhj->bhi', S, a_t)
        S = S * w_t[:, :, None, :] + sa[..., None] * b_t[:, :, None, :] + v_t[..., :, None] * k_t[..., None, :]
        y = jnp.einsum('bhij,bhj->bhi', S, r_t)
        return S, y

    s0 = jnp.zeros((bsz, nh, n, n), jnp.float32)
    xs = tuple(t.transpose(1, 0, 2, 3) for t in (r, w, k, v, aa, bb))
    _, ys = lax.scan(step, s0, xs)
    return ys.transpose(1, 0, 2, 3)


def _rwkv7_block(x, norm, mix, w_rkv, w0, w1, w2, a0, a1, a2, g1, g2, k_k, k_a, r_k, ln_w, ln_b, w_out):
    bsz, seq, d = x.shape
    f32 = jnp.float32
    h = _rmsnorm(x, norm)
    xx = jnp.pad(h, ((0, 0), (1, 0), (0, 0)))[:, :-1] - h
    xs_rkv = jnp.stack([h + xx * mix[0], h + xx * mix[1], h + xx * mix[2]])
    rkv = jnp.einsum('sbtd,sde->sbte', xs_rkv, w_rkv)
    r, k, v = rkv[0], rkv[1], rkv[2]
    xw = h + xx * mix[3]
    xa = h + xx * mix[4]
    xg = h + xx * mix[5]
    w = -jax.nn.softplus(-(w0 + jnp.tanh(xw @ w1) @ w2).astype(f32)) - 0.5
    decay = jnp.exp(-jnp.exp(w))
    a = jax.nn.sigmoid((a0 + (xa @ a1) @ a2).astype(f32))
    g = jax.nn.sigmoid(xg @ g1) @ g2
    kf = k.astype(f32)
    kk = (kf * k_k).reshape(bsz, seq, RWKV_H, RWKV_N)
    kk = kk / jnp.maximum(jnp.linalg.norm(kk, axis=-1, keepdims=True), 1e-12)
    kf = kf * (1.0 + (a - 1.0) * k_a)
    hs = lambda t: t.reshape(bsz, seq, RWKV_H, RWKV_N)
    rh, kh, vh = hs(r.astype(f32)), hs(kf), hs(v.astype(f32))
    ah = hs(a)
    y = _rwkv7_scan(rh, hs(decay), kh, vh, -kk, kk * ah)
    mu = jnp.mean(y, axis=-1, keepdims=True)
    var = jnp.mean(jnp.square(y - mu), axis=-1, keepdims=True)
    y = ((y - mu) * lax.rsqrt(var + GN_EPS)).reshape(bsz, seq, d) * ln_w + ln_b
    bonus = jnp.sum(rh * kh * r_k, axis=-1, keepdims=True) * vh
    y = (y + bonus.reshape(bsz, seq, d)).astype(x.dtype)
    return jnp.einsum('btd,de->bte', y * g, w_out)


def _conv_ffn(x, norm, w_up, conv_w, conv_b, w_down):
    h = _rmsnorm(x, norm)
    u = jnp.einsum('btd,df->btf', h, w_up)
    gate = _causal_dwconv(u[..., :D_FF], conv_w, conv_b)
    hid = jax.nn.gelu(gate, approximate=True) * u[..., D_FF:]
    return jnp.einsum('btf,fd->btd', hid, w_down)


def _fwd_setup_inputs(seed: int = 0) -> dict:
    key = jax.random.key(seed)
    ks = iter(jax.random.split(key, 48))
    f32 = jnp.float32
    nrm = lambda shape, scale: jax.random.normal(next(ks), shape, f32) * scale
    uni = lambda shape, lo, hi: jax.random.uniform(next(ks), shape, f32, lo, hi)
    d = D_MODEL
    u_a = uni((N_A, D_RNN), 0.9, 0.999)
    a_init = u_a ** (1.0 / LRU_C)
    return {
        'x': nrm((BATCH, SEQ, d), 1.0),
        'lru_norm': 1.0 + nrm((N_A, d), 0.02),
        'lru_w_in': nrm((N_A, d, 2 * D_RNN), d ** -0.5),
        'lru_b_in': nrm((N_A, 2 * D_RNN), 0.01),
        'lru_conv_w': nrm((N_A, LRU_CONV, D_RNN), LRU_CONV ** -0.5),
        'lru_conv_b': nrm((N_A, D_RNN), 0.01),
        'lru_gate_w': nrm((N_A, 2, LRU_BLOCKS, LRU_BW, LRU_BW), LRU_BW ** -0.5),
        'lru_gate_b': nrm((N_A, 2, LRU_BLOCKS, LRU_BW), 0.01),
        'lru_lambda': jnp.log(a_init) - jnp.log1p(-a_init),
        'lru_w_out': nrm((N_A, D_RNN, d), D_RNN ** -0.5),
        'lru_b_out': nrm((N_A, d), 0.01),
        'rwkv_norm': 1.0 + nrm((N_B, d), 0.02),
        'rwkv_mix': uni((N_B, 6, d), 0.0, 1.0),
        'rwkv_w_rkv': nrm((N_B, 3, d, d), d ** -0.5),
        'rwkv_w0': uni((N_B, d), -5.0, -1.0),
        'rwkv_w1': nrm((N_B, d, R_DECAY), d ** -0.5),
        'rwkv_w2': nrm((N_B, R_DECAY, d), 0.1 * R_DECAY ** -0.5),
        'rwkv_a0': nrm((N_B, d), 0.1),
        'rwkv_a1': nrm((N_B, d, R_AAA), d ** -0.5),
        'rwkv_a2': nrm((N_B, R_AAA, d), 0.1 * R_AAA ** -0.5),
        'rwkv_g1': nrm((N_B, d, R_GATE), d ** -0.5),
        'rwkv_g2': nrm((N_B, R_GATE, d), R_GATE ** -0.5),
        'rwkv_k_k': 0.85 + nrm((N_B, d), 0.05),
        'rwkv_k_a': 1.0 + nrm((N_B, d), 0.05),
        'rwkv_r_k': nrm((N_B, RWKV_H, RWKV_N), 0.1),
        'rwkv_ln_w': 1.0 + nrm((N_B, d), 0.02),
        'rwkv_ln_b': nrm((N_B, d), 0.01),
        'rwkv_w_out': nrm((N_B, d, d), d ** -0.5),
        'ffn_norm': 1.0 + nrm((DEPTH, d), 0.02),
        'ffn_w_up': nrm((DEPTH, d, 2 * D_FF), d ** -0.5),
        'ffn_conv_w': nrm((DEPTH, FFN_CONV, D_FF), FFN_CONV ** -0.5),
        'ffn_conv_b': nrm((DEPTH, D_FF), 0.01),
        'ffn_w_down': nrm((DEPTH, D_FF, d), D_FF ** -0.5),
        'final_norm': 1.0 + nrm((d,), 0.02),
    }


def _fwd_reference(x, lru_norm, lru_w_in, lru_b_in, lru_conv_w, lru_conv_b, lru_gate_w, lru_gate_b, lru_lambda, lru_w_out, lru_b_out,
              rwkv_norm, rwkv_mix, rwkv_w_rkv, rwkv_w0, rwkv_w1, rwkv_w2, rwkv_a0, rwkv_a1, rwkv_a2, rwkv_g1, rwkv_g2,
              rwkv_k_k, rwkv_k_a, rwkv_r_k, rwkv_ln_w, rwkv_ln_b, rwkv_w_out,
              ffn_norm, ffn_w_up, ffn_conv_w, ffn_conv_b, ffn_w_down, final_norm):
    for layer in range(DEPTH):
        j = layer // N_MIXERS
        if layer % N_MIXERS == 0:
            x = x + _rglru_block(x, lru_norm[j], lru_w_in[j], lru_b_in[j], lru_conv_w[j], lru_conv_b[j],
                                 lru_gate_w[j], lru_gate_b[j], lru_lambda[j], lru_w_out[j], lru_b_out[j])
        else:
            x = x + _rwkv7_block(x, rwkv_norm[j], rwkv_mix[j], rwkv_w_rkv[j], rwkv_w0[j], rwkv_w1[j], rwkv_w2[j],
                                 rwkv_a0[j], rwkv_a1[j], rwkv_a2[j], rwkv_g1[j], rwkv_g2[j], rwkv_k_k[j], rwkv_k_a[j],
                                 rwkv_r_k[j], rwkv_ln_w[j], rwkv_ln_b[j], rwkv_w_out[j])
        x = x + _conv_ffn(x, ffn_norm[layer], ffn_w_up[layer], ffn_conv_w[layer], ffn_conv_b[layer], ffn_w_down[layer])
    return _rmsnorm(x, final_norm)


import jax as _jax
import jax.numpy as _jnp

TWIN_FORMAT = 'train_step'
FWD_PARAMS = ['x', 'lru_norm', 'lru_w_in', 'lru_b_in', 'lru_conv_w', 'lru_conv_b', 'lru_gate_w', 'lru_gate_b', 'lru_lambda', 'lru_w_out', 'lru_b_out', 'rwkv_norm', 'rwkv_mix', 'rwkv_w_rkv', 'rwkv_w0', 'rwkv_w1', 'rwkv_w2', 'rwkv_a0', 'rwkv_a1', 'rwkv_a2', 'rwkv_g1', 'rwkv_g2', 'rwkv_k_k', 'rwkv_k_a', 'rwkv_r_k', 'rwkv_ln_w', 'rwkv_ln_b', 'rwkv_w_out', 'ffn_norm', 'ffn_w_up', 'ffn_conv_w', 'ffn_conv_b', 'ffn_w_down', 'final_norm']
TWIN_WEIGHTS = ['lru_norm', 'lru_w_in', 'lru_b_in', 'lru_conv_w', 'lru_conv_b', 'lru_gate_w', 'lru_gate_b', 'lru_lambda', 'lru_w_out', 'lru_b_out', 'rwkv_norm', 'rwkv_mix', 'rwkv_w_rkv', 'rwkv_w0', 'rwkv_w1', 'rwkv_w2', 'rwkv_a0', 'rwkv_a1', 'rwkv_a2', 'rwkv_g1', 'rwkv_g2', 'rwkv_k_k', 'rwkv_k_a', 'rwkv_r_k', 'rwkv_ln_w', 'rwkv_ln_b', 'rwkv_w_out', 'ffn_norm', 'ffn_w_up', 'ffn_conv_w', 'ffn_conv_b', 'ffn_w_down', 'final_norm']
TWIN_DIFF_INPUT = 'x'
TWIN_INPUTS = ['x', 'lru_norm', 'lru_w_in', 'lru_b_in', 'lru_conv_w', 'lru_conv_b', 'lru_gate_w', 'lru_gate_b', 'lru_lambda', 'lru_w_out', 'lru_b_out', 'rwkv_norm', 'rwkv_mix', 'rwkv_w_rkv', 'rwkv_w0', 'rwkv_w1', 'rwkv_w2', 'rwkv_a0', 'rwkv_a1', 'rwkv_a2', 'rwkv_g1', 'rwkv_g2', 'rwkv_k_k', 'rwkv_k_a', 'rwkv_r_k', 'rwkv_ln_w', 'rwkv_ln_b', 'rwkv_w_out', 'ffn_norm', 'ffn_w_up', 'ffn_conv_w', 'ffn_conv_b', 'ffn_w_down', 'final_norm', 'loss_target', 'm_lru_norm', 'm_lru_w_in', 'm_lru_b_in', 'm_lru_conv_w', 'm_lru_conv_b', 'm_lru_gate_w', 'm_lru_gate_b', 'm_lru_lambda', 'm_lru_w_out', 'm_lru_b_out', 'm_rwkv_norm', 'm_rwkv_mix', 'm_rwkv_w_rkv', 'm_rwkv_w0', 'm_rwkv_w1', 'm_rwkv_w2', 'm_rwkv_a0', 'm_rwkv_a1', 'm_rwkv_a2', 'm_rwkv_g1', 'm_rwkv_g2', 'm_rwkv_k_k', 'm_rwkv_k_a', 'm_rwkv_r_k', 'm_rwkv_ln_w', 'm_rwkv_ln_b', 'm_rwkv_w_out', 'm_ffn_norm', 'm_ffn_w_up', 'm_ffn_conv_w', 'm_ffn_conv_b', 'm_ffn_w_down', 'm_final_norm', 'v_lru_norm', 'v_lru_w_in', 'v_lru_b_in', 'v_lru_conv_w', 'v_lru_conv_b', 'v_lru_gate_w', 'v_lru_gate_b', 'v_lru_lambda', 'v_lru_w_out', 'v_lru_b_out', 'v_rwkv_norm', 'v_rwkv_mix', 'v_rwkv_w_rkv', 'v_rwkv_w0', 'v_rwkv_w1', 'v_rwkv_w2', 'v_rwkv_a0', 'v_rwkv_a1', 'v_rwkv_a2', 'v_rwkv_g1', 'v_rwkv_g2', 'v_rwkv_k_k', 'v_rwkv_k_a', 'v_rwkv_r_k', 'v_rwkv_ln_w', 'v_rwkv_ln_b', 'v_rwkv_w_out', 'v_ffn_norm', 'v_ffn_w_up', 'v_ffn_conv_w', 'v_ffn_conv_b', 'v_ffn_w_down', 'v_final_norm']
TWIN_OUTPUTS = ['loss', 'grad_x', 'grad_lru_norm', 'grad_lru_w_in', 'grad_lru_b_in', 'grad_lru_conv_w', 'grad_lru_conv_b', 'grad_lru_gate_w', 'grad_lru_gate_b', 'grad_lru_lambda', 'grad_lru_w_out', 'grad_lru_b_out', 'grad_rwkv_norm', 'grad_rwkv_mix', 'grad_rwkv_w_rkv', 'grad_rwkv_w0', 'grad_rwkv_w1', 'grad_rwkv_w2', 'grad_rwkv_a0', 'grad_rwkv_a1', 'grad_rwkv_a2', 'grad_rwkv_g1', 'grad_rwkv_g2', 'grad_rwkv_k_k', 'grad_rwkv_k_a', 'grad_rwkv_r_k', 'grad_rwkv_ln_w', 'grad_rwkv_ln_b', 'grad_rwkv_w_out', 'grad_ffn_norm', 'grad_ffn_w_up', 'grad_ffn_conv_w', 'grad_ffn_conv_b', 'grad_ffn_w_down', 'grad_final_norm', 'delta_lru_norm', 'delta_lru_w_in', 'delta_lru_b_in', 'delta_lru_conv_w', 'delta_lru_conv_b', 'delta_lru_gate_w', 'delta_lru_gate_b', 'delta_lru_lambda', 'delta_lru_w_out', 'delta_lru_b_out', 'delta_rwkv_norm', 'delta_rwkv_mix', 'delta_rwkv_w_rkv', 'delta_rwkv_w0', 'delta_rwkv_w1', 'delta_rwkv_w2', 'delta_rwkv_a0', 'delta_rwkv_a1', 'delta_rwkv_a2', 'delta_rwkv_g1', 'delta_rwkv_g2', 'delta_rwkv_k_k', 'delta_rwkv_k_a', 'delta_rwkv_r_k', 'delta_rwkv_ln_w', 'delta_rwkv_ln_b', 'delta_rwkv_w_out', 'delta_ffn_norm', 'delta_ffn_w_up', 'delta_ffn_conv_w', 'delta_ffn_conv_b', 'delta_ffn_w_down', 'delta_final_norm', 'new_m_lru_norm', 'new_m_lru_w_in', 'new_m_lru_b_in', 'new_m_lru_conv_w', 'new_m_lru_conv_b', 'new_m_lru_gate_w', 'new_m_lru_gate_b', 'new_m_lru_lambda', 'new_m_lru_w_out', 'new_m_lru_b_out', 'new_m_rwkv_norm', 'new_m_rwkv_mix', 'new_m_rwkv_w_rkv', 'new_m_rwkv_w0', 'new_m_rwkv_w1', 'new_m_rwkv_w2', 'new_m_rwkv_a0', 'new_m_rwkv_a1', 'new_m_rwkv_a2', 'new_m_rwkv_g1', 'new_m_rwkv_g2', 'new_m_rwkv_k_k', 'new_m_rwkv_k_a', 'new_m_rwkv_r_k', 'new_m_rwkv_ln_w', 'new_m_rwkv_ln_b', 'new_m_rwkv_w_out', 'new_m_ffn_norm', 'new_m_ffn_w_up', 'new_m_ffn_conv_w', 'new_m_ffn_conv_b', 'new_m_ffn_w_down', 'new_m_final_norm', 'new_v_lru_norm', 'new_v_lru_w_in', 'new_v_lru_b_in', 'new_v_lru_conv_w', 'new_v_lru_conv_b', 'new_v_lru_gate_w', 'new_v_lru_gate_b', 'new_v_lru_lambda', 'new_v_lru_w_out', 'new_v_lru_b_out', 'new_v_rwkv_norm', 'new_v_rwkv_mix', 'new_v_rwkv_w_rkv', 'new_v_rwkv_w0', 'new_v_rwkv_w1', 'new_v_rwkv_w2', 'new_v_rwkv_a0', 'new_v_rwkv_a1', 'new_v_rwkv_a2', 'new_v_rwkv_g1', 'new_v_rwkv_g2', 'new_v_rwkv_k_k', 'new_v_rwkv_k_a', 'new_v_rwkv_r_k', 'new_v_rwkv_ln_w', 'new_v_rwkv_ln_b', 'new_v_rwkv_w_out', 'new_v_ffn_norm', 'new_v_ffn_w_up', 'new_v_ffn_conv_w', 'new_v_ffn_conv_b', 'new_v_ffn_w_down', 'new_v_final_norm']
TWIN_LEAF_KINDS = {'loss': 'loss', 'grad_x': 'grad_x', 'grad_lru_norm': 'grad_w', 'grad_lru_w_in': 'grad_w', 'grad_lru_b_in': 'grad_w', 'grad_lru_conv_w': 'grad_w', 'grad_lru_conv_b': 'grad_w', 'grad_lru_gate_w': 'grad_w', 'grad_lru_gate_b': 'grad_w', 'grad_lru_lambda': 'grad_w', 'grad_lru_w_out': 'grad_w', 'grad_lru_b_out': 'grad_w', 'grad_rwkv_norm': 'grad_w', 'grad_rwkv_mix': 'grad_w', 'grad_rwkv_w_rkv': 'grad_w', 'grad_rwkv_w0': 'grad_w', 'grad_rwkv_w1': 'grad_w', 'grad_rwkv_w2': 'grad_w', 'grad_rwkv_a0': 'grad_w', 'grad_rwkv_a1': 'grad_w', 'grad_rwkv_a2': 'grad_w', 'grad_rwkv_g1': 'grad_w', 'grad_rwkv_g2': 'grad_w', 'grad_rwkv_k_k': 'grad_w', 'grad_rwkv_k_a': 'grad_w', 'grad_rwkv_r_k': 'grad_w', 'grad_rwkv_ln_w': 'grad_w', 'grad_rwkv_ln_b': 'grad_w', 'grad_rwkv_w_out': 'grad_w', 'grad_ffn_norm': 'grad_w', 'grad_ffn_w_up': 'grad_w', 'grad_ffn_conv_w': 'grad_w', 'grad_ffn_conv_b': 'grad_w', 'grad_ffn_w_down': 'grad_w', 'grad_final_norm': 'grad_w', 'delta_lru_norm': 'delta_w', 'delta_lru_w_in': 'delta_w', 'delta_lru_b_in': 'delta_w', 'delta_lru_conv_w': 'delta_w', 'delta_lru_conv_b': 'delta_w', 'delta_lru_gate_w': 'delta_w', 'delta_lru_gate_b': 'delta_w', 'delta_lru_lambda': 'delta_w', 'delta_lru_w_out': 'delta_w', 'delta_lru_b_out': 'delta_w', 'delta_rwkv_norm': 'delta_w', 'delta_rwkv_mix': 'delta_w', 'delta_rwkv_w_rkv': 'delta_w', 'delta_rwkv_w0': 'delta_w', 'delta_rwkv_w1': 'delta_w', 'delta_rwkv_w2': 'delta_w', 'delta_rwkv_a0': 'delta_w', 'delta_rwkv_a1': 'delta_w', 'delta_rwkv_a2': 'delta_w', 'delta_rwkv_g1': 'delta_w', 'delta_rwkv_g2': 'delta_w', 'delta_rwkv_k_k': 'delta_w', 'delta_rwkv_k_a': 'delta_w', 'delta_rwkv_r_k': 'delta_w', 'delta_rwkv_ln_w': 'delta_w', 'delta_rwkv_ln_b': 'delta_w', 'delta_rwkv_w_out': 'delta_w', 'delta_ffn_norm': 'delta_w', 'delta_ffn_w_up': 'delta_w', 'delta_ffn_conv_w': 'delta_w', 'delta_ffn_conv_b': 'delta_w', 'delta_ffn_w_down': 'delta_w', 'delta_final_norm': 'delta_w', 'new_m_lru_norm': 'new_m', 'new_m_lru_w_in': 'new_m', 'new_m_lru_b_in': 'new_m', 'new_m_lru_conv_w': 'new_m', 'new_m_lru_conv_b': 'new_m', 'new_m_lru_gate_w': 'new_m', 'new_m_lru_gate_b': 'new_m', 'new_m_lru_lambda': 'new_m', 'new_m_lru_w_out': 'new_m', 'new_m_lru_b_out': 'new_m', 'new_m_rwkv_norm': 'new_m', 'new_m_rwkv_mix': 'new_m', 'new_m_rwkv_w_rkv': 'new_m', 'new_m_rwkv_w0': 'new_m', 'new_m_rwkv_w1': 'new_m', 'new_m_rwkv_w2': 'new_m', 'new_m_rwkv_a0': 'new_m', 'new_m_rwkv_a1': 'new_m', 'new_m_rwkv_a2': 'new_m', 'new_m_rwkv_g1': 'new_m', 'new_m_rwkv_g2': 'new_m', 'new_m_rwkv_k_k': 'new_m', 'new_m_rwkv_k_a': 'new_m', 'new_m_rwkv_r_k': 'new_m', 'new_m_rwkv_ln_w': 'new_m', 'new_m_rwkv_ln_b': 'new_m', 'new_m_rwkv_w_out': 'new_m', 'new_m_ffn_norm': 'new_m', 'new_m_ffn_w_up': 'new_m', 'new_m_ffn_conv_w': 'new_m', 'new_m_ffn_conv_b': 'new_m', 'new_m_ffn_w_down': 'new_m', 'new_m_final_norm': 'new_m', 'new_v_lru_norm': 'new_v', 'new_v_lru_w_in': 'new_v', 'new_v_lru_b_in': 'new_v', 'new_v_lru_conv_w': 'new_v', 'new_v_lru_conv_b': 'new_v', 'new_v_lru_gate_w': 'new_v', 'new_v_lru_gate_b': 'new_v', 'new_v_lru_lambda': 'new_v', 'new_v_lru_w_out': 'new_v', 'new_v_lru_b_out': 'new_v', 'new_v_rwkv_norm': 'new_v', 'new_v_rwkv_mix': 'new_v', 'new_v_rwkv_w_rkv': 'new_v', 'new_v_rwkv_w0': 'new_v', 'new_v_rwkv_w1': 'new_v', 'new_v_rwkv_w2': 'new_v', 'new_v_rwkv_a0': 'new_v', 'new_v_rwkv_a1': 'new_v', 'new_v_rwkv_a2': 'new_v', 'new_v_rwkv_g1': 'new_v', 'new_v_rwkv_g2': 'new_v', 'new_v_rwkv_k_k': 'new_v', 'new_v_rwkv_k_a': 'new_v', 'new_v_rwkv_r_k': 'new_v', 'new_v_rwkv_ln_w': 'new_v', 'new_v_rwkv_ln_b': 'new_v', 'new_v_rwkv_w_out': 'new_v', 'new_v_ffn_norm': 'new_v', 'new_v_ffn_w_up': 'new_v', 'new_v_ffn_conv_w': 'new_v', 'new_v_ffn_conv_b': 'new_v', 'new_v_ffn_w_down': 'new_v', 'new_v_final_norm': 'new_v'}


def _forward(args):
    return _fwd_reference(*[args[k] for k in FWD_PARAMS])


def _output_shape():
    out = _jax.eval_shape(lambda: _forward(_fwd_setup_inputs(0)))
    return out.shape, out.dtype

N_MICROBATCH = 1
ADAM_LR = 0.001
ADAM_B1 = 0.9
ADAM_B2 = 0.999
ADAM_EPS = 1e-08
ADAM_WD = 0.01
ADAM_STEP = 10
PER_EXAMPLE_BATCH_AXIS = {'x': 0, 'loss_target': 0}
SHARED_INPUTS = []
_WEIGHT_DTYPES = {'lru_norm': _jnp.float32, 'lru_w_in': _jnp.float32, 'lru_b_in': _jnp.float32, 'lru_conv_w': _jnp.float32, 'lru_conv_b': _jnp.float32, 'lru_gate_w': _jnp.float32, 'lru_gate_b': _jnp.float32, 'lru_lambda': _jnp.float32, 'lru_w_out': _jnp.float32, 'lru_b_out': _jnp.float32, 'rwkv_norm': _jnp.float32, 'rwkv_mix': _jnp.float32, 'rwkv_w_rkv': _jnp.float32, 'rwkv_w0': _jnp.float32, 'rwkv_w1': _jnp.float32, 'rwkv_w2': _jnp.float32, 'rwkv_a0': _jnp.float32, 'rwkv_a1': _jnp.float32, 'rwkv_a2': _jnp.float32, 'rwkv_g1': _jnp.float32, 'rwkv_g2': _jnp.float32, 'rwkv_k_k': _jnp.float32, 'rwkv_k_a': _jnp.float32, 'rwkv_r_k': _jnp.float32, 'rwkv_ln_w': _jnp.float32, 'rwkv_ln_b': _jnp.float32, 'rwkv_w_out': _jnp.float32, 'ffn_norm': _jnp.float32, 'ffn_w_up': _jnp.float32, 'ffn_conv_w': _jnp.float32, 'ffn_conv_b': _jnp.float32, 'ffn_w_down': _jnp.float32, 'final_norm': _jnp.float32}
MOMENT_SCALE = {'lru_norm': 2.845906e-01, 'lru_w_in': 1.670488e-01, 'lru_b_in': 1.742981e+00, 'lru_conv_w': 1.937654e-01, 'lru_conv_b': 2.023167e+00, 'lru_gate_w': 1.290440e-01, 'lru_gate_b': 5.572428e-02, 'lru_lambda': 9.997371e-02, 'lru_w_out': 1.904317e-01, 'lru_b_out': 3.843621e-01, 'rwkv_norm': 2.105610e-01, 'rwkv_mix': 1.366190e-01, 'rwkv_w_rkv': 1.170603e-01, 'rwkv_w0': 6.363984e-02, 'rwkv_w1': 3.529819e-03, 'rwkv_w2': 7.903053e-03, 'rwkv_a0': 5.082659e-02, 'rwkv_a1': 1.853915e-02, 'rwkv_a2': 4.586994e-02, 'rwkv_g1': 9.378030e-02, 'rwkv_g2': 1.094575e-01, 'rwkv_k_k': 1.555560e-01, 'rwkv_k_a': 1.226093e-01, 'rwkv_r_k': 2.436730e-01, 'rwkv_ln_w': 1.057286e-01, 'rwkv_ln_b': 1.282916e-01, 'rwkv_w_out': 1.100031e-01, 'ffn_norm': 2.030291e-01, 'ffn_w_up': 8.348310e-02, 'ffn_conv_w': 8.362823e-02, 'ffn_conv_b': 8.059861e-02, 'ffn_w_down': 1.422207e-01, 'final_norm': 6.404050e+01}


def _to_microbatches(a, axis):
    t = _jnp.moveaxis(a, axis, 0)
    t = t.reshape((N_MICROBATCH, t.shape[0] // N_MICROBATCH) + t.shape[1:])
    return _jnp.moveaxis(t, 1, axis + 1)


def setup_inputs(seed: int = 0) -> dict:
    inp = _fwd_setup_inputs(seed)
    key = _jax.random.fold_in(_jax.random.key(seed), 7919)
    shape, _ = _output_shape()
    out = dict(inp)
    out["loss_target"] = _jax.random.normal(_jax.random.fold_in(key, 0), shape, _jnp.float32)
    for i, name in enumerate(TWIN_WEIGHTS):
        w = inp[name].astype(_jnp.float32)
        if MOMENT_SCALE is None:
            s = _jnp.sqrt(_jnp.mean(_jnp.square(w)) + 1e-30)
        else:
            s = MOMENT_SCALE[name]
        km, kv = _jax.random.split(_jax.random.fold_in(key, i + 1))
        out[name] = w
        out["m_" + name] = s * _jax.random.normal(km, w.shape, _jnp.float32)
        out["v_" + name] = (s * s) * _jax.random.uniform(kv, w.shape, _jnp.float32, 0.5, 1.5)
    if N_MICROBATCH > 1:
        for name, axis in PER_EXAMPLE_BATCH_AXIS.items():
            out[name] = _to_microbatches(out[name], axis)
    return {'x': out['x'], 'lru_norm': out['lru_norm'], 'lru_w_in': out['lru_w_in'], 'lru_b_in': out['lru_b_in'], 'lru_conv_w': out['lru_conv_w'], 'lru_conv_b': out['lru_conv_b'], 'lru_gate_w': out['lru_gate_w'], 'lru_gate_b': out['lru_gate_b'], 'lru_lambda': out['lru_lambda'], 'lru_w_out': out['lru_w_out'], 'lru_b_out': out['lru_b_out'], 'rwkv_norm': out['rwkv_norm'], 'rwkv_mix': out['rwkv_mix'], 'rwkv_w_rkv': out['rwkv_w_rkv'], 'rwkv_w0': out['rwkv_w0'], 'rwkv_w1': out['rwkv_w1'], 'rwkv_w2': out['rwkv_w2'], 'rwkv_a0': out['rwkv_a0'], 'rwkv_a1': out['rwkv_a1'], 'rwkv_a2': out['rwkv_a2'], 'rwkv_g1': out['rwkv_g1'], 'rwkv_g2': out['rwkv_g2'], 'rwkv_k_k': out['rwkv_k_k'], 'rwkv_k_a': out['rwkv_k_a'], 'rwkv_r_k': out['rwkv_r_k'], 'rwkv_ln_w': out['rwkv_ln_w'], 'rwkv_ln_b': out['rwkv_ln_b'], 'rwkv_w_out': out['rwkv_w_out'], 'ffn_norm': out['ffn_norm'], 'ffn_w_up': out['ffn_w_up'], 'ffn_conv_w': out['ffn_conv_w'], 'ffn_conv_b': out['ffn_conv_b'], 'ffn_w_down': out['ffn_w_down'], 'final_norm': out['final_norm'], 'loss_target': out['loss_target'], 'm_lru_norm': out['m_lru_norm'], 'm_lru_w_in': out['m_lru_w_in'], 'm_lru_b_in': out['m_lru_b_in'], 'm_lru_conv_w': out['m_lru_conv_w'], 'm_lru_conv_b': out['m_lru_conv_b'], 'm_lru_gate_w': out['m_lru_gate_w'], 'm_lru_gate_b': out['m_lru_gate_b'], 'm_lru_lambda': out['m_lru_lambda'], 'm_lru_w_out': out['m_lru_w_out'], 'm_lru_b_out': out['m_lru_b_out'], 'm_rwkv_norm': out['m_rwkv_norm'], 'm_rwkv_mix': out['m_rwkv_mix'], 'm_rwkv_w_rkv': out['m_rwkv_w_rkv'], 'm_rwkv_w0': out['m_rwkv_w0'], 'm_rwkv_w1': out['m_rwkv_w1'], 'm_rwkv_w2': out['m_rwkv_w2'], 'm_rwkv_a0': out['m_rwkv_a0'], 'm_rwkv_a1': out['m_rwkv_a1'], 'm_rwkv_a2': out['m_rwkv_a2'], 'm_rwkv_g1': out['m_rwkv_g1'], 'm_rwkv_g2': out['m_rwkv_g2'], 'm_rwkv_k_k': out['m_rwkv_k_k'], 'm_rwkv_k_a': out['m_rwkv_k_a'], 'm_rwkv_r_k': out['m_rwkv_r_k'], 'm_rwkv_ln_w': out['m_rwkv_ln_w'], 'm_rwkv_ln_b': out['m_rwkv_ln_b'], 'm_rwkv_w_out': out['m_rwkv_w_out'], 'm_ffn_norm': out['m_ffn_norm'], 'm_ffn_w_up': out['m_ffn_w_up'], 'm_ffn_conv_w': out['m_ffn_conv_w'], 'm_ffn_conv_b': out['m_ffn_conv_b'], 'm_ffn_w_down': out['m_ffn_w_down'], 'm_final_norm': out['m_final_norm'], 'v_lru_norm': out['v_lru_norm'], 'v_lru_w_in': out['v_lru_w_in'], 'v_lru_b_in': out['v_lru_b_in'], 'v_lru_conv_w': out['v_lru_conv_w'], 'v_lru_conv_b': out['v_lru_conv_b'], 'v_lru_gate_w': out['v_lru_gate_w'], 'v_lru_gate_b': out['v_lru_gate_b'], 'v_lru_lambda': out['v_lru_lambda'], 'v_lru_w_out': out['v_lru_w_out'], 'v_lru_b_out': out['v_lru_b_out'], 'v_rwkv_norm': out['v_rwkv_norm'], 'v_rwkv_mix': out['v_rwkv_mix'], 'v_rwkv_w_rkv': out['v_rwkv_w_rkv'], 'v_rwkv_w0': out['v_rwkv_w0'], 'v_rwkv_w1': out['v_rwkv_w1'], 'v_rwkv_w2': out['v_rwkv_w2'], 'v_rwkv_a0': out['v_rwkv_a0'], 'v_rwkv_a1': out['v_rwkv_a1'], 'v_rwkv_a2': out['v_rwkv_a2'], 'v_rwkv_g1': out['v_rwkv_g1'], 'v_rwkv_g2': out['v_rwkv_g2'], 'v_rwkv_k_k': out['v_rwkv_k_k'], 'v_rwkv_k_a': out['v_rwkv_k_a'], 'v_rwkv_r_k': out['v_rwkv_r_k'], 'v_rwkv_ln_w': out['v_rwkv_ln_w'], 'v_rwkv_ln_b': out['v_rwkv_ln_b'], 'v_rwkv_w_out': out['v_rwkv_w_out'], 'v_ffn_norm': out['v_ffn_norm'], 'v_ffn_w_up': out['v_ffn_w_up'], 'v_ffn_conv_w': out['v_ffn_conv_w'], 'v_ffn_conv_b': out['v_ffn_conv_b'], 'v_ffn_w_down': out['v_ffn_w_down'], 'v_final_norm': out['v_final_norm']}


def _loss(weights, diff, rest, loss_target):
    with _jax.named_scope("forward"):
        args = {**rest, TWIN_DIFF_INPUT: diff, **{k: w.astype(_WEIGHT_DTYPES[k]) for k, w in weights.items()}}
        y = _forward(args)
    with _jax.named_scope("loss_head"):
        err = _jnp.square(y.astype(_jnp.float32) - loss_target)
        return 0.5 * _jnp.sum(_jnp.mean(err, axis=-1)) if err.ndim else 0.5 * err


def _adamw(w, g, m, v):
    m = ADAM_B1 * m + (1.0 - ADAM_B1) * g
    v = ADAM_B2 * v + (1.0 - ADAM_B2) * _jnp.square(g)
    m_hat = m / (1.0 - ADAM_B1 ** ADAM_STEP)
    v_hat = v / (1.0 - ADAM_B2 ** ADAM_STEP)
    delta = -ADAM_LR * (m_hat / (_jnp.sqrt(v_hat) + ADAM_EPS) + ADAM_WD * w)
    return delta, m, v


def reference(x, lru_norm, lru_w_in, lru_b_in, lru_conv_w, lru_conv_b, lru_gate_w, lru_gate_b, lru_lambda, lru_w_out, lru_b_out, rwkv_norm, rwkv_mix, rwkv_w_rkv, rwkv_w0, rwkv_w1, rwkv_w2, rwkv_a0, rwkv_a1, rwkv_a2, rwkv_g1, rwkv_g2, rwkv_k_k, rwkv_k_a, rwkv_r_k, rwkv_ln_w, rwkv_ln_b, rwkv_w_out, ffn_norm, ffn_w_up, ffn_conv_w, ffn_conv_b, ffn_w_down, final_norm, loss_target, m_lru_norm, m_lru_w_in, m_lru_b_in, m_lru_conv_w, m_lru_conv_b, m_lru_gate_w, m_lru_gate_b, m_lru_lambda, m_lru_w_out, m_lru_b_out, m_rwkv_norm, m_rwkv_mix, m_rwkv_w_rkv, m_rwkv_w0, m_rwkv_w1, m_rwkv_w2, m_rwkv_a0, m_rwkv_a1, m_rwkv_a2, m_rwkv_g1, m_rwkv_g2, m_rwkv_k_k, m_rwkv_k_a, m_rwkv_r_k, m_rwkv_ln_w, m_rwkv_ln_b, m_rwkv_w_out, m_ffn_norm, m_ffn_w_up, m_ffn_conv_w, m_ffn_conv_b, m_ffn_w_down, m_final_norm, v_lru_norm, v_lru_w_in, v_lru_b_in, v_lru_conv_w, v_lru_conv_b, v_lru_gate_w, v_lru_gate_b, v_lru_lambda, v_lru_w_out, v_lru_b_out, v_rwkv_norm, v_rwkv_mix, v_rwkv_w_rkv, v_rwkv_w0, v_rwkv_w1, v_rwkv_w2, v_rwkv_a0, v_rwkv_a1, v_rwkv_a2, v_rwkv_g1, v_rwkv_g2, v_rwkv_k_k, v_rwkv_k_a, v_rwkv_r_k, v_rwkv_ln_w, v_rwkv_ln_b, v_rwkv_w_out, v_ffn_norm, v_ffn_w_up, v_ffn_conv_w, v_ffn_conv_b, v_ffn_w_down, v_final_norm):
    given = dict(x=x, lru_norm=lru_norm, lru_w_in=lru_w_in, lru_b_in=lru_b_in, lru_conv_w=lru_conv_w, lru_conv_b=lru_conv_b, lru_gate_w=lru_gate_w, lru_gate_b=lru_gate_b, lru_lambda=lru_lambda, lru_w_out=lru_w_out, lru_b_out=lru_b_out, rwkv_norm=rwkv_norm, rwkv_mix=rwkv_mix, rwkv_w_rkv=rwkv_w_rkv, rwkv_w0=rwkv_w0, rwkv_w1=rwkv_w1, rwkv_w2=rwkv_w2, rwkv_a0=rwkv_a0, rwkv_a1=rwkv_a1, rwkv_a2=rwkv_a2, rwkv_g1=rwkv_g1, rwkv_g2=rwkv_g2, rwkv_k_k=rwkv_k_k, rwkv_k_a=rwkv_k_a, rwkv_r_k=rwkv_r_k, rwkv_ln_w=rwkv_ln_w, rwkv_ln_b=rwkv_ln_b, rwkv_w_out=rwkv_w_out, ffn_norm=ffn_norm, ffn_w_up=ffn_w_up, ffn_conv_w=ffn_conv_w, ffn_conv_b=ffn_conv_b, ffn_w_down=ffn_w_down, final_norm=final_norm, loss_target=loss_target, m_lru_norm=m_lru_norm, m_lru_w_in=m_lru_w_in, m_lru_b_in=m_lru_b_in, m_lru_conv_w=m_lru_conv_w, m_lru_conv_b=m_lru_conv_b, m_lru_gate_w=m_lru_gate_w, m_lru_gate_b=m_lru_gate_b, m_lru_lambda=m_lru_lambda, m_lru_w_out=m_lru_w_out, m_lru_b_out=m_lru_b_out, m_rwkv_norm=m_rwkv_norm, m_rwkv_mix=m_rwkv_mix, m_rwkv_w_rkv=m_rwkv_w_rkv, m_rwkv_w0=m_rwkv_w0, m_rwkv_w1=m_rwkv_w1, m_rwkv_w2=m_rwkv_w2, m_rwkv_a0=m_rwkv_a0, m_rwkv_a1=m_rwkv_a1, m_rwkv_a2=m_rwkv_a2, m_rwkv_g1=m_rwkv_g1, m_rwkv_g2=m_rwkv_g2, m_rwkv_k_k=m_rwkv_k_k, m_rwkv_k_a=m_rwkv_k_a, m_rwkv_r_k=m_rwkv_r_k, m_rwkv_ln_w=m_rwkv_ln_w, m_rwkv_ln_b=m_rwkv_ln_b, m_rwkv_w_out=m_rwkv_w_out, m_ffn_norm=m_ffn_norm, m_ffn_w_up=m_ffn_w_up, m_ffn_conv_w=m_ffn_conv_w, m_ffn_conv_b=m_ffn_conv_b, m_ffn_w_down=m_ffn_w_down, m_final_norm=m_final_norm, v_lru_norm=v_lru_norm, v_lru_w_in=v_lru_w_in, v_lru_b_in=v_lru_b_in, v_lru_conv_w=v_lru_conv_w, v_lru_conv_b=v_lru_conv_b, v_lru_gate_w=v_lru_gate_w, v_lru_gate_b=v_lru_gate_b, v_lru_lambda=v_lru_lambda, v_lru_w_out=v_lru_w_out, v_lru_b_out=v_lru_b_out, v_rwkv_norm=v_rwkv_norm, v_rwkv_mix=v_rwkv_mix, v_rwkv_w_rkv=v_rwkv_w_rkv, v_rwkv_w0=v_rwkv_w0, v_rwkv_w1=v_rwkv_w1, v_rwkv_w2=v_rwkv_w2, v_rwkv_a0=v_rwkv_a0, v_rwkv_a1=v_rwkv_a1, v_rwkv_a2=v_rwkv_a2, v_rwkv_g1=v_rwkv_g1, v_rwkv_g2=v_rwkv_g2, v_rwkv_k_k=v_rwkv_k_k, v_rwkv_k_a=v_rwkv_k_a, v_rwkv_r_k=v_rwkv_r_k, v_rwkv_ln_w=v_rwkv_ln_w, v_rwkv_ln_b=v_rwkv_ln_b, v_rwkv_w_out=v_rwkv_w_out, v_ffn_norm=v_ffn_norm, v_ffn_w_up=v_ffn_w_up, v_ffn_conv_w=v_ffn_conv_w, v_ffn_conv_b=v_ffn_conv_b, v_ffn_w_down=v_ffn_w_down, v_final_norm=v_final_norm)
    weights = {n: given[n] for n in TWIN_WEIGHTS}
    shared = {n: given[n] for n in SHARED_INPUTS}
    per_example = {n: given[n] for n in ['x']}
    grad_fn = _jax.value_and_grad(_loss, argnums=(0, 1))

    def one_microbatch(ex, loss_target):
        ex = dict(ex)
        diff = ex.pop(TWIN_DIFF_INPUT)
        return grad_fn(weights, diff, {**shared, **ex}, loss_target)

    if N_MICROBATCH == 1:
        loss, (grad_w, grad_x) = one_microbatch(per_example, given["loss_target"])
    else:
        def body(carry, xs):
            loss_sum, grad_sum = carry
            l_k, (gw_k, gx_k) = one_microbatch(xs[0], xs[1])
            with _jax.named_scope("update"):
                return (loss_sum + l_k, _jax.tree.map(_jnp.add, grad_sum, gw_k)), gx_k

        init = (_jnp.zeros((), _jnp.float32), _jax.tree.map(_jnp.zeros_like, weights))
        (loss, grad_w), grad_x = _jax.lax.scan(body, init, (per_example, given["loss_target"]))
    with _jax.named_scope("update"):
        delta_w, new_m, new_v = {}, {}, {}
        for n in TWIN_WEIGHTS:
            delta_w[n], new_m[n], new_v[n] = _adamw(weights[n], grad_w[n], given["m_" + n], given["v_" + n])
    return (loss, grad_x, *[grad_w[n] for n in TWIN_WEIGHTS], *[delta_w[n] for n in TWIN_WEIGHTS],
            *[new_m[n] for n in TWIN_WEIGHTS], *[new_v[n] for n in TWIN_WEIGHTS])
```

```python
import functools

import jax
import jax.numpy as jnp
from jax import lax
from jax.experimental import pallas as pl
from jax.experimental.pallas import tpu as pltpu

F32 = jnp.float32
BF16 = jnp.bfloat16
MXU_DTYPE = BF16

HEAD = 64
LRU_C = 8.0
GN_EPS = 64e-5
RMS_EPS = 1e-6
HALO = 16
VMEM_LIMIT = 56 * 1024 * 1024

ADAM_LR, ADAM_B1, ADAM_B2, ADAM_EPS, ADAM_WD, ADAM_STEP = 0.001, 0.9, 0.999, 1e-08, 0.01, 10


def _cparams(sem):
    return pltpu.CompilerParams(dimension_semantics=sem, vmem_limit_bytes=VMEM_LIMIT)


def _pick(n, want):
    if n <= want:
        return n
    t = want
    while t >= 128:
        if n % t == 0:
            return t
        t -= 128
    return n


def _matmul(a, b, mode="nn", bias=None, residual=None, out_dtype=F32, name="mm", tm=512, tn=512, tk=1024):
    if mode == "nn":
        (M, K), (K2, N) = a.shape, b.shape
    elif mode == "nt":
        (M, K), (N, K2) = a.shape, b.shape
    else:
        (K, M), (K2, N) = a.shape, b.shape
    assert K == K2, (a.shape, b.shape, mode)
    tm, tn, tk = _pick(M, tm), _pick(N, tn), _pick(K, tk)
    nk = K // tk
    dims = {"nn": (((1,), (0,)), ((), ())), "nt": (((1,), (1,)), ((), ())), "tn": (((0,), (0,)), ((), ()))}[mode]
    a_spec = {"nn": pl.BlockSpec((tm, tk), lambda i, j, k: (i, k)),
              "nt": pl.BlockSpec((tm, tk), lambda i, j, k: (i, k)),
              "tn": pl.BlockSpec((tk, tm), lambda i, j, k: (k, i))}[mode]
    b_spec = {"nn": pl.BlockSpec((tk, tn), lambda i, j, k: (k, j)),
              "nt": pl.BlockSpec((tn, tk), lambda i, j, k: (j, k)),
              "tn": pl.BlockSpec((tk, tn), lambda i, j, k: (k, j))}[mode]
    in_specs, operands = [a_spec, b_spec], [a, b]
    if bias is not None:
        in_specs.append(pl.BlockSpec((1, tn), lambda i, j, k: (0, j)))
        operands.append(bias.reshape(1, N))
    if residual is not None:
        in_specs.append(pl.BlockSpec((tm, tn), lambda i, j, k: (i, j)))
        operands.append(residual)
    has_bias, has_res = bias is not None, residual is not None

    def kern(*refs):
        a_ref, b_ref = refs[0], refs[1]
        o_ref, acc_ref = refs[-2], refs[-1]
        k = pl.program_id(2)

        @pl.when(k == 0)
        def _():
            acc_ref[...] = jnp.zeros_like(acc_ref)

        acc_ref[...] += lax.dot_general(a_ref[...].astype(MXU_DTYPE), b_ref[...].astype(MXU_DTYPE), dims,
                                        preferred_element_type=F32)

        @pl.when(k == nk - 1)
        def _():
            r = acc_ref[...]
            pos = 2
            if has_bias:
                r = r + refs[pos][...].astype(F32)
                pos += 1
            if has_res:
                r = r + refs[pos][...].astype(F32)
            o_ref[...] = r.astype(o_ref.dtype)

    return pl.pallas_call(
        kern, name=name,
        grid=(M // tm, N // tn, nk),
        in_specs=in_specs,
        out_specs=pl.BlockSpec((tm, tn), lambda i, j, k: (i, j)),
        out_shape=jax.ShapeDtypeStruct((M, N), out_dtype),
        scratch_shapes=[pltpu.VMEM((tm, tn), F32)],
        compiler_params=_cparams(("parallel", "parallel", "arbitrary")),
    )(*operands)


def _tile_call(body, *, rows, prevs=(), nexts=(), fulls=(), row_outs=(), acc_outs=(), tm, T, name):
    M = rows[0].shape[0]
    n_tiles, tps, hb = M // tm, T // tm, tm // HALO
    n_halo_blocks = M // HALO
    nr, npv, nnx, nf, nro, nac = len(rows), len(prevs), len(nexts), len(fulls), len(row_outs), len(acc_outs)

    def kern(*refs):
        i = pl.program_id(0)
        row_refs = refs[:nr]
        prev_refs = refs[nr:nr + npv]
        next_refs = refs[nr + npv:nr + npv + nnx]
        full_refs = refs[nr + npv + nnx:nr + npv + nnx + nf]
        out_refs = refs[nr + npv + nnx + nf:nr + npv + nnx + nf + nro]
        acc_refs = refs[nr + npv + nnx + nf + nro:]
        seq_first = (i % tps) == 0
        seq_last = (i % tps) == (tps - 1)
        outs, accs = body(row_refs, prev_refs, next_refs, full_refs, seq_first, seq_last)
        for r, o in zip(out_refs, outs, strict=True):
            r[...] = o.astype(r.dtype)
        if nac:
            @pl.when(i == 0)
            def _():
                for r in acc_refs:
                    r[...] = jnp.zeros_like(r)
            for r, a in zip(acc_refs, accs, strict=True):
                r[...] += a.astype(F32)

    in_specs = [pl.BlockSpec((tm, a.shape[1]), lambda i: (i, 0)) for a in rows]
    in_specs += [pl.BlockSpec((HALO, rows[k].shape[1]), lambda i: (jnp.maximum(i * hb - 1, 0), 0)) for k in prevs]
    in_specs += [pl.BlockSpec((HALO, rows[k].shape[1]), lambda i: (jnp.minimum((i + 1) * hb, n_halo_blocks - 1), 0))
                 for k in nexts]
    in_specs += [pl.BlockSpec(f.shape, lambda i: (0, 0)) for f in fulls]
    out_specs = [pl.BlockSpec((tm, w), lambda i: (i, 0)) for (w, _) in row_outs]
    out_specs += [pl.BlockSpec(s, lambda i: (0, 0)) for s in acc_outs]
    out_shape = [jax.ShapeDtypeStruct((M, w), dt) for (w, dt) in row_outs]
    out_shape += [jax.ShapeDtypeStruct(s, F32) for s in acc_outs]
    operands = list(rows) + [rows[k] for k in prevs] + [rows[k] for k in nexts] + list(fulls)
    res = pl.pallas_call(
        kern, name=name, grid=(n_tiles,), in_specs=in_specs, out_specs=out_specs, out_shape=out_shape,
        compiler_params=_cparams(("arbitrary",)),
    )(*operands)
    return res[:nro], res[nro:]


def _f(ref):
    return ref[...].astype(F32)


def _sigmoid(x):
    return 1.0 / (1.0 + jnp.exp(-x))


def _softplus(x):
    return jnp.maximum(x, 0.0) + jnp.log(1.0 + jnp.exp(-jnp.abs(x)))


def _neg_expm1(x):
    series = -x * (1.0 + x * (0.5 + x * (1.0 / 6.0) * (1.0 + 0.25 * x)))
    return jnp.where(x > -0.01, series, 1.0 - jnp.exp(x))


def _gelu(x):
    return 0.5 * x * (1.0 + jnp.tanh(0.7978845608028654 * (x + 0.044715 * x * x * x)))


def _rms(x, g):
    return x * lax.rsqrt(jnp.mean(x * x, axis=-1, keepdims=True) + RMS_EPS) * g


@jax.custom_vjp
def _bdot(x, w):
    return jnp.dot(x.astype(MXU_DTYPE), w.astype(MXU_DTYPE), preferred_element_type=F32)


def _bdot_fwd(x, w):
    return _bdot(x, w), (x, w)


def _bdot_bwd(res, ct):
    x, w = res
    ctb = ct.astype(MXU_DTYPE)
    dx = lax.dot_general(ctb, w.astype(MXU_DTYPE), (((1,), (1,)), ((), ())), preferred_element_type=F32)
    dw = lax.dot_general(x.astype(MXU_DTYPE), ctb, (((0,), (0,)), ((), ())), preferred_element_type=F32)
    return dx.astype(x.dtype), dw.astype(w.dtype)


_bdot.defvjp(_bdot_fwd, _bdot_bwd)


@jax.custom_vjp
def _head_sum(x, e, et):
    s = jnp.dot(x, e, precision=lax.Precision.HIGHEST, preferred_element_type=F32)
    return jnp.dot(s, et, precision=lax.Precision.HIGHEST, preferred_element_type=F32)


def _head_sum_fwd(x, e, et):
    return _head_sum(x, e, et), (e, et)


def _head_sum_bwd(res, ct):
    e, et = res
    return _head_sum(ct, e, et), jnp.zeros_like(e), jnp.zeros_like(et)


_head_sum.defvjp(_head_sum_fwd, _head_sum_bwd)


def _shift_down(main, prev, s, seq_first):
    prev = jnp.where(seq_first, 0.0, prev)
    ext = jnp.concatenate([prev, main], axis=0)
    return pltpu.roll(ext, s, 0)[HALO:]


def _shift_up(main, nxt, s, seq_last):
    nxt = jnp.where(seq_last, 0.0, nxt)
    ext = jnp.concatenate([main, nxt], axis=0)
    n = ext.shape[0]
    return pltpu.roll(ext, n - s, 0)[:n - HALO]


def _colsum(x):
    return jnp.sum(x, axis=0, keepdims=True)


def _pad8(x):
    k = x.shape[0]
    return jnp.concatenate([x, jnp.zeros((8 - k, x.shape[1]), x.dtype)], axis=0) if k < 8 else x


def _rms_fwd(x, g, T, name):
    D = x.shape[1]

    def body(rows, prevs, nexts, fulls, sf, sl):
        return [_rms(_f(rows[0]), _f(fulls[0]))], []

    (h,), _ = _tile_call(body, rows=[x], fulls=[g.reshape(1, D)], row_outs=[(D, BF16)], tm=min(512, T), T=T, name=name)
    return h


def _rms_bwd(x, g, dh, dres, T, name):
    D = x.shape[1]

    def body(rows, prevs, nexts, fulls, sf, sl):
        _, vjp = jax.vjp(_rms, _f(rows[0]), _f(fulls[0]))
        dx, dg = vjp(_f(rows[1]))
        return [dx + _f(rows[2])], [dg]

    (dx,), (dg,) = _tile_call(body, rows=[x, dh, dres], fulls=[g.reshape(1, D)], row_outs=[(D, F32)],
                              acc_outs=[(1, D)], tm=min(512, T), T=T, name=name)
    return dx, dg


def _ffn_conv(u1, prev, cw, cb, sf):
    k = cw.shape[0]
    out = cb + u1 * cw[k - 1:k]
    for j in range(k - 1):
        out = out + _shift_down(u1, prev, k - 1 - j, sf) * cw[j:j + 1]
    return out


def _ffn_fwd(x, p, T, tag):
    M, D = x.shape
    F = p["w_down"].shape[0]
    hf = _rms_fwd(x, p["norm"], T, f"ffn{tag}_norm")
    uf = _matmul(hf, p["w_up"], out_dtype=BF16, name=f"ffn{tag}_up")

    def body(rows, prevs, nexts, fulls, sf, sl):
        u = rows[0]
        gate = _ffn_conv(u[:, :F].astype(F32), prevs[0][:, :F].astype(F32), _f(fulls[0]), _f(fulls[1]), sf)
        return [_gelu(gate) * u[:, F:].astype(F32)], []

    (hid,), _ = _tile_call(body, rows=[uf], prevs=[0], fulls=[p["conv_w"], p["conv_b"].reshape(1, F)],
                           row_outs=[(F, BF16)], tm=min(256, T), T=T, name=f"ffn{tag}_act")
    y = _matmul(hid, p["w_down"], residual=x, name=f"ffn{tag}_down")
    return y, (x, hf, uf, hid)


def _ffn_bwd(dy, saved, p, T, tag):
    x, hf, uf, hid = saved
    M, D = x.shape
    F = p["w_down"].shape[0]
    K = p["conv_w"].shape[0]
    d_hid = _matmul(dy, p["w_down"], mode="nt", out_dtype=BF16, name=f"ffn{tag}_down_dx")
    d_w_down = _matmul(hid, dy, mode="tn", name=f"ffn{tag}_down_dw")

    def body(rows, prevs, nexts, fulls, sf, sl):
        u, dh = rows
        cw, cb = _f(fulls[0]), _f(fulls[1])
        tm = u.shape[0]
        u1c, u1p, u1n = u[:, :F].astype(F32), prevs[0][:, :F].astype(F32), nexts[0][:, :F].astype(F32)
        u1 = jnp.concatenate([u1c, u1n], axis=0)
        u2 = jnp.concatenate([u[:, F:].astype(F32), nexts[0][:, F:].astype(F32)], axis=0)
        dhid = jnp.concatenate([_f(dh), _f(nexts[1])], axis=0)
        gate = _ffn_conv(u1, u1p, cw, cb, sf)
        (act, dact) = jax.jvp(_gelu, (gate,), (jnp.ones_like(gate),))
        d_gate = dhid * u2 * dact
        d_u2 = (dhid * act)[:tm]
        rowid = lax.broadcasted_iota(jnp.int32, d_gate.shape, 0)
        d_gate = jnp.where(jnp.logical_and(sl, rowid >= tm), 0.0, d_gate)
        dgc, dgn = d_gate[:tm], d_gate[tm:]
        d_u1 = dgc * cw[K - 1:K]
        dws = []
        for j in range(K - 1):
            s = K - 1 - j
            d_u1 = d_u1 + _shift_up(dgc, dgn, s, False) * cw[j:j + 1]
            dws.append(_colsum(dgc * _shift_down(u1c, u1p, s, sf)))
        dws.append(_colsum(dgc * u1c))
        d_cw = _pad8(jnp.concatenate(dws, axis=0))
        return [jnp.concatenate([d_u1, d_u2], axis=1)], [d_cw, _colsum(dgc)]

    (d_uf,), (d_cw, d_cb) = _tile_call(
        body, rows=[uf, d_hid], prevs=[0], nexts=[0, 1], fulls=[p["conv_w"], p["conv_b"].reshape(1, F)],
        row_outs=[(2 * F, BF16)], acc_outs=[(8, F), (1, F)], tm=min(256, T), T=T, name=f"ffn{tag}_act_bwd")
    d_hf = _matmul(d_uf, p["w_up"], mode="nt", name=f"ffn{tag}_up_dx")
    d_w_up = _matmul(hf, d_uf, mode="tn", name=f"ffn{tag}_up_dw")
    dx, d_norm = _rms_bwd(x, p["norm"], d_hf, dy, T, f"ffn{tag}_norm_bwd")
    grads = {"norm": d_norm.reshape(D), "w_up": d_w_up, "conv_w": d_cw[:K], "conv_b": d_cb.reshape(F), "w_down": d_w_down}
    return dx, grads


def _lru_conv(u2, prev, cw, cb, sf):
    return _ffn_conv(u2, prev, cw, cb, sf)


def _lru_pre(xr, wbd, gb):
    return jnp.dot(xr.astype(MXU_DTYPE), wbd, preferred_element_type=F32) + gb


def _lru_gates(xr, pre, lam):
    D = xr.shape[1]
    r_gate, i_gate = _sigmoid(pre[:, :D]), _sigmoid(pre[:, D:])
    log_a = -LRU_C * r_gate * _softplus(-lam)
    a = jnp.exp(log_a)
    mult = jnp.sqrt(_neg_expm1(2.0 * log_a))
    return a, mult * (i_gate * xr)


def _lru_scan(a, b, B, T):
    M, D = a.shape
    cw = _pick(D, 256)
    ng = T // 8

    def kern(a_ref, b_ref, o_ref):
        row = lax.broadcasted_iota(jnp.int32, (8, cw), 0)

        def step(g, carry):
            sl = pl.ds(pl.multiple_of(g * 8, 8), 8)
            a8, b8 = a_ref[sl, :], b_ref[sl, :]
            for s in (1, 2, 4):
                a_sh = jnp.where(row >= s, pltpu.roll(a8, s, 0), 1.0)
                b_sh = jnp.where(row >= s, pltpu.roll(b8, s, 0), 0.0)
                b8 = a8 * b_sh + b8
                a8 = a8 * a_sh
            h8 = a8 * carry + b8
            o_ref[sl, :] = h8
            return jnp.broadcast_to(h8[7:8, :], (8, cw))

        lax.fori_loop(0, ng, step, jnp.zeros((8, cw), F32))

    spec = pl.BlockSpec((T, cw), lambda b, c: (b, c))
    return pl.pallas_call(
        kern, name="lru_scan", grid=(B, D // cw), in_specs=[spec, spec], out_specs=spec,
        out_shape=jax.ShapeDtypeStruct((M, D), F32), compiler_params=_cparams(("parallel", "parallel")),
    )(a, b)


def _lru_scan_bwd(a, hs, dhs, B, T):
    M, D = a.shape
    cw = _pick(D, 256)
    ng = T // 8

    def kern(a_ref, h_ref, d_ref, g_ref, da_ref):
        row = lax.broadcasted_iota(jnp.int32, (8, cw), 0)

        def step(k, carry):
            g_next, a_next = carry
            g = ng - 1 - k
            sl = pl.ds(pl.multiple_of(g * 8, 8), 8)
            a8, d8, h8 = a_ref[sl, :], d_ref[sl, :], h_ref[sl, :]
            c8 = jnp.where(row < 7, pltpu.roll(a8, 7, 0), a_next)
            for s in (1, 2, 4):
                d_sh = jnp.where(row < 8 - s, pltpu.roll(d8, 8 - s, 0), 0.0)
                c_sh = jnp.where(row < 8 - s, pltpu.roll(c8, 8 - s, 0), 1.0)
                d8 = d8 + c8 * d_sh
                c8 = c8 * c_sh
            G8 = d8 + c8 * g_next
            gp = jnp.maximum(g - 1, 0)
            hp8 = h_ref[pl.ds(pl.multiple_of(gp * 8, 8), 8), :]
            hp_last = jnp.where(g > 0, jnp.broadcast_to(hp8[7:8, :], (8, cw)), 0.0)
            hprev = jnp.where(row >= 1, pltpu.roll(h8, 1, 0), hp_last)
            g_ref[sl, :] = G8
            da_ref[sl, :] = G8 * hprev
            return jnp.broadcast_to(G8[0:1, :], (8, cw)), jnp.broadcast_to(a8[0:1, :], (8, cw))

        z = jnp.zeros((8, cw), F32)
        lax.fori_loop(0, ng, step, (z, z))

    spec = pl.BlockSpec((T, cw), lambda b, c: (b, c))
    sh = jax.ShapeDtypeStruct((M, D), F32)
    return pl.pallas_call(
        kern, name="lru_scan_bwd", grid=(B, D // cw), in_specs=[spec, spec, spec], out_specs=[spec, spec],
        out_shape=[sh, sh], compiler_params=_cparams(("parallel", "parallel")),
    )(a, hs, dhs)


def _lru_fwd(x, p, B, T):
    M, D = x.shape
    h0 = _rms_fwd(x, p["norm"], T, "lru_norm")
    u0 = _matmul(h0, p["w_in"], bias=p["b_in"], name="lru_in")
    fulls = [p["conv_w"], p["conv_b"].reshape(1, D), p["wbd"], p["gate_b"].reshape(1, 2 * D), p["lam"].reshape(1, D)]

    def body(rows, prevs, nexts, fulls, sf, sl):
        xr = _lru_conv(rows[0][:, D:], prevs[0][:, D:], _f(fulls[0]), _f(fulls[1]), sf)
        a, bt = _lru_gates(xr, _lru_pre(xr, fulls[2][...], _f(fulls[3])), _f(fulls[4]))
        return [a, bt], []

    (a, bt), _ = _tile_call(body, rows=[u0], prevs=[0], fulls=fulls, row_outs=[(D, F32), (D, F32)],
                            tm=min(256, T), T=T, name="lru_gates")
    hs = _lru_scan(a, bt, B, T)

    def body2(rows, prevs, nexts, fulls, sf, sl):
        return [rows[0][...] * _gelu(rows[1][:, :D])], []

    (out,), _ = _tile_call(body2, rows=[hs, u0], row_outs=[(D, BF16)], tm=min(512, T), T=T, name="lru_mix")
    y = _matmul(out, p["w_out"], bias=p["b_out"], residual=x, name="lru_out")
    return y, (x, h0, u0, a, hs, out)


def _lru_bwd(dy, saved, p, B, T):
    x, h0, u0, a, hs, out = saved
    M, D = x.shape
    K = p["conv_w"].shape[0]
    d_out = _matmul(dy, p["w_out"], mode="nt", name="lru_out_dx")
    d_w_out = _matmul(out, dy, mode="tn", name="lru_out_dw")

    def body(rows, prevs, nexts, fulls, sf, sl):
        do, h, u, dyv = rows[0][...], rows[1][...], rows[2][:, :D], rows[3][...]
        act, dact = jax.jvp(_gelu, (u,), (jnp.ones_like(u),))
        return [do * act, do * h * dact], [_colsum(dyv)]

    (d_hs, d_u1), (d_b_out,) = _tile_call(body, rows=[d_out, hs, u0, dy], row_outs=[(D, F32), (D, F32)],
                                          acc_outs=[(1, D)], tm=min(512, T), T=T, name="lru_mix_bwd")
    g_b, d_a = _lru_scan_bwd(a, hs, d_hs, B, T)
    fulls = [p["conv_w"], p["conv_b"].reshape(1, D), p["wbd"], p["gate_b"].reshape(1, 2 * D), p["lam"].reshape(1, D)]

    def body3(rows, prevs, nexts, fulls, sf, sl):
        u, gb_c, da_c, du1 = rows
        cw, cb, wbd, gbias, lam = _f(fulls[0]), _f(fulls[1]), fulls[2][...], _f(fulls[3]), _f(fulls[4])
        tm = u.shape[0]
        u2c, u2p, u2n = u[:, D:], prevs[0][:, D:], nexts[0][:, D:]
        xr_c = _lru_conv(u2c, u2p, cw, cb, sf)
        xr_n = _lru_conv(u2n, u2c[tm - HALO:], cw, cb, False)
        nt = (((1,), (1,)), ((), ()))
        _, vjp_c = jax.vjp(_lru_gates, xr_c, _lru_pre(xr_c, wbd, gbias), lam)
        dxr_c, dpre_c, d_lam = vjp_c((da_c[...], gb_c[...]))
        dpre_cb = dpre_c.astype(MXU_DTYPE)
        dxr_c = dxr_c + lax.dot_general(dpre_cb, wbd, nt, preferred_element_type=F32)
        d_wbd = lax.dot_general(xr_c.astype(MXU_DTYPE), dpre_cb, (((0,), (0,)), ((), ())), preferred_element_type=F32)
        d_gbias = _colsum(dpre_c)
        _, vjp_n = jax.vjp(lambda t, q: _lru_gates(t, q, lam), xr_n, _lru_pre(xr_n, wbd, gbias))
        dxr_n, dpre_n = vjp_n((nexts[2][...], nexts[1][...]))
        dxr_n = dxr_n + lax.dot_general(dpre_n.astype(MXU_DTYPE), wbd, nt, preferred_element_type=F32)
        d_u2 = dxr_c * cw[K - 1:K]
        dws = []
        for j in range(K - 1):
            s = K - 1 - j
            d_u2 = d_u2 + _shift_up(dxr_c, dxr_n, s, sl) * cw[j:j + 1]
            dws.append(_colsum(dxr_c * _shift_down(u2c, u2p, s, sf)))
        dws.append(_colsum(dxr_c * u2c))
        d_u = jnp.concatenate([du1[...], d_u2], axis=1)
        return [d_u], [_pad8(jnp.concatenate(dws, axis=0)), _colsum(dxr_c), d_wbd, d_gbias, d_lam,
                       _colsum(d_u)]

    (d_u0,), (d_cw, d_cb, d_wbd, d_gb, d_lam, d_b_in) = _tile_call(
        body3, rows=[u0, g_b, d_a, d_u1], prevs=[0], nexts=[0, 1, 2], fulls=fulls, row_outs=[(2 * D, BF16)],
        acc_outs=[(8, D), (1, D), (D, 2 * D), (1, 2 * D), (1, D), (1, 2 * D)], tm=min(256, T), T=T, name="lru_gates_bwd")
    d_h0 = _matmul(d_u0, p["w_in"], mode="nt", name="lru_in_dx")
    d_w_in = _matmul(h0, d_u0, mode="tn", name="lru_in_dw")
    dx, d_norm = _rms_bwd(x, p["norm"], d_h0, dy, T, "lru_norm_bwd")
    grads = {"norm": d_norm.reshape(D), "w_in": d_w_in, "b_in": d_b_in.reshape(2 * D), "conv_w": d_cw[:K],
             "conv_b": d_cb.reshape(D), "wbd": d_wbd, "gate_b": d_gb.reshape(2 * D), "lam": d_lam.reshape(D),
             "w_out": d_w_out, "b_out": d_b_out.reshape(D)}
    return dx, grads


def _rwkv_mix(xc, xp, norm, mix, sf):
    h = _rms(xc, norm)
    hp = _rms(xp, norm)
    xx = _shift_down(h, hp, 1, sf) - h
    return h, xx


def _rwkv_pre(k, xw, xa, xg, w0, w1, w2, a0, a1, a2, g1, g2, k_k, k_a, e, et):
    wl = -_softplus(-(w0 + _bdot(jnp.tanh(_bdot(xw, w1)), w2))) - 0.5
    decay = jnp.exp(-jnp.exp(wl))
    a = _sigmoid(a0 + _bdot(_bdot(xa, a1), a2))
    g = _bdot(_sigmoid(_bdot(xg, g1)), g2)
    kk = k * k_k
    nrm = jnp.sqrt(_head_sum(kk * kk, e, et))
    kk = kk / jnp.maximum(nrm, 1e-12)
    k2 = k * (1.0 + (a - 1.0) * k_a)
    return decay, k2, -kk, kk * a, g


def _rwkv_post(y, r, k2, v, g, ln_w, ln_b, r_k, e, et):
    inv = 1.0 / HEAD
    mu = _head_sum(y, e, et) * inv
    yc = y - mu
    var = _head_sum(yc * yc, e, et) * inv
    yn = yc * lax.rsqrt(var + GN_EPS) * ln_w + ln_b
    bonus = _head_sum(r * k2 * r_k, e, et) * v
    return (yn + bonus) * g


def _seg_lane_sums(x, lo_mask):
    s0 = jnp.sum(jnp.where(lo_mask, x, 0.0), axis=1, keepdims=True)
    s1 = jnp.sum(jnp.where(lo_mask, 0.0, x), axis=1, keepdims=True)
    return s0, s1


def _seg_lane_sum(x, lo_mask):
    s0, s1 = _seg_lane_sums(x, lo_mask)
    return jnp.where(lo_mask, s0, s1)


def _col_pair(xT, t, lane, lo_mask):
    col = jnp.sum(jnp.where(lane == t, xT, 0.0), axis=1, keepdims=True)
    return jnp.where(lo_mask, col[:HEAD], col[HEAD:])


def _rwkv_scan(r, w, k, v, a, b, B, T):
    M, D = r.shape
    HP = D // 128
    TC = min(128, T)
    NC = T // TC

    def kern(r_ref, w_ref, k_ref, v_ref, a_ref, b_ref, y_ref, st_ref, S_ref, vT_ref, yT_ref):
        c = pl.program_id(2)

        @pl.when(c == 0)
        def _():
            S_ref[...] = jnp.zeros_like(S_ref)

        vT_ref[...] = v_ref[...].T
        yT_ref[...] = jnp.zeros_like(yT_ref)
        lane = lax.broadcasted_iota(jnp.int32, (2 * HEAD, TC), 1)
        lane_s = lax.broadcasted_iota(jnp.int32, (HEAD, 128), 1)
        lo = lane_s < HEAD

        def group(gi, _):
            sl = pl.ds(pl.multiple_of(gi * 8, 8), 8)
            r8, w8, k8, a8, b8 = r_ref[sl, :], w_ref[sl, :], k_ref[sl, :], a_ref[sl, :], b_ref[sl, :]
            S = S_ref[...]
            for j in range(8):
                t = gi * 8 + j
                st_ref[t] = S
                vb = _col_pair(vT_ref[...], t, lane, lo)
                sa = _seg_lane_sum(S * a8[j:j + 1, :], lo)
                S = S * w8[j:j + 1, :] + sa * b8[j:j + 1, :] + vb * k8[j:j + 1, :]
                ycol = jnp.concatenate(_seg_lane_sums(S * r8[j:j + 1, :], lo), axis=0)
                yT_ref[...] = jnp.where(lane == t, ycol, yT_ref[...])
            S_ref[...] = S
            return 0

        lax.fori_loop(0, TC // 8, group, 0)
        y_ref[...] = yT_ref[...].T

    spec = pl.BlockSpec((TC, 128), lambda bb, hp, c: (bb * NC + c, hp))
    st_spec = pl.BlockSpec((None, TC, HEAD, 128), lambda bb, hp, c: (hp, bb * NC + c, 0, 0))
    y, st = pl.pallas_call(
        kern, name="rwkv_scan", grid=(B, HP, NC), in_specs=[spec] * 6, out_specs=[spec, st_spec],
        out_shape=[jax.ShapeDtypeStruct((M, D), F32), jax.ShapeDtypeStruct((HP, M, HEAD, 128), F32)],
        scratch_shapes=[pltpu.VMEM((HEAD, 128), F32), pltpu.VMEM((2 * HEAD, TC), F32), pltpu.VMEM((2 * HEAD, TC), F32)],
        compiler_params=_cparams(("parallel", "parallel", "arbitrary")),
    )(r, w, k, v, a, b)
    return y, st


def _rwkv_scan_bwd(r, w, k, v, a, b, st, dy, B, T):
    M, D = r.shape
    HP = D // 128
    TC = min(128, T)
    NC = T // TC

    def kern(r_ref, w_ref, k_ref, v_ref, a_ref, b_ref, st_ref, dy_ref,
             dr_ref, dw_ref, dk_ref, dv_ref, da_ref, db_ref, dS_ref, vT_ref, dyT_ref, dvT_ref):
        c = pl.program_id(2)

        @pl.when(c == 0)
        def _():
            dS_ref[...] = jnp.zeros_like(dS_ref)

        vT_ref[...] = v_ref[...].T
        dyT_ref[...] = dy_ref[...].T
        dvT_ref[...] = jnp.zeros_like(dvT_ref)
        lane = lax.broadcasted_iota(jnp.int32, (2 * HEAD, TC), 1)
        lane_s = lax.broadcasted_iota(jnp.int32, (HEAD, 128), 1)
        lo = lane_s < HEAD
        row8 = lax.broadcasted_iota(jnp.int32, (8, 128), 0)

        def group(gk, _):
            gi = TC // 8 - 1 - gk
            sl = pl.ds(pl.multiple_of(gi * 8, 8), 8)
            r8, w8, k8, a8, b8 = r_ref[sl, :], w_ref[sl, :], k_ref[sl, :], a_ref[sl, :], b_ref[sl, :]
            dS = dS_ref[...]
            z8 = jnp.zeros((8, 128), F32)
            dr8, dw8, dk8, da8, db8 = z8, z8, z8, z8, z8
            for j in range(7, -1, -1):
                t = gi * 8 + j
                rj, wj, kj, aj, bj = r8[j:j + 1, :], w8[j:j + 1, :], k8[j:j + 1, :], a8[j:j + 1, :], b8[j:j + 1, :]
                Sp = st_ref[t]
                vb = _col_pair(vT_ref[...], t, lane, lo)
                dyb = _col_pair(dyT_ref[...], t, lane, lo)
                sa = _seg_lane_sum(Sp * aj, lo)
                St = Sp * wj + sa * bj + vb * kj
                dS = dS + dyb * rj
                dr8 = jnp.where(row8 == j, _colsum(St * dyb), dr8)
                dvcol = jnp.concatenate(_seg_lane_sums(dS * kj, lo), axis=0)
                dvT_ref[...] = jnp.where(lane == t, dvcol, dvT_ref[...])
                dk8 = jnp.where(row8 == j, _colsum(dS * vb), dk8)
                dsa = _seg_lane_sum(dS * bj, lo)
                db8 = jnp.where(row8 == j, _colsum(dS * sa), db8)
                dw8 = jnp.where(row8 == j, _colsum(dS * Sp), dw8)
                da8 = jnp.where(row8 == j, _colsum(Sp * dsa), da8)
                dS = dS * wj + dsa * aj
            dS_ref[...] = dS
            dr_ref[sl, :] = dr8
            dw_ref[sl, :] = dw8
            dk_ref[sl, :] = dk8
            da_ref[sl, :] = da8
            db_ref[sl, :] = db8
            return 0

        lax.fori_loop(0, TC // 8, group, 0)
        dv_ref[...] = dvT_ref[...].T

    spec = pl.BlockSpec((TC, 128), lambda bb, hp, c: (bb * NC + (NC - 1 - c), hp))
    st_spec = pl.BlockSpec((None, TC, HEAD, 128), lambda bb, hp, c: (hp, bb * NC + (NC - 1 - c), 0, 0))
    sh = jax.ShapeDtypeStruct((M, D), F32)
    return pl.pallas_call(
        kern, name="rwkv_scan_bwd", grid=(B, HP, NC), in_specs=[spec] * 6 + [st_spec, spec], out_specs=[spec] * 6,
        out_shape=[sh] * 6,
        scratch_shapes=[pltpu.VMEM((HEAD, 128), F32)] + [pltpu.VMEM((2 * HEAD, TC), F32)] * 3,
        compiler_params=_cparams(("parallel", "parallel", "arbitrary")),
    )(r, w, k, v, a, b, st, dy)


def _head_mats(D):
    ch = jnp.arange(D) // HEAD
    e = (ch[:, None] == jnp.arange(128)[None, :]).astype(F32)
    return e, e.T


def _rwkv_fwd(x, p, B, T):
    M, D = x.shape
    e, et = _head_mats(D)
    norm = p["norm"].reshape(1, D)

    def body(rows, prevs, nexts, fulls, sf, sl):
        h, xx = _rwkv_mix(rows[0][...], prevs[0][...], _f(fulls[0]), None, sf)
        mix = _f(fulls[1])
        return [h + xx * mix[i:i + 1] for i in range(6)], []

    xs, _ = _tile_call(body, rows=[x], prevs=[0], fulls=[norm, _pad8(p["mix"])], row_outs=[(D, BF16)] * 6,
                       tm=min(256, T), T=T, name="rwkv_mix")
    r = _matmul(xs[0], p["w_rkv"][0], name="rwkv_r")
    k = _matmul(xs[1], p["w_rkv"][1], name="rwkv_k")
    v = _matmul(xs[2], p["w_rkv"][2], name="rwkv_v")
    pre_fulls = [p["w0"].reshape(1, D), p["w1"], p["w2"], p["a0"].reshape(1, D), p["a1"], p["a2"], p["g1"], p["g2"],
                 p["k_k"].reshape(1, D), p["k_a"].reshape(1, D), e, et]

    def body2(rows, prevs, nexts, fulls, sf, sl):
        outs = _rwkv_pre(rows[0][...], _f(rows[1]), _f(rows[2]), _f(rows[3]), *[f[...] for f in fulls])
        return list(outs), []

    (decay, k2, kkn, bb, g), _ = _tile_call(body2, rows=[k, xs[3], xs[4], xs[5]], fulls=pre_fulls,
                                            row_outs=[(D, F32)] * 5, tm=min(256, T), T=T, name="rwkv_pre")
    y, st = _rwkv_scan(r, decay, k2, v, kkn, bb, B, T)
    post_fulls = [p["ln_w"].reshape(1, D), p["ln_b"].reshape(1, D), p["r_k"].reshape(1, D), e, et]

    def body3(rows, prevs, nexts, fulls, sf, sl):
        return [_rwkv_post(*[rr[...] for rr in rows], *[f[...] for f in fulls])], []

    (z,), _ = _tile_call(body3, rows=[y, r, k2, v, g], fulls=post_fulls, row_outs=[(D, BF16)], tm=min(256, T), T=T,
                         name="rwkv_post")
    out = _matmul(z, p["w_out"], residual=x, name="rwkv_out")
    return out, (x, xs, r, k, v, decay, k2, kkn, bb, g, y, st, z)


def _rwkv_bwd(dout, saved, p, B, T):
    x, xs, r, k, v, decay, k2, kkn, bb, g, y, st, z = saved
    M, D = x.shape
    e, et = _head_mats(D)
    d_z = _matmul(dout, p["w_out"], mode="nt", name="rwkv_out_dx")
    d_w_out = _matmul(z, dout, mode="tn", name="rwkv_out_dw")
    post_fulls = [p["ln_w"].reshape(1, D), p["ln_b"].reshape(1, D), p["r_k"].reshape(1, D), e, et]

    def body(rows, prevs, nexts, fulls, sf, sl):
        prim = [rr[...] for rr in rows[:5]] + [f[...] for f in fulls]
        _, vjp = jax.vjp(_rwkv_post, *prim)
        ct = vjp(rows[5][...])
        return list(ct[:5]), list(ct[5:8])

    (d_y, d_r1, d_k21, d_v1, d_g), (d_ln_w, d_ln_b, d_r_k) = _tile_call(
        body, rows=[y, r, k2, v, g, d_z], fulls=post_fulls, row_outs=[(D, F32)] * 5, acc_outs=[(1, D)] * 3,
        tm=min(256, T), T=T, name="rwkv_post_bwd")
    d_r2, d_w, d_k22, d_v2, d_kkn, d_bb = _rwkv_scan_bwd(r, decay, k2, v, kkn, bb, st, d_y, B, T)
    pre_fulls = [p["w0"].reshape(1, D), p["w1"], p["w2"], p["a0"].reshape(1, D), p["a1"], p["a2"], p["g1"], p["g2"],
                 p["k_k"].reshape(1, D), p["k_a"].reshape(1, D), e, et]

    def body2(rows, prevs, nexts, fulls, sf, sl):
        prim = [rows[0][...], _f(rows[1]), _f(rows[2]), _f(rows[3])] + [f[...] for f in fulls]
        _, vjp = jax.vjp(_rwkv_pre, *prim)
        ct = vjp((rows[4][...], rows[5][...] + rows[6][...], rows[7][...], rows[8][...], rows[9][...]))
        d_r = rows[10][...] + rows[11][...]
        d_v = rows[12][...] + rows[13][...]
        return [ct[0], ct[1], ct[2], ct[3], d_r, d_v], [c.astype(F32) for c in ct[4:14]]

    acc_shapes = [f.shape for f in pre_fulls[:10]]
    (d_k, d_xw, d_xa, d_xg, d_r, d_v), pgr = _tile_call(
        body2, rows=[k, xs[3], xs[4], xs[5], d_w, d_k21, d_k22, d_kkn, d_bb, d_g, d_r1, d_r2, d_v1, d_v2],
        fulls=pre_fulls, row_outs=[(D, BF16), (D, F32), (D, F32), (D, F32), (D, BF16), (D, BF16)], acc_outs=acc_shapes,
        tm=min(256, T), T=T, name="rwkv_pre_bwd")
    d_xr = _matmul(d_r, p["w_rkv"][0], mode="nt", name="rwkv_r_dx")
    d_xk = _matmul(d_k, p["w_rkv"][1], mode="nt", name="rwkv_k_dx")
    d_xv = _matmul(d_v, p["w_rkv"][2], mode="nt", name="rwkv_v_dx")
    d_wr = _matmul(xs[0], d_r, mode="tn", name="rwkv_r_dw")
    d_wk = _matmul(xs[1], d_k, mode="tn", name="rwkv_k_dw")
    d_wv = _matmul(xs[2], d_v, mode="tn", name="rwkv_v_dw")
    norm = p["norm"].reshape(1, D)

    def body3(rows, prevs, nexts, fulls, sf, sl):
        xc, xp = rows[0][...], prevs[0][...]
        nrm, mix = _f(fulls[0]), _f(fulls[1])
        h, xx = _rwkv_mix(xc, xp, nrm, None, sf)
        dxs = [rows[1 + i][...] for i in range(6)]
        dxs_n = [nexts[i][...] for i in range(6)]
        d_h = jnp.zeros_like(h)
        d_sh = jnp.zeros_like(h)
        d_sh_n = jnp.zeros_like(dxs_n[0])
        dmix = []
        for i in range(6):
            m = mix[i:i + 1]
            d_h = d_h + dxs[i] * (1.0 - m)
            d_sh = d_sh + dxs[i] * m
            d_sh_n = d_sh_n + dxs_n[i] * m
            dmix.append(_colsum(dxs[i] * xx))
        d_h = d_h + _shift_up(d_sh, d_sh_n, 1, sl)
        _, vjp = jax.vjp(_rms, xc, nrm)
        dx, dn = vjp(d_h)
        return [dx + rows[7][...]], [dn, _pad8(jnp.concatenate(dmix, axis=0))]

    (dx,), (d_norm, d_mix) = _tile_call(
        body3, rows=[x, d_xr, d_xk, d_xv, d_xw, d_xa, d_xg, dout], prevs=[0], nexts=[1, 2, 3, 4, 5, 6],
        fulls=[norm, _pad8(p["mix"])], row_outs=[(D, F32)], acc_outs=[(1, D), (8, D)], tm=min(256, T), T=T,
        name="rwkv_mix_bwd")
    names = ["w0", "w1", "w2", "a0", "a1", "a2", "g1", "g2", "k_k", "k_a"]
    grads = {n: gr.reshape(p[n].shape) for n, gr in zip(names, pgr, strict=True)}
    grads.update({"norm": d_norm.reshape(D), "mix": d_mix[:6], "w_rkv": jnp.stack([d_wr, d_wk, d_wv]),
                  "r_k": d_r_k.reshape(p["r_k"].shape), "ln_w": d_ln_w.reshape(D), "ln_b": d_ln_b.reshape(D),
                  "w_out": d_w_out})
    return dx, grads


def _loss_head(x, g, tgt, T):
    M, D = x.shape

    def body(rows, prevs, nexts, fulls, sf, sl):
        xv, gv = rows[0][...], _f(fulls[0])
        yv, vjp = jax.vjp(_rms, xv, gv)
        err = yv - rows[1][...]
        dx, dg = vjp(err * (1.0 / D))
        part = jnp.sum(_colsum(err * err), axis=1, keepdims=True) * (0.5 / D)
        return [dx], [dg, jnp.broadcast_to(part, (1, 128))]

    (dx,), (dg, loss) = _tile_call(body, rows=[x, tgt], fulls=[g.reshape(1, D)], row_outs=[(D, F32)],
                                   acc_outs=[(1, D), (1, 128)], tm=min(512, T), T=T, name="loss_head")
    return loss[0, 0], dx, dg.reshape(D)


def _local_step(x3, tgt3, P):
    B, T, D = x3.shape
    x, tgt = x3.reshape(B * T, D), tgt3.reshape(B * T, D)
    x1, s_lru = _lru_fwd(x, P["lru"], B, T)
    x2, s_f0 = _ffn_fwd(x1, P["ffn0"], T, "0")
    x3_, s_rw = _rwkv_fwd(x2, P["rwkv"], B, T)
    x4, s_f1 = _ffn_fwd(x3_, P["ffn1"], T, "1")
    loss, d4, d_fn = _loss_head(x4, P["final_norm"], tgt, T)
    d3, g_f1 = _ffn_bwd(d4, s_f1, P["ffn1"], T, "1")
    d2, g_rw = _rwkv_bwd(d3, s_rw, P["rwkv"], B, T)
    d1, g_f0 = _ffn_bwd(d2, s_f0, P["ffn0"], T, "0")
    d0, g_lru = _lru_bwd(d1, s_lru, P["lru"], B, T)
    return loss, d0.reshape(B, T, D), {"lru": g_lru, "ffn0": g_f0, "rwkv": g_rw, "ffn1": g_f1, "final_norm": d_fn}


WEIGHTS = ['lru_norm', 'lru_w_in', 'lru_b_in', 'lru_conv_w', 'lru_conv_b', 'lru_gate_w', 'lru_gate_b', 'lru_lambda',
           'lru_w_out', 'lru_b_out', 'rwkv_norm', 'rwkv_mix', 'rwkv_w_rkv', 'rwkv_w0', 'rwkv_w1', 'rwkv_w2', 'rwkv_a0',
           'rwkv_a1', 'rwkv_a2', 'rwkv_g1', 'rwkv_g2', 'rwkv_k_k', 'rwkv_k_a', 'rwkv_r_k', 'rwkv_ln_w', 'rwkv_ln_b',
           'rwkv_w_out', 'ffn_norm', 'ffn_w_up', 'ffn_conv_w', 'ffn_conv_b', 'ffn_w_down', 'final_norm']
SHARD_AXIS = {'lru_w_in': 2, 'lru_conv_w': 2, 'lru_w_out': 1, 'rwkv_norm': 1, 'rwkv_mix': 2, 'rwkv_w_rkv': 2,
              'rwkv_w0': 1, 'rwkv_w1': 1, 'rwkv_w2': 2, 'rwkv_a0': 1, 'rwkv_a1': 1, 'rwkv_a2': 2, 'rwkv_g1': 1,
              'rwkv_g2': 2, 'rwkv_k_k': 1, 'rwkv_k_a': 1, 'rwkv_ln_w': 1, 'rwkv_ln_b': 1, 'rwkv_w_out': 1,
              'ffn_w_up': 2, 'ffn_conv_w': 2, 'ffn_w_down': 1}
MXU_WEIGHTS = ('lru_w_in', 'lru_w_out', 'rwkv_w_rkv', 'rwkv_w_out', 'ffn_w_up', 'ffn_w_down')
N_CHIPS = 4
LANES = 1024


def _pack(arrs, dtype, row_mult):
    flat = jnp.concatenate([a.reshape(-1).astype(dtype) for a in arrs])
    n = flat.shape[0]
    unit = row_mult * LANES
    tot = -(-n // unit) * unit
    if tot > n:
        flat = jnp.concatenate([flat, jnp.zeros((tot - n,), dtype)])
    return flat.reshape(tot // LANES, LANES)


def _unpack(buf, shapes):
    flat = buf.reshape(-1)
    out, off = [], 0
    for s in shapes:
        n = 1
        for d in s:
            n *= d
        out.append(flat[off:off + n].reshape(s))
        off += n
    return out


def _unshard(stacked, axis):
    return jnp.concatenate([stacked[q] for q in range(N_CHIPS)], axis=axis)


def _to_shards(full, axis):
    return jnp.stack(jnp.split(full, N_CHIPS, axis=axis))


MESH_ID = pl.DeviceIdType.MESH


def _chip_exchange(buf, same_to_all, name):
    shape = buf.shape if same_to_all else buf.shape[1:]

    def body(in_ref, out_ref, send_sems, recv_sems, local_sem):
        x, y, c = lax.axis_index("x"), lax.axis_index("y"), lax.axis_index("c")
        p = 2 * x + y
        chips = [(1 - x, y), (x, 1 - y), (1 - x, 1 - y)]

        def src(q):
            return in_ref if same_to_all else in_ref.at[q]

        mine = pltpu.make_async_copy(src(p), out_ref.at[p], local_sem)
        mine.start()
        copies = []
        for j, (qx, qy) in enumerate(chips):
            q = 2 * qx + qy
            copies.append((pltpu.make_async_remote_copy(
                src_ref=src(q), dst_ref=out_ref.at[p], send_sem=send_sems.at[j], recv_sem=recv_sems.at[j],
                device_id=(qx, qy, c), device_id_type=MESH_ID),
                pltpu.make_async_remote_copy(
                src_ref=src(q), dst_ref=out_ref.at[q], send_sem=send_sems.at[j], recv_sem=recv_sems.at[j],
                device_id=(qx, qy, c), device_id_type=MESH_ID)))
        for snd, _ in copies:
            snd.start()
        for _, rcv in copies:
            rcv.wait_recv()
        for snd, _ in copies:
            snd.wait_send()
        mine.wait()

    return pl.pallas_call(
        body, name=name,
        out_shape=jax.ShapeDtypeStruct((N_CHIPS,) + tuple(shape), buf.dtype),
        in_specs=[pl.BlockSpec(memory_space=pl.ANY)], out_specs=pl.BlockSpec(memory_space=pl.ANY),
        scratch_shapes=[pltpu.SemaphoreType.DMA((3,)), pltpu.SemaphoreType.DMA((3,)), pltpu.SemaphoreType.DMA],
        compiler_params=pltpu.CompilerParams(has_side_effects=True),
    )(buf)


def _sibling_swap(buf, name):
    def body(in_ref, out_ref, send_sem, recv_sem):
        x, y, c = lax.axis_index("x"), lax.axis_index("y"), lax.axis_index("c")
        cp = pltpu.make_async_remote_copy(src_ref=in_ref, dst_ref=out_ref, send_sem=send_sem, recv_sem=recv_sem,
                                          device_id=(x, y, 1 - c), device_id_type=MESH_ID)
        cp.start()
        cp.wait()

    return pl.pallas_call(
        body, name=name, out_shape=jax.ShapeDtypeStruct(buf.shape, buf.dtype),
        in_specs=[pl.BlockSpec(memory_space=pl.ANY)], out_specs=pl.BlockSpec(memory_space=pl.ANY),
        scratch_shapes=[pltpu.SemaphoreType.DMA, pltpu.SemaphoreType.DMA],
        compiler_params=pltpu.CompilerParams(has_side_effects=True),
    )(buf)


def _by_half(mine, other):
    c = lax.axis_index("c")
    st = jnp.stack([mine, other])
    return lax.dynamic_index_in_dim(st, c, 0, keepdims=False), lax.dynamic_index_in_dim(st, 1 - c, 0, keepdims=False)


def _sum_rows_call(parts, name):
    R = parts[0].shape[0]
    tr = _pick_rows(R)
    n = len(parts)

    def kern(*refs):
        acc = refs[0][...]
        for r in refs[1:n]:
            acc = acc + r[...]
        refs[n][...] = acc

    spec = pl.BlockSpec((tr, LANES), lambda i: (i, 0))
    return pl.pallas_call(kern, name=name, grid=(R // tr,), in_specs=[spec] * n, out_specs=spec,
                          out_shape=jax.ShapeDtypeStruct((R, LANES), F32), compiler_params=_cparams(("parallel",)))(*parts)


def _pick_rows(R):
    for t in (512, 256, 128, 64, 32, 16, 8):
        if R % t == 0:
            return t
    return R


def _adamw_call(w, g, m, v, name):
    R = w.shape[0]
    tr = _pick_rows(R)
    c1 = 1.0 / (1.0 - ADAM_B1 ** ADAM_STEP)
    c2 = 1.0 / (1.0 - ADAM_B2 ** ADAM_STEP)

    def kern(w_ref, g_ref, m_ref, v_ref, d_ref, nm_ref, nv_ref):
        gv = g_ref[...]
        nm = ADAM_B1 * m_ref[...] + (1.0 - ADAM_B1) * gv
        nv = ADAM_B2 * v_ref[...] + (1.0 - ADAM_B2) * (gv * gv)
        d_ref[...] = -ADAM_LR * ((nm * c1) / (jnp.sqrt(nv * c2) + ADAM_EPS) + ADAM_WD * w_ref[...])
        nm_ref[...] = nm
        nv_ref[...] = nv

    spec = pl.BlockSpec((tr, LANES), lambda i: (i, 0))
    sh = jax.ShapeDtypeStruct((R, LANES), F32)
    return pl.pallas_call(kern, name=name, grid=(R // tr,), in_specs=[spec] * 4, out_specs=[spec] * 3,
                          out_shape=[sh] * 3, compiler_params=_cparams(("parallel",)))(w, g, m, v)


def _gate_dense(gate_w):
    _, nb, bw, _ = gate_w.shape
    eye = jnp.eye(nb, dtype=gate_w.dtype)
    dense = jnp.einsum('gncd,nm->gncmd', gate_w, eye).reshape(2, nb * bw, nb * bw)
    return jnp.concatenate([dense[0], dense[1]], axis=1)


def _gate_blocks(d_dense, nb):
    D = d_dense.shape[0]
    bw = D // nb
    g = d_dense.reshape(nb, bw, 2, nb, bw)
    return jnp.einsum('ncgnd->gncd', g)


def _step(W, M1, V1, x, tgt):
    sharded = [n for n in WEIGHTS if n in SHARD_AXIS]
    repl = [n for n in WEIGHTS if n not in SHARD_AXIS]
    big = [n for n in sharded if n in MXU_WEIGHTS]
    small = [n for n in sharded if n not in MXU_WEIGHTS]

    def gather(names, dtype, tag):
        buf = _pack([W[n] for n in names], dtype, 32)
        rh = buf.shape[0] // 2
        c = lax.axis_index("c")
        my_half = lax.dynamic_slice_in_dim(buf, c * rh, rh, 0)
        got = _chip_exchange(my_half, True, f"gather_{tag}")
        other = _sibling_swap(got, f"gather_{tag}_swap")
        h0, h1 = _by_half(got, other)
        full = jnp.concatenate([h0, h1], axis=1)
        per_chip = [_unpack(full[q], [W[n].shape for n in names]) for q in range(N_CHIPS)]
        return {n: jnp.concatenate([per_chip[q][i] for q in range(N_CHIPS)], axis=SHARD_AXIS[n])
                for i, n in enumerate(names)}

    full = {**gather(big, MXU_DTYPE, "mats"), **gather(small, F32, "vecs")}
    for n in repl:
        full[n] = W[n]

    P = {
        "lru": {"norm": full["lru_norm"][0], "w_in": full["lru_w_in"][0], "b_in": full["lru_b_in"][0],
                "conv_w": full["lru_conv_w"][0], "conv_b": full["lru_conv_b"][0],
                "wbd": _gate_dense(full["lru_gate_w"][0]).astype(MXU_DTYPE), "gate_b": full["lru_gate_b"][0].reshape(-1),
                "lam": full["lru_lambda"][0], "w_out": full["lru_w_out"][0], "b_out": full["lru_b_out"][0]},
        "rwkv": {"norm": full["rwkv_norm"][0], "mix": full["rwkv_mix"][0], "w_rkv": full["rwkv_w_rkv"][0],
                 "w0": full["rwkv_w0"][0], "w1": full["rwkv_w1"][0], "w2": full["rwkv_w2"][0], "a0": full["rwkv_a0"][0],
                 "a1": full["rwkv_a1"][0], "a2": full["rwkv_a2"][0], "g1": full["rwkv_g1"][0], "g2": full["rwkv_g2"][0],
                 "k_k": full["rwkv_k_k"][0], "k_a": full["rwkv_k_a"][0], "r_k": full["rwkv_r_k"][0],
                 "ln_w": full["rwkv_ln_w"][0], "ln_b": full["rwkv_ln_b"][0], "w_out": full["rwkv_w_out"][0]},
        "final_norm": full["final_norm"],
    }
    for l in range(2):
        P[f"ffn{l}"] = {"norm": full["ffn_norm"][l], "w_up": full["ffn_w_up"][l], "conv_w": full["ffn_conv_w"][l],
                        "conv_b": full["ffn_conv_b"][l], "w_down": full["ffn_w_down"][l]}

    loss, gx, G = _local_step(x, tgt, P)

    nb = W["lru_gate_w"].shape[2]
    gl, gr = G["lru"], G["rwkv"]
    gfull = {
        "lru_norm": gl["norm"][None], "lru_w_in": gl["w_in"][None], "lru_b_in": gl["b_in"][None],
        "lru_conv_w": gl["conv_w"][None], "lru_conv_b": gl["conv_b"][None], "lru_gate_w": _gate_blocks(gl["wbd"], nb)[None],
        "lru_gate_b": gl["gate_b"].reshape(W["lru_gate_b"].shape), "lru_lambda": gl["lam"][None],
        "lru_w_out": gl["w_out"][None], "lru_b_out": gl["b_out"][None],
        "rwkv_norm": gr["norm"][None], "rwkv_mix": gr["mix"][None], "rwkv_w_rkv": gr["w_rkv"][None],
        "rwkv_w0": gr["w0"][None], "rwkv_w1": gr["w1"][None], "rwkv_w2": gr["w2"][None], "rwkv_a0": gr["a0"][None],
        "rwkv_a1": gr["a1"][None], "rwkv_a2": gr["a2"][None], "rwkv_g1": gr["g1"][None], "rwkv_g2": gr["g2"][None],
        "rwkv_k_k": gr["k_k"][None], "rwkv_k_a": gr["k_a"][None], "rwkv_r_k": gr["r_k"][None],
        "rwkv_ln_w": gr["ln_w"][None], "rwkv_ln_b": gr["ln_b"][None], "rwkv_w_out": gr["w_out"][None],
        "final_norm": G["final_norm"],
    }
    for k in ("norm", "w_up", "conv_w", "conv_b", "w_down"):
        gfull["ffn_" + k] = jnp.stack([G["ffn0"][k], G["ffn1"][k]])

    c = lax.axis_index("c")
    shard_shapes = [W[n].shape for n in sharded]
    gs = jnp.stack([_pack([_to_shards(gfull[n], SHARD_AXIS[n])[q] for n in sharded], F32, 32) for q in range(N_CHIPS)])
    rh = gs.shape[1] // 2
    grep = _pack([gfull[n] for n in repl], F32, 16)
    rr = grep.shape[0]
    halves = gs.reshape(N_CHIPS, 2, rh, LANES)
    keep = lax.dynamic_index_in_dim(halves, c, 1, keepdims=False).reshape(N_CHIPS * rh, LANES)
    give = lax.dynamic_index_in_dim(halves, 1 - c, 1, keepdims=False).reshape(N_CHIPS * rh, LANES)
    got = _sibling_swap(jnp.concatenate([give, grep]), "reduce_pair_swap")
    pair = _sum_rows_call([jnp.concatenate([keep, grep]), got], "reduce_pair_sum")
    slots = jnp.concatenate([pair[:N_CHIPS * rh].reshape(N_CHIPS, rh, LANES),
                             jnp.broadcast_to(pair[N_CHIPS * rh:], (N_CHIPS, rr, LANES))], axis=1)
    arrived = _chip_exchange(slots, False, "reduce_chips")
    tot = _sum_rows_call([arrived[q] for q in range(N_CHIPS)], "reduce_chip_sum")
    my_half = tot[:rh]
    other_half = _sibling_swap(my_half, "reduce_half_swap")
    h0, h1 = _by_half(my_half, other_half)
    g_sh = jnp.concatenate([h0, h1])
    g_rp = tot[rh:]

    outs = {}
    for names, gbuf, tag in ((sharded, g_sh, "sharded"), (repl, g_rp, "repl")):
        shapes = [W[n].shape for n in names]
        row_mult = gbuf.shape[0]
        wb, mb, vb = (_pack([S[n] for n in names], F32, 1) for S in (W, M1, V1))
        pad = gbuf.shape[0] - wb.shape[0]
        if pad:
            wb, mb, vb = (jnp.concatenate([b, jnp.ones((pad, LANES), F32)]) for b in (wb, mb, vb))
        d, nm, nv = _adamw_call(wb, gbuf, mb, vb, f"adamw_{tag}")
        for kind, buf in (("grad", gbuf), ("delta", d), ("new_m", nm), ("new_v", nv)):
            for n, a in zip(names, _unpack(buf, shapes), strict=True):
                outs[(kind, n)] = a
    loss = lax.psum(loss, ("x", "y", "c"))
    return (loss, gx, *[outs[(kind, n)] for kind in ("grad", "delta", "new_m", "new_v") for n in WEIGHTS])


def kernel(x, lru_norm, lru_w_in, lru_b_in, lru_conv_w, lru_conv_b, lru_gate_w, lru_gate_b, lru_lambda, lru_w_out, lru_b_out, rwkv_norm, rwkv_mix, rwkv_w_rkv, rwkv_w0, rwkv_w1, rwkv_w2, rwkv_a0, rwkv_a1, rwkv_a2, rwkv_g1, rwkv_g2, rwkv_k_k, rwkv_k_a, rwkv_r_k, rwkv_ln_w, rwkv_ln_b, rwkv_w_out, ffn_norm, ffn_w_up, ffn_conv_w, ffn_conv_b, ffn_w_down, final_norm, loss_target, m_lru_norm, m_lru_w_in, m_lru_b_in, m_lru_conv_w, m_lru_conv_b, m_lru_gate_w, m_lru_gate_b, m_lru_lambda, m_lru_w_out, m_lru_b_out, m_rwkv_norm, m_rwkv_mix, m_rwkv_w_rkv, m_rwkv_w0, m_rwkv_w1, m_rwkv_w2, m_rwkv_a0, m_rwkv_a1, m_rwkv_a2, m_rwkv_g1, m_rwkv_g2, m_rwkv_k_k, m_rwkv_k_a, m_rwkv_r_k, m_rwkv_ln_w, m_rwkv_ln_b, m_rwkv_w_out, m_ffn_norm, m_ffn_w_up, m_ffn_conv_w, m_ffn_conv_b, m_ffn_w_down, m_final_norm, v_lru_norm, v_lru_w_in, v_lru_b_in, v_lru_conv_w, v_lru_conv_b, v_lru_gate_w, v_lru_gate_b, v_lru_lambda, v_lru_w_out, v_lru_b_out, v_rwkv_norm, v_rwkv_mix, v_rwkv_w_rkv, v_rwkv_w0, v_rwkv_w1, v_rwkv_w2, v_rwkv_a0, v_rwkv_a1, v_rwkv_a2, v_rwkv_g1, v_rwkv_g2, v_rwkv_k_k, v_rwkv_k_a, v_rwkv_r_k, v_rwkv_ln_w, v_rwkv_ln_b, v_rwkv_w_out, v_ffn_norm, v_ffn_w_up, v_ffn_conv_w, v_ffn_conv_b, v_ffn_w_down, v_final_norm):
    given = dict(locals())
    W = {n: given[n] for n in WEIGHTS}
    M1 = {n: given["m_" + n] for n in WEIGHTS}
    V1 = {n: given["v_" + n] for n in WEIGHTS}
    return _step(W, M1, V1, x, loss_target)
```

```python
import functools

import jax
import jax.numpy as jnp
from jax import lax
from jax.experimental import pallas as pl
from jax.experimental.pallas import tpu as pltpu

F32 = jnp.float32
BF16 = jnp.bfloat16
MXU_DTYPE = BF16

HEAD = 64
LRU_C = 8.0
GN_EPS = 64e-5
RMS_EPS = 1e-6
HALO = 16
VMEM_LIMIT = 56 * 1024 * 1024

ADAM_LR, ADAM_B1, ADAM_B2, ADAM_EPS, ADAM_WD, ADAM_STEP = 0.001, 0.9, 0.999, 1e-08, 0.01, 10


def _cparams(sem):
    return pltpu.CompilerParams(dimension_semantics=sem, vmem_limit_bytes=VMEM_LIMIT)


def _pick(n, want):
    if n <= want:
        return n
    t = want
    while t >= 128:
        if n % t == 0:
            return t
        t -= 128
    return n


def _matmul(a, b, mode="nn", bias=None, residual=None, out_dtype=F32, name="mm", tm=512, tn=512, tk=1024):
    if mode == "nn":
        (M, K), (K2, N) = a.shape, b.shape
    elif mode == "nt":
        (M, K), (N, K2) = a.shape, b.shape
    else:
        (K, M), (K2, N) = a.shape, b.shape
    assert K == K2, (a.shape, b.shape, mode)
    tm, tn, tk = _pick(M, tm), _pick(N, tn), _pick(K, tk)
    nk = K // tk
    dims = {"nn": (((1,), (0,)), ((), ())), "nt": (((1,), (1,)), ((), ())), "tn": (((0,), (0,)), ((), ()))}[mode]
    a_spec = {"nn": pl.BlockSpec((tm, tk), lambda i, j, k: (i, k)),
              "nt": pl.BlockSpec((tm, tk), lambda i, j, k: (i, k)),
              "tn": pl.BlockSpec((tk, tm), lambda i, j, k: (k, i))}[mode]
    b_spec = {"nn": pl.BlockSpec((tk, tn), lambda i, j, k: (k, j)),
              "nt": pl.BlockSpec((tn, tk), lambda i, j, k: (j, k)),
              "tn": pl.BlockSpec((tk, tn), lambda i, j, k: (k, j))}[mode]
    in_specs, operands = [a_spec, b_spec], [a, b]
    if bias is not None:
        in_specs.append(pl.BlockSpec((1, tn), lambda i, j, k: (0, j)))
        operands.append(bias.reshape(1, N))
    if residual is not None:
        in_specs.append(pl.BlockSpec((tm, tn), lambda i, j, k: (i, j)))
        operands.append(residual)
    has_bias, has_res = bias is not None, residual is not None

    def kern(*refs):
        a_ref, b_ref = refs[0], refs[1]
        o_ref, acc_ref = refs[-2], refs[-1]
        k = pl.program_id(2)

        @pl.when(k == 0)
        def _():
            acc_ref[...] = jnp.zeros_like(acc_ref)

        acc_ref[...] += lax.dot_general(a_ref[...].astype(MXU_DTYPE), b_ref[...].astype(MXU_DTYPE), dims,
                                        preferred_element_type=F32)

        @pl.when(k == nk - 1)
        def _():
            r = acc_ref[...]
            pos = 2
            if has_bias:
                r = r + refs[pos][...].astype(F32)
                pos += 1
            if has_res:
                r = r + refs[pos][...].astype(F32)
            o_ref[...] = r.astype(o_ref.dtype)

    return pl.pallas_call(
        kern, name=name,
        grid=(M // tm, N // tn, nk),
        in_specs=in_specs,
        out_specs=pl.BlockSpec((tm, tn), lambda i, j, k: (i, j)),
        out_shape=jax.ShapeDtypeStruct((M, N), out_dtype),
        scratch_shapes=[pltpu.VMEM((tm, tn), F32)],
        compiler_params=_cparams(("parallel", "parallel", "arbitrary")),
    )(*operands)


def _tile_call(body, *, rows, prevs=(), nexts=(), fulls=(), row_outs=(), acc_outs=(), tm, T, name):
    M = rows[0].shape[0]
    n_tiles, tps, hb = M // tm, T // tm, tm // HALO
    n_halo_blocks = M // HALO
    nr, npv, nnx, nf, nro, nac = len(rows), len(prevs), len(nexts), len(fulls), len(row_outs), len(acc_outs)

    def kern(*refs):
        i = pl.program_id(0)
        row_refs = refs[:nr]
        prev_refs = refs[nr:nr + npv]
        next_refs = refs[nr + npv:nr + npv + nnx]
        full_refs = refs[nr + npv + nnx:nr + npv + nnx + nf]
        out_refs = refs[nr + npv + nnx + nf:nr + npv + nnx + nf + nro]
        acc_refs = refs[nr + npv + nnx + nf + nro:]
        seq_first = (i % tps) == 0
        seq_last = (i % tps) == (tps - 1)
        outs, accs = body(row_refs, prev_refs, next_refs, full_refs, seq_first, seq_last)
        for r, o in zip(out_refs, outs, strict=True):
            r[...] = o.astype(r.dtype)
        if nac:
            @pl.when(i == 0)
            def _():
                for r in acc_refs:
                    r[...] = jnp.zeros_like(r)
            for r, a in zip(acc_refs, accs, strict=True):
                r[...] += a.astype(F32)

    in_specs = [pl.BlockSpec((tm, a.shape[1]), lambda i: (i, 0)) for a in rows]
    in_specs += [pl.BlockSpec((HALO, rows[k].shape[1]), lambda i: (jnp.maximum(i * hb - 1, 0), 0)) for k in prevs]
    in_specs += [pl.BlockSpec((HALO, rows[k].shape[1]), lambda i: (jnp.minimum((i + 1) * hb, n_halo_blocks - 1), 0))
                 for k in nexts]
    in_specs += [pl.BlockSpec(f.shape, lambda i: (0, 0)) for f in fulls]
    out_specs = [pl.BlockSpec((tm, w), lambda i: (i, 0)) for (w, _) in row_outs]
    out_specs += [pl.BlockSpec(s, lambda i: (0, 0)) for s in acc_outs]
    out_shape = [jax.ShapeDtypeStruct((M, w), dt) for (w, dt) in row_outs]
    out_shape += [jax.ShapeDtypeStruct(s, F32) for s in acc_outs]
    operands = list(rows) + [rows[k] for k in prevs] + [rows[k] for k in nexts] + list(fulls)
    res = pl.pallas_call(
        kern, name=name, grid=(n_tiles,), in_specs=in_specs, out_specs=out_specs, out_shape=out_shape,
        compiler_params=_cparams(("arbitrary",)),
    )(*operands)
    return res[:nro], res[nro:]


def _f(ref):
    return ref[...].astype(F32)


def _sigmoid(x):
    return 1.0 / (1.0 + jnp.exp(-x))


def _softplus(x):
    return jnp.maximum(x, 0.0) + jnp.log(1.0 + jnp.exp(-jnp.abs(x)))


def _neg_expm1(x):
    series = -x * (1.0 + x * (0.5 + x * (1.0 / 6.0) * (1.0 + 0.25 * x)))
    return jnp.where(x > -0.01, series, 1.0 - jnp.exp(x))


def _gelu(x):
    return 0.5 * x * (1.0 + jnp.tanh(0.7978845608028654 * (x + 0.044715 * x * x * x)))


def _rms(x, g):
    return x * lax.rsqrt(jnp.mean(x * x, axis=-1, keepdims=True) + RMS_EPS) * g


@jax.custom_vjp
def _bdot(x, w):
    return jnp.dot(x.astype(MXU_DTYPE), w.astype(MXU_DTYPE), preferred_element_type=F32)


def _bdot_fwd(x, w):
    return _bdot(x, w), (x, w)


def _bdot_bwd(res, ct):
    x, w = res
    ctb = ct.astype(MXU_DTYPE)
    dx = lax.dot_general(ctb, w.astype(MXU_DTYPE), (((1,), (1,)), ((), ())), preferred_element_type=F32)
    dw = lax.dot_general(x.astype(MXU_DTYPE), ctb, (((0,), (0,)), ((), ())), preferred_element_type=F32)
    return dx.astype(x.dtype), dw.astype(w.dtype)


_bdot.defvjp(_bdot_fwd, _bdot_bwd)


@jax.custom_vjp
def _head_sum(x, e, et):
    s = jnp.dot(x, e, precision=lax.Precision.HIGHEST, preferred_element_type=F32)
    return jnp.dot(s, et, precision=lax.Precision.HIGHEST, preferred_element_type=F32)


def _head_sum_fwd(x, e, et):
    return _head_sum(x, e, et), (e, et)


def _head_sum_bwd(res, ct):
    e, et = res
    return _head_sum(ct, e, et), jnp.zeros_like(e), jnp.zeros_like(et)


_head_sum.defvjp(_head_sum_fwd, _head_sum_bwd)


def _shift_down(main, prev, s, seq_first):
    prev = jnp.where(seq_first, 0.0, prev)
    ext = jnp.concatenate([prev, main], axis=0)
    return pltpu.roll(ext, s, 0)[HALO:]


def _shift_up(main, nxt, s, seq_last):
    nxt = jnp.where(seq_last, 0.0, nxt)
    ext = jnp.concatenate([main, nxt], axis=0)
    n = ext.shape[0]
    return pltpu.roll(ext, n - s, 0)[:n - HALO]


def _colsum(x):
    return jnp.sum(x, axis=0, keepdims=True)


def _pad8(x):
    k = x.shape[0]
    return jnp.concatenate([x, jnp.zeros((8 - k, x.shape[1]), x.dtype)], axis=0) if k < 8 else x


def _rms_fwd(x, g, T, name):
    D = x.shape[1]

    def body(rows, prevs, nexts, fulls, sf, sl):
        return [_rms(_f(rows[0]), _f(fulls[0]))], []

    (h,), _ = _tile_call(body, rows=[x], fulls=[g.reshape(1, D)], row_outs=[(D, BF16)], tm=min(512, T), T=T, name=name)
    return h


def _rms_bwd(x, g, dh, dres, T, name):
    D = x.shape[1]

    def body(rows, prevs, nexts, fulls, sf, sl):
        _, vjp = jax.vjp(_rms, _f(rows[0]), _f(fulls[0]))
        dx, dg = vjp(_f(rows[1]))
        return [dx + _f(rows[2])], [dg]

    (dx,), (dg,) = _tile_call(body, rows=[x, dh, dres], fulls=[g.reshape(1, D)], row_outs=[(D, F32)],
                              acc_outs=[(1, D)], tm=min(512, T), T=T, name=name)
    return dx, dg


def _ffn_conv(u1, prev, cw, cb, sf):
    k = cw.shape[0]
    out = cb + u1 * cw[k - 1:k]
    for j in range(k - 1):
        out = out + _shift_down(u1, prev, k - 1 - j, sf) * cw[j:j + 1]
    return out


def _ffn_fwd(x, p, T, tag):
    M, D = x.shape
    F = p["w_down"].shape[0]
    hf = _rms_fwd(x, p["norm"], T, f"ffn{tag}_norm")
    uf = _matmul(hf, p["w_up"], out_dtype=BF16, name=f"ffn{tag}_up")

    def body(rows, prevs, nexts, fulls, sf, sl):
        u = rows[0]
        gate = _ffn_conv(u[:, :F].astype(F32), prevs[0][:, :F].astype(F32), _f(fulls[0]), _f(fulls[1]), sf)
        return [_gelu(gate) * u[:, F:].astype(F32)], []

    (hid,), _ = _tile_call(body, rows=[uf], prevs=[0], fulls=[p["conv_w"], p["conv_b"].reshape(1, F)],
                           row_outs=[(F, BF16)], tm=min(256, T), T=T, name=f"ffn{tag}_act")
    y = _matmul(hid, p["w_down"], residual=x, name=f"ffn{tag}_down")
    return y, (x, hf, uf, hid)


def _ffn_bwd(dy, saved, p, T, tag):
    x, hf, uf, hid = saved
    M, D = x.shape
    F = p["w_down"].shape[0]
    K = p["conv_w"].shape[0]
    d_hid = _matmul(dy, p["w_down"], mode="nt", out_dtype=BF16, name=f"ffn{tag}_down_dx")
    d_w_down = _matmul(hid, dy, mode="tn", name=f"ffn{tag}_down_dw")

    def body(rows, prevs, nexts, fulls, sf, sl):
        u, dh = rows
        cw, cb = _f(fulls[0]), _f(fulls[1])
        tm = u.shape[0]
        u1c, u1p, u1n = u[:, :F].astype(F32), prevs[0][:, :F].astype(F32), nexts[0][:, :F].astype(F32)
        u1 = jnp.concatenate([u1c, u1n], axis=0)
        u2 = jnp.concatenate([u[:, F:].astype(F32), nexts[0][:, F:].astype(F32)], axis=0)
        dhid = jnp.concatenate([_f(dh), _f(nexts[1])], axis=0)
        gate = _ffn_conv(u1, u1p, cw, cb, sf)
        (act, dact) = jax.jvp(_gelu, (gate,), (jnp.ones_like(gate),))
        d_gate = dhid * u2 * dact
        d_u2 = (dhid * act)[:tm]
        rowid = lax.broadcasted_iota(jnp.int32, d_gate.shape, 0)
        d_gate = jnp.where(jnp.logical_and(sl, rowid >= tm), 0.0, d_gate)
        dgc, dgn = d_gate[:tm], d_gate[tm:]
        d_u1 = dgc * cw[K - 1:K]
        dws = []
        for j in range(K - 1):
            s = K - 1 - j
            d_u1 = d_u1 + _shift_up(dgc, dgn, s, False) * cw[j:j + 1]
            dws.append(_colsum(dgc * _shift_down(u1c, u1p, s, sf)))
        dws.append(_colsum(dgc * u1c))
        d_cw = _pad8(jnp.concatenate(dws, axis=0))
        return [jnp.concatenate([d_u1, d_u2], axis=1)], [d_cw, _colsum(dgc)]

    (d_uf,), (d_cw, d_cb) = _tile_call(
        body, rows=[uf, d_hid], prevs=[0], nexts=[0, 1], fulls=[p["conv_w"], p["conv_b"].reshape(1, F)],
        row_outs=[(2 * F, BF16)], acc_outs=[(8, F), (1, F)], tm=min(256, T), T=T, name=f"ffn{tag}_act_bwd")
    d_hf = _matmul(d_uf, p["w_up"], mode="nt", name=f"ffn{tag}_up_dx")
    d_w_up = _matmul(hf, d_uf, mode="tn", name=f"ffn{tag}_up_dw")
    dx, d_norm = _rms_bwd(x, p["norm"], d_hf, dy, T, f"ffn{tag}_norm_bwd")
    grads = {"norm": d_norm.reshape(D), "w_up": d_w_up, "conv_w": d_cw[:K], "conv_b": d_cb.reshape(F), "w_down": d_w_down}
    return dx, grads


def _lru_conv(u2, prev, cw, cb, sf):
    return _ffn_conv(u2, prev, cw, cb, sf)


def _lru_pre(xr, wbd, gb):
    return jnp.dot(xr.astype(MXU_DTYPE), wbd, preferred_element_type=F32) + gb


def _lru_gates(xr, pre, lam):
    D = xr.shape[1]
    r_gate, i_gate = _sigmoid(pre[:, :D]), _sigmoid(pre[:, D:])
    log_a = -LRU_C * r_gate * _softplus(-lam)
    a = jnp.exp(log_a)
    mult = jnp.sqrt(_neg_expm1(2.0 * log_a))
    return a, mult * (i_gate * xr)


def _lru_scan(a, b, B, T):
    M, D = a.shape
    cw = _pick(D, 256)
    ng = T // 8

    def kern(a_ref, b_ref, o_ref):
        row = lax.broadcasted_iota(jnp.int32, (8, cw), 0)

        def step(g, carry):
            sl = pl.ds(pl.multiple_of(g * 8, 8), 8)
            a8, b8 = a_ref[sl, :], b_ref[sl, :]
            for s in (1, 2, 4):
                a_sh = jnp.where(row >= s, pltpu.roll(a8, s, 0), 1.0)
                b_sh = jnp.where(row >= s, pltpu.roll(b8, s, 0), 0.0)
                b8 = a8 * b_sh + b8
                a8 = a8 * a_sh
            h8 = a8 * carry + b8
            o_ref[sl, :] = h8
            return jnp.broadcast_to(h8[7:8, :], (8, cw))

        lax.fori_loop(0, ng, step, jnp.zeros((8, cw), F32))

    spec = pl.BlockSpec((T, cw), lambda b, c: (b, c))
    return pl.pallas_call(
        kern, name="lru_scan", grid=(B, D // cw), in_specs=[spec, spec], out_specs=spec,
        out_shape=jax.ShapeDtypeStruct((M, D), F32), compiler_params=_cparams(("parallel", "parallel")),
    )(a, b)


def _lru_scan_bwd(a, hs, dhs, B, T):
    M, D = a.shape
    cw = _pick(D, 256)
    ng = T // 8

    def kern(a_ref, h_ref, d_ref, g_ref, da_ref):
        row = lax.broadcasted_iota(jnp.int32, (8, cw), 0)

        def step(k, carry):
            g_next, a_next = carry
            g = ng - 1 - k
            sl = pl.ds(pl.multiple_of(g * 8, 8), 8)
            a8, d8, h8 = a_ref[sl, :], d_ref[sl, :], h_ref[sl, :]
            c8 = jnp.where(row < 7, pltpu.roll(a8, 7, 0), a_next)
            for s in (1, 2, 4):
                d_sh = jnp.where(row < 8 - s, pltpu.roll(d8, 8 - s, 0), 0.0)
                c_sh = jnp.where(row < 8 - s, pltpu.roll(c8, 8 - s, 0), 1.0)
                d8 = d8 + c8 * d_sh
                c8 = c8 * c_sh
            G8 = d8 + c8 * g_next
            gp = jnp.maximum(g - 1, 0)
            hp8 = h_ref[pl.ds(pl.multiple_of(gp * 8, 8), 8), :]
            hp_last = jnp.where(g > 0, jnp.broadcast_to(hp8[7:8, :], (8, cw)), 0.0)
            hprev = jnp.where(row >= 1, pltpu.roll(h8, 1, 0), hp_last)
            g_ref[sl, :] = G8
            da_ref[sl, :] = G8 * hprev
            return jnp.broadcast_to(G8[0:1, :], (8, cw)), jnp.broadcast_to(a8[0:1, :], (8, cw))

        z = jnp.zeros((8, cw), F32)
        lax.fori_loop(0, ng, step, (z, z))

    spec = pl.BlockSpec((T, cw), lambda b, c: (b, c))
    sh = jax.ShapeDtypeStruct((M, D), F32)
    return pl.pallas_call(
        kern, name="lru_scan_bwd", grid=(B, D // cw), in_specs=[spec, spec, spec], out_specs=[spec, spec],
        out_shape=[sh, sh], compiler_params=_cparams(("parallel", "parallel")),
    )(a, hs, dhs)


def _lru_fwd(x, p, B, T):
    M, D = x.shape
    h0 = _rms_fwd(x, p["norm"], T, "lru_norm")
    u0 = _matmul(h0, p["w_in"], bias=p["b_in"], name="lru_in")
    fulls = [p["conv_w"], p["conv_b"].reshape(1, D), p["wbd"], p["gate_b"].reshape(1, 2 * D), p["lam"].reshape(1, D)]

    def body(rows, prevs, nexts, fulls, sf, sl):
        xr = _lru_conv(rows[0][:, D:], prevs[0][:, D:], _f(fulls[0]), _f(fulls[1]), sf)
        a, bt = _lru_gates(xr, _lru_pre(xr, fulls[2][...], _f(fulls[3])), _f(fulls[4]))
        return [a, bt], []

    (a, bt), _ = _tile_call(body, rows=[u0], prevs=[0], fulls=fulls, row_outs=[(D, F32), (D, F32)],
                            tm=min(256, T), T=T, name="lru_gates")
    hs = _lru_scan(a, bt, B, T)

    def body2(rows, prevs, nexts, fulls, sf, sl):
        return [rows[0][...] * _gelu(rows[1][:, :D])], []

    (out,), _ = _tile_call(body2, rows=[hs, u0], row_outs=[(D, BF16)], tm=min(512, T), T=T, name="lru_mix")
    y = _matmul(out, p["w_out"], bias=p["b_out"], residual=x, name="lru_out")
    return y, (x, h0, u0, a, hs, out)


def _lru_bwd(dy, saved, p, B, T):
    x, h0, u0, a, hs, out = saved
    M, D = x.shape
    K = p["conv_w"].shape[0]
    d_out = _matmul(dy, p["w_out"], mode="nt", name="lru_out_dx")
    d_w_out = _matmul(out, dy, mode="tn", name="lru_out_dw")

    def body(rows, prevs, nexts, fulls, sf, sl):
        do, h, u, dyv = rows[0][...], rows[1][...], rows[2][:, :D], rows[3][...]
        act, dact = jax.jvp(_gelu, (u,), (jnp.ones_like(u),))
        return [do * act, do * h * dact], [_colsum(dyv)]

    (d_hs, d_u1), (d_b_out,) = _tile_call(body, rows=[d_out, hs, u0, dy], row_outs=[(D, F32), (D, F32)],
                                          acc_outs=[(1, D)], tm=min(512, T), T=T, name="lru_mix_bwd")
    g_b, d_a = _lru_scan_bwd(a, hs, d_hs, B, T)
    fulls = [p["conv_w"], p["conv_b"].reshape(1, D), p["wbd"], p["gate_b"].reshape(1, 2 * D), p["lam"].reshape(1, D)]

    def body3(rows, prevs, nexts, fulls, sf, sl):
        u, gb_c, da_c, du1 = rows
        cw, cb, wbd, gbias, lam = _f(fulls[0]), _f(fulls[1]), fulls[2][...], _f(fulls[3]), _f(fulls[4])
        tm = u.shape[0]
        u2c, u2p, u2n = u[:, D:], prevs[0][:, D:], nexts[0][:, D:]
        xr_c = _lru_conv(u2c, u2p, cw, cb, sf)
        xr_n = _lru_conv(u2n, u2c[tm - HALO:], cw, cb, False)
        nt = (((1,), (1,)), ((), ()))
        _, vjp_c = jax.vjp(_lru_gates, xr_c, _lru_pre(xr_c, wbd, gbias), lam)
        dxr_c, dpre_c, d_lam = vjp_c((da_c[...], gb_c[...]))
        dpre_cb = dpre_c.astype(MXU_DTYPE)
        dxr_c = dxr_c + lax.dot_general(dpre_cb, wbd, nt, preferred_element_type=F32)
        d_wbd = lax.dot_general(xr_c.astype(MXU_DTYPE), dpre_cb, (((0,), (0,)), ((), ())), preferred_element_type=F32)
        d_gbias = _colsum(dpre_c)
        _, vjp_n = jax.vjp(lambda t, q: _lru_gates(t, q, lam), xr_n, _lru_pre(xr_n, wbd, gbias))
        dxr_n, dpre_n = vjp_n((nexts[2][...], nexts[1][...]))
        dxr_n = dxr_n + lax.dot_general(dpre_n.astype(MXU_DTYPE), wbd, nt, preferred_element_type=F32)
        d_u2 = dxr_c * cw[K - 1:K]
        dws = []
        for j in range(K - 1):
            s = K - 1 - j
            d_u2 = d_u2 + _shift_up(dxr_c, dxr_n, s, sl) * cw[j:j + 1]
            dws.append(_colsum(dxr_c * _shift_down(u2c, u2p, s, sf)))
        dws.append(_colsum(dxr_c * u2c))
        d_u = jnp.concatenate([du1[...], d_u2], axis=1)
        return [d_u], [_pad8(jnp.concatenate(dws, axis=0)), _colsum(dxr_c), d_wbd, d_gbias, d_lam,
                       _colsum(d_u)]

    (d_u0,), (d_cw, d_cb, d_wbd, d_gb, d_lam, d_b_in) = _tile_call(
        body3, rows=[u0, g_b, d_a, d_u1], prevs=[0], nexts=[0, 1, 2], fulls=fulls, row_outs=[(2 * D, BF16)],
        acc_outs=[(8, D), (1, D), (D, 2 * D), (1, 2 * D), (1, D), (1, 2 * D)], tm=min(256, T), T=T, name="lru_gates_bwd")
    d_h0 = _matmul(d_u0, p["w_in"], mode="nt", name="lru_in_dx")
    d_w_in = _matmul(h0, d_u0, mode="tn", name="lru_in_dw")
    dx, d_norm = _rms_bwd(x, p["norm"], d_h0, dy, T, "lru_norm_bwd")
    grads = {"norm": d_norm.reshape(D), "w_in": d_w_in, "b_in": d_b_in.reshape(2 * D), "conv_w": d_cw[:K],
             "conv_b": d_cb.reshape(D), "wbd": d_wbd, "gate_b": d_gb.reshape(2 * D), "lam": d_lam.reshape(D),
             "w_out": d_w_out, "b_out": d_b_out.reshape(D)}
    return dx, grads


def _rwkv_mix(xc, xp, norm, mix, sf):
    h = _rms(xc, norm)
    hp = _rms(xp, norm)
    xx = _shift_down(h, hp, 1, sf) - h
    return h, xx


def _rwkv_pre(k, xw, xa, xg, w0, w1, w2, a0, a1, a2, g1, g2, k_k, k_a, e, et):
    wl = -_softplus(-(w0 + _bdot(jnp.tanh(_bdot(xw, w1)), w2))) - 0.5
    decay = jnp.exp(-jnp.exp(wl))
    a = _sigmoid(a0 + _bdot(_bdot(xa, a1), a2))
    g = _bdot(_sigmoid(_bdot(xg, g1)), g2)
    kk = k * k_k
    nrm = jnp.sqrt(_head_sum(kk * kk, e, et))
    kk = kk / jnp.maximum(nrm, 1e-12)
    k2 = k * (1.0 + (a - 1.0) * k_a)
    return decay, k2, -kk, kk * a, g


def _rwkv_post(y, r, k2, v, g, ln_w, ln_b, r_k, e, et):
    inv = 1.0 / HEAD
    mu = _head_sum(y, e, et) * inv
    yc = y - mu
    var = _head_sum(yc * yc, e, et) * inv
    yn = yc * lax.rsqrt(var + GN_EPS) * ln_w + ln_b
    bonus = _head_sum(r * k2 * r_k, e, et) * v
    return (yn + bonus) * g


def _seg_lane_sums(x, lo_mask):
    s0 = jnp.sum(jnp.where(lo_mask, x, 0.0), axis=1, keepdims=True)
    s1 = jnp.sum(jnp.where(lo_mask, 0.0, x), axis=1, keepdims=True)
    return s0, s1


def _seg_lane_sum(x, lo_mask):
    s0, s1 = _seg_lane_sums(x, lo_mask)
    return jnp.where(lo_mask, s0, s1)


def _pair_consts():
    lane = lax.broadcasted_iota(jnp.int32, (HEAD, 128), 1)
    sub = lax.broadcasted_iota(jnp.int32, (HEAD, 128), 0)
    return lane < HEAD, (jnp.bitwise_and(lane, HEAD - 1) == sub).astype(F32)


def _pair_ones():
    head = jnp.arange(128) // HEAD
    blk = (head[:, None] == head[None, :]).astype(MXU_DTYPE)
    return jnp.concatenate([blk, blk], axis=0)


def _split_rows(x):
    hi = x.astype(MXU_DTYPE).astype(F32)
    return hi, x - hi


def _split_lhs(x):
    hi = x.astype(MXU_DTYPE)
    return jnp.concatenate([hi, (x - hi.astype(F32)).astype(MXU_DTYPE)], axis=1)


def _spread_lhs(diag, hi_row, mid_row):
    return jnp.concatenate([diag * hi_row, diag * mid_row], axis=1).astype(MXU_DTYPE)


def _rwkv_scan(r, w, k, v, a, b, B, T):
    M, D = r.shape
    HP, PG, TC, NC, chains = _scan_plan(B, T, D)
    NS = len(chains) * 8

    def kern(r_ref, w_ref, k_ref, v_ref, a_ref, b_ref, ones_ref, y_ref, st_ref, S_ref, lhs_ref, res_ref):
        c = pl.program_id(1)

        @pl.when(c == 0)
        def _():
            S_ref[...] = jnp.zeros_like(S_ref)

        lo, diag = _pair_consts()
        row8 = lax.broadcasted_iota(jnp.int32, (8, 128), 0)

        def blk(idx):
            return pl.ds(idx * HEAD, HEAD)

        def group(gi, _):
            sl = pl.ds(pl.multiple_of(gi * 8, 8), 8)
            tiles = [[ref[bi, sl, p * 128:(p + 1) * 128] for ref in (r_ref, w_ref, k_ref, v_ref, a_ref, b_ref)]
                     for bi, p in chains]
            for ci in range(len(chains)):
                hi, mid = _split_rows(tiles[ci][3])
                for j in range(8):
                    lhs_ref[blk(ci * 8 + j), :] = _spread_lhs(diag, hi[j:j + 1, :], mid[j:j + 1, :])
            res_ref[...] = jnp.dot(lhs_ref[...], ones_ref[...], preferred_element_type=F32)
            S = [S_ref[ci] for ci in range(len(chains))]
            for j in range(8):
                for ci, (bi, p) in enumerate(chains):
                    r8, w8, k8, v8, a8, b8 = tiles[ci]
                    idx = ci * 8 + j
                    st_ref[p, bi, gi * 8 + j] = S[ci]
                    vb = res_ref[blk(idx), :]
                    sa = _seg_lane_sum(S[ci] * a8[j:j + 1, :], lo)
                    S[ci] = S[ci] * w8[j:j + 1, :] + sa * b8[j:j + 1, :] + vb * k8[j:j + 1, :]
                    lhs_ref[blk(idx), :] = _split_lhs(S[ci] * r8[j:j + 1, :])
            for ci in range(len(chains)):
                S_ref[ci] = S[ci]
            res_ref[...] = jnp.dot(lhs_ref[...], ones_ref[...], preferred_element_type=F32)
            for ci, (bi, p) in enumerate(chains):
                y8 = jnp.zeros((8, 128), F32)
                for j in range(8):
                    y8 = jnp.where(row8 == j, _colsum(diag * res_ref[blk(ci * 8 + j), :]), y8)
                y_ref[bi, sl, p * 128:(p + 1) * 128] = y8
            return 0

        lax.fori_loop(0, TC // 8, group, 0)

    spec = pl.BlockSpec((B, TC, 128 * PG), lambda hp, c: (0, c, hp))
    st_spec = pl.BlockSpec((PG, B, TC, HEAD, 128), lambda hp, c: (hp, 0, c, 0, 0))
    y, st = pl.pallas_call(
        kern, name="rwkv_scan", grid=(HP // PG, NC),
        in_specs=[spec] * 6 + [pl.BlockSpec((256, 128), lambda hp, c: (0, 0))], out_specs=[spec, st_spec],
        out_shape=[jax.ShapeDtypeStruct((B, T, D), F32), jax.ShapeDtypeStruct((HP, B, T, HEAD, 128), F32)],
        scratch_shapes=[pltpu.VMEM((len(chains), HEAD, 128), F32), pltpu.VMEM((NS * HEAD, 256), MXU_DTYPE),
                        pltpu.VMEM((NS * HEAD, 128), F32)],
        compiler_params=_cparams(("parallel", "arbitrary")),
    )(*[x.reshape(B, T, D) for x in (r, w, k, v, a, b)], _pair_ones())
    return y.reshape(M, D), st


def _scan_plan(B, T, D):
    HP = D // 128
    PG = 2 if HP % 2 == 0 else 1
    TC = min(32, T)
    return HP, PG, TC, T // TC, [(bi, p) for bi in range(B) for p in range(PG)]


def _rwkv_scan_bwd(r, w, k, v, a, b, st, dy, B, T):
    M, D = r.shape
    HP, PG, TC, NC, chains = _scan_plan(B, T, D)
    NS = len(chains) * 8

    def kern(r_ref, w_ref, k_ref, v_ref, a_ref, b_ref, st_ref, dy_ref, ones_ref,
             dr_ref, dw_ref, dk_ref, dv_ref, da_ref, db_ref, dS_ref, lhs_ref, res_ref):
        c = pl.program_id(1)

        @pl.when(c == 0)
        def _():
            dS_ref[...] = jnp.zeros_like(dS_ref)

        lo, diag = _pair_consts()
        row8 = lax.broadcasted_iota(jnp.int32, (8, 128), 0)

        def blk(idx):
            return pl.ds(idx * HEAD, HEAD)

        def group(gk, _):
            gi = TC // 8 - 1 - gk
            sl = pl.ds(pl.multiple_of(gi * 8, 8), 8)
            tiles = [[ref[bi, sl, p * 128:(p + 1) * 128] for ref in (r_ref, w_ref, k_ref, v_ref, a_ref, b_ref, dy_ref)]
                     for bi, p in chains]
            for ci, (bi, p) in enumerate(chains):
                vh, vm = _split_rows(tiles[ci][3])
                dh, dm = _split_rows(tiles[ci][6])
                a8 = tiles[ci][4]
                for j in range(8):
                    idx = ci * 8 + j
                    lhs_ref[blk(idx), :] = _spread_lhs(diag, vh[j:j + 1, :], vm[j:j + 1, :])
                    lhs_ref[blk(NS + idx), :] = _spread_lhs(diag, dh[j:j + 1, :], dm[j:j + 1, :])
                    lhs_ref[blk(2 * NS + idx), :] = _split_lhs(st_ref[p, bi, gi * 8 + j] * a8[j:j + 1, :])
            res_ref[...] = jnp.dot(lhs_ref[...], ones_ref[...], preferred_element_type=F32)
            dS = [dS_ref[ci] for ci in range(len(chains))]
            acc = [[jnp.zeros((8, 128), F32) for _ in range(5)] for _ in chains]
            for j in range(7, -1, -1):
                for ci, (bi, p) in enumerate(chains):
                    r8, w8, k8, v8, a8, b8, dy8 = tiles[ci]
                    rj, wj, kj, aj, bj = r8[j:j + 1, :], w8[j:j + 1, :], k8[j:j + 1, :], a8[j:j + 1, :], b8[j:j + 1, :]
                    idx = ci * 8 + j
                    Sp = st_ref[p, bi, gi * 8 + j]
                    vb, dyb, sa = res_ref[blk(idx), :], res_ref[blk(NS + idx), :], res_ref[blk(2 * NS + idx), :]
                    St = Sp * wj + sa * bj + vb * kj
                    d = dS[ci] + dyb * rj
                    dsa = _seg_lane_sum(d * bj, lo)
                    lhs_ref[blk(idx), :] = _split_lhs(d * kj)
                    rows = (_colsum(St * dyb), _colsum(d * Sp), _colsum(d * vb), _colsum(Sp * dsa), _colsum(d * sa))
                    acc[ci] = [jnp.where(row8 == j, rw, a8_) for rw, a8_ in zip(rows, acc[ci], strict=True)]
                    dS[ci] = d * wj + dsa * aj
            for ci, (bi, p) in enumerate(chains):
                dS_ref[ci] = dS[ci]
                for ref, a8_ in zip((dr_ref, dw_ref, dk_ref, da_ref, db_ref), acc[ci], strict=True):
                    ref[bi, sl, p * 128:(p + 1) * 128] = a8_
            res_ref[pl.ds(0, NS * HEAD), :] = jnp.dot(lhs_ref[pl.ds(0, NS * HEAD), :], ones_ref[...],
                                                      preferred_element_type=F32)
            for ci, (bi, p) in enumerate(chains):
                dv8 = jnp.zeros((8, 128), F32)
                for j in range(8):
                    dv8 = jnp.where(row8 == j, _colsum(diag * res_ref[blk(ci * 8 + j), :]), dv8)
                dv_ref[bi, sl, p * 128:(p + 1) * 128] = dv8
            return 0

        lax.fori_loop(0, TC // 8, group, 0)

    spec = pl.BlockSpec((B, TC, 128 * PG), lambda hp, c: (0, NC - 1 - c, hp))
    st_spec = pl.BlockSpec((PG, B, TC, HEAD, 128), lambda hp, c: (hp, 0, NC - 1 - c, 0, 0))
    sh = jax.ShapeDtypeStruct((B, T, D), F32)
    outs = pl.pallas_call(
        kern, name="rwkv_scan_bwd", grid=(HP // PG, NC),
        in_specs=[spec] * 6 + [st_spec, spec, pl.BlockSpec((256, 128), lambda hp, c: (0, 0))], out_specs=[spec] * 6,
        out_shape=[sh] * 6,
        scratch_shapes=[pltpu.VMEM((len(chains), HEAD, 128), F32), pltpu.VMEM((3 * NS * HEAD, 256), MXU_DTYPE),
                        pltpu.VMEM((3 * NS * HEAD, 128), F32)],
        compiler_params=_cparams(("parallel", "arbitrary")),
    )(*[x.reshape(B, T, D) for x in (r, w, k, v, a, b)], st, dy.reshape(B, T, D), _pair_ones())
    return [o.reshape(M, D) for o in outs]


def _head_mats(D):
    ch = jnp.arange(D) // HEAD
    e = (ch[:, None] == jnp.arange(128)[None, :]).astype(F32)
    return e, e.T


def _rwkv_fwd(x, p, B, T):
    M, D = x.shape
    e, et = _head_mats(D)
    norm = p["norm"].reshape(1, D)

    def body(rows, prevs, nexts, fulls, sf, sl):
        h, xx = _rwkv_mix(rows[0][...], prevs[0][...], _f(fulls[0]), None, sf)
        mix = _f(fulls[1])
        return [h + xx * mix[i:i + 1] for i in range(6)], []

    xs, _ = _tile_call(body, rows=[x], prevs=[0], fulls=[norm, _pad8(p["mix"])], row_outs=[(D, BF16)] * 6,
                       tm=min(256, T), T=T, name="rwkv_mix")
    r = _matmul(xs[0], p["w_rkv"][0], name="rwkv_r")
    k = _matmul(xs[1], p["w_rkv"][1], name="rwkv_k")
    v = _matmul(xs[2], p["w_rkv"][2], name="rwkv_v")
    pre_fulls = [p["w0"].reshape(1, D), p["w1"], p["w2"], p["a0"].reshape(1, D), p["a1"], p["a2"], p["g1"], p["g2"],
                 p["k_k"].reshape(1, D), p["k_a"].reshape(1, D), e, et]

    def body2(rows, prevs, nexts, fulls, sf, sl):
        outs = _rwkv_pre(rows[0][...], _f(rows[1]), _f(rows[2]), _f(rows[3]), *[f[...] for f in fulls])
        return list(outs), []

    (decay, k2, kkn, bb, g), _ = _tile_call(body2, rows=[k, xs[3], xs[4], xs[5]], fulls=pre_fulls,
                                            row_outs=[(D, F32)] * 5, tm=min(256, T), T=T, name="rwkv_pre")
    y, st = _rwkv_scan(r, decay, k2, v, kkn, bb, B, T)
    post_fulls = [p["ln_w"].reshape(1, D), p["ln_b"].reshape(1, D), p["r_k"].reshape(1, D), e, et]

    def body3(rows, prevs, nexts, fulls, sf, sl):
        return [_rwkv_post(*[rr[...] for rr in rows], *[f[...] for f in fulls])], []

    (z,), _ = _tile_call(body3, rows=[y, r, k2, v, g], fulls=post_fulls, row_outs=[(D, BF16)], tm=min(256, T), T=T,
                         name="rwkv_post")
    out = _matmul(z, p["w_out"], residual=x, name="rwkv_out")
    return out, (x, xs, r, k, v, decay, k2, kkn, bb, g, y, st, z)


def _rwkv_bwd(dout, saved, p, B, T):
    x, xs, r, k, v, decay, k2, kkn, bb, g, y, st, z = saved
    M, D = x.shape
    e, et = _head_mats(D)
    d_z = _matmul(dout, p["w_out"], mode="nt", name="rwkv_out_dx")
    d_w_out = _matmul(z, dout, mode="tn", name="rwkv_out_dw")
    post_fulls = [p["ln_w"].reshape(1, D), p["ln_b"].reshape(1, D), p["r_k"].reshape(1, D), e, et]

    def body(rows, prevs, nexts, fulls, sf, sl):
        prim = [rr[...] for rr in rows[:5]] + [f[...] for f in fulls]
        _, vjp = jax.vjp(_rwkv_post, *prim)
        ct = vjp(rows[5][...])
        return list(ct[:5]), list(ct[5:8])

    (d_y, d_r1, d_k21, d_v1, d_g), (d_ln_w, d_ln_b, d_r_k) = _tile_call(
        body, rows=[y, r, k2, v, g, d_z], fulls=post_fulls, row_outs=[(D, F32)] * 5, acc_outs=[(1, D)] * 3,
        tm=min(256, T), T=T, name="rwkv_post_bwd")
    d_r2, d_w, d_k22, d_v2, d_kkn, d_bb = _rwkv_scan_bwd(r, decay, k2, v, kkn, bb, st, d_y, B, T)
    pre_fulls = [p["w0"].reshape(1, D), p["w1"], p["w2"], p["a0"].reshape(1, D), p["a1"], p["a2"], p["g1"], p["g2"],
                 p["k_k"].reshape(1, D), p["k_a"].reshape(1, D), e, et]

    def body2(rows, prevs, nexts, fulls, sf, sl):
        prim = [rows[0][...], _f(rows[1]), _f(rows[2]), _f(rows[3])] + [f[...] for f in fulls]
        _, vjp = jax.vjp(_rwkv_pre, *prim)
        ct = vjp((rows[4][...], rows[5][...] + rows[6][...], rows[7][...], rows[8][...], rows[9][...]))
        d_r = rows[10][...] + rows[11][...]
        d_v = rows[12][...] + rows[13][...]
        return [ct[0], ct[1], ct[2], ct[3], d_r, d_v], [c.astype(F32) for c in ct[4:14]]

    acc_shapes = [f.shape for f in pre_fulls[:10]]
    (d_k, d_xw, d_xa, d_xg, d_r, d_v), pgr = _tile_call(
        body2, rows=[k, xs[3], xs[4], xs[5], d_w, d_k21, d_k22, d_kkn, d_bb, d_g, d_r1, d_r2, d_v1, d_v2],
        fulls=pre_fulls, row_outs=[(D, BF16), (D, F32), (D, F32), (D, F32), (D, BF16), (D, BF16)], acc_outs=acc_shapes,
        tm=min(256, T), T=T, name="rwkv_pre_bwd")
    d_xr = _matmul(d_r, p["w_rkv"][0], mode="nt", name="rwkv_r_dx")
    d_xk = _matmul(d_k, p["w_rkv"][1], mode="nt", name="rwkv_k_dx")
    d_xv = _matmul(d_v, p["w_rkv"][2], mode="nt", name="rwkv_v_dx")
    d_wr = _matmul(xs[0], d_r, mode="tn", name="rwkv_r_dw")
    d_wk = _matmul(xs[1], d_k, mode="tn", name="rwkv_k_dw")
    d_wv = _matmul(xs[2], d_v, mode="tn", name="rwkv_v_dw")
    norm = p["norm"].reshape(1, D)

    def body3(rows, prevs, nexts, fulls, sf, sl):
        xc, xp = rows[0][...], prevs[0][...]
        nrm, mix = _f(fulls[0]), _f(fulls[1])
        h, xx = _rwkv_mix(xc, xp, nrm, None, sf)
        dxs = [rows[1 + i][...] for i in range(6)]
        dxs_n = [nexts[i][...] for i in range(6)]
        d_h = jnp.zeros_like(h)
        d_sh = jnp.zeros_like(h)
        d_sh_n = jnp.zeros_like(dxs_n[0])
        dmix = []
        for i in range(6):
            m = mix[i:i + 1]
            d_h = d_h + dxs[i] * (1.0 - m)
            d_sh = d_sh + dxs[i] * m
            d_sh_n = d_sh_n + dxs_n[i] * m
            dmix.append(_colsum(dxs[i] * xx))
        d_h = d_h + _shift_up(d_sh, d_sh_n, 1, sl)
        _, vjp = jax.vjp(_rms, xc, nrm)
        dx, dn = vjp(d_h)
        return [dx + rows[7][...]], [dn, _pad8(jnp.concatenate(dmix, axis=0))]

    (dx,), (d_norm, d_mix) = _tile_call(
        body3, rows=[x, d_xr, d_xk, d_xv, d_xw, d_xa, d_xg, dout], prevs=[0], nexts=[1, 2, 3, 4, 5, 6],
        fulls=[norm, _pad8(p["mix"])], row_outs=[(D, F32)], acc_outs=[(1, D), (8, D)], tm=min(256, T), T=T,
        name="rwkv_mix_bwd")
    names = ["w0", "w1", "w2", "a0", "a1", "a2", "g1", "g2", "k_k", "k_a"]
    grads = {n: gr.reshape(p[n].shape) for n, gr in zip(names, pgr, strict=True)}
    grads.update({"norm": d_norm.reshape(D), "mix": d_mix[:6], "w_rkv": jnp.stack([d_wr, d_wk, d_wv]),
                  "r_k": d_r_k.reshape(p["r_k"].shape), "ln_w": d_ln_w.reshape(D), "ln_b": d_ln_b.reshape(D),
                  "w_out": d_w_out})
    return dx, grads


def _loss_head(x, g, tgt, T):
    M, D = x.shape

    def body(rows, prevs, nexts, fulls, sf, sl):
        xv, gv = rows[0][...], _f(fulls[0])
        yv, vjp = jax.vjp(_rms, xv, gv)
        err = yv - rows[1][...]
        dx, dg = vjp(err * (1.0 / D))
        part = jnp.sum(_colsum(err * err), axis=1, keepdims=True) * (0.5 / D)
        return [dx], [dg, jnp.broadcast_to(part, (1, 128))]

    (dx,), (dg, loss) = _tile_call(body, rows=[x, tgt], fulls=[g.reshape(1, D)], row_outs=[(D, F32)],
                                   acc_outs=[(1, D), (1, 128)], tm=min(512, T), T=T, name="loss_head")
    return loss[0, 0], dx, dg.reshape(D)


def _local_step(x3, tgt3, P):
    B, T, D = x3.shape
    x, tgt = x3.reshape(B * T, D), tgt3.reshape(B * T, D)
    x1, s_lru = _lru_fwd(x, P["lru"], B, T)
    x2, s_f0 = _ffn_fwd(x1, P["ffn0"], T, "0")
    x3_, s_rw = _rwkv_fwd(x2, P["rwkv"], B, T)
    x4, s_f1 = _ffn_fwd(x3_, P["ffn1"], T, "1")
    loss, d4, d_fn = _loss_head(x4, P["final_norm"], tgt, T)
    d3, g_f1 = _ffn_bwd(d4, s_f1, P["ffn1"], T, "1")
    d2, g_rw = _rwkv_bwd(d3, s_rw, P["rwkv"], B, T)
    d1, g_f0 = _ffn_bwd(d2, s_f0, P["ffn0"], T, "0")
    d0, g_lru = _lru_bwd(d1, s_lru, P["lru"], B, T)
    return loss, d0.reshape(B, T, D), {"lru": g_lru, "ffn0": g_f0, "rwkv": g_rw, "ffn1": g_f1, "final_norm": d_fn}


WEIGHTS = ['lru_norm', 'lru_w_in', 'lru_b_in', 'lru_conv_w', 'lru_conv_b', 'lru_gate_w', 'lru_gate_b', 'lru_lambda',
           'lru_w_out', 'lru_b_out', 'rwkv_norm', 'rwkv_mix', 'rwkv_w_rkv', 'rwkv_w0', 'rwkv_w1', 'rwkv_w2', 'rwkv_a0',
           'rwkv_a1', 'rwkv_a2', 'rwkv_g1', 'rwkv_g2', 'rwkv_k_k', 'rwkv_k_a', 'rwkv_r_k', 'rwkv_ln_w', 'rwkv_ln_b',
           'rwkv_w_out', 'ffn_norm', 'ffn_w_up', 'ffn_conv_w', 'ffn_conv_b', 'ffn_w_down', 'final_norm']
SHARD_AXIS = {'lru_w_in': 2, 'lru_conv_w': 2, 'lru_w_out': 1, 'rwkv_norm': 1, 'rwkv_mix': 2, 'rwkv_w_rkv': 2,
              'rwkv_w0': 1, 'rwkv_w1': 1, 'rwkv_w2': 2, 'rwkv_a0': 1, 'rwkv_a1': 1, 'rwkv_a2': 2, 'rwkv_g1': 1,
              'rwkv_g2': 2, 'rwkv_k_k': 1, 'rwkv_k_a': 1, 'rwkv_ln_w': 1, 'rwkv_ln_b': 1, 'rwkv_w_out': 1,
              'ffn_w_up': 2, 'ffn_conv_w': 2, 'ffn_w_down': 1}
MXU_WEIGHTS = ('lru_w_in', 'lru_w_out', 'rwkv_w_rkv', 'rwkv_w_out', 'ffn_w_up', 'ffn_w_down')
N_CHIPS = 4
LANES = 1024


def _pack(arrs, dtype, row_mult):
    flat = jnp.concatenate([a.reshape(-1).astype(dtype) for a in arrs])
    n = flat.shape[0]
    unit = row_mult * LANES
    tot = -(-n // unit) * unit
    if tot > n:
        flat = jnp.concatenate([flat, jnp.zeros((tot - n,), dtype)])
    return flat.reshape(tot // LANES, LANES)


def _unpack(buf, shapes):
    flat = buf.reshape(-1)
    out, off = [], 0
    for s in shapes:
        n = 1
        for d in s:
            n *= d
        out.append(flat[off:off + n].reshape(s))
        off += n
    return out


def _unshard(stacked, axis):
    return jnp.concatenate([stacked[q] for q in range(N_CHIPS)], axis=axis)


def _to_shards(full, axis):
    return jnp.stack(jnp.split(full, N_CHIPS, axis=axis))


MESH_ID = pl.DeviceIdType.MESH


def _chip_exchange(buf, same_to_all, name):
    shape = buf.shape if same_to_all else buf.shape[1:]

    def body(in_ref, out_ref, send_sems, recv_sems, local_sem):
        x, y, c = lax.axis_index("x"), lax.axis_index("y"), lax.axis_index("c")
        p = 2 * x + y
        chips = [(1 - x, y), (x, 1 - y), (1 - x, 1 - y)]

        def src(q):
            return in_ref if same_to_all else in_ref.at[q]

        mine = pltpu.make_async_copy(src(p), out_ref.at[p], local_sem)
        mine.start()
        copies = []
        for j, (qx, qy) in enumerate(chips):
            q = 2 * qx + qy
            copies.append((pltpu.make_async_remote_copy(
                src_ref=src(q), dst_ref=out_ref.at[p], send_sem=send_sems.at[j], recv_sem=recv_sems.at[j],
                device_id=(qx, qy, c), device_id_type=MESH_ID),
                pltpu.make_async_remote_copy(
                src_ref=src(q), dst_ref=out_ref.at[q], send_sem=send_sems.at[j], recv_sem=recv_sems.at[j],
                device_id=(qx, qy, c), device_id_type=MESH_ID)))
        for snd, _ in copies:
            snd.start()
        for _, rcv in copies:
            rcv.wait_recv()
        for snd, _ in copies:
            snd.wait_send()
        mine.wait()

    return pl.pallas_call(
        body, name=name,
        out_shape=jax.ShapeDtypeStruct((N_CHIPS,) + tuple(shape), buf.dtype),
        in_specs=[pl.BlockSpec(memory_space=pl.ANY)], out_specs=pl.BlockSpec(memory_space=pl.ANY),
        scratch_shapes=[pltpu.SemaphoreType.DMA((3,)), pltpu.SemaphoreType.DMA((3,)), pltpu.SemaphoreType.DMA],
        compiler_params=pltpu.CompilerParams(has_side_effects=True),
    )(buf)


def _sibling_swap(buf, name):
    def body(in_ref, out_ref, send_sem, recv_sem):
        x, y, c = lax.axis_index("x"), lax.axis_index("y"), lax.axis_index("c")
        cp = pltpu.make_async_remote_copy(src_ref=in_ref, dst_ref=out_ref, send_sem=send_sem, recv_sem=recv_sem,
                                          device_id=(x, y, 1 - c), device_id_type=MESH_ID)
        cp.start()
        cp.wait()

    return pl.pallas_call(
        body, name=name, out_shape=jax.ShapeDtypeStruct(buf.shape, buf.dtype),
        in_specs=[pl.BlockSpec(memory_space=pl.ANY)], out_specs=pl.BlockSpec(memory_space=pl.ANY),
        scratch_shapes=[pltpu.SemaphoreType.DMA, pltpu.SemaphoreType.DMA],
        compiler_params=pltpu.CompilerParams(has_side_effects=True),
    )(buf)


def _by_half(mine, other):
    c = lax.axis_index("c")
    st = jnp.stack([mine, other])
    return lax.dynamic_index_in_dim(st, c, 0, keepdims=False), lax.dynamic_index_in_dim(st, 1 - c, 0, keepdims=False)


def _sum_rows_call(parts, name):
    R = parts[0].shape[0]
    tr = _pick_rows(R)
    n = len(parts)

    def kern(*refs):
        acc = refs[0][...]
        for r in refs[1:n]:
            acc = acc + r[...]
        refs[n][...] = acc

    spec = pl.BlockSpec((tr, LANES), lambda i: (i, 0))
    return pl.pallas_call(kern, name=name, grid=(R // tr,), in_specs=[spec] * n, out_specs=spec,
                          out_shape=jax.ShapeDtypeStruct((R, LANES), F32), compiler_params=_cparams(("parallel",)))(*parts)


def _pick_rows(R):
    for t in (512, 256, 128, 64, 32, 16, 8):
        if R % t == 0:
            return t
    return R


def _adamw_call(w, g, m, v, name):
    R = w.shape[0]
    tr = _pick_rows(R)
    c1 = 1.0 / (1.0 - ADAM_B1 ** ADAM_STEP)
    c2 = 1.0 / (1.0 - ADAM_B2 ** ADAM_STEP)

    def kern(w_ref, g_ref, m_ref, v_ref, d_ref, nm_ref, nv_ref):
        gv = g_ref[...]
        nm = ADAM_B1 * m_ref[...] + (1.0 - ADAM_B1) * gv
        nv = ADAM_B2 * v_ref[...] + (1.0 - ADAM_B2) * (gv * gv)
        d_ref[...] = -ADAM_LR * ((nm * c1) / (jnp.sqrt(nv * c2) + ADAM_EPS) + ADAM_WD * w_ref[...])
        nm_ref[...] = nm
        nv_ref[...] = nv

    spec = pl.BlockSpec((tr, LANES), lambda i: (i, 0))
    sh = jax.ShapeDtypeStruct((R, LANES), F32)
    return pl.pallas_call(kern, name=name, grid=(R // tr,), in_specs=[spec] * 4, out_specs=[spec] * 3,
                          out_shape=[sh] * 3, compiler_params=_cparams(("parallel",)))(w, g, m, v)


def _gate_dense(gate_w):
    _, nb, bw, _ = gate_w.shape
    eye = jnp.eye(nb, dtype=gate_w.dtype)
    dense = jnp.einsum('gncd,nm->gncmd', gate_w, eye).reshape(2, nb * bw, nb * bw)
    return jnp.concatenate([dense[0], dense[1]], axis=1)


def _gate_blocks(d_dense, nb):
    D = d_dense.shape[0]
    bw = D // nb
    g = d_dense.reshape(nb, bw, 2, nb, bw)
    return jnp.einsum('ncgnd->gncd', g)


def _step(W, M1, V1, x, tgt):
    sharded = [n for n in WEIGHTS if n in SHARD_AXIS]
    repl = [n for n in WEIGHTS if n not in SHARD_AXIS]
    big = [n for n in sharded if n in MXU_WEIGHTS]
    small = [n for n in sharded if n not in MXU_WEIGHTS]

    def gather(names, dtype, tag):
        buf = _pack([W[n] for n in names], dtype, 32)
        rh = buf.shape[0] // 2
        c = lax.axis_index("c")
        my_half = lax.dynamic_slice_in_dim(buf, c * rh, rh, 0)
        got = _chip_exchange(my_half, True, f"gather_{tag}")
        other = _sibling_swap(got, f"gather_{tag}_swap")
        h0, h1 = _by_half(got, other)
        full = jnp.concatenate([h0, h1], axis=1)
        per_chip = [_unpack(full[q], [W[n].shape for n in names]) for q in range(N_CHIPS)]
        return {n: jnp.concatenate([per_chip[q][i] for q in range(N_CHIPS)], axis=SHARD_AXIS[n])
                for i, n in enumerate(names)}

    full = {**gather(big, MXU_DTYPE, "mats"), **gather(small, F32, "vecs")}
    for n in repl:
        full[n] = W[n]

    P = {
        "lru": {"norm": full["lru_norm"][0], "w_in": full["lru_w_in"][0], "b_in": full["lru_b_in"][0],
                "conv_w": full["lru_conv_w"][0], "conv_b": full["lru_conv_b"][0],
                "wbd": _gate_dense(full["lru_gate_w"][0]).astype(MXU_DTYPE), "gate_b": full["lru_gate_b"][0].reshape(-1),
                "lam": full["lru_lambda"][0], "w_out": full["lru_w_out"][0], "b_out": full["lru_b_out"][0]},
        "rwkv": {"norm": full["rwkv_norm"][0], "mix": full["rwkv_mix"][0], "w_rkv": full["rwkv_w_rkv"][0],
                 "w0": full["rwkv_w0"][0], "w1": full["rwkv_w1"][0], "w2": full["rwkv_w2"][0], "a0": full["rwkv_a0"][0],
                 "a1": full["rwkv_a1"][0], "a2": full["rwkv_a2"][0], "g1": full["rwkv_g1"][0], "g2": full["rwkv_g2"][0],
                 "k_k": full["rwkv_k_k"][0], "k_a": full["rwkv_k_a"][0], "r_k": full["rwkv_r_k"][0],
                 "ln_w": full["rwkv_ln_w"][0], "ln_b": full["rwkv_ln_b"][0], "w_out": full["rwkv_w_out"][0]},
        "final_norm": full["final_norm"],
    }
    for l in range(2):
        P[f"ffn{l}"] = {"norm": full["ffn_norm"][l], "w_up": full["ffn_w_up"][l], "conv_w": full["ffn_conv_w"][l],
                        "conv_b": full["ffn_conv_b"][l], "w_down": full["ffn_w_down"][l]}

    loss, gx, G = _local_step(x, tgt, P)

    nb = W["lru_gate_w"].shape[2]
    gl, gr = G["lru"], G["rwkv"]
    gfull = {
        "lru_norm": gl["norm"][None], "lru_w_in": gl["w_in"][None], "lru_b_in": gl["b_in"][None],
        "lru_conv_w": gl["conv_w"][None], "lru_conv_b": gl["conv_b"][None], "lru_gate_w": _gate_blocks(gl["wbd"], nb)[None],
        "lru_gate_b": gl["gate_b"].reshape(W["lru_gate_b"].shape), "lru_lambda": gl["lam"][None],
        "lru_w_out": gl["w_out"][None], "lru_b_out": gl["b_out"][None],
        "rwkv_norm": gr["norm"][None], "rwkv_mix": gr["mix"][None], "rwkv_w_rkv": gr["w_rkv"][None],
        "rwkv_w0": gr["w0"][None], "rwkv_w1": gr["w1"][None], "rwkv_w2": gr["w2"][None], "rwkv_a0": gr["a0"][None],
        "rwkv_a1": gr["a1"][None], "rwkv_a2": gr["a2"][None], "rwkv_g1": gr["g1"][None], "rwkv_g2": gr["g2"][None],
        "rwkv_k_k": gr["k_k"][None], "rwkv_k_a": gr["k_a"][None], "rwkv_r_k": gr["r_k"][None],
        "rwkv_ln_w": gr["ln_w"][None], "rwkv_ln_b": gr["ln_b"][None], "rwkv_w_out": gr["w_out"][None],
        "final_norm": G["final_norm"],
    }
    for k in ("norm", "w_up", "conv_w", "conv_b", "w_down"):
        gfull["ffn_" + k] = jnp.stack([G["ffn0"][k], G["ffn1"][k]])

    c = lax.axis_index("c")
    shard_shapes = [W[n].shape for n in sharded]
    gs = jnp.stack([_pack([_to_shards(gfull[n], SHARD_AXIS[n])[q] for n in sharded], F32, 32) for q in range(N_CHIPS)])
    rh = gs.shape[1] // 2
    grep = _pack([gfull[n] for n in repl], F32, 16)
    rr = grep.shape[0]
    halves = gs.reshape(N_CHIPS, 2, rh, LANES)
    keep = lax.dynamic_index_in_dim(halves, c, 1, keepdims=False).reshape(N_CHIPS * rh, LANES)
    give = lax.dynamic_index_in_dim(halves, 1 - c, 1, keepdims=False).reshape(N_CHIPS * rh, LANES)
    got = _sibling_swap(jnp.concatenate([give, grep]), "reduce_pair_swap")
    pair = _sum_rows_call([jnp.concatenate([keep, grep]), got], "reduce_pair_sum")
    slots = jnp.concatenate([pair[:N_CHIPS * rh].reshape(N_CHIPS, rh, LANES),
                             jnp.broadcast_to(pair[N_CHIPS * rh:], (N_CHIPS, rr, LANES))], axis=1)
    arrived = _chip_exchange(slots, False, "reduce_chips")
    tot = _sum_rows_call([arrived[q] for q in range(N_CHIPS)], "reduce_chip_sum")
    my_half = tot[:rh]
    other_half = _sibling_swap(my_half, "reduce_half_swap")
    h0, h1 = _by_half(my_half, other_half)
    g_sh = jnp.concatenate([h0, h1])
    g_rp = tot[rh:]

    outs = {}
    for names, gbuf, tag in ((sharded, g_sh, "sharded"), (repl, g_rp, "repl")):
        shapes = [W[n].shape for n in names]
        row_mult = gbuf.shape[0]
        wb, mb, vb = (_pack([S[n] for n in names], F32, 1) for S in (W, M1, V1))
        pad = gbuf.shape[0] - wb.shape[0]
        if pad:
            wb, mb, vb = (jnp.concatenate([b, jnp.ones((pad, LANES), F32)]) for b in (wb, mb, vb))
        d, nm, nv = _adamw_call(wb, gbuf, mb, vb, f"adamw_{tag}")
        for kind, buf in (("grad", gbuf), ("delta", d), ("new_m", nm), ("new_v", nv)):
            for n, a in zip(names, _unpack(buf, shapes), strict=True):
                outs[(kind, n)] = a
    loss = lax.psum(loss, ("x", "y", "c"))
    return (loss, gx, *[outs[(kind, n)] for kind in ("grad", "delta", "new_m", "new_v") for n in WEIGHTS])


def kernel(x, lru_norm, lru_w_in, lru_b_in, lru_conv_w, lru_conv_b, lru_gate_w, lru_gate_b, lru_lambda, lru_w_out, lru_b_out, rwkv_norm, rwkv_mix, rwkv_w_rkv, rwkv_w0, rwkv_w1, rwkv_w2, rwkv_a0, rwkv_a1, rwkv_a2, rwkv_g1, rwkv_g2, rwkv_k_k, rwkv_k_a, rwkv_r_k, rwkv_ln_w, rwkv_ln_b, rwkv_w_out, ffn_norm, ffn_w_up, ffn_conv_w, ffn_conv_b, ffn_w_down, final_norm, loss_target, m_lru_norm, m_lru_w_in, m_lru_b_in, m_lru_conv_w, m_lru_conv_b, m_lru_gate_w, m_lru_gate_b, m_lru_lambda, m_lru_w_out, m_lru_b_out, m_rwkv_norm, m_rwkv_mix, m_rwkv_w_rkv, m_rwkv_w0, m_rwkv_w1, m_rwkv_w2, m_rwkv_a0, m_rwkv_a1, m_rwkv_a2, m_rwkv_g1, m_rwkv_g2, m_rwkv_k_k, m_rwkv_k_a, m_rwkv_r_k, m_rwkv_ln_w, m_rwkv_ln_b, m_rwkv_w_out, m_ffn_norm, m_ffn_w_up, m_ffn_conv_w, m_ffn_conv_b, m_ffn_w_down, m_final_norm, v_lru_norm, v_lru_w_in, v_lru_b_in, v_lru_conv_w, v_lru_conv_b, v_lru_gate_w, v_lru_gate_b, v_lru_lambda, v_lru_w_out, v_lru_b_out, v_rwkv_norm, v_rwkv_mix, v_rwkv_w_rkv, v_rwkv_w0, v_rwkv_w1, v_rwkv_w2, v_rwkv_a0, v_rwkv_a1, v_rwkv_a2, v_rwkv_g1, v_rwkv_g2, v_rwkv_k_k, v_rwkv_k_a, v_rwkv_r_k, v_rwkv_ln_w, v_rwkv_ln_b, v_rwkv_w_out, v_ffn_norm, v_ffn_w_up, v_ffn_conv_w, v_ffn_conv_b, v_ffn_w_down, v_final_norm):
    given = dict(locals())
    W = {n: given[n] for n in WEIGHTS}
    M1 = {n: given["m_" + n] for n in WEIGHTS}
    V1 = {n: given["v_" + n] for n in WEIGHTS}
    return _step(W, M1, V1, x, loss_target)
```

```python
import functools

import jax
import jax.numpy as jnp
from jax import lax
from jax.experimental import pallas as pl
from jax.experimental.pallas import tpu as pltpu

F32 = jnp.float32
BF16 = jnp.bfloat16
MXU_DTYPE = BF16

HEAD = 64
LRU_C = 8.0
GN_EPS = 64e-5
RMS_EPS = 1e-6
HALO = 16
VMEM_LIMIT = 56 * 1024 * 1024

ADAM_LR, ADAM_B1, ADAM_B2, ADAM_EPS, ADAM_WD, ADAM_STEP = 0.001, 0.9, 0.999, 1e-08, 0.01, 10


def _cparams(sem):
    return pltpu.CompilerParams(dimension_semantics=sem, vmem_limit_bytes=VMEM_LIMIT)


def _pick(n, want):
    if n <= want:
        return n
    t = want
    while t >= 128:
        if n % t == 0:
            return t
        t -= 128
    return n


def _matmul(a, b, mode="nn", bias=None, residual=None, out_dtype=F32, name="mm", tm=512, tn=512, tk=1024):
    if mode == "nn":
        (M, K), (K2, N) = a.shape, b.shape
    elif mode == "nt":
        (M, K), (N, K2) = a.shape, b.shape
    else:
        (K, M), (K2, N) = a.shape, b.shape
    assert K == K2, (a.shape, b.shape, mode)
    tm, tn, tk = _pick(M, tm), _pick(N, tn), _pick(K, tk)
    nk = K // tk
    dims = {"nn": (((1,), (0,)), ((), ())), "nt": (((1,), (1,)), ((), ())), "tn": (((0,), (0,)), ((), ()))}[mode]
    a_spec = {"nn": pl.BlockSpec((tm, tk), lambda i, j, k: (i, k)),
              "nt": pl.BlockSpec((tm, tk), lambda i, j, k: (i, k)),
              "tn": pl.BlockSpec((tk, tm), lambda i, j, k: (k, i))}[mode]
    b_spec = {"nn": pl.BlockSpec((tk, tn), lambda i, j, k: (k, j)),
              "nt": pl.BlockSpec((tn, tk), lambda i, j, k: (j, k)),
              "tn": pl.BlockSpec((tk, tn), lambda i, j, k: (k, j))}[mode]
    in_specs, operands = [a_spec, b_spec], [a, b]
    if bias is not None:
        in_specs.append(pl.BlockSpec((1, tn), lambda i, j, k: (0, j)))
        operands.append(bias.reshape(1, N))
    if residual is not None:
        in_specs.append(pl.BlockSpec((tm, tn), lambda i, j, k: (i, j)))
        operands.append(residual)
    has_bias, has_res = bias is not None, residual is not None

    def kern(*refs):
        a_ref, b_ref = refs[0], refs[1]
        o_ref, acc_ref = refs[-2], refs[-1]
        k = pl.program_id(2)

        @pl.when(k == 0)
        def _():
            acc_ref[...] = jnp.zeros_like(acc_ref)

        acc_ref[...] += lax.dot_general(a_ref[...].astype(MXU_DTYPE), b_ref[...].astype(MXU_DTYPE), dims,
                                        preferred_element_type=F32)

        @pl.when(k == nk - 1)
        def _():
            r = acc_ref[...]
            pos = 2
            if has_bias:
                r = r + refs[pos][...].astype(F32)
                pos += 1
            if has_res:
                r = r + refs[pos][...].astype(F32)
            o_ref[...] = r.astype(o_ref.dtype)

    return pl.pallas_call(
        kern, name=name,
        grid=(M // tm, N // tn, nk),
        in_specs=in_specs,
        out_specs=pl.BlockSpec((tm, tn), lambda i, j, k: (i, j)),
        out_shape=jax.ShapeDtypeStruct((M, N), out_dtype),
        scratch_shapes=[pltpu.VMEM((tm, tn), F32)],
        compiler_params=_cparams(("parallel", "parallel", "arbitrary")),
    )(*operands)


def _tile_call(body, *, rows, prevs=(), nexts=(), fulls=(), row_outs=(), acc_outs=(), tm, T, name):
    M = rows[0].shape[0]
    n_tiles, tps, hb = M // tm, T // tm, tm // HALO
    n_halo_blocks = M // HALO
    nr, npv, nnx, nf, nro, nac = len(rows), len(prevs), len(nexts), len(fulls), len(row_outs), len(acc_outs)

    def kern(*refs):
        i = pl.program_id(0)
        row_refs = refs[:nr]
        prev_refs = refs[nr:nr + npv]
        next_refs = refs[nr + npv:nr + npv + nnx]
        full_refs = refs[nr + npv + nnx:nr + npv + nnx + nf]
        out_refs = refs[nr + npv + nnx + nf:nr + npv + nnx + nf + nro]
        acc_refs = refs[nr + npv + nnx + nf + nro:]
        seq_first = (i % tps) == 0
        seq_last = (i % tps) == (tps - 1)
        outs, accs = body(row_refs, prev_refs, next_refs, full_refs, seq_first, seq_last)
        for r, o in zip(out_refs, outs, strict=True):
            r[...] = o.astype(r.dtype)
        if nac:
            @pl.when(i == 0)
            def _():
                for r in acc_refs:
                    r[...] = jnp.zeros_like(r)
            for r, a in zip(acc_refs, accs, strict=True):
                r[...] += a.astype(F32)

    in_specs = [pl.BlockSpec((tm, a.shape[1]), lambda i: (i, 0)) for a in rows]
    in_specs += [pl.BlockSpec((HALO, rows[k].shape[1]), lambda i: (jnp.maximum(i * hb - 1, 0), 0)) for k in prevs]
    in_specs += [pl.BlockSpec((HALO, rows[k].shape[1]), lambda i: (jnp.minimum((i + 1) * hb, n_halo_blocks - 1), 0))
                 for k in nexts]
    in_specs += [pl.BlockSpec(f.shape, lambda i: (0, 0)) for f in fulls]
    out_specs = [pl.BlockSpec((tm, w), lambda i: (i, 0)) for (w, _) in row_outs]
    out_specs += [pl.BlockSpec(s, lambda i: (0, 0)) for s in acc_outs]
    out_shape = [jax.ShapeDtypeStruct((M, w), dt) for (w, dt) in row_outs]
    out_shape += [jax.ShapeDtypeStruct(s, F32) for s in acc_outs]
    operands = list(rows) + [rows[k] for k in prevs] + [rows[k] for k in nexts] + list(fulls)
    res = pl.pallas_call(
        kern, name=name, grid=(n_tiles,), in_specs=in_specs, out_specs=out_specs, out_shape=out_shape,
        compiler_params=_cparams(("arbitrary",)),
    )(*operands)
    return res[:nro], res[nro:]


def _f(ref):
    return ref[...].astype(F32)


def _sigmoid(x):
    return 1.0 / (1.0 + jnp.exp(-x))


def _softplus(x):
    return jnp.maximum(x, 0.0) + jnp.log(1.0 + jnp.exp(-jnp.abs(x)))


def _neg_expm1(x):
    series = -x * (1.0 + x * (0.5 + x * (1.0 / 6.0) * (1.0 + 0.25 * x)))
    return jnp.where(x > -0.01, series, 1.0 - jnp.exp(x))


def _gelu(x):
    return 0.5 * x * (1.0 + jnp.tanh(0.7978845608028654 * (x + 0.044715 * x * x * x)))


def _rms(x, g):
    return x * lax.rsqrt(jnp.mean(x * x, axis=-1, keepdims=True) + RMS_EPS) * g


@jax.custom_vjp
def _bdot(x, w):
    return jnp.dot(x.astype(MXU_DTYPE), w.astype(MXU_DTYPE), preferred_element_type=F32)


def _bdot_fwd(x, w):
    return _bdot(x, w), (x, w)


def _bdot_bwd(res, ct):
    x, w = res
    ctb = ct.astype(MXU_DTYPE)
    dx = lax.dot_general(ctb, w.astype(MXU_DTYPE), (((1,), (1,)), ((), ())), preferred_element_type=F32)
    dw = lax.dot_general(x.astype(MXU_DTYPE), ctb, (((0,), (0,)), ((), ())), preferred_element_type=F32)
    return dx.astype(x.dtype), dw.astype(w.dtype)


_bdot.defvjp(_bdot_fwd, _bdot_bwd)


@jax.custom_vjp
def _head_sum(x, e, et):
    s = jnp.dot(x, e, precision=lax.Precision.HIGHEST, preferred_element_type=F32)
    return jnp.dot(s, et, precision=lax.Precision.HIGHEST, preferred_element_type=F32)


def _head_sum_fwd(x, e, et):
    return _head_sum(x, e, et), (e, et)


def _head_sum_bwd(res, ct):
    e, et = res
    return _head_sum(ct, e, et), jnp.zeros_like(e), jnp.zeros_like(et)


_head_sum.defvjp(_head_sum_fwd, _head_sum_bwd)


def _shift_down(main, prev, s, seq_first):
    prev = jnp.where(seq_first, 0.0, prev)
    ext = jnp.concatenate([prev, main], axis=0)
    return pltpu.roll(ext, s, 0)[HALO:]


def _shift_up(main, nxt, s, seq_last):
    nxt = jnp.where(seq_last, 0.0, nxt)
    ext = jnp.concatenate([main, nxt], axis=0)
    n = ext.shape[0]
    return pltpu.roll(ext, n - s, 0)[:n - HALO]


def _colsum(x):
    return jnp.sum(x, axis=0, keepdims=True)


def _pad8(x):
    k = x.shape[0]
    return jnp.concatenate([x, jnp.zeros((8 - k, x.shape[1]), x.dtype)], axis=0) if k < 8 else x


def _rms_fwd(x, g, T, name):
    D = x.shape[1]

    def body(rows, prevs, nexts, fulls, sf, sl):
        return [_rms(_f(rows[0]), _f(fulls[0]))], []

    (h,), _ = _tile_call(body, rows=[x], fulls=[g.reshape(1, D)], row_outs=[(D, BF16)], tm=min(512, T), T=T, name=name)
    return h


def _rms_bwd(x, g, dh, dres, T, name):
    D = x.shape[1]

    def body(rows, prevs, nexts, fulls, sf, sl):
        _, vjp = jax.vjp(_rms, _f(rows[0]), _f(fulls[0]))
        dx, dg = vjp(_f(rows[1]))
        return [dx + _f(rows[2])], [dg]

    (dx,), (dg,) = _tile_call(body, rows=[x, dh, dres], fulls=[g.reshape(1, D)], row_outs=[(D, F32)],
                              acc_outs=[(1, D)], tm=min(512, T), T=T, name=name)
    return dx, dg


def _ffn_conv(u1, prev, cw, cb, sf):
    k = cw.shape[0]
    out = cb + u1 * cw[k - 1:k]
    for j in range(k - 1):
        out = out + _shift_down(u1, prev, k - 1 - j, sf) * cw[j:j + 1]
    return out


def _ffn_fwd(x, p, T, tag):
    M, D = x.shape
    F = p["w_down"].shape[0]
    hf = _rms_fwd(x, p["norm"], T, f"ffn{tag}_norm")
    uf = _matmul(hf, p["w_up"], out_dtype=BF16, name=f"ffn{tag}_up")

    def body(rows, prevs, nexts, fulls, sf, sl):
        u = rows[0]
        gate = _ffn_conv(u[:, :F].astype(F32), prevs[0][:, :F].astype(F32), _f(fulls[0]), _f(fulls[1]), sf)
        return [_gelu(gate) * u[:, F:].astype(F32)], []

    (hid,), _ = _tile_call(body, rows=[uf], prevs=[0], fulls=[p["conv_w"], p["conv_b"].reshape(1, F)],
                           row_outs=[(F, BF16)], tm=min(256, T), T=T, name=f"ffn{tag}_act")
    y = _matmul(hid, p["w_down"], residual=x, name=f"ffn{tag}_down")
    return y, (x, hf, uf, hid)


def _ffn_bwd(dy, saved, p, T, tag):
    x, hf, uf, hid = saved
    M, D = x.shape
    F = p["w_down"].shape[0]
    K = p["conv_w"].shape[0]
    d_hid = _matmul(dy, p["w_down"], mode="nt", out_dtype=BF16, name=f"ffn{tag}_down_dx")
    d_w_down = _matmul(hid, dy, mode="tn", name=f"ffn{tag}_down_dw")

    def body(rows, prevs, nexts, fulls, sf, sl):
        u, dh = rows
        cw, cb = _f(fulls[0]), _f(fulls[1])
        tm = u.shape[0]
        u1c, u1p, u1n = u[:, :F].astype(F32), prevs[0][:, :F].astype(F32), nexts[0][:, :F].astype(F32)
        u1 = jnp.concatenate([u1c, u1n], axis=0)
        u2 = jnp.concatenate([u[:, F:].astype(F32), nexts[0][:, F:].astype(F32)], axis=0)
        dhid = jnp.concatenate([_f(dh), _f(nexts[1])], axis=0)
        gate = _ffn_conv(u1, u1p, cw, cb, sf)
        (act, dact) = jax.jvp(_gelu, (gate,), (jnp.ones_like(gate),))
        d_gate = dhid * u2 * dact
        d_u2 = (dhid * act)[:tm]
        rowid = lax.broadcasted_iota(jnp.int32, d_gate.shape, 0)
        d_gate = jnp.where(jnp.logical_and(sl, rowid >= tm), 0.0, d_gate)
        dgc, dgn = d_gate[:tm], d_gate[tm:]
        d_u1 = dgc * cw[K - 1:K]
        dws = []
        for j in range(K - 1):
            s = K - 1 - j
            d_u1 = d_u1 + _shift_up(dgc, dgn, s, False) * cw[j:j + 1]
            dws.append(_colsum(dgc * _shift_down(u1c, u1p, s, sf)))
        dws.append(_colsum(dgc * u1c))
        d_cw = _pad8(jnp.concatenate(dws, axis=0))
        return [jnp.concatenate([d_u1, d_u2], axis=1)], [d_cw, _colsum(dgc)]

    (d_uf,), (d_cw, d_cb) = _tile_call(
        body, rows=[uf, d_hid], prevs=[0], nexts=[0, 1], fulls=[p["conv_w"], p["conv_b"].reshape(1, F)],
        row_outs=[(2 * F, BF16)], acc_outs=[(8, F), (1, F)], tm=min(256, T), T=T, name=f"ffn{tag}_act_bwd")
    d_hf = _matmul(d_uf, p["w_up"], mode="nt", name=f"ffn{tag}_up_dx")
    d_w_up = _matmul(hf, d_uf, mode="tn", name=f"ffn{tag}_up_dw")
    dx, d_norm = _rms_bwd(x, p["norm"], d_hf, dy, T, f"ffn{tag}_norm_bwd")
    grads = {"norm": d_norm.reshape(D), "w_up": d_w_up, "conv_w": d_cw[:K], "conv_b": d_cb.reshape(F), "w_down": d_w_down}
    return dx, grads


def _lru_conv(u2, prev, cw, cb, sf):
    return _ffn_conv(u2, prev, cw, cb, sf)


def _lru_pre(xr, wbd, gb):
    return jnp.dot(xr.astype(MXU_DTYPE), wbd, preferred_element_type=F32) + gb


def _lru_gates(xr, pre, lam):
    D = xr.shape[1]
    r_gate, i_gate = _sigmoid(pre[:, :D]), _sigmoid(pre[:, D:])
    log_a = -LRU_C * r_gate * _softplus(-lam)
    a = jnp.exp(log_a)
    mult = jnp.sqrt(_neg_expm1(2.0 * log_a))
    return a, mult * (i_gate * xr)


def _lru_scan(a, b, B, T):
    M, D = a.shape
    cw = _pick(D, 256)
    ng = T // 8

    def kern(a_ref, b_ref, o_ref):
        row = lax.broadcasted_iota(jnp.int32, (8, cw), 0)

        def step(g, carry):
            sl = pl.ds(pl.multiple_of(g * 8, 8), 8)
            a8, b8 = a_ref[sl, :], b_ref[sl, :]
            for s in (1, 2, 4):
                a_sh = jnp.where(row >= s, pltpu.roll(a8, s, 0), 1.0)
                b_sh = jnp.where(row >= s, pltpu.roll(b8, s, 0), 0.0)
                b8 = a8 * b_sh + b8
                a8 = a8 * a_sh
            h8 = a8 * carry + b8
            o_ref[sl, :] = h8
            return jnp.broadcast_to(h8[7:8, :], (8, cw))

        lax.fori_loop(0, ng, step, jnp.zeros((8, cw), F32))

    spec = pl.BlockSpec((T, cw), lambda b, c: (b, c))
    return pl.pallas_call(
        kern, name="lru_scan", grid=(B, D // cw), in_specs=[spec, spec], out_specs=spec,
        out_shape=jax.ShapeDtypeStruct((M, D), F32), compiler_params=_cparams(("parallel", "parallel")),
    )(a, b)


def _lru_scan_bwd(a, hs, dhs, B, T):
    M, D = a.shape
    cw = _pick(D, 256)
    ng = T // 8

    def kern(a_ref, h_ref, d_ref, g_ref, da_ref):
        row = lax.broadcasted_iota(jnp.int32, (8, cw), 0)

        def step(k, carry):
            g_next, a_next = carry
            g = ng - 1 - k
            sl = pl.ds(pl.multiple_of(g * 8, 8), 8)
            a8, d8, h8 = a_ref[sl, :], d_ref[sl, :], h_ref[sl, :]
            c8 = jnp.where(row < 7, pltpu.roll(a8, 7, 0), a_next)
            for s in (1, 2, 4):
                d_sh = jnp.where(row < 8 - s, pltpu.roll(d8, 8 - s, 0), 0.0)
                c_sh = jnp.where(row < 8 - s, pltpu.roll(c8, 8 - s, 0), 1.0)
                d8 = d8 + c8 * d_sh
                c8 = c8 * c_sh
            G8 = d8 + c8 * g_next
            gp = jnp.maximum(g - 1, 0)
            hp8 = h_ref[pl.ds(pl.multiple_of(gp * 8, 8), 8), :]
            hp_last = jnp.where(g > 0, jnp.broadcast_to(hp8[7:8, :], (8, cw)), 0.0)
            hprev = jnp.where(row >= 1, pltpu.roll(h8, 1, 0), hp_last)
            g_ref[sl, :] = G8
            da_ref[sl, :] = G8 * hprev
            return jnp.broadcast_to(G8[0:1, :], (8, cw)), jnp.broadcast_to(a8[0:1, :], (8, cw))

        z = jnp.zeros((8, cw), F32)
        lax.fori_loop(0, ng, step, (z, z))

    spec = pl.BlockSpec((T, cw), lambda b, c: (b, c))
    sh = jax.ShapeDtypeStruct((M, D), F32)
    return pl.pallas_call(
        kern, name="lru_scan_bwd", grid=(B, D // cw), in_specs=[spec, spec, spec], out_specs=[spec, spec],
        out_shape=[sh, sh], compiler_params=_cparams(("parallel", "parallel")),
    )(a, hs, dhs)


def _lru_fwd(x, p, B, T):
    M, D = x.shape
    h0 = _rms_fwd(x, p["norm"], T, "lru_norm")
    u0 = _matmul(h0, p["w_in"], bias=p["b_in"], name="lru_in")
    fulls = [p["conv_w"], p["conv_b"].reshape(1, D), p["wbd"], p["gate_b"].reshape(1, 2 * D), p["lam"].reshape(1, D)]

    def body(rows, prevs, nexts, fulls, sf, sl):
        xr = _lru_conv(rows[0][:, D:], prevs[0][:, D:], _f(fulls[0]), _f(fulls[1]), sf)
        a, bt = _lru_gates(xr, _lru_pre(xr, fulls[2][...], _f(fulls[3])), _f(fulls[4]))
        return [a, bt], []

    (a, bt), _ = _tile_call(body, rows=[u0], prevs=[0], fulls=fulls, row_outs=[(D, F32), (D, F32)],
                            tm=min(256, T), T=T, name="lru_gates")
    hs = _lru_scan(a, bt, B, T)

    def body2(rows, prevs, nexts, fulls, sf, sl):
        return [rows[0][...] * _gelu(rows[1][:, :D])], []

    (out,), _ = _tile_call(body2, rows=[hs, u0], row_outs=[(D, BF16)], tm=min(512, T), T=T, name="lru_mix")
    y = _matmul(out, p["w_out"], bias=p["b_out"], residual=x, name="lru_out")
    return y, (x, h0, u0, a, hs, out)


def _lru_bwd(dy, saved, p, B, T):
    x, h0, u0, a, hs, out = saved
    M, D = x.shape
    K = p["conv_w"].shape[0]
    d_out = _matmul(dy, p["w_out"], mode="nt", name="lru_out_dx")
    d_w_out = _matmul(out, dy, mode="tn", name="lru_out_dw")

    def body(rows, prevs, nexts, fulls, sf, sl):
        do, h, u, dyv = rows[0][...], rows[1][...], rows[2][:, :D], rows[3][...]
        act, dact = jax.jvp(_gelu, (u,), (jnp.ones_like(u),))
        return [do * act, do * h * dact], [_colsum(dyv)]

    (d_hs, d_u1), (d_b_out,) = _tile_call(body, rows=[d_out, hs, u0, dy], row_outs=[(D, F32), (D, F32)],
                                          acc_outs=[(1, D)], tm=min(512, T), T=T, name="lru_mix_bwd")
    g_b, d_a = _lru_scan_bwd(a, hs, d_hs, B, T)
    fulls = [p["conv_w"], p["conv_b"].reshape(1, D), p["wbd"], p["gate_b"].reshape(1, 2 * D), p["lam"].reshape(1, D)]

    def body3(rows, prevs, nexts, fulls, sf, sl):
        u, gb_c, da_c, du1 = rows
        cw, cb, wbd, gbias, lam = _f(fulls[0]), _f(fulls[1]), fulls[2][...], _f(fulls[3]), _f(fulls[4])
        tm = u.shape[0]
        u2c, u2p, u2n = u[:, D:], prevs[0][:, D:], nexts[0][:, D:]
        xr_c = _lru_conv(u2c, u2p, cw, cb, sf)
        xr_n = _lru_conv(u2n, u2c[tm - HALO:], cw, cb, False)
        nt = (((1,), (1,)), ((), ()))
        _, vjp_c = jax.vjp(_lru_gates, xr_c, _lru_pre(xr_c, wbd, gbias), lam)
        dxr_c, dpre_c, d_lam = vjp_c((da_c[...], gb_c[...]))
        dpre_cb = dpre_c.astype(MXU_DTYPE)
        dxr_c = dxr_c + lax.dot_general(dpre_cb, wbd, nt, preferred_element_type=F32)
        d_wbd = lax.dot_general(xr_c.astype(MXU_DTYPE), dpre_cb, (((0,), (0,)), ((), ())), preferred_element_type=F32)
        d_gbias = _colsum(dpre_c)
        _, vjp_n = jax.vjp(lambda t, q: _lru_gates(t, q, lam), xr_n, _lru_pre(xr_n, wbd, gbias))
        dxr_n, dpre_n = vjp_n((nexts[2][...], nexts[1][...]))
        dxr_n = dxr_n + lax.dot_general(dpre_n.astype(MXU_DTYPE), wbd, nt, preferred_element_type=F32)
        d_u2 = dxr_c * cw[K - 1:K]
        dws = []
        for j in range(K - 1):
            s = K - 1 - j
            d_u2 = d_u2 + _shift_up(dxr_c, dxr_n, s, sl) * cw[j:j + 1]
            dws.append(_colsum(dxr_c * _shift_down(u2c, u2p, s, sf)))
        dws.append(_colsum(dxr_c * u2c))
        d_u = jnp.concatenate([du1[...], d_u2], axis=1)
        return [d_u], [_pad8(jnp.concatenate(dws, axis=0)), _colsum(dxr_c), d_wbd, d_gbias, d_lam,
                       _colsum(d_u)]

    (d_u0,), (d_cw, d_cb, d_wbd, d_gb, d_lam, d_b_in) = _tile_call(
        body3, rows=[u0, g_b, d_a, d_u1], prevs=[0], nexts=[0, 1, 2], fulls=fulls, row_outs=[(2 * D, BF16)],
        acc_outs=[(8, D), (1, D), (D, 2 * D), (1, 2 * D), (1, D), (1, 2 * D)], tm=min(256, T), T=T, name="lru_gates_bwd")
    d_h0 = _matmul(d_u0, p["w_in"], mode="nt", name="lru_in_dx")
    d_w_in = _matmul(h0, d_u0, mode="tn", name="lru_in_dw")
    dx, d_norm = _rms_bwd(x, p["norm"], d_h0, dy, T, "lru_norm_bwd")
    grads = {"norm": d_norm.reshape(D), "w_in": d_w_in, "b_in": d_b_in.reshape(2 * D), "conv_w": d_cw[:K],
             "conv_b": d_cb.reshape(D), "wbd": d_wbd, "gate_b": d_gb.reshape(2 * D), "lam": d_lam.reshape(D),
             "w_out": d_w_out, "b_out": d_b_out.reshape(D)}
    return dx, grads


def _rwkv_mix(xc, xp, norm, mix, sf):
    h = _rms(xc, norm)
    hp = _rms(xp, norm)
    xx = _shift_down(h, hp, 1, sf) - h
    return h, xx


def _rwkv_pre(k, xw, xa, xg, w0, w1, w2, a0, a1, a2, g1, g2, k_k, k_a, e, et):
    wl = -_softplus(-(w0 + _bdot(jnp.tanh(_bdot(xw, w1)), w2))) - 0.5
    decay = jnp.exp(-jnp.exp(wl))
    a = _sigmoid(a0 + _bdot(_bdot(xa, a1), a2))
    g = _bdot(_sigmoid(_bdot(xg, g1)), g2)
    kk = k * k_k
    nrm = jnp.sqrt(_head_sum(kk * kk, e, et))
    kk = kk / jnp.maximum(nrm, 1e-12)
    k2 = k * (1.0 + (a - 1.0) * k_a)
    return decay, k2, -kk, kk * a, g


def _rwkv_post(y, r, k2, v, g, ln_w, ln_b, r_k, e, et):
    inv = 1.0 / HEAD
    mu = _head_sum(y, e, et) * inv
    yc = y - mu
    var = _head_sum(yc * yc, e, et) * inv
    yn = yc * lax.rsqrt(var + GN_EPS) * ln_w + ln_b
    bonus = _head_sum(r * k2 * r_k, e, et) * v
    return (yn + bonus) * g


def _seg_lane_sums(x, lo_mask):
    s0 = jnp.sum(jnp.where(lo_mask, x, 0.0), axis=1, keepdims=True)
    s1 = jnp.sum(jnp.where(lo_mask, 0.0, x), axis=1, keepdims=True)
    return s0, s1


def _seg_lane_sum(x, lo_mask):
    s0, s1 = _seg_lane_sums(x, lo_mask)
    return jnp.where(lo_mask, s0, s1)


def _pair_consts():
    lane = lax.broadcasted_iota(jnp.int32, (HEAD, 128), 1)
    sub = lax.broadcasted_iota(jnp.int32, (HEAD, 128), 0)
    return lane < HEAD, (jnp.bitwise_and(lane, HEAD - 1) == sub).astype(F32)


def _pair_ones():
    head = jnp.arange(128) // HEAD
    blk = (head[:, None] == head[None, :]).astype(MXU_DTYPE)
    return jnp.concatenate([blk, blk], axis=0)


def _split_rows(x):
    hi = x.astype(MXU_DTYPE).astype(F32)
    return hi, x - hi


def _split_lhs(x):
    hi = x.astype(MXU_DTYPE)
    return jnp.concatenate([hi, (x - hi.astype(F32)).astype(MXU_DTYPE)], axis=1)


def _spread_lhs(diag, hi_row, mid_row):
    return jnp.concatenate([diag * hi_row, diag * mid_row], axis=1).astype(MXU_DTYPE)


def _rwkv_scan(r, w, k, v, a, b, B, T):
    M, D = r.shape
    HP, PG, TC, NC, chains = _scan_plan(B, T, D)
    NS = len(chains) * 8

    NG = TC // 8

    def kern(r_ref, w_ref, k_ref, v_ref, a_ref, b_ref, ones_ref, y_ref, st_ref, S_ref, lv_ref, rv_ref, ly_ref, ry_ref):
        c = pl.program_id(1)

        @pl.when(c == 0)
        def _():
            S_ref[...] = jnp.zeros_like(S_ref)

        lo, diag = _pair_consts()
        row8 = lax.broadcasted_iota(jnp.int32, (8, 128), 0)

        def blk(idx):
            return pl.ds(idx * HEAD, HEAD)

        def rows_of(gi):
            return pl.ds(pl.multiple_of(gi * 8, 8), 8)

        def spread(gi, slot):
            for ci, (bi, p) in enumerate(chains):
                hi, mid = _split_rows(v_ref[bi, rows_of(gi), p * 128:(p + 1) * 128])
                for j in range(8):
                    lv_ref[slot, blk(ci * 8 + j), :] = _spread_lhs(diag, hi[j:j + 1, :], mid[j:j + 1, :])
            rv_ref[slot] = jnp.dot(lv_ref[slot], ones_ref[...], preferred_element_type=F32)

        def recur(gi, slot):
            sl = rows_of(gi)
            tiles = [[ref[bi, sl, p * 128:(p + 1) * 128] for ref in (r_ref, w_ref, k_ref, a_ref, b_ref)]
                     for bi, p in chains]
            S = [S_ref[ci] for ci in range(len(chains))]
            for j in range(8):
                for ci, (bi, p) in enumerate(chains):
                    r8, w8, k8, a8, b8 = tiles[ci]
                    idx = ci * 8 + j
                    st_ref[p, bi, gi * 8 + j] = S[ci]
                    sa = _seg_lane_sum(S[ci] * a8[j:j + 1, :], lo)
                    S[ci] = S[ci] * w8[j:j + 1, :] + sa * b8[j:j + 1, :] + rv_ref[slot, blk(idx), :] * k8[j:j + 1, :]
                    ly_ref[slot, blk(idx), :] = _split_lhs(S[ci] * r8[j:j + 1, :])
            for ci in range(len(chains)):
                S_ref[ci] = S[ci]

        def emit(gi, slot):
            ry_ref[slot] = jnp.dot(ly_ref[slot], ones_ref[...], preferred_element_type=F32)
            for ci, (bi, p) in enumerate(chains):
                y8 = jnp.zeros((8, 128), F32)
                for j in range(8):
                    y8 = jnp.where(row8 == j, _colsum(diag * ry_ref[slot, blk(ci * 8 + j), :]), y8)
                y_ref[bi, rows_of(gi), p * 128:(p + 1) * 128] = y8

        spread(0, 0)
        ly_ref[1] = jnp.zeros_like(ly_ref[1])

        def two_groups(m, _):
            g0, g1 = 2 * m, 2 * m + 1
            spread(g1, 1)
            recur(g0, 0)
            emit(jnp.maximum(g0 - 1, 0), 1)
            spread(jnp.minimum(g1 + 1, NG - 1), 0)
            recur(g1, 1)
            emit(g0, 0)
            return 0

        lax.fori_loop(0, NG // 2, two_groups, 0)
        emit(NG - 1, 1)

    spec = pl.BlockSpec((B, TC, 128 * PG), lambda hp, c: (0, c, hp))
    st_spec = pl.BlockSpec((PG, B, TC, HEAD, 128), lambda hp, c: (hp, 0, c, 0, 0))
    y, st = pl.pallas_call(
        kern, name="rwkv_scan", grid=(HP // PG, NC),
        in_specs=[spec] * 6 + [pl.BlockSpec((256, 128), lambda hp, c: (0, 0))], out_specs=[spec, st_spec],
        out_shape=[jax.ShapeDtypeStruct((B, T, D), F32), jax.ShapeDtypeStruct((HP, B, T, HEAD, 128), F32)],
        scratch_shapes=[pltpu.VMEM((len(chains), HEAD, 128), F32),
                        pltpu.VMEM((2, NS * HEAD, 256), MXU_DTYPE), pltpu.VMEM((2, NS * HEAD, 128), F32),
                        pltpu.VMEM((2, NS * HEAD, 256), MXU_DTYPE), pltpu.VMEM((2, NS * HEAD, 128), F32)],
        compiler_params=_cparams(("parallel", "arbitrary")),
    )(*[x.reshape(B, T, D) for x in (r, w, k, v, a, b)], _pair_ones())
    return y.reshape(M, D), st


def _scan_plan(B, T, D):
    HP = D // 128
    PG = 1
    TC = min(64, T)
    assert TC % 16 == 0 and T % TC == 0
    return HP, PG, TC, T // TC, [(bi, p) for bi in range(B) for p in range(PG)]


def _rwkv_scan_bwd(r, w, k, v, a, b, st, dy, B, T):
    M, D = r.shape
    HP, PG, TC, NC, chains = _scan_plan(B, T, D)
    NS = len(chains) * 8

    NG = TC // 8

    def kern(r_ref, w_ref, k_ref, v_ref, a_ref, b_ref, st_ref, dy_ref, ones_ref,
             dr_ref, dw_ref, dk_ref, dv_ref, da_ref, db_ref, dS_ref, lp_ref, rp_ref, lq_ref, rq_ref):
        c = pl.program_id(1)

        @pl.when(c == 0)
        def _():
            dS_ref[...] = jnp.zeros_like(dS_ref)

        lo, diag = _pair_consts()
        row8 = lax.broadcasted_iota(jnp.int32, (8, 128), 0)

        def blk(idx):
            return pl.ds(idx * HEAD, HEAD)

        def rows_of(gi):
            return pl.ds(pl.multiple_of(gi * 8, 8), 8)

        def spread(gi, slot):
            sl = rows_of(gi)
            for ci, (bi, p) in enumerate(chains):
                lanes = slice(p * 128, (p + 1) * 128)
                vh, vm = _split_rows(v_ref[bi, sl, lanes])
                dh, dm = _split_rows(dy_ref[bi, sl, lanes])
                a8 = a_ref[bi, sl, lanes]
                for j in range(8):
                    idx = ci * 8 + j
                    lp_ref[slot, blk(idx), :] = _spread_lhs(diag, vh[j:j + 1, :], vm[j:j + 1, :])
                    lp_ref[slot, blk(NS + idx), :] = _spread_lhs(diag, dh[j:j + 1, :], dm[j:j + 1, :])
                    lp_ref[slot, blk(2 * NS + idx), :] = _split_lhs(st_ref[p, bi, gi * 8 + j] * a8[j:j + 1, :])
            rp_ref[slot] = jnp.dot(lp_ref[slot], ones_ref[...], preferred_element_type=F32)

        def recur(gi, slot):
            sl = rows_of(gi)
            tiles = [[ref[bi, sl, p * 128:(p + 1) * 128] for ref in (r_ref, w_ref, k_ref, a_ref, b_ref)]
                     for bi, p in chains]
            dS = [dS_ref[ci] for ci in range(len(chains))]
            acc = [[jnp.zeros((8, 128), F32) for _ in range(5)] for _ in chains]
            for j in range(7, -1, -1):
                for ci, (bi, p) in enumerate(chains):
                    r8, w8, k8, a8, b8 = tiles[ci]
                    rj, wj, kj, aj, bj = r8[j:j + 1, :], w8[j:j + 1, :], k8[j:j + 1, :], a8[j:j + 1, :], b8[j:j + 1, :]
                    idx = ci * 8 + j
                    Sp = st_ref[p, bi, gi * 8 + j]
                    vb, dyb, sa = rp_ref[slot, blk(idx), :], rp_ref[slot, blk(NS + idx), :], rp_ref[slot, blk(2 * NS + idx), :]
                    St = Sp * wj + sa * bj + vb * kj
                    d = dS[ci] + dyb * rj
                    dsa = _seg_lane_sum(d * bj, lo)
                    lq_ref[slot, blk(idx), :] = _split_lhs(d * kj)
                    rows = (_colsum(St * dyb), _colsum(d * Sp), _colsum(d * vb), _colsum(Sp * dsa), _colsum(d * sa))
                    acc[ci] = [jnp.where(row8 == j, rw, a8_) for rw, a8_ in zip(rows, acc[ci], strict=True)]
                    dS[ci] = d * wj + dsa * aj
            for ci, (bi, p) in enumerate(chains):
                dS_ref[ci] = dS[ci]
                for ref, a8_ in zip((dr_ref, dw_ref, dk_ref, da_ref, db_ref), acc[ci], strict=True):
                    ref[bi, sl, p * 128:(p + 1) * 128] = a8_

        def emit(gi, slot):
            rq_ref[slot] = jnp.dot(lq_ref[slot], ones_ref[...], preferred_element_type=F32)
            for ci, (bi, p) in enumerate(chains):
                dv8 = jnp.zeros((8, 128), F32)
                for j in range(8):
                    dv8 = jnp.where(row8 == j, _colsum(diag * rq_ref[slot, blk(ci * 8 + j), :]), dv8)
                dv_ref[bi, rows_of(gi), p * 128:(p + 1) * 128] = dv8

        spread(NG - 1, 0)
        lq_ref[1] = jnp.zeros_like(lq_ref[1])

        def two_groups(m, _):
            g0, g1 = NG - 1 - 2 * m, NG - 2 - 2 * m
            spread(g1, 1)
            recur(g0, 0)
            emit(jnp.minimum(g0 + 1, NG - 1), 1)
            spread(jnp.maximum(g1 - 1, 0), 0)
            recur(g1, 1)
            emit(g0, 0)
            return 0

        lax.fori_loop(0, NG // 2, two_groups, 0)
        emit(0, 1)

    spec = pl.BlockSpec((B, TC, 128 * PG), lambda hp, c: (0, NC - 1 - c, hp))
    st_spec = pl.BlockSpec((PG, B, TC, HEAD, 128), lambda hp, c: (hp, 0, NC - 1 - c, 0, 0))
    sh = jax.ShapeDtypeStruct((B, T, D), F32)
    outs = pl.pallas_call(
        kern, name="rwkv_scan_bwd", grid=(HP // PG, NC),
        in_specs=[spec] * 6 + [st_spec, spec, pl.BlockSpec((256, 128), lambda hp, c: (0, 0))], out_specs=[spec] * 6,
        out_shape=[sh] * 6,
        scratch_shapes=[pltpu.VMEM((len(chains), HEAD, 128), F32),
                        pltpu.VMEM((2, 3 * NS * HEAD, 256), MXU_DTYPE), pltpu.VMEM((2, 3 * NS * HEAD, 128), F32),
                        pltpu.VMEM((2, NS * HEAD, 256), MXU_DTYPE), pltpu.VMEM((2, NS * HEAD, 128), F32)],
        compiler_params=_cparams(("parallel", "arbitrary")),
    )(*[x.reshape(B, T, D) for x in (r, w, k, v, a, b)], st, dy.reshape(B, T, D), _pair_ones())
    return [o.reshape(M, D) for o in outs]


def _head_mats(D):
    ch = jnp.arange(D) // HEAD
    e = (ch[:, None] == jnp.arange(128)[None, :]).astype(F32)
    return e, e.T


def _rwkv_fwd(x, p, B, T):
    M, D = x.shape
    e, et = _head_mats(D)
    norm = p["norm"].reshape(1, D)

    def body(rows, prevs, nexts, fulls, sf, sl):
        h, xx = _rwkv_mix(rows[0][...], prevs[0][...], _f(fulls[0]), None, sf)
        mix = _f(fulls[1])
        return [h + xx * mix[i:i + 1] for i in range(6)], []

    xs, _ = _tile_call(body, rows=[x], prevs=[0], fulls=[norm, _pad8(p["mix"])], row_outs=[(D, BF16)] * 6,
                       tm=min(256, T), T=T, name="rwkv_mix")
    r = _matmul(xs[0], p["w_rkv"][0], name="rwkv_r")
    k = _matmul(xs[1], p["w_rkv"][1], name="rwkv_k")
    v = _matmul(xs[2], p["w_rkv"][2], name="rwkv_v")
    pre_fulls = [p["w0"].reshape(1, D), p["w1"], p["w2"], p["a0"].reshape(1, D), p["a1"], p["a2"], p["g1"], p["g2"],
                 p["k_k"].reshape(1, D), p["k_a"].reshape(1, D), e, et]

    def body2(rows, prevs, nexts, fulls, sf, sl):
        outs = _rwkv_pre(rows[0][...], _f(rows[1]), _f(rows[2]), _f(rows[3]), *[f[...] for f in fulls])
        return list(outs), []

    (decay, k2, kkn, bb, g), _ = _tile_call(body2, rows=[k, xs[3], xs[4], xs[5]], fulls=pre_fulls,
                                            row_outs=[(D, F32)] * 5, tm=min(256, T), T=T, name="rwkv_pre")
    y, st = _rwkv_scan(r, decay, k2, v, kkn, bb, B, T)
    post_fulls = [p["ln_w"].reshape(1, D), p["ln_b"].reshape(1, D), p["r_k"].reshape(1, D), e, et]

    def body3(rows, prevs, nexts, fulls, sf, sl):
        return [_rwkv_post(*[rr[...] for rr in rows], *[f[...] for f in fulls])], []

    (z,), _ = _tile_call(body3, rows=[y, r, k2, v, g], fulls=post_fulls, row_outs=[(D, BF16)], tm=min(256, T), T=T,
                         name="rwkv_post")
    out = _matmul(z, p["w_out"], residual=x, name="rwkv_out")
    return out, (x, xs, r, k, v, decay, k2, kkn, bb, g, y, st, z)


def _rwkv_bwd(dout, saved, p, B, T):
    x, xs, r, k, v, decay, k2, kkn, bb, g, y, st, z = saved
    M, D = x.shape
    e, et = _head_mats(D)
    d_z = _matmul(dout, p["w_out"], mode="nt", name="rwkv_out_dx")
    d_w_out = _matmul(z, dout, mode="tn", name="rwkv_out_dw")
    post_fulls = [p["ln_w"].reshape(1, D), p["ln_b"].reshape(1, D), p["r_k"].reshape(1, D), e, et]

    def body(rows, prevs, nexts, fulls, sf, sl):
        prim = [rr[...] for rr in rows[:5]] + [f[...] for f in fulls]
        _, vjp = jax.vjp(_rwkv_post, *prim)
        ct = vjp(rows[5][...])
        return list(ct[:5]), list(ct[5:8])

    (d_y, d_r1, d_k21, d_v1, d_g), (d_ln_w, d_ln_b, d_r_k) = _tile_call(
        body, rows=[y, r, k2, v, g, d_z], fulls=post_fulls, row_outs=[(D, F32)] * 5, acc_outs=[(1, D)] * 3,
        tm=min(256, T), T=T, name="rwkv_post_bwd")
    d_r2, d_w, d_k22, d_v2, d_kkn, d_bb = _rwkv_scan_bwd(r, decay, k2, v, kkn, bb, st, d_y, B, T)
    pre_fulls = [p["w0"].reshape(1, D), p["w1"], p["w2"], p["a0"].reshape(1, D), p["a1"], p["a2"], p["g1"], p["g2"],
                 p["k_k"].reshape(1, D), p["k_a"].reshape(1, D), e, et]

    def body2(rows, prevs, nexts, fulls, sf, sl):
        prim = [rows[0][...], _f(rows[1]), _f(rows[2]), _f(rows[3])] + [f[...] for f in fulls]
        _, vjp = jax.vjp(_rwkv_pre, *prim)
        ct = vjp((rows[4][...], rows[5][...] + rows[6][...], rows[7][...], rows[8][...], rows[9][...]))
        d_r = rows[10][...] + rows[11][...]
        d_v = rows[12][...] + rows[13][...]
        return [ct[0], ct[1], ct[2], ct[3], d_r, d_v], [c.astype(F32) for c in ct[4:14]]

    acc_shapes = [f.shape for f in pre_fulls[:10]]
    (d_k, d_xw, d_xa, d_xg, d_r, d_v), pgr = _tile_call(
        body2, rows=[k, xs[3], xs[4], xs[5], d_w, d_k21, d_k22, d_kkn, d_bb, d_g, d_r1, d_r2, d_v1, d_v2],
        fulls=pre_fulls, row_outs=[(D, BF16), (D, F32), (D, F32), (D, F32), (D, BF16), (D, BF16)], acc_outs=acc_shapes,
        tm=min(256, T), T=T, name="rwkv_pre_bwd")
    d_xr = _matmul(d_r, p["w_rkv"][0], mode="nt", name="rwkv_r_dx")
    d_xk = _matmul(d_k, p["w_rkv"][1], mode="nt", name="rwkv_k_dx")
    d_xv = _matmul(d_v, p["w_rkv"][2], mode="nt", name="rwkv_v_dx")
    d_wr = _matmul(xs[0], d_r, mode="tn", name="rwkv_r_dw")
    d_wk = _matmul(xs[1], d_k, mode="tn", name="rwkv_k_dw")
    d_wv = _matmul(xs[2], d_v, mode="tn", name="rwkv_v_dw")
    norm = p["norm"].reshape(1, D)

    def body3(rows, prevs, nexts, fulls, sf, sl):
        xc, xp = rows[0][...], prevs[0][...]
        nrm, mix = _f(fulls[0]), _f(fulls[1])
        h, xx = _rwkv_mix(xc, xp, nrm, None, sf)
        dxs = [rows[1 + i][...] for i in range(6)]
        dxs_n = [nexts[i][...] for i in range(6)]
        d_h = jnp.zeros_like(h)
        d_sh = jnp.zeros_like(h)
        d_sh_n = jnp.zeros_like(dxs_n[0])
        dmix = []
        for i in range(6):
            m = mix[i:i + 1]
            d_h = d_h + dxs[i] * (1.0 - m)
            d_sh = d_sh + dxs[i] * m
            d_sh_n = d_sh_n + dxs_n[i] * m
            dmix.append(_colsum(dxs[i] * xx))
        d_h = d_h + _shift_up(d_sh, d_sh_n, 1, sl)
        _, vjp = jax.vjp(_rms, xc, nrm)
        dx, dn = vjp(d_h)
        return [dx + rows[7][...]], [dn, _pad8(jnp.concatenate(dmix, axis=0))]

    (dx,), (d_norm, d_mix) = _tile_call(
        body3, rows=[x, d_xr, d_xk, d_xv, d_xw, d_xa, d_xg, dout], prevs=[0], nexts=[1, 2, 3, 4, 5, 6],
        fulls=[norm, _pad8(p["mix"])], row_outs=[(D, F32)], acc_outs=[(1, D), (8, D)], tm=min(256, T), T=T,
        name="rwkv_mix_bwd")
    names = ["w0", "w1", "w2", "a0", "a1", "a2", "g1", "g2", "k_k", "k_a"]
    grads = {n: gr.reshape(p[n].shape) for n, gr in zip(names, pgr, strict=True)}
    grads.update({"norm": d_norm.reshape(D), "mix": d_mix[:6], "w_rkv": jnp.stack([d_wr, d_wk, d_wv]),
                  "r_k": d_r_k.reshape(p["r_k"].shape), "ln_w": d_ln_w.reshape(D), "ln_b": d_ln_b.reshape(D),
                  "w_out": d_w_out})
    return dx, grads


def _loss_head(x, g, tgt, T):
    M, D = x.shape

    def body(rows, prevs, nexts, fulls, sf, sl):
        xv, gv = rows[0][...], _f(fulls[0])
        yv, vjp = jax.vjp(_rms, xv, gv)
        err = yv - rows[1][...]
        dx, dg = vjp(err * (1.0 / D))
        part = jnp.sum(_colsum(err * err), axis=1, keepdims=True) * (0.5 / D)
        return [dx], [dg, jnp.broadcast_to(part, (1, 128))]

    (dx,), (dg, loss) = _tile_call(body, rows=[x, tgt], fulls=[g.reshape(1, D)], row_outs=[(D, F32)],
                                   acc_outs=[(1, D), (1, 128)], tm=min(512, T), T=T, name="loss_head")
    return loss[0, 0], dx, dg.reshape(D)


def _local_step(x3, tgt3, P):
    B, T, D = x3.shape
    x, tgt = x3.reshape(B * T, D), tgt3.reshape(B * T, D)
    x1, s_lru = _lru_fwd(x, P["lru"], B, T)
    x2, s_f0 = _ffn_fwd(x1, P["ffn0"], T, "0")
    x3_, s_rw = _rwkv_fwd(x2, P["rwkv"], B, T)
    x4, s_f1 = _ffn_fwd(x3_, P["ffn1"], T, "1")
    loss, d4, d_fn = _loss_head(x4, P["final_norm"], tgt, T)
    d3, g_f1 = _ffn_bwd(d4, s_f1, P["ffn1"], T, "1")
    d2, g_rw = _rwkv_bwd(d3, s_rw, P["rwkv"], B, T)
    d1, g_f0 = _ffn_bwd(d2, s_f0, P["ffn0"], T, "0")
    d0, g_lru = _lru_bwd(d1, s_lru, P["lru"], B, T)
    return loss, d0.reshape(B, T, D), {"lru": g_lru, "ffn0": g_f0, "rwkv": g_rw, "ffn1": g_f1, "final_norm": d_fn}


WEIGHTS = ['lru_norm', 'lru_w_in', 'lru_b_in', 'lru_conv_w', 'lru_conv_b', 'lru_gate_w', 'lru_gate_b', 'lru_lambda',
           'lru_w_out', 'lru_b_out', 'rwkv_norm', 'rwkv_mix', 'rwkv_w_rkv', 'rwkv_w0', 'rwkv_w1', 'rwkv_w2', 'rwkv_a0',
           'rwkv_a1', 'rwkv_a2', 'rwkv_g1', 'rwkv_g2', 'rwkv_k_k', 'rwkv_k_a', 'rwkv_r_k', 'rwkv_ln_w', 'rwkv_ln_b',
           'rwkv_w_out', 'ffn_norm', 'ffn_w_up', 'ffn_conv_w', 'ffn_conv_b', 'ffn_w_down', 'final_norm']
SHARD_AXIS = {'lru_w_in': 2, 'lru_conv_w': 2, 'lru_w_out': 1, 'rwkv_norm': 1, 'rwkv_mix': 2, 'rwkv_w_rkv': 2,
              'rwkv_w0': 1, 'rwkv_w1': 1, 'rwkv_w2': 2, 'rwkv_a0': 1, 'rwkv_a1': 1, 'rwkv_a2': 2, 'rwkv_g1': 1,
              'rwkv_g2': 2, 'rwkv_k_k': 1, 'rwkv_k_a': 1, 'rwkv_ln_w': 1, 'rwkv_ln_b': 1, 'rwkv_w_out': 1,
              'ffn_w_up': 2, 'ffn_conv_w': 2, 'ffn_w_down': 1}
MXU_WEIGHTS = ('lru_w_in', 'lru_w_out', 'rwkv_w_rkv', 'rwkv_w_out', 'ffn_w_up', 'ffn_w_down')
N_CHIPS = 4
LANES = 1024


def _pack(arrs, dtype, row_mult):
    flat = jnp.concatenate([a.reshape(-1).astype(dtype) for a in arrs])
    n = flat.shape[0]
    unit = row_mult * LANES
    tot = -(-n // unit) * unit
    if tot > n:
        flat = jnp.concatenate([flat, jnp.zeros((tot - n,), dtype)])
    return flat.reshape(tot // LANES, LANES)


def _unpack(buf, shapes):
    flat = buf.reshape(-1)
    out, off = [], 0
    for s in shapes:
        n = 1
        for d in s:
            n *= d
        out.append(flat[off:off + n].reshape(s))
        off += n
    return out


def _unshard(stacked, axis):
    return jnp.concatenate([stacked[q] for q in range(N_CHIPS)], axis=axis)


def _to_shards(full, axis):
    return jnp.stack(jnp.split(full, N_CHIPS, axis=axis))


MESH_ID = pl.DeviceIdType.MESH


def _chip_exchange(buf, same_to_all, name):
    shape = buf.shape if same_to_all else buf.shape[1:]

    def body(in_ref, out_ref, send_sems, recv_sems, local_sem):
        x, y, c = lax.axis_index("x"), lax.axis_index("y"), lax.axis_index("c")
        p = 2 * x + y
        chips = [(1 - x, y), (x, 1 - y), (1 - x, 1 - y)]

        def src(q):
            return in_ref if same_to_all else in_ref.at[q]

        mine = pltpu.make_async_copy(src(p), out_ref.at[p], local_sem)
        mine.start()
        copies = []
        for j, (qx, qy) in enumerate(chips):
            q = 2 * qx + qy
            copies.append((pltpu.make_async_remote_copy(
                src_ref=src(q), dst_ref=out_ref.at[p], send_sem=send_sems.at[j], recv_sem=recv_sems.at[j],
                device_id=(qx, qy, c), device_id_type=MESH_ID),
                pltpu.make_async_remote_copy(
                src_ref=src(q), dst_ref=out_ref.at[q], send_sem=send_sems.at[j], recv_sem=recv_sems.at[j],
                device_id=(qx, qy, c), device_id_type=MESH_ID)))
        for snd, _ in copies:
            snd.start()
        for _, rcv in copies:
            rcv.wait_recv()
        for snd, _ in copies:
            snd.wait_send()
        mine.wait()

    return pl.pallas_call(
        body, name=name,
        out_shape=jax.ShapeDtypeStruct((N_CHIPS,) + tuple(shape), buf.dtype),
        in_specs=[pl.BlockSpec(memory_space=pl.ANY)], out_specs=pl.BlockSpec(memory_space=pl.ANY),
        scratch_shapes=[pltpu.SemaphoreType.DMA((3,)), pltpu.SemaphoreType.DMA((3,)), pltpu.SemaphoreType.DMA],
        compiler_params=pltpu.CompilerParams(has_side_effects=True),
    )(buf)


def _sibling_swap(buf, name):
    def body(in_ref, out_ref, send_sem, recv_sem):
        x, y, c = lax.axis_index("x"), lax.axis_index("y"), lax.axis_index("c")
        cp = pltpu.make_async_remote_copy(src_ref=in_ref, dst_ref=out_ref, send_sem=send_sem, recv_sem=recv_sem,
                                          device_id=(x, y, 1 - c), device_id_type=MESH_ID)
        cp.start()
        cp.wait()

    return pl.pallas_call(
        body, name=name, out_shape=jax.ShapeDtypeStruct(buf.shape, buf.dtype),
        in_specs=[pl.BlockSpec(memory_space=pl.ANY)], out_specs=pl.BlockSpec(memory_space=pl.ANY),
        scratch_shapes=[pltpu.SemaphoreType.DMA, pltpu.SemaphoreType.DMA],
        compiler_params=pltpu.CompilerParams(has_side_effects=True),
    )(buf)


def _by_half(mine, other):
    c = lax.axis_index("c")
    st = jnp.stack([mine, other])
    return lax.dynamic_index_in_dim(st, c, 0, keepdims=False), lax.dynamic_index_in_dim(st, 1 - c, 0, keepdims=False)


def _sum_rows_call(parts, name):
    R = parts[0].shape[0]
    tr = _pick_rows(R)
    n = len(parts)

    def kern(*refs):
        acc = refs[0][...]
        for r in refs[1:n]:
            acc = acc + r[...]
        refs[n][...] = acc

    spec = pl.BlockSpec((tr, LANES), lambda i: (i, 0))
    return pl.pallas_call(kern, name=name, grid=(R // tr,), in_specs=[spec] * n, out_specs=spec,
                          out_shape=jax.ShapeDtypeStruct((R, LANES), F32), compiler_params=_cparams(("parallel",)))(*parts)


def _pick_rows(R):
    for t in (512, 256, 128, 64, 32, 16, 8):
        if R % t == 0:
            return t
    return R


def _adamw_call(w, g, m, v, name):
    R = w.shape[0]
    tr = _pick_rows(R)
    c1 = 1.0 / (1.0 - ADAM_B1 ** ADAM_STEP)
    c2 = 1.0 / (1.0 - ADAM_B2 ** ADAM_STEP)

    def kern(w_ref, g_ref, m_ref, v_ref, d_ref, nm_ref, nv_ref):
        gv = g_ref[...]
        nm = ADAM_B1 * m_ref[...] + (1.0 - ADAM_B1) * gv
        nv = ADAM_B2 * v_ref[...] + (1.0 - ADAM_B2) * (gv * gv)
        d_ref[...] = -ADAM_LR * ((nm * c1) / (jnp.sqrt(nv * c2) + ADAM_EPS) + ADAM_WD * w_ref[...])
        nm_ref[...] = nm
        nv_ref[...] = nv

    spec = pl.BlockSpec((tr, LANES), lambda i: (i, 0))
    sh = jax.ShapeDtypeStruct((R, LANES), F32)
    return pl.pallas_call(kern, name=name, grid=(R // tr,), in_specs=[spec] * 4, out_specs=[spec] * 3,
                          out_shape=[sh] * 3, compiler_params=_cparams(("parallel",)))(w, g, m, v)


def _gate_dense(gate_w):
    _, nb, bw, _ = gate_w.shape
    eye = jnp.eye(nb, dtype=gate_w.dtype)
    dense = jnp.einsum('gncd,nm->gncmd', gate_w, eye).reshape(2, nb * bw, nb * bw)
    return jnp.concatenate([dense[0], dense[1]], axis=1)


def _gate_blocks(d_dense, nb):
    D = d_dense.shape[0]
    bw = D // nb
    g = d_dense.reshape(nb, bw, 2, nb, bw)
    return jnp.einsum('ncgnd->gncd', g)


def _step(W, M1, V1, x, tgt):
    sharded = [n for n in WEIGHTS if n in SHARD_AXIS]
    repl = [n for n in WEIGHTS if n not in SHARD_AXIS]
    big = [n for n in sharded if n in MXU_WEIGHTS]
    small = [n for n in sharded if n not in MXU_WEIGHTS]

    def gather(names, dtype, tag):
        buf = _pack([W[n] for n in names], dtype, 32)
        rh = buf.shape[0] // 2
        c = lax.axis_index("c")
        my_half = lax.dynamic_slice_in_dim(buf, c * rh, rh, 0)
        got = _chip_exchange(my_half, True, f"gather_{tag}")
        other = _sibling_swap(got, f"gather_{tag}_swap")
        h0, h1 = _by_half(got, other)
        full = jnp.concatenate([h0, h1], axis=1)
        per_chip = [_unpack(full[q], [W[n].shape for n in names]) for q in range(N_CHIPS)]
        return {n: jnp.concatenate([per_chip[q][i] for q in range(N_CHIPS)], axis=SHARD_AXIS[n])
                for i, n in enumerate(names)}

    full = {**gather(big, MXU_DTYPE, "mats"), **gather(small, F32, "vecs")}
    for n in repl:
        full[n] = W[n]

    P = {
        "lru": {"norm": full["lru_norm"][0], "w_in": full["lru_w_in"][0], "b_in": full["lru_b_in"][0],
                "conv_w": full["lru_conv_w"][0], "conv_b": full["lru_conv_b"][0],
                "wbd": _gate_dense(full["lru_gate_w"][0]).astype(MXU_DTYPE), "gate_b": full["lru_gate_b"][0].reshape(-1),
                "lam": full["lru_lambda"][0], "w_out": full["lru_w_out"][0], "b_out": full["lru_b_out"][0]},
        "rwkv": {"norm": full["rwkv_norm"][0], "mix": full["rwkv_mix"][0], "w_rkv": full["rwkv_w_rkv"][0],
                 "w0": full["rwkv_w0"][0], "w1": full["rwkv_w1"][0], "w2": full["rwkv_w2"][0], "a0": full["rwkv_a0"][0],
                 "a1": full["rwkv_a1"][0], "a2": full["rwkv_a2"][0], "g1": full["rwkv_g1"][0], "g2": full["rwkv_g2"][0],
                 "k_k": full["rwkv_k_k"][0], "k_a": full["rwkv_k_a"][0], "r_k": full["rwkv_r_k"][0],
                 "ln_w": full["rwkv_ln_w"][0], "ln_b": full["rwkv_ln_b"][0], "w_out": full["rwkv_w_out"][0]},
        "final_norm": full["final_norm"],
    }
    for l in range(2):
        P[f"ffn{l}"] = {"norm": full["ffn_norm"][l], "w_up": full["ffn_w_up"][l], "conv_w": full["ffn_conv_w"][l],
                        "conv_b": full["ffn_conv_b"][l], "w_down": full["ffn_w_down"][l]}

    loss, gx, G = _local_step(x, tgt, P)

    nb = W["lru_gate_w"].shape[2]
    gl, gr = G["lru"], G["rwkv"]
    gfull = {
        "lru_norm": gl["norm"][None], "lru_w_in": gl["w_in"][None], "lru_b_in": gl["b_in"][None],
        "lru_conv_w": gl["conv_w"][None], "lru_conv_b": gl["conv_b"][None], "lru_gate_w": _gate_blocks(gl["wbd"], nb)[None],
        "lru_gate_b": gl["gate_b"].reshape(W["lru_gate_b"].shape), "lru_lambda": gl["lam"][None],
        "lru_w_out": gl["w_out"][None], "lru_b_out": gl["b_out"][None],
        "rwkv_norm": gr["norm"][None], "rwkv_mix": gr["mix"][None], "rwkv_w_rkv": gr["w_rkv"][None],
        "rwkv_w0": gr["w0"][None], "rwkv_w1": gr["w1"][None], "rwkv_w2": gr["w2"][None], "rwkv_a0": gr["a0"][None],
        "rwkv_a1": gr["a1"][None], "rwkv_a2": gr["a2"][None], "rwkv_g1": gr["g1"][None], "rwkv_g2": gr["g2"][None],
        "rwkv_k_k": gr["k_k"][None], "rwkv_k_a": gr["k_a"][None], "rwkv_r_k": gr["r_k"][None],
        "rwkv_ln_w": gr["ln_w"][None], "rwkv_ln_b": gr["ln_b"][None], "rwkv_w_out": gr["w_out"][None],
        "final_norm": G["final_norm"],
    }
    for k in ("norm", "w_up", "conv_w", "conv_b", "w_down"):
        gfull["ffn_" + k] = jnp.stack([G["ffn0"][k], G["ffn1"][k]])

    c = lax.axis_index("c")
    shard_shapes = [W[n].shape for n in sharded]
    gs = jnp.stack([_pack([_to_shards(gfull[n], SHARD_AXIS[n])[q] for n in sharded], F32, 32) for q in range(N_CHIPS)])
    rh = gs.shape[1] // 2
    grep = _pack([gfull[n] for n in repl], F32, 16)
    rr = grep.shape[0]
    halves = gs.reshape(N_CHIPS, 2, rh, LANES)
    keep = lax.dynamic_index_in_dim(halves, c, 1, keepdims=False).reshape(N_CHIPS * rh, LANES)
    give = lax.dynamic_index_in_dim(halves, 1 - c, 1, keepdims=False).reshape(N_CHIPS * rh, LANES)
    got = _sibling_swap(jnp.concatenate([give, grep]), "reduce_pair_swap")
    pair = _sum_rows_call([jnp.concatenate([keep, grep]), got], "reduce_pair_sum")
    slots = jnp.concatenate([pair[:N_CHIPS * rh].reshape(N_CHIPS, rh, LANES),
                             jnp.broadcast_to(pair[N_CHIPS * rh:], (N_CHIPS, rr, LANES))], axis=1)
    arrived = _chip_exchange(slots, False, "reduce_chips")
    tot = _sum_rows_call([arrived[q] for q in range(N_CHIPS)], "reduce_chip_sum")
    my_half = tot[:rh]
    other_half = _sibling_swap(my_half, "reduce_half_swap")
    h0, h1 = _by_half(my_half, other_half)
    g_sh = jnp.concatenate([h0, h1])
    g_rp = tot[rh:]

    outs = {}
    for names, gbuf, tag in ((sharded, g_sh, "sharded"), (repl, g_rp, "repl")):
        shapes = [W[n].shape for n in names]
        row_mult = gbuf.shape[0]
        wb, mb, vb = (_pack([S[n] for n in names], F32, 1) for S in (W, M1, V1))
        pad = gbuf.shape[0] - wb.shape[0]
        if pad:
            wb, mb, vb = (jnp.concatenate([b, jnp.ones((pad, LANES), F32)]) for b in (wb, mb, vb))
        d, nm, nv = _adamw_call(wb, gbuf, mb, vb, f"adamw_{tag}")
        for kind, buf in (("grad", gbuf), ("delta", d), ("new_m", nm), ("new_v", nv)):
            for n, a in zip(names, _unpack(buf, shapes), strict=True):
                outs[(kind, n)] = a
    loss = lax.psum(loss, ("x", "y", "c"))
    return (loss, gx, *[outs[(kind, n)] for kind in ("grad", "delta", "new_m", "new_v") for n in WEIGHTS])


def kernel(x, lru_norm, lru_w_in, lru_b_in, lru_conv_w, lru_conv_b, lru_gate_w, lru_gate_b, lru_lambda, lru_w_out, lru_b_out, rwkv_norm, rwkv_mix, rwkv_w_rkv, rwkv_w0, rwkv_w1, rwkv_w2, rwkv_a0, rwkv_a1, rwkv_a2, rwkv_g1, rwkv_g2, rwkv_k_k, rwkv_k_a, rwkv_r_k, rwkv_ln_w, rwkv_ln_b, rwkv_w_out, ffn_norm, ffn_w_up, ffn_conv_w, ffn_conv_b, ffn_w_down, final_norm, loss_target, m_lru_norm, m_lru_w_in, m_lru_b_in, m_lru_conv_w, m_lru_conv_b, m_lru_gate_w, m_lru_gate_b, m_lru_lambda, m_lru_w_out, m_lru_b_out, m_rwkv_norm, m_rwkv_mix, m_rwkv_w_rkv, m_rwkv_w0, m_rwkv_w1, m_rwkv_w2, m_rwkv_a0, m_rwkv_a1, m_rwkv_a2, m_rwkv_g1, m_rwkv_g2, m_rwkv_k_k, m_rwkv_k_a, m_rwkv_r_k, m_rwkv_ln_w, m_rwkv_ln_b, m_rwkv_w_out, m_ffn_norm, m_ffn_w_up, m_ffn_conv_w, m_ffn_conv_b, m_ffn_w_down, m_final_norm, v_lru_norm, v_lru_w_in, v_lru_b_in, v_lru_conv_w, v_lru_conv_b, v_lru_gate_w, v_lru_gate_b, v_lru_lambda, v_lru_w_out, v_lru_b_out, v_rwkv_norm, v_rwkv_mix, v_rwkv_w_rkv, v_rwkv_w0, v_rwkv_w1, v_rwkv_w2, v_rwkv_a0, v_rwkv_a1, v_rwkv_a2, v_rwkv_g1, v_rwkv_g2, v_rwkv_k_k, v_rwkv_k_a, v_rwkv_r_k, v_rwkv_ln_w, v_rwkv_ln_b, v_rwkv_w_out, v_ffn_norm, v_ffn_w_up, v_ffn_conv_w, v_ffn_conv_b, v_ffn_w_down, v_final_norm):
    given = dict(locals())
    W = {n: given[n] for n in WEIGHTS}
    M1 = {n: given["m_" + n] for n in WEIGHTS}
    V1 = {n: given["v_" + n] for n in WEIGHTS}
    return _step(W, M1, V1, x, loss_target)
```

```python
import functools

import jax
import jax.numpy as jnp
from jax import lax
from jax.experimental import pallas as pl
from jax.experimental.pallas import tpu as pltpu

F32 = jnp.float32
BF16 = jnp.bfloat16
MXU_DTYPE = BF16

HEAD = 64
LRU_C = 8.0
GN_EPS = 64e-5
RMS_EPS = 1e-6
HALO = 16
VMEM_LIMIT = 56 * 1024 * 1024

ADAM_LR, ADAM_B1, ADAM_B2, ADAM_EPS, ADAM_WD, ADAM_STEP = 0.001, 0.9, 0.999, 1e-08, 0.01, 10


def _cparams(sem):
    return pltpu.CompilerParams(dimension_semantics=sem, vmem_limit_bytes=VMEM_LIMIT)


def _pick(n, want):
    if n <= want:
        return n
    t = want
    while t >= 128:
        if n % t == 0:
            return t
        t -= 128
    return n


class _W:
    def __init__(self, arr, layer, split):
        self.arr, self.layer, self.split = arr, layer, split
        _, _, r, c = arr.shape
        self.shape = (r, N_CHIPS * c) if split == "col" else (N_CHIPS * r, c)


def _matmul(a, b, mode="nn", bias=None, residual=None, out_dtype=F32, name="mm", tm=512, tn=512, tk=1024,
            out_cols_by_chip=False):
    bshape = b.shape
    if mode == "nn":
        (M, K), (K2, N) = a.shape, bshape
    elif mode == "nt":
        (M, K), (N, K2) = a.shape, bshape
    else:
        (K, M), (K2, N) = a.shape, bshape
    assert K == K2, (a.shape, bshape, mode)
    lim_n, lim_k = N, K
    if isinstance(b, _W):
        per = b.arr.shape[3] if b.split == "col" else b.arr.shape[2]
        if (mode == "nn") == (b.split == "col"):
            lim_n = per
        else:
            lim_k = per
    if out_cols_by_chip:
        lim_n = min(lim_n, N // N_CHIPS)
    tm, tn, tk = _pick(M, tm), _pick(lim_n, tn), _pick(lim_k, tk)
    nk = K // tk
    dims = {"nn": (((1,), (0,)), ((), ())), "nt": (((1,), (1,)), ((), ())), "tn": (((0,), (0,)), ((), ()))}[mode]
    a_spec = {"nn": pl.BlockSpec((tm, tk), lambda i, j, k: (i, k)),
              "nt": pl.BlockSpec((tm, tk), lambda i, j, k: (i, k)),
              "tn": pl.BlockSpec((tk, tm), lambda i, j, k: (k, i))}[mode]
    if isinstance(b, _W):
        lay = b.layer
        if mode == "nn" and b.split == "col":
            per = b.arr.shape[3] // tn
            b_spec = pl.BlockSpec((None, None, tk, tn), lambda i, j, k: (j // per, lay, k, j % per))
        elif mode == "nn":
            per = b.arr.shape[2] // tk
            b_spec = pl.BlockSpec((None, None, tk, tn), lambda i, j, k: (k // per, lay, k % per, j))
        elif mode == "nt" and b.split == "col":
            per = b.arr.shape[3] // tk
            b_spec = pl.BlockSpec((None, None, tn, tk), lambda i, j, k: (k // per, lay, j, k % per))
        else:
            assert mode == "nt"
            per = b.arr.shape[2] // tn
            b_spec = pl.BlockSpec((None, None, tn, tk), lambda i, j, k: (j // per, lay, j % per, k))
        b = b.arr
    else:
        b_spec = {"nn": pl.BlockSpec((tk, tn), lambda i, j, k: (k, j)),
                  "nt": pl.BlockSpec((tn, tk), lambda i, j, k: (j, k)),
                  "tn": pl.BlockSpec((tk, tn), lambda i, j, k: (k, j))}[mode]
    if out_cols_by_chip:
        opc = N // N_CHIPS // tn
        out_spec = pl.BlockSpec((None, tm, tn), lambda i, j, k: (j // opc, i, j % opc))
        out_shape = jax.ShapeDtypeStruct((N_CHIPS, M, N // N_CHIPS), out_dtype)
    else:
        out_spec = pl.BlockSpec((tm, tn), lambda i, j, k: (i, j))
        out_shape = jax.ShapeDtypeStruct((M, N), out_dtype)
    in_specs, operands = [a_spec, b_spec], [a, b]
    if bias is not None:
        in_specs.append(pl.BlockSpec((1, tn), lambda i, j, k: (0, j)))
        operands.append(bias.reshape(1, N))
    if residual is not None:
        in_specs.append(pl.BlockSpec((tm, tn), lambda i, j, k: (i, j)))
        operands.append(residual)
    has_bias, has_res = bias is not None, residual is not None

    def kern(*refs):
        a_ref, b_ref = refs[0], refs[1]
        o_ref, acc_ref = refs[-2], refs[-1]
        k = pl.program_id(2)

        @pl.when(k == 0)
        def _():
            acc_ref[...] = jnp.zeros_like(acc_ref)

        acc_ref[...] += lax.dot_general(a_ref[...].astype(MXU_DTYPE), b_ref[...].astype(MXU_DTYPE), dims,
                                        preferred_element_type=F32)

        @pl.when(k == nk - 1)
        def _():
            r = acc_ref[...]
            pos = 2
            if has_bias:
                r = r + refs[pos][...].astype(F32)
                pos += 1
            if has_res:
                r = r + refs[pos][...].astype(F32)
            o_ref[...] = r.astype(o_ref.dtype)

    return pl.pallas_call(
        kern, name=name,
        grid=(M // tm, N // tn, nk),
        in_specs=in_specs,
        out_specs=out_spec,
        out_shape=out_shape,
        scratch_shapes=[pltpu.VMEM((tm, tn), F32)],
        compiler_params=_cparams(("parallel", "parallel", "arbitrary")),
    )(*operands)


def _tile_call(body, *, rows, prevs=(), nexts=(), fulls=(), row_outs=(), acc_outs=(), tm, T, name):
    M = rows[0].shape[0]
    n_tiles, tps, hb = M // tm, T // tm, tm // HALO
    n_halo_blocks = M // HALO
    nr, npv, nnx, nf, nro, nac = len(rows), len(prevs), len(nexts), len(fulls), len(row_outs), len(acc_outs)

    def kern(*refs):
        i = pl.program_id(0)
        row_refs = refs[:nr]
        prev_refs = refs[nr:nr + npv]
        next_refs = refs[nr + npv:nr + npv + nnx]
        full_refs = refs[nr + npv + nnx:nr + npv + nnx + nf]
        out_refs = refs[nr + npv + nnx + nf:nr + npv + nnx + nf + nro]
        acc_refs = refs[nr + npv + nnx + nf + nro:]
        seq_first = (i % tps) == 0
        seq_last = (i % tps) == (tps - 1)
        outs, accs = body(row_refs, prev_refs, next_refs, full_refs, seq_first, seq_last)
        for r, o in zip(out_refs, outs, strict=True):
            r[...] = o.astype(r.dtype)
        if nac:
            @pl.when(i == 0)
            def _():
                for r in acc_refs:
                    r[...] = jnp.zeros_like(r)
            for r, a in zip(acc_refs, accs, strict=True):
                r[...] += a.astype(F32)

    in_specs = [pl.BlockSpec((tm, a.shape[1]), lambda i: (i, 0)) for a in rows]
    in_specs += [pl.BlockSpec((HALO, rows[k].shape[1]), lambda i: (jnp.maximum(i * hb - 1, 0), 0)) for k in prevs]
    in_specs += [pl.BlockSpec((HALO, rows[k].shape[1]), lambda i: (jnp.minimum((i + 1) * hb, n_halo_blocks - 1), 0))
                 for k in nexts]
    in_specs += [pl.BlockSpec(f.shape, lambda i: (0, 0)) for f in fulls]
    out_specs = [pl.BlockSpec((tm, w), lambda i: (i, 0)) for (w, _) in row_outs]
    out_specs += [pl.BlockSpec(s, lambda i: (0, 0)) for s in acc_outs]
    out_shape = [jax.ShapeDtypeStruct((M, w), dt) for (w, dt) in row_outs]
    out_shape += [jax.ShapeDtypeStruct(s, F32) for s in acc_outs]
    operands = list(rows) + [rows[k] for k in prevs] + [rows[k] for k in nexts] + list(fulls)
    res = pl.pallas_call(
        kern, name=name, grid=(n_tiles,), in_specs=in_specs, out_specs=out_specs, out_shape=out_shape,
        compiler_params=_cparams(("arbitrary",)),
    )(*operands)
    return res[:nro], res[nro:]


def _f(ref):
    return ref[...].astype(F32)


def _sigmoid(x):
    return 1.0 / (1.0 + jnp.exp(-x))


def _softplus(x):
    return jnp.maximum(x, 0.0) + jnp.log(1.0 + jnp.exp(-jnp.abs(x)))


def _neg_expm1(x):
    series = -x * (1.0 + x * (0.5 + x * (1.0 / 6.0) * (1.0 + 0.25 * x)))
    return jnp.where(x > -0.01, series, 1.0 - jnp.exp(x))


def _gelu(x):
    return 0.5 * x * (1.0 + jnp.tanh(0.7978845608028654 * (x + 0.044715 * x * x * x)))


def _rms(x, g):
    return x * lax.rsqrt(jnp.mean(x * x, axis=-1, keepdims=True) + RMS_EPS) * g


@jax.custom_vjp
def _bdot(x, w):
    return jnp.dot(x.astype(MXU_DTYPE), w.astype(MXU_DTYPE), preferred_element_type=F32)


def _bdot_fwd(x, w):
    return _bdot(x, w), (x, w)


def _bdot_bwd(res, ct):
    x, w = res
    ctb = ct.astype(MXU_DTYPE)
    dx = lax.dot_general(ctb, w.astype(MXU_DTYPE), (((1,), (1,)), ((), ())), preferred_element_type=F32)
    dw = lax.dot_general(x.astype(MXU_DTYPE), ctb, (((0,), (0,)), ((), ())), preferred_element_type=F32)
    return dx.astype(x.dtype), dw.astype(w.dtype)


_bdot.defvjp(_bdot_fwd, _bdot_bwd)


@jax.custom_vjp
def _head_sum(x, e, et):
    s = jnp.dot(x, e, precision=lax.Precision.HIGHEST, preferred_element_type=F32)
    return jnp.dot(s, et, precision=lax.Precision.HIGHEST, preferred_element_type=F32)


def _head_sum_fwd(x, e, et):
    return _head_sum(x, e, et), (e, et)


def _head_sum_bwd(res, ct):
    e, et = res
    return _head_sum(ct, e, et), jnp.zeros_like(e), jnp.zeros_like(et)


_head_sum.defvjp(_head_sum_fwd, _head_sum_bwd)


def _shift_down(main, prev, s, seq_first):
    prev = jnp.where(seq_first, 0.0, prev)
    ext = jnp.concatenate([prev, main], axis=0)
    return pltpu.roll(ext, s, 0)[HALO:]


def _shift_up(main, nxt, s, seq_last):
    nxt = jnp.where(seq_last, 0.0, nxt)
    ext = jnp.concatenate([main, nxt], axis=0)
    n = ext.shape[0]
    return pltpu.roll(ext, n - s, 0)[:n - HALO]


def _colsum(x):
    return jnp.sum(x, axis=0, keepdims=True)


def _pad8(x):
    k = x.shape[0]
    return jnp.concatenate([x, jnp.zeros((8 - k, x.shape[1]), x.dtype)], axis=0) if k < 8 else x


def _rms_fwd(x, g, T, name):
    D = x.shape[1]

    def body(rows, prevs, nexts, fulls, sf, sl):
        return [_rms(_f(rows[0]), _f(fulls[0]))], []

    (h,), _ = _tile_call(body, rows=[x], fulls=[g.reshape(1, D)], row_outs=[(D, BF16)], tm=min(512, T), T=T, name=name)
    return h


def _rms_bwd(x, g, dh, dres, T, name):
    D = x.shape[1]

    def body(rows, prevs, nexts, fulls, sf, sl):
        _, vjp = jax.vjp(_rms, _f(rows[0]), _f(fulls[0]))
        dx, dg = vjp(_f(rows[1]))
        return [dx + _f(rows[2])], [dg]

    (dx,), (dg,) = _tile_call(body, rows=[x, dh, dres], fulls=[g.reshape(1, D)], row_outs=[(D, F32)],
                              acc_outs=[(1, D)], tm=min(512, T), T=T, name=name)
    return dx, dg


def _ffn_conv(u1, prev, cw, cb, sf):
    k = cw.shape[0]
    out = cb + u1 * cw[k - 1:k]
    for j in range(k - 1):
        out = out + _shift_down(u1, prev, k - 1 - j, sf) * cw[j:j + 1]
    return out


def _ffn_fwd(x, p, T, tag):
    M, D = x.shape
    F = p["w_down"].shape[0]
    hf = _rms_fwd(x, p["norm"], T, f"ffn{tag}_norm")
    uf = _matmul(hf, p["w_up"], out_dtype=BF16, name=f"ffn{tag}_up")

    def body(rows, prevs, nexts, fulls, sf, sl):
        u = rows[0]
        gate = _ffn_conv(u[:, :F].astype(F32), prevs[0][:, :F].astype(F32), _f(fulls[0]), _f(fulls[1]), sf)
        return [_gelu(gate) * u[:, F:].astype(F32)], []

    (hid,), _ = _tile_call(body, rows=[uf], prevs=[0], fulls=[p["conv_w"], p["conv_b"].reshape(1, F)],
                           row_outs=[(F, BF16)], tm=min(256, T), T=T, name=f"ffn{tag}_act")
    y = _matmul(hid, p["w_down"], residual=x, name=f"ffn{tag}_down")
    return y, (x, hf, uf, hid)


def _ffn_bwd(dy, saved, p, T, tag):
    x, hf, uf, hid = saved
    M, D = x.shape
    F = p["w_down"].shape[0]
    K = p["conv_w"].shape[0]
    d_hid = _matmul(dy, p["w_down"], mode="nt", out_dtype=BF16, name=f"ffn{tag}_down_dx")
    d_w_down = _matmul(hid, dy, mode="tn", name=f"ffn{tag}_down_dw")

    def body(rows, prevs, nexts, fulls, sf, sl):
        u, dh = rows
        cw, cb = _f(fulls[0]), _f(fulls[1])
        tm = u.shape[0]
        u1c, u1p, u1n = u[:, :F].astype(F32), prevs[0][:, :F].astype(F32), nexts[0][:, :F].astype(F32)
        u1 = jnp.concatenate([u1c, u1n], axis=0)
        u2 = jnp.concatenate([u[:, F:].astype(F32), nexts[0][:, F:].astype(F32)], axis=0)
        dhid = jnp.concatenate([_f(dh), _f(nexts[1])], axis=0)
        gate = _ffn_conv(u1, u1p, cw, cb, sf)
        (act, dact) = jax.jvp(_gelu, (gate,), (jnp.ones_like(gate),))
        d_gate = dhid * u2 * dact
        d_u2 = (dhid * act)[:tm]
        rowid = lax.broadcasted_iota(jnp.int32, d_gate.shape, 0)
        d_gate = jnp.where(jnp.logical_and(sl, rowid >= tm), 0.0, d_gate)
        dgc, dgn = d_gate[:tm], d_gate[tm:]
        d_u1 = dgc * cw[K - 1:K]
        dws = []
        for j in range(K - 1):
            s = K - 1 - j
            d_u1 = d_u1 + _shift_up(dgc, dgn, s, False) * cw[j:j + 1]
            dws.append(_colsum(dgc * _shift_down(u1c, u1p, s, sf)))
        dws.append(_colsum(dgc * u1c))
        d_cw = _pad8(jnp.concatenate(dws, axis=0))
        return [jnp.concatenate([d_u1, d_u2], axis=1)], [d_cw, _colsum(dgc)]

    (d_uf,), (d_cw, d_cb) = _tile_call(
        body, rows=[uf, d_hid], prevs=[0], nexts=[0, 1], fulls=[p["conv_w"], p["conv_b"].reshape(1, F)],
        row_outs=[(2 * F, BF16)], acc_outs=[(8, F), (1, F)], tm=min(256, T), T=T, name=f"ffn{tag}_act_bwd")
    d_hf = _matmul(d_uf, p["w_up"], mode="nt", name=f"ffn{tag}_up_dx")
    d_w_up = _matmul(hf, d_uf, mode="tn", name=f"ffn{tag}_up_dw", out_cols_by_chip=True)
    dx, d_norm = _rms_bwd(x, p["norm"], d_hf, dy, T, f"ffn{tag}_norm_bwd")
    grads = {"norm": d_norm.reshape(D), "w_up": d_w_up, "conv_w": d_cw[:K], "conv_b": d_cb.reshape(F), "w_down": d_w_down}
    return dx, grads


def _lru_conv(u2, prev, cw, cb, sf):
    return _ffn_conv(u2, prev, cw, cb, sf)


def _lru_pre(xr, wbd, gb):
    return jnp.dot(xr.astype(MXU_DTYPE), wbd, preferred_element_type=F32) + gb


def _lru_gates(xr, pre, lam):
    D = xr.shape[1]
    r_gate, i_gate = _sigmoid(pre[:, :D]), _sigmoid(pre[:, D:])
    log_a = -LRU_C * r_gate * _softplus(-lam)
    a = jnp.exp(log_a)
    mult = jnp.sqrt(_neg_expm1(2.0 * log_a))
    return a, mult * (i_gate * xr)


def _lru_scan(a, b, B, T):
    M, D = a.shape
    cw = _pick(D, 256)
    ng = T // 8

    def kern(a_ref, b_ref, o_ref):
        row = lax.broadcasted_iota(jnp.int32, (8, cw), 0)

        def step(g, carry):
            sl = pl.ds(pl.multiple_of(g * 8, 8), 8)
            a8, b8 = a_ref[sl, :], b_ref[sl, :]
            for s in (1, 2, 4):
                a_sh = jnp.where(row >= s, pltpu.roll(a8, s, 0), 1.0)
                b_sh = jnp.where(row >= s, pltpu.roll(b8, s, 0), 0.0)
                b8 = a8 * b_sh + b8
                a8 = a8 * a_sh
            h8 = a8 * carry + b8
            o_ref[sl, :] = h8
            return jnp.broadcast_to(h8[7:8, :], (8, cw))

        lax.fori_loop(0, ng, step, jnp.zeros((8, cw), F32))

    spec = pl.BlockSpec((T, cw), lambda b, c: (b, c))
    return pl.pallas_call(
        kern, name="lru_scan", grid=(B, D // cw), in_specs=[spec, spec], out_specs=spec,
        out_shape=jax.ShapeDtypeStruct((M, D), F32), compiler_params=_cparams(("parallel", "parallel")),
    )(a, b)


def _lru_scan_bwd(a, hs, dhs, B, T):
    M, D = a.shape
    cw = _pick(D, 256)
    ng = T // 8

    def kern(a_ref, h_ref, d_ref, g_ref, da_ref):
        row = lax.broadcasted_iota(jnp.int32, (8, cw), 0)

        def step(k, carry):
            g_next, a_next = carry
            g = ng - 1 - k
            sl = pl.ds(pl.multiple_of(g * 8, 8), 8)
            a8, d8, h8 = a_ref[sl, :], d_ref[sl, :], h_ref[sl, :]
            c8 = jnp.where(row < 7, pltpu.roll(a8, 7, 0), a_next)
            for s in (1, 2, 4):
                d_sh = jnp.where(row < 8 - s, pltpu.roll(d8, 8 - s, 0), 0.0)
                c_sh = jnp.where(row < 8 - s, pltpu.roll(c8, 8 - s, 0), 1.0)
                d8 = d8 + c8 * d_sh
                c8 = c8 * c_sh
            G8 = d8 + c8 * g_next
            gp = jnp.maximum(g - 1, 0)
            hp8 = h_ref[pl.ds(pl.multiple_of(gp * 8, 8), 8), :]
            hp_last = jnp.where(g > 0, jnp.broadcast_to(hp8[7:8, :], (8, cw)), 0.0)
            hprev = jnp.where(row >= 1, pltpu.roll(h8, 1, 0), hp_last)
            g_ref[sl, :] = G8
            da_ref[sl, :] = G8 * hprev
            return jnp.broadcast_to(G8[0:1, :], (8, cw)), jnp.broadcast_to(a8[0:1, :], (8, cw))

        z = jnp.zeros((8, cw), F32)
        lax.fori_loop(0, ng, step, (z, z))

    spec = pl.BlockSpec((T, cw), lambda b, c: (b, c))
    sh = jax.ShapeDtypeStruct((M, D), F32)
    return pl.pallas_call(
        kern, name="lru_scan_bwd", grid=(B, D // cw), in_specs=[spec, spec, spec], out_specs=[spec, spec],
        out_shape=[sh, sh], compiler_params=_cparams(("parallel", "parallel")),
    )(a, hs, dhs)


def _lru_fwd(x, p, B, T):
    M, D = x.shape
    h0 = _rms_fwd(x, p["norm"], T, "lru_norm")
    u0 = _matmul(h0, p["w_in"], bias=p["b_in"], name="lru_in")
    fulls = [p["conv_w"], p["conv_b"].reshape(1, D), p["wbd"], p["gate_b"].reshape(1, 2 * D), p["lam"].reshape(1, D)]

    def body(rows, prevs, nexts, fulls, sf, sl):
        xr = _lru_conv(rows[0][:, D:], prevs[0][:, D:], _f(fulls[0]), _f(fulls[1]), sf)
        a, bt = _lru_gates(xr, _lru_pre(xr, fulls[2][...], _f(fulls[3])), _f(fulls[4]))
        return [a, bt], []

    (a, bt), _ = _tile_call(body, rows=[u0], prevs=[0], fulls=fulls, row_outs=[(D, F32), (D, F32)],
                            tm=min(256, T), T=T, name="lru_gates")
    hs = _lru_scan(a, bt, B, T)

    def body2(rows, prevs, nexts, fulls, sf, sl):
        return [rows[0][...] * _gelu(rows[1][:, :D])], []

    (out,), _ = _tile_call(body2, rows=[hs, u0], row_outs=[(D, BF16)], tm=min(512, T), T=T, name="lru_mix")
    y = _matmul(out, p["w_out"], bias=p["b_out"], residual=x, name="lru_out")
    return y, (x, h0, u0, a, hs, out)


def _lru_bwd(dy, saved, p, B, T):
    x, h0, u0, a, hs, out = saved
    M, D = x.shape
    K = p["conv_w"].shape[0]
    d_out = _matmul(dy, p["w_out"], mode="nt", name="lru_out_dx")
    d_w_out = _matmul(out, dy, mode="tn", name="lru_out_dw")

    def body(rows, prevs, nexts, fulls, sf, sl):
        do, h, u, dyv = rows[0][...], rows[1][...], rows[2][:, :D], rows[3][...]
        act, dact = jax.jvp(_gelu, (u,), (jnp.ones_like(u),))
        return [do * act, do * h * dact], [_colsum(dyv)]

    (d_hs, d_u1), (d_b_out,) = _tile_call(body, rows=[d_out, hs, u0, dy], row_outs=[(D, F32), (D, F32)],
                                          acc_outs=[(1, D)], tm=min(512, T), T=T, name="lru_mix_bwd")
    g_b, d_a = _lru_scan_bwd(a, hs, d_hs, B, T)
    fulls = [p["conv_w"], p["conv_b"].reshape(1, D), p["wbd"], p["gate_b"].reshape(1, 2 * D), p["lam"].reshape(1, D)]

    def body3(rows, prevs, nexts, fulls, sf, sl):
        u, gb_c, da_c, du1 = rows
        cw, cb, wbd, gbias, lam = _f(fulls[0]), _f(fulls[1]), fulls[2][...], _f(fulls[3]), _f(fulls[4])
        tm = u.shape[0]
        u2c, u2p, u2n = u[:, D:], prevs[0][:, D:], nexts[0][:, D:]
        xr_c = _lru_conv(u2c, u2p, cw, cb, sf)
        xr_n = _lru_conv(u2n, u2c[tm - HALO:], cw, cb, False)
        nt = (((1,), (1,)), ((), ()))
        _, vjp_c = jax.vjp(_lru_gates, xr_c, _lru_pre(xr_c, wbd, gbias), lam)
        dxr_c, dpre_c, d_lam = vjp_c((da_c[...], gb_c[...]))
        dpre_cb = dpre_c.astype(MXU_DTYPE)
        dxr_c = dxr_c + lax.dot_general(dpre_cb, wbd, nt, preferred_element_type=F32)
        d_wbd = lax.dot_general(xr_c.astype(MXU_DTYPE), dpre_cb, (((0,), (0,)), ((), ())), preferred_element_type=F32)
        d_gbias = _colsum(dpre_c)
        _, vjp_n = jax.vjp(lambda t, q: _lru_gates(t, q, lam), xr_n, _lru_pre(xr_n, wbd, gbias))
        dxr_n, dpre_n = vjp_n((nexts[2][...], nexts[1][...]))
        dxr_n = dxr_n + lax.dot_general(dpre_n.astype(MXU_DTYPE), wbd, nt, preferred_element_type=F32)
        d_u2 = dxr_c * cw[K - 1:K]
        dws = []
        for j in range(K - 1):
            s = K - 1 - j
            d_u2 = d_u2 + _shift_up(dxr_c, dxr_n, s, sl) * cw[j:j + 1]
            dws.append(_colsum(dxr_c * _shift_down(u2c, u2p, s, sf)))
        dws.append(_colsum(dxr_c * u2c))
        d_u = jnp.concatenate([du1[...], d_u2], axis=1)
        return [d_u], [_pad8(jnp.concatenate(dws, axis=0)), _colsum(dxr_c), d_wbd, d_gbias, d_lam,
                       _colsum(d_u)]

    (d_u0,), (d_cw, d_cb, d_wbd, d_gb, d_lam, d_b_in) = _tile_call(
        body3, rows=[u0, g_b, d_a, d_u1], prevs=[0], nexts=[0, 1, 2], fulls=fulls, row_outs=[(2 * D, BF16)],
        acc_outs=[(8, D), (1, D), (D, 2 * D), (1, 2 * D), (1, D), (1, 2 * D)], tm=min(256, T), T=T, name="lru_gates_bwd")
    d_h0 = _matmul(d_u0, p["w_in"], mode="nt", name="lru_in_dx")
    d_w_in = _matmul(h0, d_u0, mode="tn", name="lru_in_dw", out_cols_by_chip=True)
    dx, d_norm = _rms_bwd(x, p["norm"], d_h0, dy, T, "lru_norm_bwd")
    grads = {"norm": d_norm.reshape(D), "w_in": d_w_in, "b_in": d_b_in.reshape(2 * D), "conv_w": d_cw[:K],
             "conv_b": d_cb.reshape(D), "wbd": d_wbd, "gate_b": d_gb.reshape(2 * D), "lam": d_lam.reshape(D),
             "w_out": d_w_out, "b_out": d_b_out.reshape(D)}
    return dx, grads


def _rwkv_mix(xc, xp, norm, mix, sf):
    h = _rms(xc, norm)
    hp = _rms(xp, norm)
    xx = _shift_down(h, hp, 1, sf) - h
    return h, xx


def _rwkv_pre(k, xw, xa, xg, w0, w1, w2, a0, a1, a2, g1, g2, k_k, k_a, e, et):
    wl = -_softplus(-(w0 + _bdot(jnp.tanh(_bdot(xw, w1)), w2))) - 0.5
    decay = jnp.exp(-jnp.exp(wl))
    a = _sigmoid(a0 + _bdot(_bdot(xa, a1), a2))
    g = _bdot(_sigmoid(_bdot(xg, g1)), g2)
    kk = k * k_k
    nrm = jnp.sqrt(_head_sum(kk * kk, e, et))
    kk = kk / jnp.maximum(nrm, 1e-12)
    k2 = k * (1.0 + (a - 1.0) * k_a)
    return decay, k2, -kk, kk * a, g


def _rwkv_post(y, r, k2, v, g, ln_w, ln_b, r_k, e, et):
    inv = 1.0 / HEAD
    mu = _head_sum(y, e, et) * inv
    yc = y - mu
    var = _head_sum(yc * yc, e, et) * inv
    yn = yc * lax.rsqrt(var + GN_EPS) * ln_w + ln_b
    bonus = _head_sum(r * k2 * r_k, e, et) * v
    return (yn + bonus) * g


def _seg_lane_sums(x, lo_mask):
    s0 = jnp.sum(jnp.where(lo_mask, x, 0.0), axis=1, keepdims=True)
    s1 = jnp.sum(jnp.where(lo_mask, 0.0, x), axis=1, keepdims=True)
    return s0, s1


def _seg_lane_sum(x, lo_mask):
    s0, s1 = _seg_lane_sums(x, lo_mask)
    return jnp.where(lo_mask, s0, s1)


def _pair_consts():
    lane = lax.broadcasted_iota(jnp.int32, (HEAD, 128), 1)
    sub = lax.broadcasted_iota(jnp.int32, (HEAD, 128), 0)
    return lane < HEAD, (jnp.bitwise_and(lane, HEAD - 1) == sub).astype(F32)


def _pair_ones():
    head = jnp.arange(128) // HEAD
    blk = (head[:, None] == head[None, :]).astype(MXU_DTYPE)
    return jnp.concatenate([blk, blk], axis=0)


def _split_rows(x):
    hi = x.astype(MXU_DTYPE).astype(F32)
    return hi, x - hi


def _split_lhs(x):
    hi = x.astype(MXU_DTYPE)
    return jnp.concatenate([hi, (x - hi.astype(F32)).astype(MXU_DTYPE)], axis=1)


def _spread_lhs(diag, hi_row, mid_row):
    return jnp.concatenate([diag * hi_row, diag * mid_row], axis=1).astype(MXU_DTYPE)


def _rwkv_scan(r, w, k, v, a, b, B, T):
    M, D = r.shape
    HP, PG, TC, NC, chains = _scan_plan(B, T, D)
    NS = len(chains) * 8

    NG = TC // 8

    def kern(r_ref, w_ref, k_ref, v_ref, a_ref, b_ref, ones_ref, y_ref, st_ref, S_ref, lv_ref, rv_ref, ly_ref, ry_ref):
        c = pl.program_id(1)

        @pl.when(c == 0)
        def _():
            S_ref[...] = jnp.zeros_like(S_ref)

        lo, diag = _pair_consts()
        row8 = lax.broadcasted_iota(jnp.int32, (8, 128), 0)

        def blk(idx):
            return pl.ds(idx * HEAD, HEAD)

        def rows_of(gi):
            return pl.ds(pl.multiple_of(gi * 8, 8), 8)

        def spread(gi, slot):
            for ci, (bi, p) in enumerate(chains):
                hi, mid = _split_rows(v_ref[bi, rows_of(gi), p * 128:(p + 1) * 128])
                for j in range(8):
                    lv_ref[slot, blk(ci * 8 + j), :] = _spread_lhs(diag, hi[j:j + 1, :], mid[j:j + 1, :])
            rv_ref[slot] = jnp.dot(lv_ref[slot], ones_ref[...], preferred_element_type=F32)

        def recur(gi, slot):
            sl = rows_of(gi)
            tiles = [[ref[bi, sl, p * 128:(p + 1) * 128] for ref in (r_ref, w_ref, k_ref, a_ref, b_ref)]
                     for bi, p in chains]
            S = [S_ref[ci] for ci in range(len(chains))]
            for j in range(8):
                for ci, (bi, p) in enumerate(chains):
                    r8, w8, k8, a8, b8 = tiles[ci]
                    idx = ci * 8 + j
                    st_ref[p, bi, gi * 8 + j] = S[ci]
                    sa = _seg_lane_sum(S[ci] * a8[j:j + 1, :], lo)
                    S[ci] = S[ci] * w8[j:j + 1, :] + sa * b8[j:j + 1, :] + rv_ref[slot, blk(idx), :] * k8[j:j + 1, :]
                    ly_ref[slot, blk(idx), :] = _split_lhs(S[ci] * r8[j:j + 1, :])
            for ci in range(len(chains)):
                S_ref[ci] = S[ci]

        def emit(gi, slot):
            ry_ref[slot] = jnp.dot(ly_ref[slot], ones_ref[...], preferred_element_type=F32)
            for ci, (bi, p) in enumerate(chains):
                y8 = jnp.zeros((8, 128), F32)
                for j in range(8):
                    y8 = jnp.where(row8 == j, _colsum(diag * ry_ref[slot, blk(ci * 8 + j), :]), y8)
                y_ref[bi, rows_of(gi), p * 128:(p + 1) * 128] = y8

        spread(0, 0)
        ly_ref[1] = jnp.zeros_like(ly_ref[1])

        def two_groups(m, _):
            g0, g1 = 2 * m, 2 * m + 1
            spread(g1, 1)
            recur(g0, 0)
            emit(jnp.maximum(g0 - 1, 0), 1)
            spread(jnp.minimum(g1 + 1, NG - 1), 0)
            recur(g1, 1)
            emit(g0, 0)
            return 0

        lax.fori_loop(0, NG // 2, two_groups, 0)
        emit(NG - 1, 1)

    spec = pl.BlockSpec((B, TC, 128 * PG), lambda hp, c: (0, c, hp))
    st_spec = pl.BlockSpec((PG, B, TC, HEAD, 128), lambda hp, c: (hp, 0, c, 0, 0))
    y, st = pl.pallas_call(
        kern, name="rwkv_scan", grid=(HP // PG, NC),
        in_specs=[spec] * 6 + [pl.BlockSpec((256, 128), lambda hp, c: (0, 0))], out_specs=[spec, st_spec],
        out_shape=[jax.ShapeDtypeStruct((B, T, D), F32), jax.ShapeDtypeStruct((HP, B, T, HEAD, 128), F32)],
        scratch_shapes=[pltpu.VMEM((len(chains), HEAD, 128), F32),
                        pltpu.VMEM((2, NS * HEAD, 256), MXU_DTYPE), pltpu.VMEM((2, NS * HEAD, 128), F32),
                        pltpu.VMEM((2, NS * HEAD, 256), MXU_DTYPE), pltpu.VMEM((2, NS * HEAD, 128), F32)],
        compiler_params=_cparams(("parallel", "arbitrary")),
    )(*[x.reshape(B, T, D) for x in (r, w, k, v, a, b)], _pair_ones())
    return y.reshape(M, D), st


def _scan_plan(B, T, D):
    HP = D // 128
    PG = 1
    TC = min(64, T)
    assert TC % 16 == 0 and T % TC == 0
    return HP, PG, TC, T // TC, [(bi, p) for bi in range(B) for p in range(PG)]


def _rwkv_scan_bwd(r, w, k, v, a, b, st, dy, B, T):
    M, D = r.shape
    HP, PG, TC, NC, chains = _scan_plan(B, T, D)
    NS = len(chains) * 8

    NG = TC // 8

    def kern(r_ref, w_ref, k_ref, v_ref, a_ref, b_ref, st_ref, dy_ref, ones_ref,
             dr_ref, dw_ref, dk_ref, dv_ref, da_ref, db_ref, dS_ref, lp_ref, rp_ref, lq_ref, rq_ref):
        c = pl.program_id(1)

        @pl.when(c == 0)
        def _():
            dS_ref[...] = jnp.zeros_like(dS_ref)

        lo, diag = _pair_consts()
        row8 = lax.broadcasted_iota(jnp.int32, (8, 128), 0)

        def blk(idx):
            return pl.ds(idx * HEAD, HEAD)

        def rows_of(gi):
            return pl.ds(pl.multiple_of(gi * 8, 8), 8)

        def spread(gi, slot):
            sl = rows_of(gi)
            for ci, (bi, p) in enumerate(chains):
                lanes = slice(p * 128, (p + 1) * 128)
                vh, vm = _split_rows(v_ref[bi, sl, lanes])
                dh, dm = _split_rows(dy_ref[bi, sl, lanes])
                a8 = a_ref[bi, sl, lanes]
                for j in range(8):
                    idx = ci * 8 + j
                    lp_ref[slot, blk(idx), :] = _spread_lhs(diag, vh[j:j + 1, :], vm[j:j + 1, :])
                    lp_ref[slot, blk(NS + idx), :] = _spread_lhs(diag, dh[j:j + 1, :], dm[j:j + 1, :])
                    lp_ref[slot, blk(2 * NS + idx), :] = _split_lhs(st_ref[p, bi, gi * 8 + j] * a8[j:j + 1, :])
            rp_ref[slot] = jnp.dot(lp_ref[slot], ones_ref[...], preferred_element_type=F32)

        def recur(gi, slot):
            sl = rows_of(gi)
            tiles = [[ref[bi, sl, p * 128:(p + 1) * 128] for ref in (r_ref, w_ref, k_ref, a_ref, b_ref)]
                     for bi, p in chains]
            dS = [dS_ref[ci] for ci in range(len(chains))]
            acc = [[jnp.zeros((8, 128), F32) for _ in range(5)] for _ in chains]
            for j in range(7, -1, -1):
                for ci, (bi, p) in enumerate(chains):
                    r8, w8, k8, a8, b8 = tiles[ci]
                    rj, wj, kj, aj, bj = r8[j:j + 1, :], w8[j:j + 1, :], k8[j:j + 1, :], a8[j:j + 1, :], b8[j:j + 1, :]
                    idx = ci * 8 + j
                    Sp = st_ref[p, bi, gi * 8 + j]
                    vb, dyb, sa = rp_ref[slot, blk(idx), :], rp_ref[slot, blk(NS + idx), :], rp_ref[slot, blk(2 * NS + idx), :]
                    St = Sp * wj + sa * bj + vb * kj
                    d = dS[ci] + dyb * rj
                    dsa = _seg_lane_sum(d * bj, lo)
                    lq_ref[slot, blk(idx), :] = _split_lhs(d * kj)
                    rows = (_colsum(St * dyb), _colsum(d * Sp), _colsum(d * vb), _colsum(Sp * dsa), _colsum(d * sa))
                    acc[ci] = [jnp.where(row8 == j, rw, a8_) for rw, a8_ in zip(rows, acc[ci], strict=True)]
                    dS[ci] = d * wj + dsa * aj
            for ci, (bi, p) in enumerate(chains):
                dS_ref[ci] = dS[ci]
                for ref, a8_ in zip((dr_ref, dw_ref, dk_ref, da_ref, db_ref), acc[ci], strict=True):
                    ref[bi, sl, p * 128:(p + 1) * 128] = a8_

        def emit(gi, slot):
            rq_ref[slot] = jnp.dot(lq_ref[slot], ones_ref[...], preferred_element_type=F32)
            for ci, (bi, p) in enumerate(chains):
                dv8 = jnp.zeros((8, 128), F32)
                for j in range(8):
                    dv8 = jnp.where(row8 == j, _colsum(diag * rq_ref[slot, blk(ci * 8 + j), :]), dv8)
                dv_ref[bi, rows_of(gi), p * 128:(p + 1) * 128] = dv8

        spread(NG - 1, 0)
        lq_ref[1] = jnp.zeros_like(lq_ref[1])

        def two_groups(m, _):
            g0, g1 = NG - 1 - 2 * m, NG - 2 - 2 * m
            spread(g1, 1)
            recur(g0, 0)
            emit(jnp.minimum(g0 + 1, NG - 1), 1)
            spread(jnp.maximum(g1 - 1, 0), 0)
            recur(g1, 1)
            emit(g0, 0)
            return 0

        lax.fori_loop(0, NG // 2, two_groups, 0)
        emit(0, 1)

    spec = pl.BlockSpec((B, TC, 128 * PG), lambda hp, c: (0, NC - 1 - c, hp))
    st_spec = pl.BlockSpec((PG, B, TC, HEAD, 128), lambda hp, c: (hp, 0, NC - 1 - c, 0, 0))
    sh = jax.ShapeDtypeStruct((B, T, D), F32)
    outs = pl.pallas_call(
        kern, name="rwkv_scan_bwd", grid=(HP // PG, NC),
        in_specs=[spec] * 6 + [st_spec, spec, pl.BlockSpec((256, 128), lambda hp, c: (0, 0))], out_specs=[spec] * 6,
        out_shape=[sh] * 6,
        scratch_shapes=[pltpu.VMEM((len(chains), HEAD, 128), F32),
                        pltpu.VMEM((2, 3 * NS * HEAD, 256), MXU_DTYPE), pltpu.VMEM((2, 3 * NS * HEAD, 128), F32),
                        pltpu.VMEM((2, NS * HEAD, 256), MXU_DTYPE), pltpu.VMEM((2, NS * HEAD, 128), F32)],
        compiler_params=_cparams(("parallel", "arbitrary")),
    )(*[x.reshape(B, T, D) for x in (r, w, k, v, a, b)], st, dy.reshape(B, T, D), _pair_ones())
    return [o.reshape(M, D) for o in outs]


def _head_mats(D):
    ch = jnp.arange(D) // HEAD
    e = (ch[:, None] == jnp.arange(128)[None, :]).astype(F32)
    return e, e.T


def _rwkv_fwd(x, p, B, T):
    M, D = x.shape
    e, et = _head_mats(D)
    norm = p["norm"].reshape(1, D)

    def body(rows, prevs, nexts, fulls, sf, sl):
        h, xx = _rwkv_mix(rows[0][...], prevs[0][...], _f(fulls[0]), None, sf)
        mix = _f(fulls[1])
        return [h + xx * mix[i:i + 1] for i in range(6)], []

    xs, _ = _tile_call(body, rows=[x], prevs=[0], fulls=[norm, _pad8(p["mix"])], row_outs=[(D, BF16)] * 6,
                       tm=min(256, T), T=T, name="rwkv_mix")
    r = _matmul(xs[0], p["w_r"], name="rwkv_r")
    k = _matmul(xs[1], p["w_k"], name="rwkv_k")
    v = _matmul(xs[2], p["w_v"], name="rwkv_v")
    pre_fulls = [p["w0"].reshape(1, D), p["w1"], p["w2"], p["a0"].reshape(1, D), p["a1"], p["a2"], p["g1"], p["g2"],
                 p["k_k"].reshape(1, D), p["k_a"].reshape(1, D), e, et]

    def body2(rows, prevs, nexts, fulls, sf, sl):
        outs = _rwkv_pre(rows[0][...], _f(rows[1]), _f(rows[2]), _f(rows[3]), *[f[...] for f in fulls])
        return list(outs), []

    (decay, k2, kkn, bb, g), _ = _tile_call(body2, rows=[k, xs[3], xs[4], xs[5]], fulls=pre_fulls,
                                            row_outs=[(D, F32)] * 5, tm=min(256, T), T=T, name="rwkv_pre")
    y, st = _rwkv_scan(r, decay, k2, v, kkn, bb, B, T)
    post_fulls = [p["ln_w"].reshape(1, D), p["ln_b"].reshape(1, D), p["r_k"].reshape(1, D), e, et]

    def body3(rows, prevs, nexts, fulls, sf, sl):
        return [_rwkv_post(*[rr[...] for rr in rows], *[f[...] for f in fulls])], []

    (z,), _ = _tile_call(body3, rows=[y, r, k2, v, g], fulls=post_fulls, row_outs=[(D, BF16)], tm=min(256, T), T=T,
                         name="rwkv_post")
    out = _matmul(z, p["w_out"], residual=x, name="rwkv_out")
    return out, (x, xs, r, k, v, decay, k2, kkn, bb, g, y, st, z)


def _rwkv_bwd(dout, saved, p, B, T):
    x, xs, r, k, v, decay, k2, kkn, bb, g, y, st, z = saved
    M, D = x.shape
    e, et = _head_mats(D)
    d_z = _matmul(dout, p["w_out"], mode="nt", name="rwkv_out_dx")
    d_w_out = _matmul(z, dout, mode="tn", name="rwkv_out_dw")
    post_fulls = [p["ln_w"].reshape(1, D), p["ln_b"].reshape(1, D), p["r_k"].reshape(1, D), e, et]

    def body(rows, prevs, nexts, fulls, sf, sl):
        prim = [rr[...] for rr in rows[:5]] + [f[...] for f in fulls]
        _, vjp = jax.vjp(_rwkv_post, *prim)
        ct = vjp(rows[5][...])
        return list(ct[:5]), list(ct[5:8])

    (d_y, d_r1, d_k21, d_v1, d_g), (d_ln_w, d_ln_b, d_r_k) = _tile_call(
        body, rows=[y, r, k2, v, g, d_z], fulls=post_fulls, row_outs=[(D, F32)] * 5, acc_outs=[(1, D)] * 3,
        tm=min(256, T), T=T, name="rwkv_post_bwd")
    d_r2, d_w, d_k22, d_v2, d_kkn, d_bb = _rwkv_scan_bwd(r, decay, k2, v, kkn, bb, st, d_y, B, T)
    pre_fulls = [p["w0"].reshape(1, D), p["w1"], p["w2"], p["a0"].reshape(1, D), p["a1"], p["a2"], p["g1"], p["g2"],
                 p["k_k"].reshape(1, D), p["k_a"].reshape(1, D), e, et]

    def body2(rows, prevs, nexts, fulls, sf, sl):
        prim = [rows[0][...], _f(rows[1]), _f(rows[2]), _f(rows[3])] + [f[...] for f in fulls]
        _, vjp = jax.vjp(_rwkv_pre, *prim)
        ct = vjp((rows[4][...], rows[5][...] + rows[6][...], rows[7][...], rows[8][...], rows[9][...]))
        d_r = rows[10][...] + rows[11][...]
        d_v = rows[12][...] + rows[13][...]
        return [ct[0], ct[1], ct[2], ct[3], d_r, d_v], [c.astype(F32) for c in ct[4:14]]

    acc_shapes = [f.shape for f in pre_fulls[:10]]
    (d_k, d_xw, d_xa, d_xg, d_r, d_v), pgr = _tile_call(
        body2, rows=[k, xs[3], xs[4], xs[5], d_w, d_k21, d_k22, d_kkn, d_bb, d_g, d_r1, d_r2, d_v1, d_v2],
        fulls=pre_fulls, row_outs=[(D, BF16), (D, F32), (D, F32), (D, F32), (D, BF16), (D, BF16)], acc_outs=acc_shapes,
        tm=min(256, T), T=T, name="rwkv_pre_bwd")
    d_xr = _matmul(d_r, p["w_r"], mode="nt", name="rwkv_r_dx")
    d_xk = _matmul(d_k, p["w_k"], mode="nt", name="rwkv_k_dx")
    d_xv = _matmul(d_v, p["w_v"], mode="nt", name="rwkv_v_dx")
    d_wr = _matmul(xs[0], d_r, mode="tn", name="rwkv_r_dw")
    d_wk = _matmul(xs[1], d_k, mode="tn", name="rwkv_k_dw")
    d_wv = _matmul(xs[2], d_v, mode="tn", name="rwkv_v_dw")
    norm = p["norm"].reshape(1, D)

    def body3(rows, prevs, nexts, fulls, sf, sl):
        xc, xp = rows[0][...], prevs[0][...]
        nrm, mix = _f(fulls[0]), _f(fulls[1])
        h, xx = _rwkv_mix(xc, xp, nrm, None, sf)
        dxs = [rows[1 + i][...] for i in range(6)]
        dxs_n = [nexts[i][...] for i in range(6)]
        d_h = jnp.zeros_like(h)
        d_sh = jnp.zeros_like(h)
        d_sh_n = jnp.zeros_like(dxs_n[0])
        dmix = []
        for i in range(6):
            m = mix[i:i + 1]
            d_h = d_h + dxs[i] * (1.0 - m)
            d_sh = d_sh + dxs[i] * m
            d_sh_n = d_sh_n + dxs_n[i] * m
            dmix.append(_colsum(dxs[i] * xx))
        d_h = d_h + _shift_up(d_sh, d_sh_n, 1, sl)
        _, vjp = jax.vjp(_rms, xc, nrm)
        dx, dn = vjp(d_h)
        return [dx + rows[7][...]], [dn, _pad8(jnp.concatenate(dmix, axis=0))]

    (dx,), (d_norm, d_mix) = _tile_call(
        body3, rows=[x, d_xr, d_xk, d_xv, d_xw, d_xa, d_xg, dout], prevs=[0], nexts=[1, 2, 3, 4, 5, 6],
        fulls=[norm, _pad8(p["mix"])], row_outs=[(D, F32)], acc_outs=[(1, D), (8, D)], tm=min(256, T), T=T,
        name="rwkv_mix_bwd")
    names = ["w0", "w1", "w2", "a0", "a1", "a2", "g1", "g2", "k_k", "k_a"]
    grads = {n: gr.reshape(p[n].shape) for n, gr in zip(names, pgr, strict=True)}
    grads.update({"norm": d_norm.reshape(D), "mix": d_mix[:6], "w_r": d_wr, "w_k": d_wk, "w_v": d_wv,
                  "r_k": d_r_k.reshape(p["r_k"].shape), "ln_w": d_ln_w.reshape(D), "ln_b": d_ln_b.reshape(D),
                  "w_out": d_w_out})
    return dx, grads


def _loss_head(x, g, tgt, T):
    M, D = x.shape

    def body(rows, prevs, nexts, fulls, sf, sl):
        xv, gv = rows[0][...], _f(fulls[0])
        yv, vjp = jax.vjp(_rms, xv, gv)
        err = yv - rows[1][...]
        dx, dg = vjp(err * (1.0 / D))
        part = jnp.sum(_colsum(err * err), axis=1, keepdims=True) * (0.5 / D)
        return [dx], [dg, jnp.broadcast_to(part, (1, 128))]

    (dx,), (dg, loss) = _tile_call(body, rows=[x, tgt], fulls=[g.reshape(1, D)], row_outs=[(D, F32)],
                                   acc_outs=[(1, D), (1, 128)], tm=min(512, T), T=T, name="loss_head")
    return loss[0, 0], dx, dg.reshape(D)


def _local_step(x3, tgt3, P):
    B, T, D = x3.shape
    x, tgt = x3.reshape(B * T, D), tgt3.reshape(B * T, D)
    x1, s_lru = _lru_fwd(x, P["lru"], B, T)
    x2, s_f0 = _ffn_fwd(x1, P["ffn0"], T, "0")
    x3_, s_rw = _rwkv_fwd(x2, P["rwkv"], B, T)
    x4, s_f1 = _ffn_fwd(x3_, P["ffn1"], T, "1")
    loss, d4, d_fn = _loss_head(x4, P["final_norm"], tgt, T)
    d3, g_f1 = _ffn_bwd(d4, s_f1, P["ffn1"], T, "1")
    d2, g_rw = _rwkv_bwd(d3, s_rw, P["rwkv"], B, T)
    d1, g_f0 = _ffn_bwd(d2, s_f0, P["ffn0"], T, "0")
    d0, g_lru = _lru_bwd(d1, s_lru, P["lru"], B, T)
    return loss, d0.reshape(B, T, D), {"lru": g_lru, "ffn0": g_f0, "rwkv": g_rw, "ffn1": g_f1, "final_norm": d_fn}


WEIGHTS = ['lru_norm', 'lru_w_in', 'lru_b_in', 'lru_conv_w', 'lru_conv_b', 'lru_gate_w', 'lru_gate_b', 'lru_lambda',
           'lru_w_out', 'lru_b_out', 'rwkv_norm', 'rwkv_mix', 'rwkv_w_rkv', 'rwkv_w0', 'rwkv_w1', 'rwkv_w2', 'rwkv_a0',
           'rwkv_a1', 'rwkv_a2', 'rwkv_g1', 'rwkv_g2', 'rwkv_k_k', 'rwkv_k_a', 'rwkv_r_k', 'rwkv_ln_w', 'rwkv_ln_b',
           'rwkv_w_out', 'ffn_norm', 'ffn_w_up', 'ffn_conv_w', 'ffn_conv_b', 'ffn_w_down', 'final_norm']
SHARD_AXIS = {'lru_w_in': 2, 'lru_conv_w': 2, 'lru_w_out': 1, 'rwkv_norm': 1, 'rwkv_mix': 2, 'rwkv_w_rkv': 2,
              'rwkv_w0': 1, 'rwkv_w1': 1, 'rwkv_w2': 2, 'rwkv_a0': 1, 'rwkv_a1': 1, 'rwkv_a2': 2, 'rwkv_g1': 1,
              'rwkv_g2': 2, 'rwkv_k_k': 1, 'rwkv_k_a': 1, 'rwkv_ln_w': 1, 'rwkv_ln_b': 1, 'rwkv_w_out': 1,
              'ffn_w_up': 2, 'ffn_conv_w': 2, 'ffn_w_down': 1}
MXU_WEIGHTS = ('lru_w_in', 'lru_w_out', 'rwkv_w_rkv', 'rwkv_w_out', 'ffn_w_up', 'ffn_w_down')
N_CHIPS = 4
LANES = 1024


def _pack(arrs, dtype, row_mult):
    flat = jnp.concatenate([a.reshape(-1).astype(dtype) for a in arrs])
    n = flat.shape[0]
    unit = row_mult * LANES
    tot = -(-n // unit) * unit
    if tot > n:
        flat = jnp.concatenate([flat, jnp.zeros((tot - n,), dtype)])
    return flat.reshape(tot // LANES, LANES)


def _unpack(buf, shapes):
    flat = buf.reshape(-1)
    out, off = [], 0
    for s in shapes:
        n = 1
        for d in s:
            n *= d
        out.append(flat[off:off + n].reshape(s))
        off += n
    return out


def _to_shards(full, axis):
    return jnp.stack(jnp.split(full, N_CHIPS, axis=axis))


MESH_ID = pl.DeviceIdType.MESH


ANY_SPEC = pl.BlockSpec(memory_space=pl.ANY)
COMM_PARAMS = pltpu.CompilerParams(has_side_effects=True)


def _mesh_place():
    x, y, c = lax.axis_index("x"), lax.axis_index("y"), lax.axis_index("c")
    return x, y, c, 2 * x + y, [(1 - x, y), (x, 1 - y), (1 - x, 1 - y)]


def _gather_all(arrs, name):
    n = len(arrs)

    def body(*refs):
        ins, outs = refs[:n], refs[n:2 * n]
        send_sems, recv_sems, local_sems = refs[2 * n:]
        x, y, c, p, chips = _mesh_place()
        sibling = (x, y, 1 - c)

        def rows(a, which):
            h = arrs[a].shape[0] // 2
            return pl.ds(which * h, h)

        def copy(a, k, region, src, to):
            return pltpu.make_async_remote_copy(src_ref=src, dst_ref=region, send_sem=send_sems.at[a, k],
                                                recv_sem=recv_sems.at[a, k], device_id=to, device_id_type=MESH_ID)

        local = [pltpu.make_async_copy(ins[a], outs[a].at[p], local_sems.at[a]) for a in range(n)]
        for cp in local:
            cp.start()
        first = [copy(a, j, outs[a].at[p, rows(a, c)], ins[a].at[rows(a, c)], (qx, qy, c))
                 for a in range(n) for j, (qx, qy) in enumerate(chips)]
        for cp in first:
            cp.start()
        passed = []
        for a in range(n):
            for j, (qx, qy) in enumerate(chips):
                region = outs[a].at[2 * qx + qy, rows(a, c)]
                copy(a, j, region, region, (qx, qy, c)).wait_recv()
                fw = copy(a, 3 + j, region, region, sibling)
                fw.start()
                passed.append(fw)
        for a in range(n):
            for j, (qx, qy) in enumerate(chips):
                region = outs[a].at[2 * qx + qy, rows(a, 1 - c)]
                copy(a, 3 + j, region, region, sibling).wait_recv()
        for cp in first + passed:
            cp.wait_send()
        for cp in local:
            cp.wait()

    return pl.pallas_call(
        body, name=name, out_shape=[jax.ShapeDtypeStruct((N_CHIPS,) + a.shape, a.dtype) for a in arrs],
        in_specs=[ANY_SPEC] * n, out_specs=[ANY_SPEC] * n,
        scratch_shapes=[pltpu.SemaphoreType.DMA((n, 6)), pltpu.SemaphoreType.DMA((n, 6)), pltpu.SemaphoreType.DMA((n,))],
        compiler_params=COMM_PARAMS,
    )(*arrs)


def _pair_swap_all(gs, rep, name):
    n = len(gs)

    def body(*refs):
        ins, outs = refs[:n + 1], refs[n + 1:2 * n + 2]
        send_sems, recv_sems = refs[2 * n + 2:]
        x, y, c, _, _ = _mesh_place()
        copies = []
        for a in range(n + 1):
            src = ins[a]
            if a < n:
                h = gs[a].shape[1] // 2
                src = src.at[:, pl.ds((1 - c) * h, h), :]
            copies.append(pltpu.make_async_remote_copy(src_ref=src, dst_ref=outs[a], send_sem=send_sems.at[a],
                                                       recv_sem=recv_sems.at[a], device_id=(x, y, 1 - c),
                                                       device_id_type=MESH_ID))
        for cp in copies:
            cp.start()
        for cp in copies:
            cp.wait()

    shapes = [jax.ShapeDtypeStruct((N_CHIPS, g.shape[1] // 2, g.shape[2]), g.dtype) for g in gs]
    shapes.append(jax.ShapeDtypeStruct(rep.shape, rep.dtype))
    res = pl.pallas_call(
        body, name=name, out_shape=shapes, in_specs=[ANY_SPEC] * (n + 1), out_specs=[ANY_SPEC] * (n + 1),
        scratch_shapes=[pltpu.SemaphoreType.DMA((n + 1,)), pltpu.SemaphoreType.DMA((n + 1,))],
        compiler_params=COMM_PARAMS,
    )(*gs, rep)
    return res[:n], res[n]


def _chip_exchange_all(ps, rep, name):
    n = len(ps)

    def body(*refs):
        ins, outs = refs[:n + 1], refs[n + 1:2 * n + 2]
        send_sems, recv_sems, local_sems = refs[2 * n + 2:]
        x, y, c, p, chips = _mesh_place()

        def src(a, q):
            return ins[a].at[q] if a < n else ins[a]

        local = [pltpu.make_async_copy(src(a, p), outs[a].at[p], local_sems.at[a]) for a in range(n + 1)]
        for cp in local:
            cp.start()
        sends, recvs = [], []
        for a in range(n + 1):
            for j, (qx, qy) in enumerate(chips):
                q = 2 * qx + qy
                for dst, keep in ((outs[a].at[p], sends), (outs[a].at[q], recvs)):
                    keep.append(pltpu.make_async_remote_copy(
                        src_ref=src(a, q), dst_ref=dst, send_sem=send_sems.at[a, j], recv_sem=recv_sems.at[a, j],
                        device_id=(qx, qy, c), device_id_type=MESH_ID))
        for cp in sends:
            cp.start()
        for cp in recvs:
            cp.wait_recv()
        for cp in sends:
            cp.wait_send()
        for cp in local:
            cp.wait()

    shapes = [jax.ShapeDtypeStruct(g.shape, g.dtype) for g in ps]
    shapes.append(jax.ShapeDtypeStruct((N_CHIPS,) + rep.shape, rep.dtype))
    res = pl.pallas_call(
        body, name=name, out_shape=shapes, in_specs=[ANY_SPEC] * (n + 1), out_specs=[ANY_SPEC] * (n + 1),
        scratch_shapes=[pltpu.SemaphoreType.DMA((n + 1, 3)), pltpu.SemaphoreType.DMA((n + 1, 3)),
                        pltpu.SemaphoreType.DMA((n + 1,))],
        compiler_params=COMM_PARAMS,
    )(*ps, rep)
    return res[:n], res[n]


def _half_swap_all(ts, name):
    n = len(ts)

    def body(*refs):
        ins, outs = refs[:n], refs[n:2 * n]
        send_sems, recv_sems = refs[2 * n:]
        x, y, c, _, _ = _mesh_place()
        copies = [pltpu.make_async_remote_copy(src_ref=ins[a], dst_ref=outs[a], send_sem=send_sems.at[a],
                                               recv_sem=recv_sems.at[a], device_id=(x, y, 1 - c), device_id_type=MESH_ID)
                  for a in range(n)]
        for cp in copies:
            cp.start()
        for cp in copies:
            cp.wait()

    return pl.pallas_call(
        body, name=name, out_shape=[jax.ShapeDtypeStruct(t.shape, t.dtype) for t in ts],
        in_specs=[ANY_SPEC] * n, out_specs=[ANY_SPEC] * n,
        scratch_shapes=[pltpu.SemaphoreType.DMA((n,)), pltpu.SemaphoreType.DMA((n,))],
        compiler_params=COMM_PARAMS,
    )(*ts)


def _pick_rows(R, cap=256):
    for t in (512, 256, 128, 64, 32, 16, 8):
        if t <= cap and R % t == 0:
            return t
    return R


def _pair_sum(g, got, name):
    _, R, C = g.shape
    h = R // 2
    th = _pick_rows(h)

    def kern(g_ref, got_ref, o_ref):
        both = g_ref[...]
        mine = jnp.where(lax.axis_index("c") == 0, both[0], both[1])
        o_ref[...] = (mine + got_ref[...]).astype(o_ref.dtype)

    return pl.pallas_call(
        kern, name=name, grid=(N_CHIPS, h // th),
        in_specs=[pl.BlockSpec((None, 2, th, C), lambda q, i: (q, 0, i, 0)), pl.BlockSpec((None, th, C), lambda q, i: (q, i, 0))],
        out_specs=pl.BlockSpec((None, th, C), lambda q, i: (q, i, 0)),
        out_shape=jax.ShapeDtypeStruct((N_CHIPS, h, C), BF16), compiler_params=_cparams(("parallel", "parallel")),
    )(g.reshape(N_CHIPS, 2, h, C), got)


def _rep_pair_sum(rep, got, name):
    R, C = rep.shape
    tr = _pick_rows(R)

    def kern(a_ref, b_ref, o_ref):
        o_ref[...] = (a_ref[...] + b_ref[...]).astype(o_ref.dtype)

    spec = pl.BlockSpec((tr, C), lambda i: (i, 0))
    return pl.pallas_call(kern, name=name, grid=(R // tr,), in_specs=[spec, spec], out_specs=spec,
                          out_shape=jax.ShapeDtypeStruct((R, C), BF16), compiler_params=_cparams(("parallel",)))(rep, got)


def _chip_sum(arrived, name):
    _, R, C = arrived.shape
    tr = _pick_rows(R)

    def kern(a_ref, o_ref):
        acc = a_ref[0].astype(F32)
        for q in range(1, N_CHIPS):
            acc = acc + a_ref[q].astype(F32)
        o_ref[...] = acc

    return pl.pallas_call(
        kern, name=name, grid=(R // tr,), in_specs=[pl.BlockSpec((N_CHIPS, tr, C), lambda i: (0, i, 0))],
        out_specs=pl.BlockSpec((tr, C), lambda i: (i, 0)), out_shape=jax.ShapeDtypeStruct((R, C), F32),
        compiler_params=_cparams(("parallel",)),
    )(arrived)


def _adam_math(w, g, m, v):
    c1 = 1.0 / (1.0 - ADAM_B1 ** ADAM_STEP)
    c2 = 1.0 / (1.0 - ADAM_B2 ** ADAM_STEP)
    nm = ADAM_B1 * m + (1.0 - ADAM_B1) * g
    nv = ADAM_B2 * v + (1.0 - ADAM_B2) * (g * g)
    return -ADAM_LR * ((nm * c1) / (jnp.sqrt(nv * c2) + ADAM_EPS) + ADAM_WD * w), nm, nv


def _adamw_halves(w, m, v, mine, other, name):
    R, C = w.shape
    h = R // 2
    th = _pick_rows(h)
    nt = h // th

    def kern(w_ref, m_ref, v_ref, a_ref, b_ref, g_ref, d_ref, nm_ref, nv_ref):
        g = jnp.where(pl.program_id(0) == lax.axis_index("c"), a_ref[...], b_ref[...])
        d, nm, nv = _adam_math(w_ref[...], g, m_ref[...], v_ref[...])
        g_ref[...] = g
        d_ref[...] = d
        nm_ref[...] = nm
        nv_ref[...] = nv

    full = pl.BlockSpec((th, C), lambda hh, i: (hh * nt + i, 0))
    half = pl.BlockSpec((th, C), lambda hh, i: (i, 0))
    sh = jax.ShapeDtypeStruct((R, C), F32)
    return pl.pallas_call(kern, name=name, grid=(2, nt), in_specs=[full] * 3 + [half] * 2, out_specs=[full] * 4,
                          out_shape=[sh] * 4, compiler_params=_cparams(("parallel", "parallel")))(w, m, v, mine, other)


def _adamw_call(w, g, m, v, name):
    R = w.shape[0]
    tr = _pick_rows(R)
    c1 = 1.0 / (1.0 - ADAM_B1 ** ADAM_STEP)
    c2 = 1.0 / (1.0 - ADAM_B2 ** ADAM_STEP)

    def kern(w_ref, g_ref, m_ref, v_ref, d_ref, nm_ref, nv_ref):
        gv = g_ref[...]
        nm = ADAM_B1 * m_ref[...] + (1.0 - ADAM_B1) * gv
        nv = ADAM_B2 * v_ref[...] + (1.0 - ADAM_B2) * (gv * gv)
        d_ref[...] = -ADAM_LR * ((nm * c1) / (jnp.sqrt(nv * c2) + ADAM_EPS) + ADAM_WD * w_ref[...])
        nm_ref[...] = nm
        nv_ref[...] = nv

    spec = pl.BlockSpec((tr, LANES), lambda i: (i, 0))
    sh = jax.ShapeDtypeStruct((R, LANES), F32)
    return pl.pallas_call(kern, name=name, grid=(R // tr,), in_specs=[spec] * 4, out_specs=[spec] * 3,
                          out_shape=[sh] * 3, compiler_params=_cparams(("parallel",)))(w, g, m, v)


def _gate_dense(gate_w):
    _, nb, bw, _ = gate_w.shape
    eye = jnp.eye(nb, dtype=gate_w.dtype)
    dense = jnp.einsum('gncd,nm->gncmd', gate_w, eye).reshape(2, nb * bw, nb * bw)
    return jnp.concatenate([dense[0], dense[1]], axis=1)


def _gate_blocks(d_dense, nb):
    D = d_dense.shape[0]
    bw = D // nb
    g = d_dense.reshape(nb, bw, 2, nb, bw)
    return jnp.einsum('ncgnd->gncd', g)


def _step(W, M1, V1, x, tgt):
    sharded = [n for n in WEIGHTS if n in SHARD_AXIS]
    repl = [n for n in WEIGHTS if n not in SHARD_AXIS]
    small = [n for n in sharded if n not in MXU_WEIGHTS]
    D = x.shape[-1]

    def rows2d(a):
        return a.reshape(-1, a.shape[-1])

    small_buf = _pack([W[n] for n in small], F32, 16)
    gathered = _gather_all([rows2d(W[n]).astype(MXU_DTYPE) for n in MXU_WEIGHTS] + [small_buf], "gather_weights")
    mats = {n: g.reshape((N_CHIPS,) + W[n].shape[-3:]) for n, g in zip(MXU_WEIGHTS, gathered[:-1], strict=True)}
    per_chip = [_unpack(gathered[-1][q], [W[n].shape for n in small]) for q in range(N_CHIPS)]
    full = {n: jnp.concatenate([per_chip[q][i] for q in range(N_CHIPS)], axis=SHARD_AXIS[n]) for i, n in enumerate(small)}
    for n in repl:
        full[n] = W[n]

    P = {
        "lru": {"norm": full["lru_norm"][0], "w_in": _W(mats["lru_w_in"], 0, "col"), "b_in": full["lru_b_in"][0],
                "conv_w": full["lru_conv_w"][0], "conv_b": full["lru_conv_b"][0],
                "wbd": _gate_dense(full["lru_gate_w"][0]).astype(MXU_DTYPE), "gate_b": full["lru_gate_b"][0].reshape(-1),
                "lam": full["lru_lambda"][0], "w_out": _W(mats["lru_w_out"], 0, "row"), "b_out": full["lru_b_out"][0]},
        "rwkv": {"norm": full["rwkv_norm"][0], "mix": full["rwkv_mix"][0],
                 "w_r": _W(mats["rwkv_w_rkv"], 0, "row"), "w_k": _W(mats["rwkv_w_rkv"], 1, "row"),
                 "w_v": _W(mats["rwkv_w_rkv"], 2, "row"),
                 "w0": full["rwkv_w0"][0], "w1": full["rwkv_w1"][0], "w2": full["rwkv_w2"][0], "a0": full["rwkv_a0"][0],
                 "a1": full["rwkv_a1"][0], "a2": full["rwkv_a2"][0], "g1": full["rwkv_g1"][0], "g2": full["rwkv_g2"][0],
                 "k_k": full["rwkv_k_k"][0], "k_a": full["rwkv_k_a"][0], "r_k": full["rwkv_r_k"][0],
                 "ln_w": full["rwkv_ln_w"][0], "ln_b": full["rwkv_ln_b"][0], "w_out": _W(mats["rwkv_w_out"], 0, "row")},
        "final_norm": full["final_norm"],
    }
    for l in range(2):
        P[f"ffn{l}"] = {"norm": full["ffn_norm"][l], "w_up": _W(mats["ffn_w_up"], l, "col"),
                        "conv_w": full["ffn_conv_w"][l], "conv_b": full["ffn_conv_b"][l],
                        "w_down": _W(mats["ffn_w_down"], l, "row")}

    loss, gx, G = _local_step(x, tgt, P)

    nb = W["lru_gate_w"].shape[2]
    gl, gr = G["lru"], G["rwkv"]
    gfull = {
        "lru_norm": gl["norm"][None], "lru_b_in": gl["b_in"][None],
        "lru_conv_w": gl["conv_w"][None], "lru_conv_b": gl["conv_b"][None], "lru_gate_w": _gate_blocks(gl["wbd"], nb)[None],
        "lru_gate_b": gl["gate_b"].reshape(W["lru_gate_b"].shape), "lru_lambda": gl["lam"][None],
        "lru_b_out": gl["b_out"][None],
        "rwkv_norm": gr["norm"][None], "rwkv_mix": gr["mix"][None],
        "rwkv_w0": gr["w0"][None], "rwkv_w1": gr["w1"][None], "rwkv_w2": gr["w2"][None], "rwkv_a0": gr["a0"][None],
        "rwkv_a1": gr["a1"][None], "rwkv_a2": gr["a2"][None], "rwkv_g1": gr["g1"][None], "rwkv_g2": gr["g2"][None],
        "rwkv_k_k": gr["k_k"][None], "rwkv_k_a": gr["k_a"][None], "rwkv_r_k": gr["r_k"][None],
        "rwkv_ln_w": gr["ln_w"][None], "rwkv_ln_b": gr["ln_b"][None],
        "final_norm": G["final_norm"],
    }
    for k in ("norm", "conv_w", "conv_b"):
        gfull["ffn_" + k] = jnp.stack([G["ffn0"][k], G["ffn1"][k]])

    small_g = jnp.stack([_pack([_to_shards(gfull[n], SHARD_AXIS[n])[q] for n in small], F32, 16) for q in range(N_CHIPS)])
    by_rows = lambda g: g.reshape(N_CHIPS, g.shape[0] // N_CHIPS, g.shape[1])
    pieces = [("lru_w_in", (0,), gl["w_in"]), ("lru_w_out", (0,), by_rows(gl["w_out"])),
              ("rwkv_w_rkv", (0, 0), by_rows(gr["w_r"])), ("rwkv_w_rkv", (0, 1), by_rows(gr["w_k"])),
              ("rwkv_w_rkv", (0, 2), by_rows(gr["w_v"])), ("rwkv_w_out", (0,), by_rows(gr["w_out"])),
              ("ffn_w_up", (0,), G["ffn0"]["w_up"]), ("ffn_w_up", (1,), G["ffn1"]["w_up"]),
              ("ffn_w_down", (0,), by_rows(G["ffn0"]["w_down"])), ("ffn_w_down", (1,), by_rows(G["ffn1"]["w_down"]))]
    gs = [g for _, _, g in pieces] + [small_g]
    grep = _pack([gfull[n] for n in repl], F32, 16)

    got, got_rep = _pair_swap_all(gs, grep, "reduce_pair_swap")
    pair = [_pair_sum(g, r, f"reduce_pair_sum{i}") for i, (g, r) in enumerate(zip(gs, got, strict=True))]
    pair_rep = _rep_pair_sum(grep, got_rep, "reduce_pair_sum_rep")
    arrived, arrived_rep = _chip_exchange_all(pair, pair_rep, "reduce_chips")
    mine = [_chip_sum(a, f"reduce_chip_sum{i}") for i, a in enumerate(arrived)]
    g_rp = _chip_sum(arrived_rep, "reduce_chip_sum_rep")
    other = _half_swap_all(mine, "reduce_half_swap")

    outs, parts = {}, {}
    for i, (n, idx, _) in enumerate(pieces):
        w, m, v = (rows2d(S[n][idx]) for S in (W, M1, V1))
        res = _adamw_halves(w, m, v, mine[i], other[i], f"adamw{i}")
        for kind, a in zip(("grad", "delta", "new_m", "new_v"), res, strict=True):
            parts.setdefault((kind, n), []).append(a)
    for (kind, n), lst in parts.items():
        a = lst[0] if len(lst) == 1 else jnp.stack(lst)
        outs[(kind, n)] = a.reshape(W[n].shape)
    wb, mb, vb = (_pack([S[n] for n in small], F32, 16) for S in (W, M1, V1))
    res = _adamw_halves(wb, mb, vb, mine[-1], other[-1], "adamw_small")
    for kind, buf in zip(("grad", "delta", "new_m", "new_v"), res, strict=True):
        for n, a in zip(small, _unpack(buf, [W[n].shape for n in small]), strict=True):
            outs[(kind, n)] = a
    wb, mb, vb = (_pack([S[n] for n in repl], F32, 16) for S in (W, M1, V1))
    d, nm, nv = _adamw_call(wb, g_rp, mb, vb, "adamw_repl")
    for kind, buf in (("grad", g_rp), ("delta", d), ("new_m", nm), ("new_v", nv)):
        for n, a in zip(repl, _unpack(buf, [W[n].shape for n in repl]), strict=True):
            outs[(kind, n)] = a
    loss = lax.psum(loss, ("x", "y", "c"))
    return (loss, gx, *[outs[(kind, n)] for kind in ("grad", "delta", "new_m", "new_v") for n in WEIGHTS])


def kernel(x, lru_norm, lru_w_in, lru_b_in, lru_conv_w, lru_conv_b, lru_gate_w, lru_gate_b, lru_lambda, lru_w_out, lru_b_out, rwkv_norm, rwkv_mix, rwkv_w_rkv, rwkv_w0, rwkv_w1, rwkv_w2, rwkv_a0, rwkv_a1, rwkv_a2, rwkv_g1, rwkv_g2, rwkv_k_k, rwkv_k_a, rwkv_r_k, rwkv_ln_w, rwkv_ln_b, rwkv_w_out, ffn_norm, ffn_w_up, ffn_conv_w, ffn_conv_b, ffn_w_down, final_norm, loss_target, m_lru_norm, m_lru_w_in, m_lru_b_in, m_lru_conv_w, m_lru_conv_b, m_lru_gate_w, m_lru_gate_b, m_lru_lambda, m_lru_w_out, m_lru_b_out, m_rwkv_norm, m_rwkv_mix, m_rwkv_w_rkv, m_rwkv_w0, m_rwkv_w1, m_rwkv_w2, m_rwkv_a0, m_rwkv_a1, m_rwkv_a2, m_rwkv_g1, m_rwkv_g2, m_rwkv_k_k, m_rwkv_k_a, m_rwkv_r_k, m_rwkv_ln_w, m_rwkv_ln_b, m_rwkv_w_out, m_ffn_norm, m_ffn_w_up, m_ffn_conv_w, m_ffn_conv_b, m_ffn_w_down, m_final_norm, v_lru_norm, v_lru_w_in, v_lru_b_in, v_lru_conv_w, v_lru_conv_b, v_lru_gate_w, v_lru_gate_b, v_lru_lambda, v_lru_w_out, v_lru_b_out, v_rwkv_norm, v_rwkv_mix, v_rwkv_w_rkv, v_rwkv_w0, v_rwkv_w1, v_rwkv_w2, v_rwkv_a0, v_rwkv_a1, v_rwkv_a2, v_rwkv_g1, v_rwkv_g2, v_rwkv_k_k, v_rwkv_k_a, v_rwkv_r_k, v_rwkv_ln_w, v_rwkv_ln_b, v_rwkv_w_out, v_ffn_norm, v_ffn_w_up, v_ffn_conv_w, v_ffn_conv_b, v_ffn_w_down, v_final_norm):
    given = dict(locals())
    W = {n: given[n] for n in WEIGHTS}
    M1 = {n: given["m_" + n] for n in WEIGHTS}
    V1 = {n: given["v_" + n] for n in WEIGHTS}
    return _step(W, M1, V1, x, loss_target)
```

```python
import functools

import jax
import jax.numpy as jnp
from jax import lax
from jax.experimental import pallas as pl
from jax.experimental.pallas import tpu as pltpu

F32 = jnp.float32
BF16 = jnp.bfloat16
MXU_DTYPE = BF16

HEAD = 64
LRU_C = 8.0
GN_EPS = 64e-5
RMS_EPS = 1e-6
HALO = 16
VMEM_LIMIT = 56 * 1024 * 1024

ADAM_LR, ADAM_B1, ADAM_B2, ADAM_EPS, ADAM_WD, ADAM_STEP = 0.001, 0.9, 0.999, 1e-08, 0.01, 10


def _cparams(sem):
    return pltpu.CompilerParams(dimension_semantics=sem, vmem_limit_bytes=VMEM_LIMIT)


def _pick(n, want):
    if n <= want:
        return n
    t = want
    while t >= 128:
        if n % t == 0:
            return t
        t -= 128
    return n


class _W:
    def __init__(self, arr, layer):
        self.arr, self.layer = arr, layer
        self.shape = (arr.shape[2], N_CHIPS * arr.shape[3])


def _matmul(a, b, mode="nn", bias=None, residual=None, out_dtype=F32, name="mm", tm=1024, tn=1024, tk=1024,
            out_cols_by_chip=False):
    bshape = b.shape
    if mode == "nn":
        (M, K), (K2, N) = a.shape, bshape
    elif mode == "nt":
        (M, K), (N, K2) = a.shape, bshape
    else:
        (K, M), (K2, N) = a.shape, bshape
    assert K == K2, (a.shape, bshape, mode)
    lim_n, lim_k = N, K
    if isinstance(b, _W):
        if mode == "nn":
            lim_n = b.arr.shape[3]
        else:
            lim_k = b.arr.shape[3]
    if out_cols_by_chip:
        lim_n = min(lim_n, N // N_CHIPS)
    tm, tn, tk = _pick(M, tm), _pick(lim_n, tn), _pick(lim_k, tk)
    nk = K // tk
    dims = {"nn": (((1,), (0,)), ((), ())), "nt": (((1,), (1,)), ((), ())), "tn": (((0,), (0,)), ((), ()))}[mode]
    a_spec = {"nn": pl.BlockSpec((tm, tk), lambda i, j, k: (i, k)),
              "nt": pl.BlockSpec((tm, tk), lambda i, j, k: (i, k)),
              "tn": pl.BlockSpec((tk, tm), lambda i, j, k: (k, i))}[mode]
    if isinstance(b, _W):
        lay = b.layer
        if mode == "nn":
            per = b.arr.shape[3] // tn
            b_spec = pl.BlockSpec((None, None, tk, tn), lambda i, j, k: (j // per, lay, k, j % per))
        else:
            assert mode == "nt"
            per = b.arr.shape[3] // tk
            b_spec = pl.BlockSpec((None, None, tn, tk), lambda i, j, k: (k // per, lay, j, k % per))
        b = b.arr
    else:
        b_spec = {"nn": pl.BlockSpec((tk, tn), lambda i, j, k: (k, j)),
                  "nt": pl.BlockSpec((tn, tk), lambda i, j, k: (j, k)),
                  "tn": pl.BlockSpec((tk, tn), lambda i, j, k: (k, j))}[mode]
    if out_cols_by_chip:
        opc = N // N_CHIPS // tn
        out_spec = pl.BlockSpec((None, tm, tn), lambda i, j, k: (j // opc, i, j % opc))
        out_shape = jax.ShapeDtypeStruct((N_CHIPS, M, N // N_CHIPS), out_dtype)
    else:
        out_spec = pl.BlockSpec((tm, tn), lambda i, j, k: (i, j))
        out_shape = jax.ShapeDtypeStruct((M, N), out_dtype)
    in_specs, operands = [a_spec, b_spec], [a, b]
    if bias is not None:
        in_specs.append(pl.BlockSpec((1, tn), lambda i, j, k: (0, j)))
        operands.append(bias.reshape(1, N))
    if residual is not None:
        in_specs.append(pl.BlockSpec((tm, tn), lambda i, j, k: (i, j)))
        operands.append(residual)
    has_bias, has_res = bias is not None, residual is not None

    def kern(*refs):
        a_ref, b_ref = refs[0], refs[1]
        o_ref = refs[2 + has_bias + has_res]

        def finish(r):
            pos = 2
            if has_bias:
                r = r + refs[pos][...].astype(F32)
                pos += 1
            if has_res:
                r = r + refs[pos][...].astype(F32)
            o_ref[...] = r.astype(o_ref.dtype)

        part = lax.dot_general(a_ref[...].astype(MXU_DTYPE), b_ref[...].astype(MXU_DTYPE), dims,
                               preferred_element_type=F32)
        if nk == 1:
            finish(part)
            return
        acc_ref = refs[-1]
        k = pl.program_id(2)

        @pl.when(k == 0)
        def _():
            acc_ref[...] = part

        @pl.when(jnp.logical_and(k > 0, k < nk - 1))
        def _():
            acc_ref[...] += part

        @pl.when(k == nk - 1)
        def _():
            finish(acc_ref[...] + part)

    return pl.pallas_call(
        kern, name=name,
        grid=(M // tm, N // tn, nk),
        in_specs=in_specs,
        out_specs=out_spec,
        out_shape=out_shape,
        scratch_shapes=[pltpu.VMEM((tm, tn), F32)] if nk > 1 else [],
        compiler_params=_cparams(("parallel", "parallel", "arbitrary")),
    )(*operands)


def _tile_call(body, *, rows, prevs=(), nexts=(), fulls=(), row_outs=(), acc_outs=(), tm, T, name):
    M = rows[0].shape[0]
    n_tiles, tps, hb = M // tm, T // tm, tm // HALO
    n_halo_blocks = M // HALO
    nr, npv, nnx, nf, nro, nac = len(rows), len(prevs), len(nexts), len(fulls), len(row_outs), len(acc_outs)

    def kern(*refs):
        i = pl.program_id(0)
        row_refs = refs[:nr]
        prev_refs = refs[nr:nr + npv]
        next_refs = refs[nr + npv:nr + npv + nnx]
        full_refs = refs[nr + npv + nnx:nr + npv + nnx + nf]
        out_refs = refs[nr + npv + nnx + nf:nr + npv + nnx + nf + nro]
        acc_refs = refs[nr + npv + nnx + nf + nro:]
        seq_first = (i % tps) == 0
        seq_last = (i % tps) == (tps - 1)
        outs, accs = body(row_refs, prev_refs, next_refs, full_refs, seq_first, seq_last)
        for r, o in zip(out_refs, outs, strict=True):
            r[...] = o.astype(r.dtype)
        if nac:
            @pl.when(i == 0)
            def _():
                for r in acc_refs:
                    r[...] = jnp.zeros_like(r)
            for r, a in zip(acc_refs, accs, strict=True):
                r[...] += a.astype(F32)

    in_specs = [pl.BlockSpec((tm, a.shape[1]), lambda i: (i, 0)) for a in rows]
    in_specs += [pl.BlockSpec((HALO, rows[k].shape[1]), lambda i: (jnp.maximum(i * hb - 1, 0), 0)) for k in prevs]
    in_specs += [pl.BlockSpec((HALO, rows[k].shape[1]), lambda i: (jnp.minimum((i + 1) * hb, n_halo_blocks - 1), 0))
                 for k in nexts]
    in_specs += [pl.BlockSpec(f.shape, lambda i: (0, 0)) for f in fulls]
    out_specs = [pl.BlockSpec((tm, w), lambda i: (i, 0)) for (w, _) in row_outs]
    out_specs += [pl.BlockSpec(s, lambda i: (0, 0)) for s in acc_outs]
    out_shape = [jax.ShapeDtypeStruct((M, w), dt) for (w, dt) in row_outs]
    out_shape += [jax.ShapeDtypeStruct(s, F32) for s in acc_outs]
    operands = list(rows) + [rows[k] for k in prevs] + [rows[k] for k in nexts] + list(fulls)
    res = pl.pallas_call(
        kern, name=name, grid=(n_tiles,), in_specs=in_specs, out_specs=out_specs, out_shape=out_shape,
        compiler_params=_cparams(("arbitrary",)),
    )(*operands)
    return res[:nro], res[nro:]


def _f(ref):
    return ref[...].astype(F32)


def _sigmoid(x):
    return 1.0 / (1.0 + jnp.exp(-x))


def _softplus(x):
    return jnp.maximum(x, 0.0) + jnp.log(1.0 + jnp.exp(-jnp.abs(x)))


def _neg_expm1(x):
    series = -x * (1.0 + x * (0.5 + x * (1.0 / 6.0) * (1.0 + 0.25 * x)))
    return jnp.where(x > -0.01, series, 1.0 - jnp.exp(x))


def _gelu(x):
    return 0.5 * x * (1.0 + jnp.tanh(0.7978845608028654 * (x + 0.044715 * x * x * x)))


def _rms(x, g):
    return x * lax.rsqrt(jnp.mean(x * x, axis=-1, keepdims=True) + RMS_EPS) * g


@jax.custom_vjp
def _bdot(x, w):
    return jnp.dot(x.astype(MXU_DTYPE), w.astype(MXU_DTYPE), preferred_element_type=F32)


def _bdot_fwd(x, w):
    return _bdot(x, w), (x, w)


def _bdot_bwd(res, ct):
    x, w = res
    ctb = ct.astype(MXU_DTYPE)
    dx = lax.dot_general(ctb, w.astype(MXU_DTYPE), (((1,), (1,)), ((), ())), preferred_element_type=F32)
    dw = lax.dot_general(x.astype(MXU_DTYPE), ctb, (((0,), (0,)), ((), ())), preferred_element_type=F32)
    return dx.astype(x.dtype), dw.astype(w.dtype)


_bdot.defvjp(_bdot_fwd, _bdot_bwd)


@jax.custom_vjp
def _head_sum(x, e, et):
    s = jnp.dot(_split_lhs(x), e, preferred_element_type=F32)
    return jnp.dot(_split_lhs(s), et, preferred_element_type=F32)


def _head_sum_fwd(x, e, et):
    return _head_sum(x, e, et), (e, et)


def _head_sum_bwd(res, ct):
    e, et = res
    return _head_sum(ct, e, et), jnp.zeros_like(e), jnp.zeros_like(et)


_head_sum.defvjp(_head_sum_fwd, _head_sum_bwd)


def _shift_down(main, prev, s, seq_first):
    prev = jnp.where(seq_first, 0.0, prev)
    ext = jnp.concatenate([prev, main], axis=0)
    return pltpu.roll(ext, s, 0)[HALO:]


def _shift_up(main, nxt, s, seq_last):
    nxt = jnp.where(seq_last, 0.0, nxt)
    ext = jnp.concatenate([main, nxt], axis=0)
    n = ext.shape[0]
    return pltpu.roll(ext, n - s, 0)[:n - HALO]


def _colsum(x):
    return jnp.sum(x, axis=0, keepdims=True)


def _pad8(x):
    k = x.shape[0]
    return jnp.concatenate([x, jnp.zeros((8 - k, x.shape[1]), x.dtype)], axis=0) if k < 8 else x


def _rms_fwd(x, g, T, name):
    D = x.shape[1]

    def body(rows, prevs, nexts, fulls, sf, sl):
        return [_rms(_f(rows[0]), _f(fulls[0]))], []

    (h,), _ = _tile_call(body, rows=[x], fulls=[g.reshape(1, D)], row_outs=[(D, BF16)], tm=min(512, T), T=T, name=name)
    return h


def _rms_bwd(x, g, dh, dres, T, name):
    D = x.shape[1]

    def body(rows, prevs, nexts, fulls, sf, sl):
        _, vjp = jax.vjp(_rms, _f(rows[0]), _f(fulls[0]))
        dx, dg = vjp(_f(rows[1]))
        return [dx + _f(rows[2])], [dg]

    (dx,), (dg,) = _tile_call(body, rows=[x, dh, dres], fulls=[g.reshape(1, D)], row_outs=[(D, F32)],
                              acc_outs=[(1, D)], tm=min(512, T), T=T, name=name)
    return dx, dg


def _ffn_conv(u1, prev, cw, cb, sf):
    k = cw.shape[0]
    out = cb + u1 * cw[k - 1:k]
    for j in range(k - 1):
        out = out + _shift_down(u1, prev, k - 1 - j, sf) * cw[j:j + 1]
    return out


def _ffn_fwd(x, p, T, tag):
    M, D = x.shape
    F = p["w_down"].shape[0]
    hf = _rms_fwd(x, p["norm"], T, f"ffn{tag}_norm")
    uf = _matmul(hf, p["w_up"], out_dtype=BF16, name=f"ffn{tag}_up")

    def body(rows, prevs, nexts, fulls, sf, sl):
        u = rows[0]
        gate = _ffn_conv(u[:, :F].astype(F32), prevs[0][:, :F].astype(F32), _f(fulls[0]), _f(fulls[1]), sf)
        return [_gelu(gate) * u[:, F:].astype(F32)], []

    (hid,), _ = _tile_call(body, rows=[uf], prevs=[0], fulls=[p["conv_w"], p["conv_b"].reshape(1, F)],
                           row_outs=[(F, BF16)], tm=min(256, T), T=T, name=f"ffn{tag}_act")
    y = _matmul(hid, p["w_down"], residual=x, name=f"ffn{tag}_down")
    return y, (x, hf, uf, hid)


def _ffn_bwd(dy, saved, p, T, tag):
    x, hf, uf, hid = saved
    M, D = x.shape
    F = p["w_down"].shape[0]
    K = p["conv_w"].shape[0]
    d_hid = _matmul(dy, p["w_down"], mode="nt", out_dtype=BF16, name=f"ffn{tag}_down_dx")
    d_w_down = _matmul(hid, dy, mode="tn", name=f"ffn{tag}_down_dw")

    def body(rows, prevs, nexts, fulls, sf, sl):
        u, dh = rows
        cw, cb = _f(fulls[0]), _f(fulls[1])
        tm = u.shape[0]
        u1c, u1p, u1n = u[:, :F].astype(F32), prevs[0][:, :F].astype(F32), nexts[0][:, :F].astype(F32)
        u1 = jnp.concatenate([u1c, u1n], axis=0)
        u2 = jnp.concatenate([u[:, F:].astype(F32), nexts[0][:, F:].astype(F32)], axis=0)
        dhid = jnp.concatenate([_f(dh), _f(nexts[1])], axis=0)
        gate = _ffn_conv(u1, u1p, cw, cb, sf)
        (act, dact) = jax.jvp(_gelu, (gate,), (jnp.ones_like(gate),))
        d_gate = dhid * u2 * dact
        d_u2 = (dhid * act)[:tm]
        rowid = lax.broadcasted_iota(jnp.int32, d_gate.shape, 0)
        d_gate = jnp.where(jnp.logical_and(sl, rowid >= tm), 0.0, d_gate)
        dgc, dgn = d_gate[:tm], d_gate[tm:]
        d_u1 = dgc * cw[K - 1:K]
        dws = []
        for j in range(K - 1):
            s = K - 1 - j
            d_u1 = d_u1 + _shift_up(dgc, dgn, s, False) * cw[j:j + 1]
            dws.append(_colsum(dgc * _shift_down(u1c, u1p, s, sf)))
        dws.append(_colsum(dgc * u1c))
        d_cw = _pad8(jnp.concatenate(dws, axis=0))
        return [jnp.concatenate([d_u1, d_u2], axis=1)], [d_cw, _colsum(dgc)]

    (d_uf,), (d_cw, d_cb) = _tile_call(
        body, rows=[uf, d_hid], prevs=[0], nexts=[0, 1], fulls=[p["conv_w"], p["conv_b"].reshape(1, F)],
        row_outs=[(2 * F, BF16)], acc_outs=[(8, F), (1, F)], tm=min(256, T), T=T, name=f"ffn{tag}_act_bwd")
    d_hf = _matmul(d_uf, p["w_up"], mode="nt", name=f"ffn{tag}_up_dx")
    d_w_up = _matmul(hf, d_uf, mode="tn", name=f"ffn{tag}_up_dw", out_cols_by_chip=True)
    dx, d_norm = _rms_bwd(x, p["norm"], d_hf, dy, T, f"ffn{tag}_norm_bwd")
    grads = {"norm": d_norm.reshape(D), "w_up": d_w_up, "conv_w": d_cw[:K], "conv_b": d_cb.reshape(F), "w_down": d_w_down}
    return dx, grads


def _lru_conv(u2, prev, cw, cb, sf):
    return _ffn_conv(u2, prev, cw, cb, sf)


def _lru_pre(xr, wbd, gb):
    return jnp.dot(xr.astype(MXU_DTYPE), wbd, preferred_element_type=F32) + gb


def _lru_gates(xr, pre, lam):
    D = xr.shape[1]
    r_gate, i_gate = _sigmoid(pre[:, :D]), _sigmoid(pre[:, D:])
    log_a = -LRU_C * r_gate * _softplus(-lam)
    a = jnp.exp(log_a)
    mult = jnp.sqrt(_neg_expm1(2.0 * log_a))
    return a, mult * (i_gate * xr)


def _lru_scan(a, b, B, T):
    M, D = a.shape
    cw = _pick(D, 256)
    ng = T // 8

    def kern(a_ref, b_ref, o_ref):
        row = lax.broadcasted_iota(jnp.int32, (8, cw), 0)

        def step(g, carry):
            sl = pl.ds(pl.multiple_of(g * 8, 8), 8)
            a8, b8 = a_ref[sl, :], b_ref[sl, :]
            for s in (1, 2, 4):
                a_sh = jnp.where(row >= s, pltpu.roll(a8, s, 0), 1.0)
                b_sh = jnp.where(row >= s, pltpu.roll(b8, s, 0), 0.0)
                b8 = a8 * b_sh + b8
                a8 = a8 * a_sh
            h8 = a8 * carry + b8
            o_ref[sl, :] = h8
            return jnp.broadcast_to(h8[7:8, :], (8, cw))

        lax.fori_loop(0, ng, step, jnp.zeros((8, cw), F32))

    spec = pl.BlockSpec((T, cw), lambda b, c: (b, c))
    return pl.pallas_call(
        kern, name="lru_scan", grid=(B, D // cw), in_specs=[spec, spec], out_specs=spec,
        out_shape=jax.ShapeDtypeStruct((M, D), F32), compiler_params=_cparams(("parallel", "parallel")),
    )(a, b)


def _lru_scan_bwd(a, hs, dhs, B, T):
    M, D = a.shape
    cw = _pick(D, 256)
    ng = T // 8

    def kern(a_ref, h_ref, d_ref, g_ref, da_ref):
        row = lax.broadcasted_iota(jnp.int32, (8, cw), 0)

        def step(k, carry):
            g_next, a_next = carry
            g = ng - 1 - k
            sl = pl.ds(pl.multiple_of(g * 8, 8), 8)
            a8, d8, h8 = a_ref[sl, :], d_ref[sl, :], h_ref[sl, :]
            c8 = jnp.where(row < 7, pltpu.roll(a8, 7, 0), a_next)
            for s in (1, 2, 4):
                d_sh = jnp.where(row < 8 - s, pltpu.roll(d8, 8 - s, 0), 0.0)
                c_sh = jnp.where(row < 8 - s, pltpu.roll(c8, 8 - s, 0), 1.0)
                d8 = d8 + c8 * d_sh
                c8 = c8 * c_sh
            G8 = d8 + c8 * g_next
            gp = jnp.maximum(g - 1, 0)
            hp8 = h_ref[pl.ds(pl.multiple_of(gp * 8, 8), 8), :]
            hp_last = jnp.where(g > 0, jnp.broadcast_to(hp8[7:8, :], (8, cw)), 0.0)
            hprev = jnp.where(row >= 1, pltpu.roll(h8, 1, 0), hp_last)
            g_ref[sl, :] = G8
            da_ref[sl, :] = G8 * hprev
            return jnp.broadcast_to(G8[0:1, :], (8, cw)), jnp.broadcast_to(a8[0:1, :], (8, cw))

        z = jnp.zeros((8, cw), F32)
        lax.fori_loop(0, ng, step, (z, z))

    spec = pl.BlockSpec((T, cw), lambda b, c: (b, c))
    sh = jax.ShapeDtypeStruct((M, D), F32)
    return pl.pallas_call(
        kern, name="lru_scan_bwd", grid=(B, D // cw), in_specs=[spec, spec, spec], out_specs=[spec, spec],
        out_shape=[sh, sh], compiler_params=_cparams(("parallel", "parallel")),
    )(a, hs, dhs)


def _lru_fwd(x, p, B, T):
    M, D = x.shape
    h0 = _rms_fwd(x, p["norm"], T, "lru_norm")
    u0 = _matmul(h0, p["w_in"], bias=p["b_in"], name="lru_in")
    fulls = [p["conv_w"], p["conv_b"].reshape(1, D), p["wbd"], p["gate_b"].reshape(1, 2 * D), p["lam"].reshape(1, D)]

    def body(rows, prevs, nexts, fulls, sf, sl):
        xr = _lru_conv(rows[0][:, D:], prevs[0][:, D:], _f(fulls[0]), _f(fulls[1]), sf)
        a, bt = _lru_gates(xr, _lru_pre(xr, fulls[2][...], _f(fulls[3])), _f(fulls[4]))
        return [a, bt], []

    (a, bt), _ = _tile_call(body, rows=[u0], prevs=[0], fulls=fulls, row_outs=[(D, F32), (D, F32)],
                            tm=min(256, T), T=T, name="lru_gates")
    hs = _lru_scan(a, bt, B, T)

    def body2(rows, prevs, nexts, fulls, sf, sl):
        return [rows[0][...] * _gelu(rows[1][:, :D])], []

    (out,), _ = _tile_call(body2, rows=[hs, u0], row_outs=[(D, BF16)], tm=min(512, T), T=T, name="lru_mix")
    y = _matmul(out, p["w_out"], bias=p["b_out"], residual=x, name="lru_out")
    return y, (x, h0, u0, a, hs, out)


def _lru_bwd(dy, saved, p, B, T):
    x, h0, u0, a, hs, out = saved
    M, D = x.shape
    K = p["conv_w"].shape[0]
    d_out = _matmul(dy, p["w_out"], mode="nt", name="lru_out_dx")
    d_w_out = _matmul(out, dy, mode="tn", name="lru_out_dw")

    def body(rows, prevs, nexts, fulls, sf, sl):
        do, h, u, dyv = rows[0][...], rows[1][...], rows[2][:, :D], rows[3][...]
        act, dact = jax.jvp(_gelu, (u,), (jnp.ones_like(u),))
        return [do * act, do * h * dact], [_colsum(dyv)]

    (d_hs, d_u1), (d_b_out,) = _tile_call(body, rows=[d_out, hs, u0, dy], row_outs=[(D, F32), (D, F32)],
                                          acc_outs=[(1, D)], tm=min(512, T), T=T, name="lru_mix_bwd")
    g_b, d_a = _lru_scan_bwd(a, hs, d_hs, B, T)
    fulls = [p["conv_w"], p["conv_b"].reshape(1, D), p["wbd"], p["gate_b"].reshape(1, 2 * D), p["lam"].reshape(1, D)]

    def body3(rows, prevs, nexts, fulls, sf, sl):
        u, gb_c, da_c, du1 = rows
        cw, cb, wbd, gbias, lam = _f(fulls[0]), _f(fulls[1]), fulls[2][...], _f(fulls[3]), _f(fulls[4])
        tm = u.shape[0]
        u2c, u2p, u2n = u[:, D:], prevs[0][:, D:], nexts[0][:, D:]
        xr_c = _lru_conv(u2c, u2p, cw, cb, sf)
        xr_n = _lru_conv(u2n, u2c[tm - HALO:], cw, cb, False)
        nt = (((1,), (1,)), ((), ()))
        _, vjp_c = jax.vjp(_lru_gates, xr_c, _lru_pre(xr_c, wbd, gbias), lam)
        dxr_c, dpre_c, d_lam = vjp_c((da_c[...], gb_c[...]))
        dpre_cb = dpre_c.astype(MXU_DTYPE)
        dxr_c = dxr_c + lax.dot_general(dpre_cb, wbd, nt, preferred_element_type=F32)
        d_wbd = lax.dot_general(xr_c.astype(MXU_DTYPE), dpre_cb, (((0,), (0,)), ((), ())), preferred_element_type=F32)
        d_gbias = _colsum(dpre_c)
        _, vjp_n = jax.vjp(lambda t, q: _lru_gates(t, q, lam), xr_n, _lru_pre(xr_n, wbd, gbias))
        dxr_n, dpre_n = vjp_n((nexts[2][...], nexts[1][...]))
        dxr_n = dxr_n + lax.dot_general(dpre_n.astype(MXU_DTYPE), wbd, nt, preferred_element_type=F32)
        d_u2 = dxr_c * cw[K - 1:K]
        dws = []
        for j in range(K - 1):
            s = K - 1 - j
            d_u2 = d_u2 + _shift_up(dxr_c, dxr_n, s, sl) * cw[j:j + 1]
            dws.append(_colsum(dxr_c * _shift_down(u2c, u2p, s, sf)))
        dws.append(_colsum(dxr_c * u2c))
        d_u = jnp.concatenate([du1[...], d_u2], axis=1)
        return [d_u], [_pad8(jnp.concatenate(dws, axis=0)), _colsum(dxr_c), d_wbd, d_gbias, d_lam,
                       _colsum(d_u)]

    (d_u0,), (d_cw, d_cb, d_wbd, d_gb, d_lam, d_b_in) = _tile_call(
        body3, rows=[u0, g_b, d_a, d_u1], prevs=[0], nexts=[0, 1, 2], fulls=fulls, row_outs=[(2 * D, BF16)],
        acc_outs=[(8, D), (1, D), (D, 2 * D), (1, 2 * D), (1, D), (1, 2 * D)], tm=min(256, T), T=T, name="lru_gates_bwd")
    d_h0 = _matmul(d_u0, p["w_in"], mode="nt", name="lru_in_dx")
    d_w_in = _matmul(h0, d_u0, mode="tn", name="lru_in_dw", out_cols_by_chip=True)
    dx, d_norm = _rms_bwd(x, p["norm"], d_h0, dy, T, "lru_norm_bwd")
    grads = {"norm": d_norm.reshape(D), "w_in": d_w_in, "b_in": d_b_in.reshape(2 * D), "conv_w": d_cw[:K],
             "conv_b": d_cb.reshape(D), "wbd": d_wbd, "gate_b": d_gb.reshape(2 * D), "lam": d_lam.reshape(D),
             "w_out": d_w_out, "b_out": d_b_out.reshape(D)}
    return dx, grads


def _rwkv_mix(xc, xp, norm, mix, sf):
    h = _rms(xc, norm)
    hp = _rms(xp, norm)
    xx = _shift_down(h, hp, 1, sf) - h
    return h, xx


def _rwkv_pre(k, xw, xa, xg, w0, w1, w2, a0, a1, a2, g1, g2, k_k, k_a, e, et):
    wl = -_softplus(-(w0 + _bdot(jnp.tanh(_bdot(xw, w1)), w2))) - 0.5
    decay = jnp.exp(-jnp.exp(wl))
    a = _sigmoid(a0 + _bdot(_bdot(xa, a1), a2))
    g = _bdot(_sigmoid(_bdot(xg, g1)), g2)
    kk = k * k_k
    nrm = jnp.sqrt(_head_sum(kk * kk, e, et))
    kk = kk / jnp.maximum(nrm, 1e-12)
    k2 = k * (1.0 + (a - 1.0) * k_a)
    return decay, k2, -kk, kk * a, g


def _rwkv_post(y, r, k2, v, g, ln_w, ln_b, r_k, e, et):
    inv = 1.0 / HEAD
    mu = _head_sum(y, e, et) * inv
    yc = y - mu
    var = _head_sum(yc * yc, e, et) * inv
    yn = yc * lax.rsqrt(var + GN_EPS) * ln_w + ln_b
    bonus = _head_sum(r * k2 * r_k, e, et) * v
    return (yn + bonus) * g


def _seg_lane_sums(x, lo_mask):
    s0 = jnp.sum(jnp.where(lo_mask, x, 0.0), axis=1, keepdims=True)
    s1 = jnp.sum(jnp.where(lo_mask, 0.0, x), axis=1, keepdims=True)
    return s0, s1


def _seg_lane_sum(x, lo_mask):
    s0, s1 = _seg_lane_sums(x, lo_mask)
    return jnp.where(lo_mask, s0, s1)


def _pair_consts():
    lane = lax.broadcasted_iota(jnp.int32, (HEAD, 128), 1)
    sub = lax.broadcasted_iota(jnp.int32, (HEAD, 128), 0)
    return lane < HEAD, (jnp.bitwise_and(lane, HEAD - 1) == sub).astype(F32)


def _pair_ones():
    head = jnp.arange(128) // HEAD
    return (head[:, None] == head[None, :]).astype(MXU_DTYPE)


def _split_lhs(x):
    hi = x.astype(MXU_DTYPE)
    return jnp.concatenate([hi, (x - hi.astype(F32)).astype(MXU_DTYPE)], axis=1)


def _spread_lhs(diag, row):
    return (diag * row).astype(MXU_DTYPE)


def _rwkv_scan(r, w, k, v, a, b, B, T):
    M, D = r.shape
    HP, PG, TC, NC, chains = _scan_plan(B, T, D, pairs=2)
    NS = len(chains) * 8
    NG = TC // 8

    def kern(r_ref, w_ref, k_ref, v_ref, a_ref, b_ref, ones_ref, y_ref, st_ref, S_ref, lv_ref, rv_ref, ly_ref, ry_ref):
        c = pl.program_id(1)

        @pl.when(c == 0)
        def _():
            S_ref[...] = jnp.zeros_like(S_ref)

        lo, diag = _pair_consts()
        row8 = lax.broadcasted_iota(jnp.int32, (8, 128), 0)

        def blk(idx):
            return pl.ds(idx * HEAD, HEAD)

        def rows_of(gi):
            return pl.ds(pl.multiple_of(gi * 8, 8), 8)

        def spread(gi, slot):
            for ci, (bi, p) in enumerate(chains):
                v8 = v_ref[bi, rows_of(gi), p * 128:(p + 1) * 128]
                for j in range(8):
                    lv_ref[slot, blk(ci * 8 + j), :] = _spread_lhs(diag, v8[j:j + 1, :])
            rv_ref[slot] = jnp.dot(lv_ref[slot], ones_ref[...], preferred_element_type=F32)

        def recur(gi, slot):
            sl = rows_of(gi)
            tiles = [[ref[bi, sl, p * 128:(p + 1) * 128] for ref in (r_ref, w_ref, k_ref, a_ref, b_ref)]
                     for bi, p in chains]
            S = [S_ref[ci] for ci in range(len(chains))]
            for j in range(8):
                for ci, (bi, p) in enumerate(chains):
                    r8, w8, k8, a8, b8 = tiles[ci]
                    idx = ci * 8 + j
                    st_ref[p, bi, gi * 8 + j] = S[ci]
                    sa = _seg_lane_sum(S[ci] * a8[j:j + 1, :], lo)
                    S[ci] = S[ci] * w8[j:j + 1, :] + sa * b8[j:j + 1, :] + rv_ref[slot, blk(idx), :] * k8[j:j + 1, :]
                    ly_ref[slot, blk(idx), :] = (S[ci] * r8[j:j + 1, :]).astype(MXU_DTYPE)
            for ci in range(len(chains)):
                S_ref[ci] = S[ci]

        def emit(gi, slot):
            ry_ref[slot] = jnp.dot(ly_ref[slot], ones_ref[...], preferred_element_type=F32)
            for ci, (bi, p) in enumerate(chains):
                y8 = jnp.zeros((8, 128), F32)
                for j in range(8):
                    y8 = jnp.where(row8 == j, _colsum(diag * ry_ref[slot, blk(ci * 8 + j), :]), y8)
                y_ref[bi, rows_of(gi), p * 128:(p + 1) * 128] = y8

        spread(0, 0)
        ly_ref[1] = jnp.zeros_like(ly_ref[1])

        def two_groups(m, _):
            g0, g1 = 2 * m, 2 * m + 1
            spread(g1, 1)
            recur(g0, 0)
            emit(jnp.maximum(g0 - 1, 0), 1)
            spread(jnp.minimum(g1 + 1, NG - 1), 0)
            recur(g1, 1)
            emit(g0, 0)
            return 0

        lax.fori_loop(0, NG // 2, two_groups, 0)
        emit(NG - 1, 1)

    spec = pl.BlockSpec((B, TC, 128 * PG), lambda hp, c: (0, c, hp))
    st_spec = pl.BlockSpec((PG, B, TC, HEAD, 128), lambda hp, c: (hp, 0, c, 0, 0))
    y, st = pl.pallas_call(
        kern, name="rwkv_scan", grid=(HP // PG, NC),
        in_specs=[spec] * 6 + [pl.BlockSpec((128, 128), lambda hp, c: (0, 0))], out_specs=[spec, st_spec],
        out_shape=[jax.ShapeDtypeStruct((B, T, D), F32), jax.ShapeDtypeStruct((HP, B, T, HEAD, 128), F32)],
        scratch_shapes=[pltpu.VMEM((len(chains), HEAD, 128), F32),
                        pltpu.VMEM((2, NS * HEAD, 128), MXU_DTYPE), pltpu.VMEM((2, NS * HEAD, 128), F32),
                        pltpu.VMEM((2, NS * HEAD, 128), MXU_DTYPE), pltpu.VMEM((2, NS * HEAD, 128), F32)],
        compiler_params=_cparams(("parallel", "arbitrary")),
    )(*[x.reshape(B, T, D) for x in (r, w, k, v, a, b)], _pair_ones())
    return y.reshape(M, D), st


def _scan_plan(B, T, D, pairs=1):
    HP = D // 128
    PG = pairs if HP % pairs == 0 else 1
    TC = min(64, T)
    assert TC % 16 == 0 and T % TC == 0
    return HP, PG, TC, T // TC, [(bi, p) for bi in range(B) for p in range(PG)]


def _rwkv_scan_bwd(r, w, k, v, a, b, st, dy, B, T):
    M, D = r.shape
    HP, PG, TC, NC, chains = _scan_plan(B, T, D)
    NS = len(chains) * 8

    NG = TC // 8

    def kern(r_ref, w_ref, k_ref, v_ref, a_ref, b_ref, st_ref, dy_ref, ones_ref,
             dr_ref, dw_ref, dk_ref, dv_ref, da_ref, db_ref, dS_ref, lp_ref, rp_ref, lq_ref, rq_ref):
        c = pl.program_id(1)

        @pl.when(c == 0)
        def _():
            dS_ref[...] = jnp.zeros_like(dS_ref)

        lo, diag = _pair_consts()
        row8 = lax.broadcasted_iota(jnp.int32, (8, 128), 0)

        def blk(idx):
            return pl.ds(idx * HEAD, HEAD)

        def rows_of(gi):
            return pl.ds(pl.multiple_of(gi * 8, 8), 8)

        def spread(gi, slot):
            sl = rows_of(gi)
            for ci, (bi, p) in enumerate(chains):
                lanes = slice(p * 128, (p + 1) * 128)
                v8, dy8, a8 = v_ref[bi, sl, lanes], dy_ref[bi, sl, lanes], a_ref[bi, sl, lanes]
                for j in range(8):
                    idx = ci * 8 + j
                    lp_ref[slot, blk(idx), :] = _spread_lhs(diag, v8[j:j + 1, :])
                    lp_ref[slot, blk(NS + idx), :] = _spread_lhs(diag, dy8[j:j + 1, :])
                    lp_ref[slot, blk(2 * NS + idx), :] = (st_ref[p, bi, gi * 8 + j] * a8[j:j + 1, :]).astype(MXU_DTYPE)
            rp_ref[slot] = jnp.dot(lp_ref[slot], ones_ref[...], preferred_element_type=F32)

        def recur(gi, slot):
            sl = rows_of(gi)
            tiles = [[ref[bi, sl, p * 128:(p + 1) * 128] for ref in (r_ref, w_ref, k_ref, a_ref, b_ref)]
                     for bi, p in chains]
            dS = [dS_ref[ci] for ci in range(len(chains))]
            acc = [[jnp.zeros((8, 128), F32) for _ in range(5)] for _ in chains]
            St = [None] * len(chains)
            for j in range(7, -1, -1):
                for ci, (bi, p) in enumerate(chains):
                    r8, w8, k8, a8, b8 = tiles[ci]
                    rj, wj, kj, aj, bj = r8[j:j + 1, :], w8[j:j + 1, :], k8[j:j + 1, :], a8[j:j + 1, :], b8[j:j + 1, :]
                    idx = ci * 8 + j
                    Sp = st_ref[p, bi, gi * 8 + j]
                    vb, dyb, sa = rp_ref[slot, blk(idx), :], rp_ref[slot, blk(NS + idx), :], rp_ref[slot, blk(2 * NS + idx), :]
                    if j == 7:
                        St[ci] = Sp * wj + sa * bj + vb * kj
                    d = dS[ci] + dyb * rj
                    dsa = _seg_lane_sum(d * bj, lo)
                    lq_ref[slot, blk(idx), :] = (d * kj).astype(MXU_DTYPE)
                    rows = (_colsum(St[ci] * dyb), _colsum(d * Sp), _colsum(d * vb), _colsum(Sp * dsa), _colsum(d * sa))
                    acc[ci] = [jnp.where(row8 == j, rw, a8_) for rw, a8_ in zip(rows, acc[ci], strict=True)]
                    dS[ci] = d * wj + dsa * aj
                    St[ci] = Sp
            for ci, (bi, p) in enumerate(chains):
                dS_ref[ci] = dS[ci]
                for ref, a8_ in zip((dr_ref, dw_ref, dk_ref, da_ref, db_ref), acc[ci], strict=True):
                    ref[bi, sl, p * 128:(p + 1) * 128] = a8_

        def emit(gi, slot):
            rq_ref[slot] = jnp.dot(lq_ref[slot], ones_ref[...], preferred_element_type=F32)
            for ci, (bi, p) in enumerate(chains):
                dv8 = jnp.zeros((8, 128), F32)
                for j in range(8):
                    dv8 = jnp.where(row8 == j, _colsum(diag * rq_ref[slot, blk(ci * 8 + j), :]), dv8)
                dv_ref[bi, rows_of(gi), p * 128:(p + 1) * 128] = dv8

        spread(NG - 1, 0)
        lq_ref[1] = jnp.zeros_like(lq_ref[1])

        def two_groups(m, _):
            g0, g1 = NG - 1 - 2 * m, NG - 2 - 2 * m
            spread(g1, 1)
            recur(g0, 0)
            emit(jnp.minimum(g0 + 1, NG - 1), 1)
            spread(jnp.maximum(g1 - 1, 0), 0)
            recur(g1, 1)
            emit(g0, 0)
            return 0

        lax.fori_loop(0, NG // 2, two_groups, 0)
        emit(0, 1)

    spec = pl.BlockSpec((B, TC, 128 * PG), lambda hp, c: (0, NC - 1 - c, hp))
    st_spec = pl.BlockSpec((PG, B, TC, HEAD, 128), lambda hp, c: (hp, 0, NC - 1 - c, 0, 0))
    sh = jax.ShapeDtypeStruct((B, T, D), F32)
    outs = pl.pallas_call(
        kern, name="rwkv_scan_bwd", grid=(HP // PG, NC),
        in_specs=[spec] * 6 + [st_spec, spec, pl.BlockSpec((128, 128), lambda hp, c: (0, 0))], out_specs=[spec] * 6,
        out_shape=[sh] * 6,
        scratch_shapes=[pltpu.VMEM((len(chains), HEAD, 128), F32),
                        pltpu.VMEM((2, 3 * NS * HEAD, 128), MXU_DTYPE), pltpu.VMEM((2, 3 * NS * HEAD, 128), F32),
                        pltpu.VMEM((2, NS * HEAD, 128), MXU_DTYPE), pltpu.VMEM((2, NS * HEAD, 128), F32)],
        compiler_params=_cparams(("parallel", "arbitrary")),
    )(*[x.reshape(B, T, D) for x in (r, w, k, v, a, b)], st, dy.reshape(B, T, D), _pair_ones())
    return [o.reshape(M, D) for o in outs]


def _head_mats(D):
    ch = jnp.arange(D) // HEAD
    e = (ch[:, None] == jnp.arange(128)[None, :]).astype(MXU_DTYPE)
    return jnp.concatenate([e, e], axis=0), jnp.concatenate([e.T, e.T], axis=0)


def _rwkv_fwd(x, p, B, T):
    M, D = x.shape
    e, et = _head_mats(D)
    norm = p["norm"].reshape(1, D)

    def body(rows, prevs, nexts, fulls, sf, sl):
        h, xx = _rwkv_mix(rows[0][...], prevs[0][...], _f(fulls[0]), None, sf)
        mix = _f(fulls[1])
        return [h + xx * mix[i:i + 1] for i in range(6)], []

    xs, _ = _tile_call(body, rows=[x], prevs=[0], fulls=[norm, _pad8(p["mix"])], row_outs=[(D, BF16)] * 6,
                       tm=min(256, T), T=T, name="rwkv_mix")
    r = _matmul(xs[0], p["w_r"], name="rwkv_r")
    k = _matmul(xs[1], p["w_k"], name="rwkv_k")
    v = _matmul(xs[2], p["w_v"], name="rwkv_v")
    pre_fulls = [p["w0"].reshape(1, D), p["w1"], p["w2"], p["a0"].reshape(1, D), p["a1"], p["a2"], p["g1"], p["g2"],
                 p["k_k"].reshape(1, D), p["k_a"].reshape(1, D), e, et]

    def body2(rows, prevs, nexts, fulls, sf, sl):
        outs = _rwkv_pre(rows[0][...], _f(rows[1]), _f(rows[2]), _f(rows[3]), *[f[...] for f in fulls])
        return list(outs), []

    (decay, k2, kkn, bb, g), _ = _tile_call(body2, rows=[k, xs[3], xs[4], xs[5]], fulls=pre_fulls,
                                            row_outs=[(D, F32)] * 5, tm=min(256, T), T=T, name="rwkv_pre")
    y, st = _rwkv_scan(r, decay, k2, v, kkn, bb, B, T)
    post_fulls = [p["ln_w"].reshape(1, D), p["ln_b"].reshape(1, D), p["r_k"].reshape(1, D), e, et]

    def body3(rows, prevs, nexts, fulls, sf, sl):
        return [_rwkv_post(*[rr[...] for rr in rows], *[f[...] for f in fulls])], []

    (z,), _ = _tile_call(body3, rows=[y, r, k2, v, g], fulls=post_fulls, row_outs=[(D, BF16)], tm=min(256, T), T=T,
                         name="rwkv_post")
    out = _matmul(z, p["w_out"], residual=x, name="rwkv_out")
    return out, (x, xs, r, k, v, decay, k2, kkn, bb, g, y, st, z)


def _rwkv_bwd(dout, saved, p, B, T):
    x, xs, r, k, v, decay, k2, kkn, bb, g, y, st, z = saved
    M, D = x.shape
    e, et = _head_mats(D)
    d_z = _matmul(dout, p["w_out"], mode="nt", name="rwkv_out_dx")
    d_w_out = _matmul(z, dout, mode="tn", name="rwkv_out_dw")
    post_fulls = [p["ln_w"].reshape(1, D), p["ln_b"].reshape(1, D), p["r_k"].reshape(1, D), e, et]

    def body(rows, prevs, nexts, fulls, sf, sl):
        prim = [rr[...] for rr in rows[:5]] + [f[...] for f in fulls]
        _, vjp = jax.vjp(_rwkv_post, *prim)
        ct = vjp(rows[5][...])
        return list(ct[:5]), list(ct[5:8])

    (d_y, d_r1, d_k21, d_v1, d_g), (d_ln_w, d_ln_b, d_r_k) = _tile_call(
        body, rows=[y, r, k2, v, g, d_z], fulls=post_fulls, row_outs=[(D, F32)] * 5, acc_outs=[(1, D)] * 3,
        tm=min(256, T), T=T, name="rwkv_post_bwd")
    d_r2, d_w, d_k22, d_v2, d_kkn, d_bb = _rwkv_scan_bwd(r, decay, k2, v, kkn, bb, st, d_y, B, T)
    pre_fulls = [p["w0"].reshape(1, D), p["w1"], p["w2"], p["a0"].reshape(1, D), p["a1"], p["a2"], p["g1"], p["g2"],
                 p["k_k"].reshape(1, D), p["k_a"].reshape(1, D), e, et]

    def body2(rows, prevs, nexts, fulls, sf, sl):
        prim = [rows[0][...], _f(rows[1]), _f(rows[2]), _f(rows[3])] + [f[...] for f in fulls]
        _, vjp = jax.vjp(_rwkv_pre, *prim)
        ct = vjp((rows[4][...], rows[5][...] + rows[6][...], rows[7][...], rows[8][...], rows[9][...]))
        d_r = rows[10][...] + rows[11][...]
        d_v = rows[12][...] + rows[13][...]
        return [ct[0], ct[1], ct[2], ct[3], d_r, d_v], [c.astype(F32) for c in ct[4:14]]

    acc_shapes = [f.shape for f in pre_fulls[:10]]
    (d_k, d_xw, d_xa, d_xg, d_r, d_v), pgr = _tile_call(
        body2, rows=[k, xs[3], xs[4], xs[5], d_w, d_k21, d_k22, d_kkn, d_bb, d_g, d_r1, d_r2, d_v1, d_v2],
        fulls=pre_fulls, row_outs=[(D, BF16), (D, F32), (D, F32), (D, F32), (D, BF16), (D, BF16)], acc_outs=acc_shapes,
        tm=min(128, T), T=T, name="rwkv_pre_bwd")
    d_xr = _matmul(d_r, p["w_r"], mode="nt", name="rwkv_r_dx")
    d_xk = _matmul(d_k, p["w_k"], mode="nt", name="rwkv_k_dx")
    d_xv = _matmul(d_v, p["w_v"], mode="nt", name="rwkv_v_dx")
    d_wr = _matmul(xs[0], d_r, mode="tn", name="rwkv_r_dw")
    d_wk = _matmul(xs[1], d_k, mode="tn", name="rwkv_k_dw")
    d_wv = _matmul(xs[2], d_v, mode="tn", name="rwkv_v_dw")
    norm = p["norm"].reshape(1, D)

    def body3(rows, prevs, nexts, fulls, sf, sl):
        xc, xp = rows[0][...], prevs[0][...]
        nrm, mix = _f(fulls[0]), _f(fulls[1])
        h, xx = _rwkv_mix(xc, xp, nrm, None, sf)
        dxs = [rows[1 + i][...] for i in range(6)]
        dxs_n = [nexts[i][...] for i in range(6)]
        d_h = jnp.zeros_like(h)
        d_sh = jnp.zeros_like(h)
        d_sh_n = jnp.zeros_like(dxs_n[0])
        dmix = []
        for i in range(6):
            m = mix[i:i + 1]
            d_h = d_h + dxs[i] * (1.0 - m)
            d_sh = d_sh + dxs[i] * m
            d_sh_n = d_sh_n + dxs_n[i] * m
            dmix.append(_colsum(dxs[i] * xx))
        d_h = d_h + _shift_up(d_sh, d_sh_n, 1, sl)
        _, vjp = jax.vjp(_rms, xc, nrm)
        dx, dn = vjp(d_h)
        return [dx + rows[7][...]], [dn, _pad8(jnp.concatenate(dmix, axis=0))]

    (dx,), (d_norm, d_mix) = _tile_call(
        body3, rows=[x, d_xr, d_xk, d_xv, d_xw, d_xa, d_xg, dout], prevs=[0], nexts=[1, 2, 3, 4, 5, 6],
        fulls=[norm, _pad8(p["mix"])], row_outs=[(D, F32)], acc_outs=[(1, D), (8, D)], tm=min(256, T), T=T,
        name="rwkv_mix_bwd")
    names = ["w0", "w1", "w2", "a0", "a1", "a2", "g1", "g2", "k_k", "k_a"]
    grads = {n: gr.reshape(p[n].shape) for n, gr in zip(names, pgr, strict=True)}
    grads.update({"norm": d_norm.reshape(D), "mix": d_mix[:6], "w_r": d_wr, "w_k": d_wk, "w_v": d_wv,
                  "r_k": d_r_k.reshape(p["r_k"].shape), "ln_w": d_ln_w.reshape(D), "ln_b": d_ln_b.reshape(D),
                  "w_out": d_w_out})
    return dx, grads


def _loss_head(x, g, tgt, T):
    M, D = x.shape

    def body(rows, prevs, nexts, fulls, sf, sl):
        xv, gv = rows[0][...], _f(fulls[0])
        yv, vjp = jax.vjp(_rms, xv, gv)
        err = yv - rows[1][...]
        dx, dg = vjp(err * (1.0 / D))
        part = jnp.sum(_colsum(err * err), axis=1, keepdims=True) * (0.5 / D)
        return [dx], [dg, jnp.broadcast_to(part, (1, 128))]

    (dx,), (dg, loss) = _tile_call(body, rows=[x, tgt], fulls=[g.reshape(1, D)], row_outs=[(D, F32)],
                                   acc_outs=[(1, D), (1, 128)], tm=min(512, T), T=T, name="loss_head")
    return loss[0, 0], dx, dg.reshape(D)


def _local_step(x3, tgt3, P):
    B, T, D = x3.shape
    x, tgt = x3.reshape(B * T, D), tgt3.reshape(B * T, D)
    x1, s_lru = _lru_fwd(x, P["lru"], B, T)
    x2, s_f0 = _ffn_fwd(x1, P["ffn0"], T, "0")
    x3_, s_rw = _rwkv_fwd(x2, P["rwkv"], B, T)
    x4, s_f1 = _ffn_fwd(x3_, P["ffn1"], T, "1")
    loss, d4, d_fn = _loss_head(x4, P["final_norm"], tgt, T)
    d3, g_f1 = _ffn_bwd(d4, s_f1, P["ffn1"], T, "1")
    d2, g_rw = _rwkv_bwd(d3, s_rw, P["rwkv"], B, T)
    d1, g_f0 = _ffn_bwd(d2, s_f0, P["ffn0"], T, "0")
    d0, g_lru = _lru_bwd(d1, s_lru, P["lru"], B, T)
    return loss, d0.reshape(B, T, D), {"lru": g_lru, "ffn0": g_f0, "rwkv": g_rw, "ffn1": g_f1, "final_norm": d_fn}


WEIGHTS = ['lru_norm', 'lru_w_in', 'lru_b_in', 'lru_conv_w', 'lru_conv_b', 'lru_gate_w', 'lru_gate_b', 'lru_lambda',
           'lru_w_out', 'lru_b_out', 'rwkv_norm', 'rwkv_mix', 'rwkv_w_rkv', 'rwkv_w0', 'rwkv_w1', 'rwkv_w2', 'rwkv_a0',
           'rwkv_a1', 'rwkv_a2', 'rwkv_g1', 'rwkv_g2', 'rwkv_k_k', 'rwkv_k_a', 'rwkv_r_k', 'rwkv_ln_w', 'rwkv_ln_b',
           'rwkv_w_out', 'ffn_norm', 'ffn_w_up', 'ffn_conv_w', 'ffn_conv_b', 'ffn_w_down', 'final_norm']
SHARD_AXIS = {'lru_w_in': 2, 'lru_conv_w': 2, 'lru_w_out': 1, 'rwkv_norm': 1, 'rwkv_mix': 2, 'rwkv_w_rkv': 2,
              'rwkv_w0': 1, 'rwkv_w1': 1, 'rwkv_w2': 2, 'rwkv_a0': 1, 'rwkv_a1': 1, 'rwkv_a2': 2, 'rwkv_g1': 1,
              'rwkv_g2': 2, 'rwkv_k_k': 1, 'rwkv_k_a': 1, 'rwkv_ln_w': 1, 'rwkv_ln_b': 1, 'rwkv_w_out': 1,
              'ffn_w_up': 2, 'ffn_conv_w': 2, 'ffn_w_down': 1}
MXU_WEIGHTS = ('lru_w_in', 'lru_w_out', 'rwkv_w_rkv', 'rwkv_w_out', 'ffn_w_up', 'ffn_w_down')
N_CHIPS = 4
LANES = 1024


def _pack(arrs, dtype, row_mult):
    flat = jnp.concatenate([a.reshape(-1).astype(dtype) for a in arrs])
    n = flat.shape[0]
    unit = row_mult * LANES
    tot = -(-n // unit) * unit
    if tot > n:
        flat = jnp.concatenate([flat, jnp.zeros((tot - n,), dtype)])
    return flat.reshape(tot // LANES, LANES)


def _unpack(buf, shapes):
    flat = buf.reshape(-1)
    out, off = [], 0
    for s in shapes:
        n = 1
        for d in s:
            n *= d
        out.append(flat[off:off + n].reshape(s))
        off += n
    return out


def _to_shards(full, axis):
    return jnp.stack(jnp.split(full, N_CHIPS, axis=axis))


MESH_ID = pl.DeviceIdType.MESH


ANY_SPEC = pl.BlockSpec(memory_space=pl.ANY)
COMM_PARAMS = pltpu.CompilerParams(has_side_effects=True)


def _mesh_place():
    x, y, c = lax.axis_index("x"), lax.axis_index("y"), lax.axis_index("c")
    return x, y, c, 2 * x + y, [(1 - x, y), (x, 1 - y), (1 - x, 1 - y)]


def _gather_all(arrs, name):
    n = len(arrs)

    def body(*refs):
        ins, outs = refs[:n], refs[n:2 * n]
        send_sems, recv_sems, local_sems = refs[2 * n:]
        x, y, c, p, chips = _mesh_place()
        sibling = (x, y, 1 - c)

        def rows(a, which):
            h = arrs[a].shape[0] // 2
            return pl.ds(which * h, h)

        def copy(a, k, region, src, to):
            return pltpu.make_async_remote_copy(src_ref=src, dst_ref=region, send_sem=send_sems.at[a, k],
                                                recv_sem=recv_sems.at[a, k], device_id=to, device_id_type=MESH_ID)

        local = [pltpu.make_async_copy(ins[a], outs[a].at[p], local_sems.at[a]) for a in range(n)]
        for cp in local:
            cp.start()
        first = [copy(a, j, outs[a].at[p, rows(a, c)], ins[a].at[rows(a, c)], (qx, qy, c))
                 for a in range(n) for j, (qx, qy) in enumerate(chips)]
        for cp in first:
            cp.start()
        passed = []
        for a in range(n):
            for j, (qx, qy) in enumerate(chips):
                region = outs[a].at[2 * qx + qy, rows(a, c)]
                copy(a, j, region, region, (qx, qy, c)).wait_recv()
                fw = copy(a, 3 + j, region, region, sibling)
                fw.start()
                passed.append(fw)
        for a in range(n):
            for j, (qx, qy) in enumerate(chips):
                region = outs[a].at[2 * qx + qy, rows(a, 1 - c)]
                copy(a, 3 + j, region, region, sibling).wait_recv()
        for cp in first + passed:
            cp.wait_send()
        for cp in local:
            cp.wait()

    return pl.pallas_call(
        body, name=name, out_shape=[jax.ShapeDtypeStruct((N_CHIPS,) + a.shape, a.dtype) for a in arrs],
        in_specs=[ANY_SPEC] * n, out_specs=[ANY_SPEC] * n,
        scratch_shapes=[pltpu.SemaphoreType.DMA((n, 6)), pltpu.SemaphoreType.DMA((n, 6)), pltpu.SemaphoreType.DMA((n,))],
        compiler_params=COMM_PARAMS,
    )(*arrs)


def _pair_swap_all(gs, rep, name):
    n = len(gs)

    def body(*refs):
        ins, outs = refs[:n + 1], refs[n + 1:2 * n + 2]
        send_sems, recv_sems = refs[2 * n + 2:]
        x, y, c, _, _ = _mesh_place()
        copies = []
        for a in range(n + 1):
            src = ins[a]
            if a < n:
                h = gs[a].shape[1] // 2
                src = src.at[:, pl.ds((1 - c) * h, h), :]
            copies.append(pltpu.make_async_remote_copy(src_ref=src, dst_ref=outs[a], send_sem=send_sems.at[a],
                                                       recv_sem=recv_sems.at[a], device_id=(x, y, 1 - c),
                                                       device_id_type=MESH_ID))
        for cp in copies:
            cp.start()
        for cp in copies:
            cp.wait()

    shapes = [jax.ShapeDtypeStruct((N_CHIPS, g.shape[1] // 2, g.shape[2]), g.dtype) for g in gs]
    shapes.append(jax.ShapeDtypeStruct(rep.shape, rep.dtype))
    res = pl.pallas_call(
        body, name=name, out_shape=shapes, in_specs=[ANY_SPEC] * (n + 1), out_specs=[ANY_SPEC] * (n + 1),
        scratch_shapes=[pltpu.SemaphoreType.DMA((n + 1,)), pltpu.SemaphoreType.DMA((n + 1,))],
        compiler_params=COMM_PARAMS,
    )(*gs, rep)
    return res[:n], res[n]


def _chip_exchange_all(ps, rep, name):
    n = len(ps)

    def body(*refs):
        ins, outs = refs[:n + 1], refs[n + 1:2 * n + 2]
        send_sems, recv_sems, local_sems = refs[2 * n + 2:]
        x, y, c, p, chips = _mesh_place()

        def src(a, q):
            return ins[a].at[q] if a < n else ins[a]

        local = [pltpu.make_async_copy(src(a, p), outs[a].at[p], local_sems.at[a]) for a in range(n + 1)]
        for cp in local:
            cp.start()
        sends, recvs = [], []
        for a in range(n + 1):
            for j, (qx, qy) in enumerate(chips):
                q = 2 * qx + qy
                for dst, keep in ((outs[a].at[p], sends), (outs[a].at[q], recvs)):
                    keep.append(pltpu.make_async_remote_copy(
                        src_ref=src(a, q), dst_ref=dst, send_sem=send_sems.at[a, j], recv_sem=recv_sems.at[a, j],
                        device_id=(qx, qy, c), device_id_type=MESH_ID))
        for cp in sends:
            cp.start()
        for cp in recvs:
            cp.wait_recv()
        for cp in sends:
            cp.wait_send()
        for cp in local:
            cp.wait()

    shapes = [jax.ShapeDtypeStruct(g.shape, g.dtype) for g in ps]
    shapes.append(jax.ShapeDtypeStruct((N_CHIPS,) + rep.shape, rep.dtype))
    res = pl.pallas_call(
        body, name=name, out_shape=shapes, in_specs=[ANY_SPEC] * (n + 1), out_specs=[ANY_SPEC] * (n + 1),
        scratch_shapes=[pltpu.SemaphoreType.DMA((n + 1, 3)), pltpu.SemaphoreType.DMA((n + 1, 3)),
                        pltpu.SemaphoreType.DMA((n + 1,))],
        compiler_params=COMM_PARAMS,
    )(*ps, rep)
    return res[:n], res[n]


def _half_swap_all(ts, name):
    n = len(ts)

    def body(*refs):
        ins, outs = refs[:n], refs[n:2 * n]
        send_sems, recv_sems = refs[2 * n:]
        x, y, c, _, _ = _mesh_place()
        copies = [pltpu.make_async_remote_copy(src_ref=ins[a], dst_ref=outs[a], send_sem=send_sems.at[a],
                                               recv_sem=recv_sems.at[a], device_id=(x, y, 1 - c), device_id_type=MESH_ID)
                  for a in range(n)]
        for cp in copies:
            cp.start()
        for cp in copies:
            cp.wait()

    return pl.pallas_call(
        body, name=name, out_shape=[jax.ShapeDtypeStruct(t.shape, t.dtype) for t in ts],
        in_specs=[ANY_SPEC] * n, out_specs=[ANY_SPEC] * n,
        scratch_shapes=[pltpu.SemaphoreType.DMA((n,)), pltpu.SemaphoreType.DMA((n,))],
        compiler_params=COMM_PARAMS,
    )(*ts)


def _pick_rows(R, cap=256):
    for t in (512, 256, 128, 64, 32, 16, 8):
        if t <= cap and R % t == 0:
            return t
    return R


def _pair_sum(g, got, name):
    _, R, C = g.shape
    h = R // 2
    th = _pick_rows(h)

    def kern(g_ref, got_ref, o_ref):
        both = g_ref[...]
        mine = jnp.where(lax.axis_index("c") == 0, both[0], both[1])
        o_ref[...] = (mine + got_ref[...]).astype(o_ref.dtype)

    return pl.pallas_call(
        kern, name=name, grid=(N_CHIPS, h // th),
        in_specs=[pl.BlockSpec((None, 2, th, C), lambda q, i: (q, 0, i, 0)), pl.BlockSpec((None, th, C), lambda q, i: (q, i, 0))],
        out_specs=pl.BlockSpec((None, th, C), lambda q, i: (q, i, 0)),
        out_shape=jax.ShapeDtypeStruct((N_CHIPS, h, C), BF16), compiler_params=_cparams(("parallel", "parallel")),
    )(g.reshape(N_CHIPS, 2, h, C), got)


def _rep_pair_sum(rep, got, name):
    R, C = rep.shape
    tr = _pick_rows(R)

    def kern(a_ref, b_ref, o_ref):
        o_ref[...] = (a_ref[...] + b_ref[...]).astype(o_ref.dtype)

    spec = pl.BlockSpec((tr, C), lambda i: (i, 0))
    return pl.pallas_call(kern, name=name, grid=(R // tr,), in_specs=[spec, spec], out_specs=spec,
                          out_shape=jax.ShapeDtypeStruct((R, C), BF16), compiler_params=_cparams(("parallel",)))(rep, got)


def _chip_sum(arrived, name):
    _, R, C = arrived.shape
    tr = _pick_rows(R)

    def kern(a_ref, o_ref):
        acc = a_ref[0].astype(F32)
        for q in range(1, N_CHIPS):
            acc = acc + a_ref[q].astype(F32)
        o_ref[...] = acc

    return pl.pallas_call(
        kern, name=name, grid=(R // tr,), in_specs=[pl.BlockSpec((N_CHIPS, tr, C), lambda i: (0, i, 0))],
        out_specs=pl.BlockSpec((tr, C), lambda i: (i, 0)), out_shape=jax.ShapeDtypeStruct((R, C), F32),
        compiler_params=_cparams(("parallel",)),
    )(arrived)


def _adam_math(w, g, m, v):
    c1 = 1.0 / (1.0 - ADAM_B1 ** ADAM_STEP)
    c2 = 1.0 / (1.0 - ADAM_B2 ** ADAM_STEP)
    nm = ADAM_B1 * m + (1.0 - ADAM_B1) * g
    nv = ADAM_B2 * v + (1.0 - ADAM_B2) * (g * g)
    return -ADAM_LR * ((nm * c1) / (jnp.sqrt(nv * c2) + ADAM_EPS) + ADAM_WD * w), nm, nv


def _adamw_halves(w, m, v, mine, other, name):
    R, C = w.shape
    h = R // 2
    th = _pick_rows(h)
    nt = h // th

    def kern(w_ref, m_ref, v_ref, a_ref, b_ref, g_ref, d_ref, nm_ref, nv_ref):
        g = jnp.where(pl.program_id(0) == lax.axis_index("c"), a_ref[...], b_ref[...])
        d, nm, nv = _adam_math(w_ref[...], g, m_ref[...], v_ref[...])
        g_ref[...] = g
        d_ref[...] = d
        nm_ref[...] = nm
        nv_ref[...] = nv

    full = pl.BlockSpec((th, C), lambda hh, i: (hh * nt + i, 0))
    half = pl.BlockSpec((th, C), lambda hh, i: (i, 0))
    sh = jax.ShapeDtypeStruct((R, C), F32)
    return pl.pallas_call(kern, name=name, grid=(2, nt), in_specs=[full] * 3 + [half] * 2, out_specs=[full] * 4,
                          out_shape=[sh] * 4, compiler_params=_cparams(("parallel", "parallel")))(w, m, v, mine, other)


def _adamw_call(w, g, m, v, name):
    R = w.shape[0]
    tr = _pick_rows(R)
    c1 = 1.0 / (1.0 - ADAM_B1 ** ADAM_STEP)
    c2 = 1.0 / (1.0 - ADAM_B2 ** ADAM_STEP)

    def kern(w_ref, g_ref, m_ref, v_ref, d_ref, nm_ref, nv_ref):
        gv = g_ref[...]
        nm = ADAM_B1 * m_ref[...] + (1.0 - ADAM_B1) * gv
        nv = ADAM_B2 * v_ref[...] + (1.0 - ADAM_B2) * (gv * gv)
        d_ref[...] = -ADAM_LR * ((nm * c1) / (jnp.sqrt(nv * c2) + ADAM_EPS) + ADAM_WD * w_ref[...])
        nm_ref[...] = nm
        nv_ref[...] = nv

    spec = pl.BlockSpec((tr, LANES), lambda i: (i, 0))
    sh = jax.ShapeDtypeStruct((R, LANES), F32)
    return pl.pallas_call(kern, name=name, grid=(R // tr,), in_specs=[spec] * 4, out_specs=[spec] * 3,
                          out_shape=[sh] * 3, compiler_params=_cparams(("parallel",)))(w, g, m, v)


def _gate_dense(gate_w):
    _, nb, bw, _ = gate_w.shape
    eye = jnp.eye(nb, dtype=gate_w.dtype)
    dense = jnp.einsum('gncd,nm->gncmd', gate_w, eye).reshape(2, nb * bw, nb * bw)
    return jnp.concatenate([dense[0], dense[1]], axis=1)


def _gate_blocks(d_dense, nb):
    D = d_dense.shape[0]
    bw = D // nb
    g = d_dense.reshape(nb, bw, 2, nb, bw)
    return jnp.einsum('ncgnd->gncd', g)


def _step(W, M1, V1, x, tgt):
    sharded = [n for n in WEIGHTS if n in SHARD_AXIS]
    repl = [n for n in WEIGHTS if n not in SHARD_AXIS]
    small = [n for n in sharded if n not in MXU_WEIGHTS]
    D = x.shape[-1]

    def rows2d(a):
        return a.reshape(-1, a.shape[-1])

    small_buf = _pack([W[n] for n in small], F32, 16)
    gathered = _gather_all([rows2d(W[n]).astype(MXU_DTYPE) for n in MXU_WEIGHTS] + [small_buf], "gather_weights")
    mats = {n: g.reshape((N_CHIPS,) + W[n].shape[-3:]) for n, g in zip(MXU_WEIGHTS, gathered[:-1], strict=True)}
    per_chip = [_unpack(gathered[-1][q], [W[n].shape for n in small]) for q in range(N_CHIPS)]
    full = {n: jnp.concatenate([per_chip[q][i] for q in range(N_CHIPS)], axis=SHARD_AXIS[n]) for i, n in enumerate(small)}
    for n in repl:
        full[n] = W[n]

    def by_rows(name, layer):
        m = mats[name][:, layer]
        return m.reshape(N_CHIPS * m.shape[1], m.shape[2])

    P = {
        "lru": {"norm": full["lru_norm"][0], "w_in": _W(mats["lru_w_in"], 0), "b_in": full["lru_b_in"][0],
                "conv_w": full["lru_conv_w"][0], "conv_b": full["lru_conv_b"][0],
                "wbd": _gate_dense(full["lru_gate_w"][0]).astype(MXU_DTYPE), "gate_b": full["lru_gate_b"][0].reshape(-1),
                "lam": full["lru_lambda"][0], "w_out": by_rows("lru_w_out", 0), "b_out": full["lru_b_out"][0]},
        "rwkv": {"norm": full["rwkv_norm"][0], "mix": full["rwkv_mix"][0],
                 "w_r": by_rows("rwkv_w_rkv", 0), "w_k": by_rows("rwkv_w_rkv", 1), "w_v": by_rows("rwkv_w_rkv", 2),
                 "w0": full["rwkv_w0"][0], "w1": full["rwkv_w1"][0], "w2": full["rwkv_w2"][0], "a0": full["rwkv_a0"][0],
                 "a1": full["rwkv_a1"][0], "a2": full["rwkv_a2"][0], "g1": full["rwkv_g1"][0], "g2": full["rwkv_g2"][0],
                 "k_k": full["rwkv_k_k"][0], "k_a": full["rwkv_k_a"][0], "r_k": full["rwkv_r_k"][0],
                 "ln_w": full["rwkv_ln_w"][0], "ln_b": full["rwkv_ln_b"][0], "w_out": by_rows("rwkv_w_out", 0)},
        "final_norm": full["final_norm"],
    }
    for l in range(2):
        P[f"ffn{l}"] = {"norm": full["ffn_norm"][l], "w_up": _W(mats["ffn_w_up"], l),
                        "conv_w": full["ffn_conv_w"][l], "conv_b": full["ffn_conv_b"][l],
                        "w_down": by_rows("ffn_w_down", l)}

    loss, gx, G = _local_step(x, tgt, P)

    nb = W["lru_gate_w"].shape[2]
    gl, gr = G["lru"], G["rwkv"]
    gfull = {
        "lru_norm": gl["norm"][None], "lru_b_in": gl["b_in"][None],
        "lru_conv_w": gl["conv_w"][None], "lru_conv_b": gl["conv_b"][None], "lru_gate_w": _gate_blocks(gl["wbd"], nb)[None],
        "lru_gate_b": gl["gate_b"].reshape(W["lru_gate_b"].shape), "lru_lambda": gl["lam"][None],
        "lru_b_out": gl["b_out"][None],
        "rwkv_norm": gr["norm"][None], "rwkv_mix": gr["mix"][None],
        "rwkv_w0": gr["w0"][None], "rwkv_w1": gr["w1"][None], "rwkv_w2": gr["w2"][None], "rwkv_a0": gr["a0"][None],
        "rwkv_a1": gr["a1"][None], "rwkv_a2": gr["a2"][None], "rwkv_g1": gr["g1"][None], "rwkv_g2": gr["g2"][None],
        "rwkv_k_k": gr["k_k"][None], "rwkv_k_a": gr["k_a"][None], "rwkv_r_k": gr["r_k"][None],
        "rwkv_ln_w": gr["ln_w"][None], "rwkv_ln_b": gr["ln_b"][None],
        "final_norm": G["final_norm"],
    }
    for k in ("norm", "conv_w", "conv_b"):
        gfull["ffn_" + k] = jnp.stack([G["ffn0"][k], G["ffn1"][k]])

    small_g = jnp.stack([_pack([_to_shards(gfull[n], SHARD_AXIS[n])[q] for n in small], F32, 16) for q in range(N_CHIPS)])
    cut = lambda g: g.reshape(N_CHIPS, g.shape[0] // N_CHIPS, g.shape[1])
    pieces = [("lru_w_in", (0,), gl["w_in"]), ("lru_w_out", (0,), cut(gl["w_out"])),
              ("rwkv_w_rkv", (0, 0), cut(gr["w_r"])), ("rwkv_w_rkv", (0, 1), cut(gr["w_k"])),
              ("rwkv_w_rkv", (0, 2), cut(gr["w_v"])), ("rwkv_w_out", (0,), cut(gr["w_out"])),
              ("ffn_w_up", (0,), G["ffn0"]["w_up"]), ("ffn_w_up", (1,), G["ffn1"]["w_up"]),
              ("ffn_w_down", (0,), cut(G["ffn0"]["w_down"])), ("ffn_w_down", (1,), cut(G["ffn1"]["w_down"]))]
    gs = [g for _, _, g in pieces] + [small_g]
    grep = _pack([gfull[n] for n in repl], F32, 16)

    got, got_rep = _pair_swap_all(gs, grep, "reduce_pair_swap")
    pair = [_pair_sum(g, r, f"reduce_pair_sum{i}") for i, (g, r) in enumerate(zip(gs, got, strict=True))]
    pair_rep = _rep_pair_sum(grep, got_rep, "reduce_pair_sum_rep")
    arrived, arrived_rep = _chip_exchange_all(pair, pair_rep, "reduce_chips")
    mine = [_chip_sum(a, f"reduce_chip_sum{i}") for i, a in enumerate(arrived)]
    g_rp = _chip_sum(arrived_rep, "reduce_chip_sum_rep")
    other = _half_swap_all(mine, "reduce_half_swap")

    outs, parts = {}, {}
    for i, (n, idx, _) in enumerate(pieces):
        w, m, v = (rows2d(S[n][idx]) for S in (W, M1, V1))
        res = _adamw_halves(w, m, v, mine[i], other[i], f"adamw{i}")
        for kind, a in zip(("grad", "delta", "new_m", "new_v"), res, strict=True):
            parts.setdefault((kind, n), []).append(a)
    for (kind, n), lst in parts.items():
        a = lst[0] if len(lst) == 1 else jnp.stack(lst)
        outs[(kind, n)] = a.reshape(W[n].shape)
    wb, mb, vb = (_pack([S[n] for n in small], F32, 16) for S in (W, M1, V1))
    res = _adamw_halves(wb, mb, vb, mine[-1], other[-1], "adamw_small")
    for kind, buf in zip(("grad", "delta", "new_m", "new_v"), res, strict=True):
        for n, a in zip(small, _unpack(buf, [W[n].shape for n in small]), strict=True):
            outs[(kind, n)] = a
    wb, mb, vb = (_pack([S[n] for n in repl], F32, 16) for S in (W, M1, V1))
    d, nm, nv = _adamw_call(wb, g_rp, mb, vb, "adamw_repl")
    for kind, buf in (("grad", g_rp), ("delta", d), ("new_m", nm), ("new_v", nv)):
        for n, a in zip(repl, _unpack(buf, [W[n].shape for n in repl]), strict=True):
            outs[(kind, n)] = a
    loss = lax.psum(loss, ("x", "y", "c"))
    return (loss, gx, *[outs[(kind, n)] for kind in ("grad", "delta", "new_m", "new_v") for n in WEIGHTS])


def kernel(x, lru_norm, lru_w_in, lru_b_in, lru_conv_w, lru_conv_b, lru_gate_w, lru_gate_b, lru_lambda, lru_w_out, lru_b_out, rwkv_norm, rwkv_mix, rwkv_w_rkv, rwkv_w0, rwkv_w1, rwkv_w2, rwkv_a0, rwkv_a1, rwkv_a2, rwkv_g1, rwkv_g2, rwkv_k_k, rwkv_k_a, rwkv_r_k, rwkv_ln_w, rwkv_ln_b, rwkv_w_out, ffn_norm, ffn_w_up, ffn_conv_w, ffn_conv_b, ffn_w_down, final_norm, loss_target, m_lru_norm, m_lru_w_in, m_lru_b_in, m_lru_conv_w, m_lru_conv_b, m_lru_gate_w, m_lru_gate_b, m_lru_lambda, m_lru_w_out, m_lru_b_out, m_rwkv_norm, m_rwkv_mix, m_rwkv_w_rkv, m_rwkv_w0, m_rwkv_w1, m_rwkv_w2, m_rwkv_a0, m_rwkv_a1, m_rwkv_a2, m_rwkv_g1, m_rwkv_g2, m_rwkv_k_k, m_rwkv_k_a, m_rwkv_r_k, m_rwkv_ln_w, m_rwkv_ln_b, m_rwkv_w_out, m_ffn_norm, m_ffn_w_up, m_ffn_conv_w, m_ffn_conv_b, m_ffn_w_down, m_final_norm, v_lru_norm, v_lru_w_in, v_lru_b_in, v_lru_conv_w, v_lru_conv_b, v_lru_gate_w, v_lru_gate_b, v_lru_lambda, v_lru_w_out, v_lru_b_out, v_rwkv_norm, v_rwkv_mix, v_rwkv_w_rkv, v_rwkv_w0, v_rwkv_w1, v_rwkv_w2, v_rwkv_a0, v_rwkv_a1, v_rwkv_a2, v_rwkv_g1, v_rwkv_g2, v_rwkv_k_k, v_rwkv_k_a, v_rwkv_r_k, v_rwkv_ln_w, v_rwkv_ln_b, v_rwkv_w_out, v_ffn_norm, v_ffn_w_up, v_ffn_conv_w, v_ffn_conv_b, v_ffn_w_down, v_final_norm):
    given = dict(locals())
    W = {n: given[n] for n in WEIGHTS}
    M1 = {n: given["m_" + n] for n in WEIGHTS}
    V1 = {n: given["v_" + n] for n in WEIGHTS}
    return _step(W, M1, V1, x, loss_target)
```

```python
import functools

import jax
import jax.numpy as jnp
from jax import lax
from jax.experimental import pallas as pl
from jax.experimental.pallas import tpu as pltpu

F32 = jnp.float32
BF16 = jnp.bfloat16
MXU_DTYPE = BF16

HEAD = 64
LRU_C = 8.0
GN_EPS = 64e-5
RMS_EPS = 1e-6
HALO = 16
VMEM_LIMIT = 56 * 1024 * 1024

ADAM_LR, ADAM_B1, ADAM_B2, ADAM_EPS, ADAM_WD, ADAM_STEP = 0.001, 0.9, 0.999, 1e-08, 0.01, 10


def _cparams(sem):
    return pltpu.CompilerParams(dimension_semantics=sem, vmem_limit_bytes=VMEM_LIMIT)


def _pick(n, want):
    if n <= want:
        return n
    t = want
    while t >= 128:
        if n % t == 0:
            return t
        t -= 128
    return n


class _W:
    def __init__(self, arr, layer):
        self.arr, self.layer = arr, layer
        self.shape = (arr.shape[2], N_CHIPS * arr.shape[3])


def _matmul(a, b, mode="nn", bias=None, residual=None, out_dtype=F32, name="mm", tm=1024, tn=1024, tk=1024,
            out_cols_by_chip=False):
    bshape = b.shape
    if mode == "nn":
        (M, K), (K2, N) = a.shape, bshape
    elif mode == "nt":
        (M, K), (N, K2) = a.shape, bshape
    else:
        (K, M), (K2, N) = a.shape, bshape
    assert K == K2, (a.shape, bshape, mode)
    lim_n, lim_k = N, K
    if isinstance(b, _W):
        if mode == "nn":
            lim_n = b.arr.shape[3]
        else:
            lim_k = b.arr.shape[3]
    if out_cols_by_chip:
        lim_n = min(lim_n, N // N_CHIPS)
    tm, tn, tk = _pick(M, tm), _pick(lim_n, tn), _pick(lim_k, tk)
    nk = K // tk
    dims = {"nn": (((1,), (0,)), ((), ())), "nt": (((1,), (1,)), ((), ())), "tn": (((0,), (0,)), ((), ()))}[mode]
    a_spec = {"nn": pl.BlockSpec((tm, tk), lambda i, j, k: (i, k)),
              "nt": pl.BlockSpec((tm, tk), lambda i, j, k: (i, k)),
              "tn": pl.BlockSpec((tk, tm), lambda i, j, k: (k, i))}[mode]
    if isinstance(b, _W):
        lay = b.layer
        if mode == "nn":
            per = b.arr.shape[3] // tn
            b_spec = pl.BlockSpec((None, None, tk, tn), lambda i, j, k: (j // per, lay, k, j % per))
        else:
            assert mode == "nt"
            per = b.arr.shape[3] // tk
            b_spec = pl.BlockSpec((None, None, tn, tk), lambda i, j, k: (k // per, lay, j, k % per))
        b = b.arr
    else:
        b_spec = {"nn": pl.BlockSpec((tk, tn), lambda i, j, k: (k, j)),
                  "nt": pl.BlockSpec((tn, tk), lambda i, j, k: (j, k)),
                  "tn": pl.BlockSpec((tk, tn), lambda i, j, k: (k, j))}[mode]
    if out_cols_by_chip:
        opc = N // N_CHIPS // tn
        out_spec = pl.BlockSpec((None, tm, tn), lambda i, j, k: (j // opc, i, j % opc))
        out_shape = jax.ShapeDtypeStruct((N_CHIPS, M, N // N_CHIPS), out_dtype)
    else:
        out_spec = pl.BlockSpec((tm, tn), lambda i, j, k: (i, j))
        out_shape = jax.ShapeDtypeStruct((M, N), out_dtype)
    in_specs, operands = [a_spec, b_spec], [a, b]
    if bias is not None:
        in_specs.append(pl.BlockSpec((1, tn), lambda i, j, k: (0, j)))
        operands.append(bias.reshape(1, N))
    if residual is not None:
        in_specs.append(pl.BlockSpec((tm, tn), lambda i, j, k: (i, j)))
        operands.append(residual)
    has_bias, has_res = bias is not None, residual is not None

    def kern(*refs):
        a_ref, b_ref = refs[0], refs[1]
        o_ref = refs[2 + has_bias + has_res]

        def finish(r):
            pos = 2
            if has_bias:
                r = r + refs[pos][...].astype(F32)
                pos += 1
            if has_res:
                r = r + refs[pos][...].astype(F32)
            o_ref[...] = r.astype(o_ref.dtype)

        part = lax.dot_general(a_ref[...].astype(MXU_DTYPE), b_ref[...].astype(MXU_DTYPE), dims,
                               preferred_element_type=F32)
        if nk == 1:
            finish(part)
            return
        acc_ref = refs[-1]
        k = pl.program_id(2)

        @pl.when(k == 0)
        def _():
            acc_ref[...] = part

        @pl.when(jnp.logical_and(k > 0, k < nk - 1))
        def _():
            acc_ref[...] += part

        @pl.when(k == nk - 1)
        def _():
            finish(acc_ref[...] + part)

    return pl.pallas_call(
        kern, name=name,
        grid=(M // tm, N // tn, nk),
        in_specs=in_specs,
        out_specs=out_spec,
        out_shape=out_shape,
        scratch_shapes=[pltpu.VMEM((tm, tn), F32)] if nk > 1 else [],
        compiler_params=_cparams(("parallel", "parallel", "arbitrary")),
    )(*operands)


def _tile_call(body, *, rows, prevs=(), nexts=(), fulls=(), row_outs=(), acc_outs=(), tm, T, name):
    M = rows[0].shape[0]
    n_tiles, tps, hb = M // tm, T // tm, tm // HALO
    n_halo_blocks = M // HALO
    nr, npv, nnx, nf, nro, nac = len(rows), len(prevs), len(nexts), len(fulls), len(row_outs), len(acc_outs)

    def kern(*refs):
        i = pl.program_id(0)
        row_refs = refs[:nr]
        prev_refs = refs[nr:nr + npv]
        next_refs = refs[nr + npv:nr + npv + nnx]
        full_refs = refs[nr + npv + nnx:nr + npv + nnx + nf]
        out_refs = refs[nr + npv + nnx + nf:nr + npv + nnx + nf + nro]
        acc_refs = refs[nr + npv + nnx + nf + nro:]
        seq_first = (i % tps) == 0
        seq_last = (i % tps) == (tps - 1)
        outs, accs = body(row_refs, prev_refs, next_refs, full_refs, seq_first, seq_last)
        for r, o in zip(out_refs, outs, strict=True):
            r[...] = o.astype(r.dtype)
        if nac:
            @pl.when(i == 0)
            def _():
                for r in acc_refs:
                    r[...] = jnp.zeros_like(r)
            for r, a in zip(acc_refs, accs, strict=True):
                r[...] += a.astype(F32)

    in_specs = [pl.BlockSpec((tm, a.shape[1]), lambda i: (i, 0)) for a in rows]
    in_specs += [pl.BlockSpec((HALO, rows[k].shape[1]), lambda i: (jnp.maximum(i * hb - 1, 0), 0)) for k in prevs]
    in_specs += [pl.BlockSpec((HALO, rows[k].shape[1]), lambda i: (jnp.minimum((i + 1) * hb, n_halo_blocks - 1), 0))
                 for k in nexts]
    in_specs += [pl.BlockSpec(f.shape, lambda i: (0, 0)) for f in fulls]
    out_specs = [pl.BlockSpec((tm, w), lambda i: (i, 0)) for (w, _) in row_outs]
    out_specs += [pl.BlockSpec(s, lambda i: (0, 0)) for s in acc_outs]
    out_shape = [jax.ShapeDtypeStruct((M, w), dt) for (w, dt) in row_outs]
    out_shape += [jax.ShapeDtypeStruct(s, F32) for s in acc_outs]
    operands = list(rows) + [rows[k] for k in prevs] + [rows[k] for k in nexts] + list(fulls)
    res = pl.pallas_call(
        kern, name=name, grid=(n_tiles,), in_specs=in_specs, out_specs=out_specs, out_shape=out_shape,
        compiler_params=_cparams(("arbitrary",)),
    )(*operands)
    return res[:nro], res[nro:]


def _f(ref):
    return ref[...].astype(F32)


def _sigmoid(x):
    return 1.0 / (1.0 + jnp.exp(-x))


def _softplus(x):
    return jnp.maximum(x, 0.0) + jnp.log(1.0 + jnp.exp(-jnp.abs(x)))


def _neg_expm1(x):
    series = -x * (1.0 + x * (0.5 + x * (1.0 / 6.0) * (1.0 + 0.25 * x)))
    return jnp.where(x > -0.01, series, 1.0 - jnp.exp(x))


def _gelu(x):
    return 0.5 * x * (1.0 + jnp.tanh(0.7978845608028654 * (x + 0.044715 * x * x * x)))


def _rms(x, g):
    return x * lax.rsqrt(jnp.mean(x * x, axis=-1, keepdims=True) + RMS_EPS) * g


@jax.custom_vjp
def _bdot(x, w):
    return jnp.dot(x.astype(MXU_DTYPE), w.astype(MXU_DTYPE), preferred_element_type=F32)


def _bdot_fwd(x, w):
    return _bdot(x, w), (x, w)


def _bdot_bwd(res, ct):
    x, w = res
    ctb = ct.astype(MXU_DTYPE)
    dx = lax.dot_general(ctb, w.astype(MXU_DTYPE), (((1,), (1,)), ((), ())), preferred_element_type=F32)
    dw = lax.dot_general(x.astype(MXU_DTYPE), ctb, (((0,), (0,)), ((), ())), preferred_element_type=F32)
    return dx.astype(x.dtype), dw.astype(w.dtype)


_bdot.defvjp(_bdot_fwd, _bdot_bwd)


@jax.custom_vjp
def _head_sum(x, e, et):
    s = jnp.dot(_split_lhs(x), e, preferred_element_type=F32)
    return jnp.dot(_split_lhs(s), et, preferred_element_type=F32)


def _head_sum_fwd(x, e, et):
    return _head_sum(x, e, et), (e, et)


def _head_sum_bwd(res, ct):
    e, et = res
    return _head_sum(ct, e, et), jnp.zeros_like(e), jnp.zeros_like(et)


_head_sum.defvjp(_head_sum_fwd, _head_sum_bwd)


def _shift_down(main, prev, s, seq_first):
    prev = jnp.where(seq_first, 0.0, prev)
    ext = jnp.concatenate([prev, main], axis=0)
    return pltpu.roll(ext, s, 0)[HALO:]


def _shift_up(main, nxt, s, seq_last):
    nxt = jnp.where(seq_last, 0.0, nxt)
    ext = jnp.concatenate([main, nxt], axis=0)
    n = ext.shape[0]
    return pltpu.roll(ext, n - s, 0)[:n - HALO]


def _colsum(x):
    return jnp.sum(x, axis=0, keepdims=True)


def _pad8(x):
    k = x.shape[0]
    return jnp.concatenate([x, jnp.zeros((8 - k, x.shape[1]), x.dtype)], axis=0) if k < 8 else x


def _rms_fwd(x, g, T, name):
    D = x.shape[1]

    def body(rows, prevs, nexts, fulls, sf, sl):
        return [_rms(_f(rows[0]), _f(fulls[0]))], []

    (h,), _ = _tile_call(body, rows=[x], fulls=[g.reshape(1, D)], row_outs=[(D, BF16)], tm=min(512, T), T=T, name=name)
    return h


def _rms_bwd(x, g, dh, dres, T, name):
    D = x.shape[1]

    def body(rows, prevs, nexts, fulls, sf, sl):
        _, vjp = jax.vjp(_rms, _f(rows[0]), _f(fulls[0]))
        dx, dg = vjp(_f(rows[1]))
        return [dx + _f(rows[2])], [dg]

    (dx,), (dg,) = _tile_call(body, rows=[x, dh, dres], fulls=[g.reshape(1, D)], row_outs=[(D, F32)],
                              acc_outs=[(1, D)], tm=min(512, T), T=T, name=name)
    return dx, dg


def _ffn_conv(u1, prev, cw, cb, sf):
    k = cw.shape[0]
    out = cb + u1 * cw[k - 1:k]
    for j in range(k - 1):
        out = out + _shift_down(u1, prev, k - 1 - j, sf) * cw[j:j + 1]
    return out


def _ffn_fwd(x, p, T, tag):
    M, D = x.shape
    F = p["w_down"].shape[0]
    hf = _rms_fwd(x, p["norm"], T, f"ffn{tag}_norm")
    uf = _matmul(hf, p["w_up"], out_dtype=BF16, name=f"ffn{tag}_up")

    def body(rows, prevs, nexts, fulls, sf, sl):
        u = rows[0]
        gate = _ffn_conv(u[:, :F].astype(F32), prevs[0][:, :F].astype(F32), _f(fulls[0]), _f(fulls[1]), sf)
        return [_gelu(gate) * u[:, F:].astype(F32)], []

    (hid,), _ = _tile_call(body, rows=[uf], prevs=[0], fulls=[p["conv_w"], p["conv_b"].reshape(1, F)],
                           row_outs=[(F, BF16)], tm=min(256, T), T=T, name=f"ffn{tag}_act")
    y = _matmul(hid, p["w_down"], residual=x, name=f"ffn{tag}_down")
    return y, (x, hf, uf, hid)


def _ffn_bwd(dy, saved, p, T, tag):
    x, hf, uf, hid = saved
    M, D = x.shape
    F = p["w_down"].shape[0]
    K = p["conv_w"].shape[0]
    d_hid = _matmul(dy, p["w_down"], mode="nt", out_dtype=BF16, name=f"ffn{tag}_down_dx")
    d_w_down = _matmul(hid, dy, mode="tn", name=f"ffn{tag}_down_dw")

    def body(rows, prevs, nexts, fulls, sf, sl):
        u, dh = rows
        cw, cb = _f(fulls[0]), _f(fulls[1])
        tm = u.shape[0]
        u1c, u1p, u1n = u[:, :F].astype(F32), prevs[0][:, :F].astype(F32), nexts[0][:, :F].astype(F32)
        u1 = jnp.concatenate([u1c, u1n], axis=0)
        u2 = jnp.concatenate([u[:, F:].astype(F32), nexts[0][:, F:].astype(F32)], axis=0)
        dhid = jnp.concatenate([_f(dh), _f(nexts[1])], axis=0)
        gate = _ffn_conv(u1, u1p, cw, cb, sf)
        (act, dact) = jax.jvp(_gelu, (gate,), (jnp.ones_like(gate),))
        d_gate = dhid * u2 * dact
        d_u2 = (dhid * act)[:tm]
        rowid = lax.broadcasted_iota(jnp.int32, d_gate.shape, 0)
        d_gate = jnp.where(jnp.logical_and(sl, rowid >= tm), 0.0, d_gate)
        dgc, dgn = d_gate[:tm], d_gate[tm:]
        d_u1 = dgc * cw[K - 1:K]
        dws = []
        for j in range(K - 1):
            s = K - 1 - j
            d_u1 = d_u1 + _shift_up(dgc, dgn, s, False) * cw[j:j + 1]
            dws.append(_colsum(dgc * _shift_down(u1c, u1p, s, sf)))
        dws.append(_colsum(dgc * u1c))
        d_cw = _pad8(jnp.concatenate(dws, axis=0))
        return [jnp.concatenate([d_u1, d_u2], axis=1)], [d_cw, _colsum(dgc)]

    (d_uf,), (d_cw, d_cb) = _tile_call(
        body, rows=[uf, d_hid], prevs=[0], nexts=[0, 1], fulls=[p["conv_w"], p["conv_b"].reshape(1, F)],
        row_outs=[(2 * F, BF16)], acc_outs=[(8, F), (1, F)], tm=min(256, T), T=T, name=f"ffn{tag}_act_bwd")
    d_hf = _matmul(d_uf, p["w_up"], mode="nt", name=f"ffn{tag}_up_dx")
    d_w_up = _matmul(hf, d_uf, mode="tn", name=f"ffn{tag}_up_dw", out_cols_by_chip=True)
    dx, d_norm = _rms_bwd(x, p["norm"], d_hf, dy, T, f"ffn{tag}_norm_bwd")
    grads = {"norm": d_norm.reshape(D), "w_up": d_w_up, "conv_w": d_cw[:K], "conv_b": d_cb.reshape(F), "w_down": d_w_down}
    return dx, grads


def _lru_conv(u2, prev, cw, cb, sf):
    return _ffn_conv(u2, prev, cw, cb, sf)


def _lru_pre(xr, wg, gb):
    D, GB = xr.shape[1], wg.shape[1] // 2
    parts = [jnp.dot(xr[:, r * GB:(r + 1) * GB].astype(MXU_DTYPE), wg[r * GB:(r + 1) * GB], preferred_element_type=F32)
             for r in range(D // GB)]
    return jnp.concatenate([q[:, :GB] for q in parts] + [q[:, GB:] for q in parts], axis=1) + gb


def _lru_pre_t(xr, dpre, wg, with_dw):
    D, GB = xr.shape[1], wg.shape[1] // 2
    dx, dw = [], []
    for r in range(D // GB):
        dp = jnp.concatenate([dpre[:, r * GB:(r + 1) * GB], dpre[:, D + r * GB:D + (r + 1) * GB]], axis=1).astype(MXU_DTYPE)
        dx.append(lax.dot_general(dp, wg[r * GB:(r + 1) * GB], (((1,), (1,)), ((), ())), preferred_element_type=F32))
        if with_dw:
            dw.append(lax.dot_general(xr[:, r * GB:(r + 1) * GB].astype(MXU_DTYPE), dp, (((0,), (0,)), ((), ())),
                                      preferred_element_type=F32))
    return jnp.concatenate(dx, axis=1), (jnp.concatenate(dw, axis=0) if with_dw else None)


def _lru_gates(xr, pre, lam):
    D = xr.shape[1]
    r_gate, i_gate = _sigmoid(pre[:, :D]), _sigmoid(pre[:, D:])
    log_a = -LRU_C * r_gate * _softplus(-lam)
    a = jnp.exp(log_a)
    mult = jnp.sqrt(_neg_expm1(2.0 * log_a))
    return a, mult * (i_gate * xr)


def _lru_scan(a, b, B, T):
    M, D = a.shape
    cw = _pick(D, 256)
    ng = T // 8

    def kern(a_ref, b_ref, o_ref):
        row = lax.broadcasted_iota(jnp.int32, (8, cw), 0)

        def step(g, carry):
            sl = pl.ds(pl.multiple_of(g * 8, 8), 8)
            a8, b8 = a_ref[sl, :], b_ref[sl, :]
            for s in (1, 2, 4):
                a_sh = jnp.where(row >= s, pltpu.roll(a8, s, 0), 1.0)
                b_sh = jnp.where(row >= s, pltpu.roll(b8, s, 0), 0.0)
                b8 = a8 * b_sh + b8
                a8 = a8 * a_sh
            h8 = a8 * carry + b8
            o_ref[sl, :] = h8
            return jnp.broadcast_to(h8[7:8, :], (8, cw))

        lax.fori_loop(0, ng, step, jnp.zeros((8, cw), F32))

    spec = pl.BlockSpec((T, cw), lambda b, c: (b, c))
    return pl.pallas_call(
        kern, name="lru_scan", grid=(B, D // cw), in_specs=[spec, spec], out_specs=spec,
        out_shape=jax.ShapeDtypeStruct((M, D), F32), compiler_params=_cparams(("parallel", "parallel")),
    )(a, b)


def _lru_scan_bwd(a, hs, dhs, B, T):
    M, D = a.shape
    cw = _pick(D, 256)
    ng = T // 8

    def kern(a_ref, h_ref, d_ref, g_ref, da_ref):
        row = lax.broadcasted_iota(jnp.int32, (8, cw), 0)

        def step(k, carry):
            g_next, a_next = carry
            g = ng - 1 - k
            sl = pl.ds(pl.multiple_of(g * 8, 8), 8)
            a8, d8, h8 = a_ref[sl, :], d_ref[sl, :], h_ref[sl, :]
            c8 = jnp.where(row < 7, pltpu.roll(a8, 7, 0), a_next)
            for s in (1, 2, 4):
                d_sh = jnp.where(row < 8 - s, pltpu.roll(d8, 8 - s, 0), 0.0)
                c_sh = jnp.where(row < 8 - s, pltpu.roll(c8, 8 - s, 0), 1.0)
                d8 = d8 + c8 * d_sh
                c8 = c8 * c_sh
            G8 = d8 + c8 * g_next
            gp = jnp.maximum(g - 1, 0)
            hp8 = h_ref[pl.ds(pl.multiple_of(gp * 8, 8), 8), :]
            hp_last = jnp.where(g > 0, jnp.broadcast_to(hp8[7:8, :], (8, cw)), 0.0)
            hprev = jnp.where(row >= 1, pltpu.roll(h8, 1, 0), hp_last)
            g_ref[sl, :] = G8
            da_ref[sl, :] = G8 * hprev
            return jnp.broadcast_to(G8[0:1, :], (8, cw)), jnp.broadcast_to(a8[0:1, :], (8, cw))

        z = jnp.zeros((8, cw), F32)
        lax.fori_loop(0, ng, step, (z, z))

    spec = pl.BlockSpec((T, cw), lambda b, c: (b, c))
    sh = jax.ShapeDtypeStruct((M, D), F32)
    return pl.pallas_call(
        kern, name="lru_scan_bwd", grid=(B, D // cw), in_specs=[spec, spec, spec], out_specs=[spec, spec],
        out_shape=[sh, sh], compiler_params=_cparams(("parallel", "parallel")),
    )(a, hs, dhs)


def _lru_fwd(x, p, B, T):
    M, D = x.shape
    h0 = _rms_fwd(x, p["norm"], T, "lru_norm")
    u0 = _matmul(h0, p["w_in"], bias=p["b_in"], name="lru_in")
    fulls = [p["conv_w"], p["conv_b"].reshape(1, D), p["wbd"], p["gate_b"].reshape(1, 2 * D), p["lam"].reshape(1, D)]

    def body(rows, prevs, nexts, fulls, sf, sl):
        xr = _lru_conv(rows[0][:, D:], prevs[0][:, D:], _f(fulls[0]), _f(fulls[1]), sf)
        a, bt = _lru_gates(xr, _lru_pre(xr, fulls[2][...], _f(fulls[3])), _f(fulls[4]))
        return [a, bt], []

    (a, bt), _ = _tile_call(body, rows=[u0], prevs=[0], fulls=fulls, row_outs=[(D, F32), (D, F32)],
                            tm=min(256, T), T=T, name="lru_gates")
    hs = _lru_scan(a, bt, B, T)

    def body2(rows, prevs, nexts, fulls, sf, sl):
        return [rows[0][...] * _gelu(rows[1][:, :D])], []

    (out,), _ = _tile_call(body2, rows=[hs, u0], row_outs=[(D, BF16)], tm=min(512, T), T=T, name="lru_mix")
    y = _matmul(out, p["w_out"], bias=p["b_out"], residual=x, name="lru_out")
    return y, (x, h0, u0, a, hs, out)


def _lru_bwd(dy, saved, p, B, T):
    x, h0, u0, a, hs, out = saved
    M, D = x.shape
    K = p["conv_w"].shape[0]
    d_out = _matmul(dy, p["w_out"], mode="nt", name="lru_out_dx")
    d_w_out = _matmul(out, dy, mode="tn", name="lru_out_dw")

    def body(rows, prevs, nexts, fulls, sf, sl):
        do, h, u, dyv = rows[0][...], rows[1][...], rows[2][:, :D], rows[3][...]
        act, dact = jax.jvp(_gelu, (u,), (jnp.ones_like(u),))
        return [do * act, do * h * dact], [_colsum(dyv)]

    (d_hs, d_u1), (d_b_out,) = _tile_call(body, rows=[d_out, hs, u0, dy], row_outs=[(D, F32), (D, F32)],
                                          acc_outs=[(1, D)], tm=min(512, T), T=T, name="lru_mix_bwd")
    g_b, d_a = _lru_scan_bwd(a, hs, d_hs, B, T)
    fulls = [p["conv_w"], p["conv_b"].reshape(1, D), p["wbd"], p["gate_b"].reshape(1, 2 * D), p["lam"].reshape(1, D)]

    def body3(rows, prevs, nexts, fulls, sf, sl):
        u, gb_c, da_c, du1 = rows
        cw, cb, wbd, gbias, lam = _f(fulls[0]), _f(fulls[1]), fulls[2][...], _f(fulls[3]), _f(fulls[4])
        tm = u.shape[0]
        u2c, u2p, u2n = u[:, D:], prevs[0][:, D:], nexts[0][:, D:]
        xr_c = _lru_conv(u2c, u2p, cw, cb, sf)
        xr_n = _lru_conv(u2n, u2c[tm - HALO:], cw, cb, False)
        _, vjp_c = jax.vjp(_lru_gates, xr_c, _lru_pre(xr_c, wbd, gbias), lam)
        dxr_c, dpre_c, d_lam = vjp_c((da_c[...], gb_c[...]))
        dxr_add, d_wbd = _lru_pre_t(xr_c, dpre_c, wbd, True)
        dxr_c = dxr_c + dxr_add
        d_gbias = _colsum(dpre_c)
        _, vjp_n = jax.vjp(lambda t, q: _lru_gates(t, q, lam), xr_n, _lru_pre(xr_n, wbd, gbias))
        dxr_n, dpre_n = vjp_n((nexts[2][...], nexts[1][...]))
        dxr_n = dxr_n + _lru_pre_t(xr_n, dpre_n, wbd, False)[0]
        d_u2 = dxr_c * cw[K - 1:K]
        dws = []
        for j in range(K - 1):
            s = K - 1 - j
            d_u2 = d_u2 + _shift_up(dxr_c, dxr_n, s, sl) * cw[j:j + 1]
            dws.append(_colsum(dxr_c * _shift_down(u2c, u2p, s, sf)))
        dws.append(_colsum(dxr_c * u2c))
        d_u = jnp.concatenate([du1[...], d_u2], axis=1)
        return [d_u], [_pad8(jnp.concatenate(dws, axis=0)), _colsum(dxr_c), d_wbd, d_gbias, d_lam,
                       _colsum(d_u)]

    (d_u0,), (d_cw, d_cb, d_wbd, d_gb, d_lam, d_b_in) = _tile_call(
        body3, rows=[u0, g_b, d_a, d_u1], prevs=[0], nexts=[0, 1, 2], fulls=fulls, row_outs=[(2 * D, BF16)],
        acc_outs=[(8, D), (1, D), p["wbd"].shape, (1, 2 * D), (1, D), (1, 2 * D)], tm=min(256, T), T=T,
        name="lru_gates_bwd")
    d_h0 = _matmul(d_u0, p["w_in"], mode="nt", name="lru_in_dx")
    d_w_in = _matmul(h0, d_u0, mode="tn", name="lru_in_dw", out_cols_by_chip=True)
    dx, d_norm = _rms_bwd(x, p["norm"], d_h0, dy, T, "lru_norm_bwd")
    grads = {"norm": d_norm.reshape(D), "w_in": d_w_in, "b_in": d_b_in.reshape(2 * D), "conv_w": d_cw[:K],
             "conv_b": d_cb.reshape(D), "wbd": d_wbd, "gate_b": d_gb.reshape(2 * D), "lam": d_lam.reshape(D),
             "w_out": d_w_out, "b_out": d_b_out.reshape(D)}
    return dx, grads


def _rwkv_mix(xc, xp, norm, mix, sf):
    h = _rms(xc, norm)
    hp = _rms(xp, norm)
    xx = _shift_down(h, hp, 1, sf) - h
    return h, xx


def _rwkv_pre(k, xw, xa, xg, w0, w1, w2, a0, a1, a2, g1, g2, k_k, k_a, e, et):
    wl = -_softplus(-(w0 + _bdot(jnp.tanh(_bdot(xw, w1)), w2))) - 0.5
    decay = jnp.exp(-jnp.exp(wl))
    a = _sigmoid(a0 + _bdot(_bdot(xa, a1), a2))
    g = _bdot(_sigmoid(_bdot(xg, g1)), g2)
    kk = k * k_k
    nrm = jnp.sqrt(_head_sum(kk * kk, e, et))
    kk = kk / jnp.maximum(nrm, 1e-12)
    k2 = k * (1.0 + (a - 1.0) * k_a)
    return decay, k2, -kk, kk * a, g


def _rwkv_post(y, r, k2, v, g, ln_w, ln_b, r_k, e, et):
    inv = 1.0 / HEAD
    mu = _head_sum(y, e, et) * inv
    yc = y - mu
    var = _head_sum(yc * yc, e, et) * inv
    yn = yc * lax.rsqrt(var + GN_EPS) * ln_w + ln_b
    bonus = _head_sum(r * k2 * r_k, e, et) * v
    return (yn + bonus) * g


def _seg_lane_sums(x, lo_mask):
    s0 = jnp.sum(jnp.where(lo_mask, x, 0.0), axis=1, keepdims=True)
    s1 = jnp.sum(jnp.where(lo_mask, 0.0, x), axis=1, keepdims=True)
    return s0, s1


def _seg_lane_sum(x, lo_mask):
    s0, s1 = _seg_lane_sums(x, lo_mask)
    return jnp.where(lo_mask, s0, s1)


def _pair_consts():
    lane = lax.broadcasted_iota(jnp.int32, (HEAD, 128), 1)
    sub = lax.broadcasted_iota(jnp.int32, (HEAD, 128), 0)
    return lane < HEAD, (jnp.bitwise_and(lane, HEAD - 1) == sub).astype(F32)


def _pair_ones():
    head = jnp.arange(128) // HEAD
    return (head[:, None] == head[None, :]).astype(MXU_DTYPE)


def _split_lhs(x):
    hi = x.astype(MXU_DTYPE)
    return jnp.concatenate([hi, (x - hi.astype(F32)).astype(MXU_DTYPE)], axis=1)


def _spread_lhs(diag, row):
    return (diag * row).astype(MXU_DTYPE)


def _rwkv_scan(r, w, k, v, a, b, B, T):
    M, D = r.shape
    HP, PG, TC, NC, chains = _scan_plan(B, T, D, pairs=2, chunk=64)
    NS = len(chains) * 8
    NG = TC // 8

    def kern(r_ref, w_ref, k_ref, v_ref, a_ref, b_ref, ones_ref, y_ref, st_ref, S_ref, lv_ref, rv_ref, ly_ref, ry_ref):
        c = pl.program_id(1)

        @pl.when(c == 0)
        def _():
            S_ref[...] = jnp.zeros_like(S_ref)

        lo, diag = _pair_consts()
        row8 = lax.broadcasted_iota(jnp.int32, (8, 128), 0)

        def blk(idx):
            return pl.ds(idx * HEAD, HEAD)

        def rows_of(gi):
            return pl.ds(pl.multiple_of(gi * 8, 8), 8)

        def spread(gi, slot):
            for ci, (bi, p) in enumerate(chains):
                v8 = v_ref[bi, rows_of(gi), p * 128:(p + 1) * 128]
                for j in range(8):
                    lv_ref[slot, blk(ci * 8 + j), :] = _spread_lhs(diag, v8[j:j + 1, :])
            rv_ref[slot] = jnp.dot(lv_ref[slot], ones_ref[...], preferred_element_type=F32)

        def recur(gi, slot):
            sl = rows_of(gi)
            tiles = [[ref[bi, sl, p * 128:(p + 1) * 128] for ref in (r_ref, w_ref, k_ref, a_ref, b_ref)]
                     for bi, p in chains]
            S = [S_ref[ci] for ci in range(len(chains))]
            for j in range(8):
                for ci, (bi, p) in enumerate(chains):
                    r8, w8, k8, a8, b8 = tiles[ci]
                    idx = ci * 8 + j
                    st_ref[p, bi, gi * 8 + j] = S[ci]
                    sa = _seg_lane_sum(S[ci] * a8[j:j + 1, :], lo)
                    S[ci] = S[ci] * w8[j:j + 1, :] + sa * b8[j:j + 1, :] + rv_ref[slot, blk(idx), :] * k8[j:j + 1, :]
                    ly_ref[slot, blk(idx), :] = (S[ci] * r8[j:j + 1, :]).astype(MXU_DTYPE)
            for ci in range(len(chains)):
                S_ref[ci] = S[ci]

        def emit(gi, slot):
            ry_ref[slot] = jnp.dot(ly_ref[slot], ones_ref[...], preferred_element_type=F32)
            for ci, (bi, p) in enumerate(chains):
                y8 = jnp.zeros((8, 128), F32)
                for j in range(8):
                    y8 = jnp.where(row8 == j, _colsum(diag * ry_ref[slot, blk(ci * 8 + j), :]), y8)
                y_ref[bi, rows_of(gi), p * 128:(p + 1) * 128] = y8

        spread(0, 0)
        ly_ref[1] = jnp.zeros_like(ly_ref[1])

        def two_groups(m, _):
            g0, g1 = 2 * m, 2 * m + 1
            spread(g1, 1)
            recur(g0, 0)
            emit(jnp.maximum(g0 - 1, 0), 1)
            spread(jnp.minimum(g1 + 1, NG - 1), 0)
            recur(g1, 1)
            emit(g0, 0)
            return 0

        lax.fori_loop(0, NG // 2, two_groups, 0)
        emit(NG - 1, 1)

    spec = pl.BlockSpec((B, TC, 128 * PG), lambda hp, c: (0, c, hp))
    st_spec = pl.BlockSpec((PG, B, TC, HEAD, 128), lambda hp, c: (hp, 0, c, 0, 0))
    y, st = pl.pallas_call(
        kern, name="rwkv_scan", grid=(HP // PG, NC),
        in_specs=[spec] * 6 + [pl.BlockSpec((128, 128), lambda hp, c: (0, 0))], out_specs=[spec, st_spec],
        out_shape=[jax.ShapeDtypeStruct((B, T, D), F32), jax.ShapeDtypeStruct((HP, B, T, HEAD, 128), F32)],
        scratch_shapes=[pltpu.VMEM((len(chains), HEAD, 128), F32),
                        pltpu.VMEM((2, NS * HEAD, 128), MXU_DTYPE), pltpu.VMEM((2, NS * HEAD, 128), F32),
                        pltpu.VMEM((2, NS * HEAD, 128), MXU_DTYPE), pltpu.VMEM((2, NS * HEAD, 128), F32)],
        compiler_params=_cparams(("parallel", "arbitrary")),
    )(*[x.reshape(B, T, D) for x in (r, w, k, v, a, b)], _pair_ones())
    return y.reshape(M, D), st


def _scan_plan(B, T, D, pairs, chunk):
    HP = D // 128
    PG = pairs if HP % pairs == 0 else 1
    TC = min(chunk, T)
    assert TC % 16 == 0 and T % TC == 0
    return HP, PG, TC, T // TC, [(bi, p) for bi in range(B) for p in range(PG)]


def _rwkv_scan_bwd(r, w, k, v, a, b, st, dy, B, T):
    M, D = r.shape
    HP, PG, TC, NC, chains = _scan_plan(B, T, D, pairs=1, chunk=128)
    NS = len(chains) * 8

    NG = TC // 8

    def kern(r_ref, w_ref, k_ref, v_ref, a_ref, b_ref, st_ref, dy_ref, ones_ref,
             dr_ref, dw_ref, dk_ref, dv_ref, da_ref, db_ref, dS_ref, lp_ref, rp_ref, lq_ref, rq_ref):
        c = pl.program_id(1)

        @pl.when(c == 0)
        def _():
            dS_ref[...] = jnp.zeros_like(dS_ref)

        lo, diag = _pair_consts()
        row8 = lax.broadcasted_iota(jnp.int32, (8, 128), 0)

        def blk(idx):
            return pl.ds(idx * HEAD, HEAD)

        def rows_of(gi):
            return pl.ds(pl.multiple_of(gi * 8, 8), 8)

        def spread(gi, slot):
            sl = rows_of(gi)
            for ci, (bi, p) in enumerate(chains):
                lanes = slice(p * 128, (p + 1) * 128)
                v8, dy8, a8 = v_ref[bi, sl, lanes], dy_ref[bi, sl, lanes], a_ref[bi, sl, lanes]
                for j in range(8):
                    idx = ci * 8 + j
                    lp_ref[slot, blk(idx), :] = _spread_lhs(diag, v8[j:j + 1, :])
                    lp_ref[slot, blk(NS + idx), :] = _spread_lhs(diag, dy8[j:j + 1, :])
                    lp_ref[slot, blk(2 * NS + idx), :] = (st_ref[p, bi, gi * 8 + j] * a8[j:j + 1, :]).astype(MXU_DTYPE)
            rp_ref[slot] = jnp.dot(lp_ref[slot], ones_ref[...], preferred_element_type=F32)

        def recur(gi, slot):
            sl = rows_of(gi)
            tiles = [[ref[bi, sl, p * 128:(p + 1) * 128] for ref in (r_ref, w_ref, k_ref, a_ref, b_ref)]
                     for bi, p in chains]
            dS = [dS_ref[ci] for ci in range(len(chains))]
            acc = [[jnp.zeros((8, 128), F32) for _ in range(5)] for _ in chains]
            St = [None] * len(chains)
            for j in range(7, -1, -1):
                for ci, (bi, p) in enumerate(chains):
                    r8, w8, k8, a8, b8 = tiles[ci]
                    rj, wj, kj, aj, bj = r8[j:j + 1, :], w8[j:j + 1, :], k8[j:j + 1, :], a8[j:j + 1, :], b8[j:j + 1, :]
                    idx = ci * 8 + j
                    Sp = st_ref[p, bi, gi * 8 + j]
                    vb, dyb, sa = rp_ref[slot, blk(idx), :], rp_ref[slot, blk(NS + idx), :], rp_ref[slot, blk(2 * NS + idx), :]
                    if j == 7:
                        St[ci] = Sp * wj + sa * bj + vb * kj
                    d = dS[ci] + dyb * rj
                    dsa = _seg_lane_sum(d * bj, lo)
                    lq_ref[slot, blk(idx), :] = (d * kj).astype(MXU_DTYPE)
                    rows = (_colsum(St[ci] * dyb), _colsum(d * Sp), _colsum(d * vb), _colsum(Sp * dsa), _colsum(d * sa))
                    acc[ci] = [jnp.where(row8 == j, rw, a8_) for rw, a8_ in zip(rows, acc[ci], strict=True)]
                    dS[ci] = d * wj + dsa * aj
                    St[ci] = Sp
            for ci, (bi, p) in enumerate(chains):
                dS_ref[ci] = dS[ci]
                for ref, a8_ in zip((dr_ref, dw_ref, dk_ref, da_ref, db_ref), acc[ci], strict=True):
                    ref[bi, sl, p * 128:(p + 1) * 128] = a8_

        def emit(gi, slot):
            rq_ref[slot] = jnp.dot(lq_ref[slot], ones_ref[...], preferred_element_type=F32)
            for ci, (bi, p) in enumerate(chains):
                dv8 = jnp.zeros((8, 128), F32)
                for j in range(8):
                    dv8 = jnp.where(row8 == j, _colsum(diag * rq_ref[slot, blk(ci * 8 + j), :]), dv8)
                dv_ref[bi, rows_of(gi), p * 128:(p + 1) * 128] = dv8

        spread(NG - 1, 0)
        lq_ref[1] = jnp.zeros_like(lq_ref[1])

        def two_groups(m, _):
            g0, g1 = NG - 1 - 2 * m, NG - 2 - 2 * m
            spread(g1, 1)
            recur(g0, 0)
            emit(jnp.minimum(g0 + 1, NG - 1), 1)
            spread(jnp.maximum(g1 - 1, 0), 0)
            recur(g1, 1)
            emit(g0, 0)
            return 0

        lax.fori_loop(0, NG // 2, two_groups, 0)
        emit(0, 1)

    spec = pl.BlockSpec((B, TC, 128 * PG), lambda hp, c: (0, NC - 1 - c, hp))
    st_spec = pl.BlockSpec((PG, B, TC, HEAD, 128), lambda hp, c: (hp, 0, NC - 1 - c, 0, 0))
    sh = jax.ShapeDtypeStruct((B, T, D), F32)
    outs = pl.pallas_call(
        kern, name="rwkv_scan_bwd", grid=(HP // PG, NC),
        in_specs=[spec] * 6 + [st_spec, spec, pl.BlockSpec((128, 128), lambda hp, c: (0, 0))], out_specs=[spec] * 6,
        out_shape=[sh] * 6,
        scratch_shapes=[pltpu.VMEM((len(chains), HEAD, 128), F32),
                        pltpu.VMEM((2, 3 * NS * HEAD, 128), MXU_DTYPE), pltpu.VMEM((2, 3 * NS * HEAD, 128), F32),
                        pltpu.VMEM((2, NS * HEAD, 128), MXU_DTYPE), pltpu.VMEM((2, NS * HEAD, 128), F32)],
        compiler_params=_cparams(("parallel", "arbitrary")),
    )(*[x.reshape(B, T, D) for x in (r, w, k, v, a, b)], st, dy.reshape(B, T, D), _pair_ones())
    return [o.reshape(M, D) for o in outs]


def _head_mats(D):
    ch = jnp.arange(D) // HEAD
    e = (ch[:, None] == jnp.arange(128)[None, :]).astype(MXU_DTYPE)
    return jnp.concatenate([e, e], axis=0), jnp.concatenate([e.T, e.T], axis=0)


def _rwkv_fwd(x, p, B, T):
    M, D = x.shape
    e, et = _head_mats(D)
    norm = p["norm"].reshape(1, D)

    def body(rows, prevs, nexts, fulls, sf, sl):
        h, xx = _rwkv_mix(rows[0][...], prevs[0][...], _f(fulls[0]), None, sf)
        mix = _f(fulls[1])
        return [h + xx * mix[i:i + 1] for i in range(6)], []

    xs, _ = _tile_call(body, rows=[x], prevs=[0], fulls=[norm, _pad8(p["mix"])], row_outs=[(D, BF16)] * 6,
                       tm=min(256, T), T=T, name="rwkv_mix")
    r = _matmul(xs[0], p["w_r"], name="rwkv_r")
    k = _matmul(xs[1], p["w_k"], name="rwkv_k")
    v = _matmul(xs[2], p["w_v"], name="rwkv_v")
    pre_fulls = [p["w0"].reshape(1, D), p["w1"], p["w2"], p["a0"].reshape(1, D), p["a1"], p["a2"], p["g1"], p["g2"],
                 p["k_k"].reshape(1, D), p["k_a"].reshape(1, D), e, et]

    def body2(rows, prevs, nexts, fulls, sf, sl):
        outs = _rwkv_pre(rows[0][...], _f(rows[1]), _f(rows[2]), _f(rows[3]), *[f[...] for f in fulls])
        return list(outs), []

    (decay, k2, kkn, bb, g), _ = _tile_call(body2, rows=[k, xs[3], xs[4], xs[5]], fulls=pre_fulls,
                                            row_outs=[(D, F32)] * 5, tm=min(256, T), T=T, name="rwkv_pre")
    y, st = _rwkv_scan(r, decay, k2, v, kkn, bb, B, T)
    post_fulls = [p["ln_w"].reshape(1, D), p["ln_b"].reshape(1, D), p["r_k"].reshape(1, D), e, et]

    def body3(rows, prevs, nexts, fulls, sf, sl):
        return [_rwkv_post(*[rr[...] for rr in rows], *[f[...] for f in fulls])], []

    (z,), _ = _tile_call(body3, rows=[y, r, k2, v, g], fulls=post_fulls, row_outs=[(D, BF16)], tm=min(256, T), T=T,
                         name="rwkv_post")
    out = _matmul(z, p["w_out"], residual=x, name="rwkv_out")
    return out, (x, xs, r, k, v, decay, k2, kkn, bb, g, y, st, z)


def _rwkv_bwd(dout, saved, p, B, T):
    x, xs, r, k, v, decay, k2, kkn, bb, g, y, st, z = saved
    M, D = x.shape
    e, et = _head_mats(D)
    d_z = _matmul(dout, p["w_out"], mode="nt", name="rwkv_out_dx")
    d_w_out = _matmul(z, dout, mode="tn", name="rwkv_out_dw")
    post_fulls = [p["ln_w"].reshape(1, D), p["ln_b"].reshape(1, D), p["r_k"].reshape(1, D), e, et]

    def body(rows, prevs, nexts, fulls, sf, sl):
        prim = [rr[...] for rr in rows[:5]] + [f[...] for f in fulls]
        _, vjp = jax.vjp(_rwkv_post, *prim)
        ct = vjp(rows[5][...])
        return list(ct[:5]), list(ct[5:8])

    (d_y, d_r1, d_k21, d_v1, d_g), (d_ln_w, d_ln_b, d_r_k) = _tile_call(
        body, rows=[y, r, k2, v, g, d_z], fulls=post_fulls, row_outs=[(D, F32)] * 5, acc_outs=[(1, D)] * 3,
        tm=min(256, T), T=T, name="rwkv_post_bwd")
    d_r2, d_w, d_k22, d_v2, d_kkn, d_bb = _rwkv_scan_bwd(r, decay, k2, v, kkn, bb, st, d_y, B, T)
    pre_fulls = [p["w0"].reshape(1, D), p["w1"], p["w2"], p["a0"].reshape(1, D), p["a1"], p["a2"], p["g1"], p["g2"],
                 p["k_k"].reshape(1, D), p["k_a"].reshape(1, D), e, et]

    def body2(rows, prevs, nexts, fulls, sf, sl):
        prim = [rows[0][...], _f(rows[1]), _f(rows[2]), _f(rows[3])] + [f[...] for f in fulls]
        _, vjp = jax.vjp(_rwkv_pre, *prim)
        ct = vjp((rows[4][...], rows[5][...] + rows[6][...], rows[7][...], rows[8][...], rows[9][...]))
        d_r = rows[10][...] + rows[11][...]
        d_v = rows[12][...] + rows[13][...]
        return [ct[0], ct[1], ct[2], ct[3], d_r, d_v], [c.astype(F32) for c in ct[4:14]]

    acc_shapes = [f.shape for f in pre_fulls[:10]]
    (d_k, d_xw, d_xa, d_xg, d_r, d_v), pgr = _tile_call(
        body2, rows=[k, xs[3], xs[4], xs[5], d_w, d_k21, d_k22, d_kkn, d_bb, d_g, d_r1, d_r2, d_v1, d_v2],
        fulls=pre_fulls, row_outs=[(D, BF16), (D, F32), (D, F32), (D, F32), (D, BF16), (D, BF16)], acc_outs=acc_shapes,
        tm=min(128, T), T=T, name="rwkv_pre_bwd")
    d_xr = _matmul(d_r, p["w_r"], mode="nt", name="rwkv_r_dx")
    d_xk = _matmul(d_k, p["w_k"], mode="nt", name="rwkv_k_dx")
    d_xv = _matmul(d_v, p["w_v"], mode="nt", name="rwkv_v_dx")
    d_wr = _matmul(xs[0], d_r, mode="tn", name="rwkv_r_dw")
    d_wk = _matmul(xs[1], d_k, mode="tn", name="rwkv_k_dw")
    d_wv = _matmul(xs[2], d_v, mode="tn", name="rwkv_v_dw")
    norm = p["norm"].reshape(1, D)

    def body3(rows, prevs, nexts, fulls, sf, sl):
        xc, xp = rows[0][...], prevs[0][...]
        nrm, mix = _f(fulls[0]), _f(fulls[1])
        h, xx = _rwkv_mix(xc, xp, nrm, None, sf)
        dxs = [rows[1 + i][...] for i in range(6)]
        dxs_n = [nexts[i][...] for i in range(6)]
        d_h = jnp.zeros_like(h)
        d_sh = jnp.zeros_like(h)
        d_sh_n = jnp.zeros_like(dxs_n[0])
        dmix = []
        for i in range(6):
            m = mix[i:i + 1]
            d_h = d_h + dxs[i] * (1.0 - m)
            d_sh = d_sh + dxs[i] * m
            d_sh_n = d_sh_n + dxs_n[i] * m
            dmix.append(_colsum(dxs[i] * xx))
        d_h = d_h + _shift_up(d_sh, d_sh_n, 1, sl)
        _, vjp = jax.vjp(_rms, xc, nrm)
        dx, dn = vjp(d_h)
        return [dx + rows[7][...]], [dn, _pad8(jnp.concatenate(dmix, axis=0))]

    (dx,), (d_norm, d_mix) = _tile_call(
        body3, rows=[x, d_xr, d_xk, d_xv, d_xw, d_xa, d_xg, dout], prevs=[0], nexts=[1, 2, 3, 4, 5, 6],
        fulls=[norm, _pad8(p["mix"])], row_outs=[(D, F32)], acc_outs=[(1, D), (8, D)], tm=min(256, T), T=T,
        name="rwkv_mix_bwd")
    names = ["w0", "w1", "w2", "a0", "a1", "a2", "g1", "g2", "k_k", "k_a"]
    grads = {n: gr.reshape(p[n].shape) for n, gr in zip(names, pgr, strict=True)}
    grads.update({"norm": d_norm.reshape(D), "mix": d_mix[:6], "w_r": d_wr, "w_k": d_wk, "w_v": d_wv,
                  "r_k": d_r_k.reshape(p["r_k"].shape), "ln_w": d_ln_w.reshape(D), "ln_b": d_ln_b.reshape(D),
                  "w_out": d_w_out})
    return dx, grads


def _loss_head(x, g, tgt, T):
    M, D = x.shape

    def body(rows, prevs, nexts, fulls, sf, sl):
        xv, gv = rows[0][...], _f(fulls[0])
        yv, vjp = jax.vjp(_rms, xv, gv)
        err = yv - rows[1][...]
        dx, dg = vjp(err * (1.0 / D))
        part = jnp.sum(_colsum(err * err), axis=1, keepdims=True) * (0.5 / D)
        return [dx], [dg, jnp.broadcast_to(part, (1, 128))]

    (dx,), (dg, loss) = _tile_call(body, rows=[x, tgt], fulls=[g.reshape(1, D)], row_outs=[(D, F32)],
                                   acc_outs=[(1, D), (1, 128)], tm=min(512, T), T=T, name="loss_head")
    return loss[0, 0], dx, dg.reshape(D)


def _local_step(x3, tgt3, P):
    B, T, D = x3.shape
    x, tgt = x3.reshape(B * T, D), tgt3.reshape(B * T, D)
    x1, s_lru = _lru_fwd(x, P["lru"], B, T)
    x2, s_f0 = _ffn_fwd(x1, P["ffn0"], T, "0")
    x3_, s_rw = _rwkv_fwd(x2, P["rwkv"], B, T)
    x4, s_f1 = _ffn_fwd(x3_, P["ffn1"], T, "1")
    loss, d4, d_fn = _loss_head(x4, P["final_norm"], tgt, T)
    d3, g_f1 = _ffn_bwd(d4, s_f1, P["ffn1"], T, "1")
    d2, g_rw = _rwkv_bwd(d3, s_rw, P["rwkv"], B, T)
    d1, g_f0 = _ffn_bwd(d2, s_f0, P["ffn0"], T, "0")
    d0, g_lru = _lru_bwd(d1, s_lru, P["lru"], B, T)
    return loss, d0.reshape(B, T, D), {"lru": g_lru, "ffn0": g_f0, "rwkv": g_rw, "ffn1": g_f1, "final_norm": d_fn}


WEIGHTS = ['lru_norm', 'lru_w_in', 'lru_b_in', 'lru_conv_w', 'lru_conv_b', 'lru_gate_w', 'lru_gate_b', 'lru_lambda',
           'lru_w_out', 'lru_b_out', 'rwkv_norm', 'rwkv_mix', 'rwkv_w_rkv', 'rwkv_w0', 'rwkv_w1', 'rwkv_w2', 'rwkv_a0',
           'rwkv_a1', 'rwkv_a2', 'rwkv_g1', 'rwkv_g2', 'rwkv_k_k', 'rwkv_k_a', 'rwkv_r_k', 'rwkv_ln_w', 'rwkv_ln_b',
           'rwkv_w_out', 'ffn_norm', 'ffn_w_up', 'ffn_conv_w', 'ffn_conv_b', 'ffn_w_down', 'final_norm']
SHARD_AXIS = {'lru_w_in': 2, 'lru_conv_w': 2, 'lru_w_out': 1, 'rwkv_norm': 1, 'rwkv_mix': 2, 'rwkv_w_rkv': 2,
              'rwkv_w0': 1, 'rwkv_w1': 1, 'rwkv_w2': 2, 'rwkv_a0': 1, 'rwkv_a1': 1, 'rwkv_a2': 2, 'rwkv_g1': 1,
              'rwkv_g2': 2, 'rwkv_k_k': 1, 'rwkv_k_a': 1, 'rwkv_ln_w': 1, 'rwkv_ln_b': 1, 'rwkv_w_out': 1,
              'ffn_w_up': 2, 'ffn_conv_w': 2, 'ffn_w_down': 1}
MXU_WEIGHTS = ('lru_w_in', 'lru_w_out', 'rwkv_w_rkv', 'rwkv_w_out', 'ffn_w_up', 'ffn_w_down')
N_CHIPS = 4
LANES = 1024


def _pack(arrs, dtype, row_mult):
    flat = jnp.concatenate([a.reshape(-1).astype(dtype) for a in arrs])
    n = flat.shape[0]
    unit = row_mult * LANES
    tot = -(-n // unit) * unit
    if tot > n:
        flat = jnp.concatenate([flat, jnp.zeros((tot - n,), dtype)])
    return flat.reshape(tot // LANES, LANES)


def _unpack(buf, shapes):
    flat = buf.reshape(-1)
    out, off = [], 0
    for s in shapes:
        n = 1
        for d in s:
            n *= d
        out.append(flat[off:off + n].reshape(s))
        off += n
    return out


def _to_shards(full, axis):
    return jnp.stack(jnp.split(full, N_CHIPS, axis=axis))


MESH_ID = pl.DeviceIdType.MESH


ANY_SPEC = pl.BlockSpec(memory_space=pl.ANY)
COMM_PARAMS = pltpu.CompilerParams(has_side_effects=True)


def _mesh_place():
    x, y, c = lax.axis_index("x"), lax.axis_index("y"), lax.axis_index("c")
    return x, y, c, 2 * x + y, [(1 - x, y), (x, 1 - y), (1 - x, 1 - y)]


def _gather_all(arrs, name):
    n = len(arrs)

    def body(*refs):
        ins, outs = refs[:n], refs[n:2 * n]
        send_sems, recv_sems, local_sems = refs[2 * n:]
        x, y, c, p, chips = _mesh_place()
        sibling = (x, y, 1 - c)

        def rows(a, which):
            h = arrs[a].shape[0] // 2
            return pl.ds(which * h, h)

        def copy(a, k, region, src, to):
            return pltpu.make_async_remote_copy(src_ref=src, dst_ref=region, send_sem=send_sems.at[a, k],
                                                recv_sem=recv_sems.at[a, k], device_id=to, device_id_type=MESH_ID)

        local = [pltpu.make_async_copy(ins[a], outs[a].at[p], local_sems.at[a]) for a in range(n)]
        for cp in local:
            cp.start()
        first = [copy(a, j, outs[a].at[p, rows(a, c)], ins[a].at[rows(a, c)], (qx, qy, c))
                 for a in range(n) for j, (qx, qy) in enumerate(chips)]
        for cp in first:
            cp.start()
        passed = []
        for a in range(n):
            for j, (qx, qy) in enumerate(chips):
                region = outs[a].at[2 * qx + qy, rows(a, c)]
                copy(a, j, region, region, (qx, qy, c)).wait_recv()
                fw = copy(a, 3 + j, region, region, sibling)
                fw.start()
                passed.append(fw)
        for a in range(n):
            for j, (qx, qy) in enumerate(chips):
                region = outs[a].at[2 * qx + qy, rows(a, 1 - c)]
                copy(a, 3 + j, region, region, sibling).wait_recv()
        for cp in first + passed:
            cp.wait_send()
        for cp in local:
            cp.wait()

    return pl.pallas_call(
        body, name=name, out_shape=[jax.ShapeDtypeStruct((N_CHIPS,) + a.shape, a.dtype) for a in arrs],
        in_specs=[ANY_SPEC] * n, out_specs=[ANY_SPEC] * n,
        scratch_shapes=[pltpu.SemaphoreType.DMA((n, 6)), pltpu.SemaphoreType.DMA((n, 6)), pltpu.SemaphoreType.DMA((n,))],
        compiler_params=COMM_PARAMS,
    )(*arrs)


def _pair_swap_all(gs, rep, name):
    n = len(gs)

    def body(*refs):
        ins, outs = refs[:n + 1], refs[n + 1:2 * n + 2]
        send_sems, recv_sems = refs[2 * n + 2:]
        x, y, c, _, _ = _mesh_place()
        copies = []
        for a in range(n + 1):
            src = ins[a]
            if a < n:
                h = gs[a].shape[1] // 2
                src = src.at[:, pl.ds((1 - c) * h, h), :]
            copies.append(pltpu.make_async_remote_copy(src_ref=src, dst_ref=outs[a], send_sem=send_sems.at[a],
                                                       recv_sem=recv_sems.at[a], device_id=(x, y, 1 - c),
                                                       device_id_type=MESH_ID))
        for cp in copies:
            cp.start()
        for cp in copies:
            cp.wait()

    shapes = [jax.ShapeDtypeStruct((N_CHIPS, g.shape[1] // 2, g.shape[2]), g.dtype) for g in gs]
    shapes.append(jax.ShapeDtypeStruct(rep.shape, rep.dtype))
    res = pl.pallas_call(
        body, name=name, out_shape=shapes, in_specs=[ANY_SPEC] * (n + 1), out_specs=[ANY_SPEC] * (n + 1),
        scratch_shapes=[pltpu.SemaphoreType.DMA((n + 1,)), pltpu.SemaphoreType.DMA((n + 1,))],
        compiler_params=COMM_PARAMS,
    )(*gs, rep)
    return res[:n], res[n]


def _chip_exchange_all(ps, rep, name):
    n = len(ps)

    def body(*refs):
        ins, outs = refs[:n + 1], refs[n + 1:2 * n + 2]
        send_sems, recv_sems, local_sems = refs[2 * n + 2:]
        x, y, c, p, chips = _mesh_place()

        def src(a, q):
            return ins[a].at[q] if a < n else ins[a]

        local = [pltpu.make_async_copy(src(a, p), outs[a].at[p], local_sems.at[a]) for a in range(n + 1)]
        for cp in local:
            cp.start()
        sends, recvs = [], []
        for a in range(n + 1):
            for j, (qx, qy) in enumerate(chips):
                q = 2 * qx + qy
                for dst, keep in ((outs[a].at[p], sends), (outs[a].at[q], recvs)):
                    keep.append(pltpu.make_async_remote_copy(
                        src_ref=src(a, q), dst_ref=dst, send_sem=send_sems.at[a, j], recv_sem=recv_sems.at[a, j],
                        device_id=(qx, qy, c), device_id_type=MESH_ID))
        for cp in sends:
            cp.start()
        for cp in recvs:
            cp.wait_recv()
        for cp in sends:
            cp.wait_send()
        for cp in local:
            cp.wait()

    shapes = [jax.ShapeDtypeStruct(g.shape, g.dtype) for g in ps]
    shapes.append(jax.ShapeDtypeStruct((N_CHIPS,) + rep.shape, rep.dtype))
    res = pl.pallas_call(
        body, name=name, out_shape=shapes, in_specs=[ANY_SPEC] * (n + 1), out_specs=[ANY_SPEC] * (n + 1),
        scratch_shapes=[pltpu.SemaphoreType.DMA((n + 1, 3)), pltpu.SemaphoreType.DMA((n + 1, 3)),
                        pltpu.SemaphoreType.DMA((n + 1,))],
        compiler_params=COMM_PARAMS,
    )(*ps, rep)
    return res[:n], res[n]


def _half_swap_all(ts, name):
    n = len(ts)

    def body(*refs):
        ins, outs = refs[:n], refs[n:2 * n]
        send_sems, recv_sems = refs[2 * n:]
        x, y, c, _, _ = _mesh_place()
        copies = [pltpu.make_async_remote_copy(src_ref=ins[a], dst_ref=outs[a], send_sem=send_sems.at[a],
                                               recv_sem=recv_sems.at[a], device_id=(x, y, 1 - c), device_id_type=MESH_ID)
                  for a in range(n)]
        for cp in copies:
            cp.start()
        for cp in copies:
            cp.wait()

    return pl.pallas_call(
        body, name=name, out_shape=[jax.ShapeDtypeStruct(t.shape, t.dtype) for t in ts],
        in_specs=[ANY_SPEC] * n, out_specs=[ANY_SPEC] * n,
        scratch_shapes=[pltpu.SemaphoreType.DMA((n,)), pltpu.SemaphoreType.DMA((n,))],
        compiler_params=COMM_PARAMS,
    )(*ts)


def _pick_rows(R, cap=256):
    for t in (512, 256, 128, 64, 32, 16, 8):
        if t <= cap and R % t == 0:
            return t
    return R


def _pair_sum(g, got, name):
    _, R, C = g.shape
    h = R // 2
    th = _pick_rows(h)

    def kern(g_ref, got_ref, o_ref):
        both = g_ref[...]
        mine = jnp.where(lax.axis_index("c") == 0, both[0], both[1])
        o_ref[...] = (mine + got_ref[...]).astype(o_ref.dtype)

    return pl.pallas_call(
        kern, name=name, grid=(N_CHIPS, h // th),
        in_specs=[pl.BlockSpec((None, 2, th, C), lambda q, i: (q, 0, i, 0)), pl.BlockSpec((None, th, C), lambda q, i: (q, i, 0))],
        out_specs=pl.BlockSpec((None, th, C), lambda q, i: (q, i, 0)),
        out_shape=jax.ShapeDtypeStruct((N_CHIPS, h, C), BF16), compiler_params=_cparams(("parallel", "parallel")),
    )(g.reshape(N_CHIPS, 2, h, C), got)


def _rep_pair_sum(rep, got, name):
    R, C = rep.shape
    tr = _pick_rows(R)

    def kern(a_ref, b_ref, o_ref):
        o_ref[...] = (a_ref[...] + b_ref[...]).astype(o_ref.dtype)

    spec = pl.BlockSpec((tr, C), lambda i: (i, 0))
    return pl.pallas_call(kern, name=name, grid=(R // tr,), in_specs=[spec, spec], out_specs=spec,
                          out_shape=jax.ShapeDtypeStruct((R, C), BF16), compiler_params=_cparams(("parallel",)))(rep, got)


def _chip_sum(arrived, name):
    _, R, C = arrived.shape
    tr = _pick_rows(R)

    def kern(a_ref, o_ref):
        acc = a_ref[0].astype(F32)
        for q in range(1, N_CHIPS):
            acc = acc + a_ref[q].astype(F32)
        o_ref[...] = acc

    return pl.pallas_call(
        kern, name=name, grid=(R // tr,), in_specs=[pl.BlockSpec((N_CHIPS, tr, C), lambda i: (0, i, 0))],
        out_specs=pl.BlockSpec((tr, C), lambda i: (i, 0)), out_shape=jax.ShapeDtypeStruct((R, C), F32),
        compiler_params=_cparams(("parallel",)),
    )(arrived)


def _adam_math(w, g, m, v):
    c1 = 1.0 / (1.0 - ADAM_B1 ** ADAM_STEP)
    c2 = 1.0 / (1.0 - ADAM_B2 ** ADAM_STEP)
    nm = ADAM_B1 * m + (1.0 - ADAM_B1) * g
    nv = ADAM_B2 * v + (1.0 - ADAM_B2) * (g * g)
    return -ADAM_LR * ((nm * c1) / (jnp.sqrt(nv * c2) + ADAM_EPS) + ADAM_WD * w), nm, nv


def _adamw_halves(w, m, v, mine, other, name):
    R, C = w.shape
    h = R // 2
    th = _pick_rows(h)
    nt = h // th

    def kern(w_ref, m_ref, v_ref, a_ref, b_ref, g_ref, d_ref, nm_ref, nv_ref):
        g = jnp.where(pl.program_id(0) == lax.axis_index("c"), a_ref[...], b_ref[...])
        d, nm, nv = _adam_math(w_ref[...], g, m_ref[...], v_ref[...])
        g_ref[...] = g
        d_ref[...] = d
        nm_ref[...] = nm
        nv_ref[...] = nv

    full = pl.BlockSpec((th, C), lambda hh, i: (hh * nt + i, 0))
    half = pl.BlockSpec((th, C), lambda hh, i: (i, 0))
    sh = jax.ShapeDtypeStruct((R, C), F32)
    return pl.pallas_call(kern, name=name, grid=(2, nt), in_specs=[full] * 3 + [half] * 2, out_specs=[full] * 4,
                          out_shape=[sh] * 4, compiler_params=_cparams(("parallel", "parallel")))(w, m, v, mine, other)


def _adamw_call(w, g, m, v, name):
    R = w.shape[0]
    tr = _pick_rows(R)
    c1 = 1.0 / (1.0 - ADAM_B1 ** ADAM_STEP)
    c2 = 1.0 / (1.0 - ADAM_B2 ** ADAM_STEP)

    def kern(w_ref, g_ref, m_ref, v_ref, d_ref, nm_ref, nv_ref):
        gv = g_ref[...]
        nm = ADAM_B1 * m_ref[...] + (1.0 - ADAM_B1) * gv
        nv = ADAM_B2 * v_ref[...] + (1.0 - ADAM_B2) * (gv * gv)
        d_ref[...] = -ADAM_LR * ((nm * c1) / (jnp.sqrt(nv * c2) + ADAM_EPS) + ADAM_WD * w_ref[...])
        nm_ref[...] = nm
        nv_ref[...] = nv

    spec = pl.BlockSpec((tr, LANES), lambda i: (i, 0))
    sh = jax.ShapeDtypeStruct((R, LANES), F32)
    return pl.pallas_call(kern, name=name, grid=(R // tr,), in_specs=[spec] * 4, out_specs=[spec] * 3,
                          out_shape=[sh] * 3, compiler_params=_cparams(("parallel",)))(w, g, m, v)


GATE_GROUP = 256


def _gate_dense(gate_w):
    _, nb, bw, _ = gate_w.shape
    D = nb * bw
    gb = min(GATE_GROUP, D)
    per = gb // bw
    w = gate_w.reshape(2, D // gb, per, bw, bw)
    eye = jnp.eye(per, dtype=gate_w.dtype)
    dense = jnp.einsum('grncd,nm->rncgmd', w, eye)
    return dense.reshape(D, 2 * gb)


def _gate_blocks(d_dense, nb):
    D, gb2 = d_dense.shape
    gb, bw = gb2 // 2, D // nb
    per = gb // bw
    g = d_dense.reshape(D // gb, per, bw, 2, per, bw)
    return jnp.einsum('rncgnd->grncd', g).reshape(2, nb, bw, bw)


def _step(W, M1, V1, x, tgt):
    sharded = [n for n in WEIGHTS if n in SHARD_AXIS]
    repl = [n for n in WEIGHTS if n not in SHARD_AXIS]
    small = [n for n in sharded if n not in MXU_WEIGHTS]
    D = x.shape[-1]

    def rows2d(a):
        return a.reshape(-1, a.shape[-1])

    small_buf = _pack([W[n] for n in small], F32, 16)
    gathered = _gather_all([rows2d(W[n]).astype(MXU_DTYPE) for n in MXU_WEIGHTS] + [small_buf], "gather_weights")
    mats = {n: g.reshape((N_CHIPS,) + W[n].shape[-3:]) for n, g in zip(MXU_WEIGHTS, gathered[:-1], strict=True)}
    per_chip = [_unpack(gathered[-1][q], [W[n].shape for n in small]) for q in range(N_CHIPS)]
    full = {n: jnp.concatenate([per_chip[q][i] for q in range(N_CHIPS)], axis=SHARD_AXIS[n]) for i, n in enumerate(small)}
    for n in repl:
        full[n] = W[n]

    def by_rows(name, layer):
        m = mats[name][:, layer]
        return m.reshape(N_CHIPS * m.shape[1], m.shape[2])

    P = {
        "lru": {"norm": full["lru_norm"][0], "w_in": _W(mats["lru_w_in"], 0), "b_in": full["lru_b_in"][0],
                "conv_w": full["lru_conv_w"][0], "conv_b": full["lru_conv_b"][0],
                "wbd": _gate_dense(full["lru_gate_w"][0]).astype(MXU_DTYPE), "gate_b": full["lru_gate_b"][0].reshape(-1),
                "lam": full["lru_lambda"][0], "w_out": by_rows("lru_w_out", 0), "b_out": full["lru_b_out"][0]},
        "rwkv": {"norm": full["rwkv_norm"][0], "mix": full["rwkv_mix"][0],
                 "w_r": by_rows("rwkv_w_rkv", 0), "w_k": by_rows("rwkv_w_rkv", 1), "w_v": by_rows("rwkv_w_rkv", 2),
                 "w0": full["rwkv_w0"][0], "w1": full["rwkv_w1"][0], "w2": full["rwkv_w2"][0], "a0": full["rwkv_a0"][0],
                 "a1": full["rwkv_a1"][0], "a2": full["rwkv_a2"][0], "g1": full["rwkv_g1"][0], "g2": full["rwkv_g2"][0],
                 "k_k": full["rwkv_k_k"][0], "k_a": full["rwkv_k_a"][0], "r_k": full["rwkv_r_k"][0],
                 "ln_w": full["rwkv_ln_w"][0], "ln_b": full["rwkv_ln_b"][0], "w_out": by_rows("rwkv_w_out", 0)},
        "final_norm": full["final_norm"],
    }
    for l in range(2):
        P[f"ffn{l}"] = {"norm": full["ffn_norm"][l], "w_up": _W(mats["ffn_w_up"], l),
                        "conv_w": full["ffn_conv_w"][l], "conv_b": full["ffn_conv_b"][l],
                        "w_down": by_rows("ffn_w_down", l)}

    loss, gx, G = _local_step(x, tgt, P)

    nb = W["lru_gate_w"].shape[2]
    gl, gr = G["lru"], G["rwkv"]
    gfull = {
        "lru_norm": gl["norm"][None], "lru_b_in": gl["b_in"][None],
        "lru_conv_w": gl["conv_w"][None], "lru_conv_b": gl["conv_b"][None], "lru_gate_w": _gate_blocks(gl["wbd"], nb)[None],
        "lru_gate_b": gl["gate_b"].reshape(W["lru_gate_b"].shape), "lru_lambda": gl["lam"][None],
        "lru_b_out": gl["b_out"][None],
        "rwkv_norm": gr["norm"][None], "rwkv_mix": gr["mix"][None],
        "rwkv_w0": gr["w0"][None], "rwkv_w1": gr["w1"][None], "rwkv_w2": gr["w2"][None], "rwkv_a0": gr["a0"][None],
        "rwkv_a1": gr["a1"][None], "rwkv_a2": gr["a2"][None], "rwkv_g1": gr["g1"][None], "rwkv_g2": gr["g2"][None],
        "rwkv_k_k": gr["k_k"][None], "rwkv_k_a": gr["k_a"][None], "rwkv_r_k": gr["r_k"][None],
        "rwkv_ln_w": gr["ln_w"][None], "rwkv_ln_b": gr["ln_b"][None],
        "final_norm": G["final_norm"],
    }
    for k in ("norm", "conv_w", "conv_b"):
        gfull["ffn_" + k] = jnp.stack([G["ffn0"][k], G["ffn1"][k]])

    small_g = jnp.stack([_pack([_to_shards(gfull[n], SHARD_AXIS[n])[q] for n in small], F32, 16) for q in range(N_CHIPS)])
    cut = lambda g: g.reshape(N_CHIPS, g.shape[0] // N_CHIPS, g.shape[1])
    pieces = [("lru_w_in", (0,), gl["w_in"]), ("lru_w_out", (0,), cut(gl["w_out"])),
              ("rwkv_w_rkv", (0, 0), cut(gr["w_r"])), ("rwkv_w_rkv", (0, 1), cut(gr["w_k"])),
              ("rwkv_w_rkv", (0, 2), cut(gr["w_v"])), ("rwkv_w_out", (0,), cut(gr["w_out"])),
              ("ffn_w_up", (0,), G["ffn0"]["w_up"]), ("ffn_w_up", (1,), G["ffn1"]["w_up"]),
              ("ffn_w_down", (0,), cut(G["ffn0"]["w_down"])), ("ffn_w_down", (1,), cut(G["ffn1"]["w_down"]))]
    gs = [g for _, _, g in pieces] + [small_g]
    grep = _pack([gfull[n] for n in repl], F32, 16)

    got, got_rep = _pair_swap_all(gs, grep, "reduce_pair_swap")
    pair = [_pair_sum(g, r, f"reduce_pair_sum{i}") for i, (g, r) in enumerate(zip(gs, got, strict=True))]
    pair_rep = _rep_pair_sum(grep, got_rep, "reduce_pair_sum_rep")
    arrived, arrived_rep = _chip_exchange_all(pair, pair_rep, "reduce_chips")
    mine = [_chip_sum(a, f"reduce_chip_sum{i}") for i, a in enumerate(arrived)]
    g_rp = _chip_sum(arrived_rep, "reduce_chip_sum_rep")
    other = _half_swap_all(mine, "reduce_half_swap")

    outs, parts = {}, {}
    for i, (n, idx, _) in enumerate(pieces):
        w, m, v = (rows2d(S[n][idx]) for S in (W, M1, V1))
        res = _adamw_halves(w, m, v, mine[i], other[i], f"adamw{i}")
        for kind, a in zip(("grad", "delta", "new_m", "new_v"), res, strict=True):
            parts.setdefault((kind, n), []).append(a)
    for (kind, n), lst in parts.items():
        a = lst[0] if len(lst) == 1 else jnp.stack(lst)
        outs[(kind, n)] = a.reshape(W[n].shape)
    wb, mb, vb = (_pack([S[n] for n in small], F32, 16) for S in (W, M1, V1))
    res = _adamw_halves(wb, mb, vb, mine[-1], other[-1], "adamw_small")
    for kind, buf in zip(("grad", "delta", "new_m", "new_v"), res, strict=True):
        for n, a in zip(small, _unpack(buf, [W[n].shape for n in small]), strict=True):
            outs[(kind, n)] = a
    wb, mb, vb = (_pack([S[n] for n in repl], F32, 16) for S in (W, M1, V1))
    d, nm, nv = _adamw_call(wb, g_rp, mb, vb, "adamw_repl")
    for kind, buf in (("grad", g_rp), ("delta", d), ("new_m", nm), ("new_v", nv)):
        for n, a in zip(repl, _unpack(buf, [W[n].shape for n in repl]), strict=True):
            outs[(kind, n)] = a
    loss = lax.psum(loss, ("x", "y", "c"))
    return (loss, gx, *[outs[(kind, n)] for kind in ("grad", "delta", "new_m", "new_v") for n in WEIGHTS])


def kernel(x, lru_norm, lru_w_in, lru_b_in, lru_conv_w, lru_conv_b, lru_gate_w, lru_gate_b, lru_lambda, lru_w_out, lru_b_out, rwkv_norm, rwkv_mix, rwkv_w_rkv, rwkv_w0, rwkv_w1, rwkv_w2, rwkv_a0, rwkv_a1, rwkv_a2, rwkv_g1, rwkv_g2, rwkv_k_k, rwkv_k_a, rwkv_r_k, rwkv_ln_w, rwkv_ln_b, rwkv_w_out, ffn_norm, ffn_w_up, ffn_conv_w, ffn_conv_b, ffn_w_down, final_norm, loss_target, m_lru_norm, m_lru_w_in, m_lru_b_in, m_lru_conv_w, m_lru_conv_b, m_lru_gate_w, m_lru_gate_b, m_lru_lambda, m_lru_w_out, m_lru_b_out, m_rwkv_norm, m_rwkv_mix, m_rwkv_w_rkv, m_rwkv_w0, m_rwkv_w1, m_rwkv_w2, m_rwkv_a0, m_rwkv_a1, m_rwkv_a2, m_rwkv_g1, m_rwkv_g2, m_rwkv_k_k, m_rwkv_k_a, m_rwkv_r_k, m_rwkv_ln_w, m_rwkv_ln_b, m_rwkv_w_out, m_ffn_norm, m_ffn_w_up, m_ffn_conv_w, m_ffn_conv_b, m_ffn_w_down, m_final_norm, v_lru_norm, v_lru_w_in, v_lru_b_in, v_lru_conv_w, v_lru_conv_b, v_lru_gate_w, v_lru_gate_b, v_lru_lambda, v_lru_w_out, v_lru_b_out, v_rwkv_norm, v_rwkv_mix, v_rwkv_w_rkv, v_rwkv_w0, v_rwkv_w1, v_rwkv_w2, v_rwkv_a0, v_rwkv_a1, v_rwkv_a2, v_rwkv_g1, v_rwkv_g2, v_rwkv_k_k, v_rwkv_k_a, v_rwkv_r_k, v_rwkv_ln_w, v_rwkv_ln_b, v_rwkv_w_out, v_ffn_norm, v_ffn_w_up, v_ffn_conv_w, v_ffn_conv_b, v_ffn_w_down, v_final_norm):
    given = dict(locals())
    W = {n: given[n] for n in WEIGHTS}
    M1 = {n: given["m_" + n] for n in WEIGHTS}
    V1 = {n: given["v_" + n] for n in WEIGHTS}
    return _step(W, M1, V1, x, loss_target)
```

```python
import functools

import jax
import jax.numpy as jnp
from jax import lax
from jax.experimental import pallas as pl
from jax.experimental.pallas import tpu as pltpu

F32 = jnp.float32
BF16 = jnp.bfloat16
MXU_DTYPE = BF16

HEAD = 64
LRU_C = 8.0
GN_EPS = 64e-5
RMS_EPS = 1e-6
HALO = 16
VMEM_LIMIT = 56 * 1024 * 1024

ADAM_LR, ADAM_B1, ADAM_B2, ADAM_EPS, ADAM_WD, ADAM_STEP = 0.001, 0.9, 0.999, 1e-08, 0.01, 10


def _cparams(sem):
    return pltpu.CompilerParams(dimension_semantics=sem, vmem_limit_bytes=VMEM_LIMIT)


def _pick(n, want):
    if n <= want:
        return n
    t = want
    while t >= 128:
        if n % t == 0:
            return t
        t -= 128
    return n


class _W:
    def __init__(self, arr, layer):
        self.arr, self.layer = arr, layer
        self.shape = (arr.shape[2], N_CHIPS * arr.shape[3])


def _matmul(a, b, mode="nn", bias=None, residual=None, out_dtype=F32, name="mm", tm=1024, tn=1024, tk=1024,
            out_cols_by_chip=False):
    bshape = b.shape
    if mode == "nn":
        (M, K), (K2, N) = a.shape, bshape
    elif mode == "nt":
        (M, K), (N, K2) = a.shape, bshape
    else:
        (K, M), (K2, N) = a.shape, bshape
    assert K == K2, (a.shape, bshape, mode)
    lim_n, lim_k = N, K
    if isinstance(b, _W):
        if mode == "nn":
            lim_n = b.arr.shape[3]
        else:
            lim_k = b.arr.shape[3]
    if out_cols_by_chip:
        lim_n = min(lim_n, N // N_CHIPS)
    tm, tn, tk = _pick(M, tm), _pick(lim_n, tn), _pick(lim_k, tk)
    nk = K // tk
    dims = {"nn": (((1,), (0,)), ((), ())), "nt": (((1,), (1,)), ((), ())), "tn": (((0,), (0,)), ((), ()))}[mode]
    a_spec = {"nn": pl.BlockSpec((tm, tk), lambda i, j, k: (i, k)),
              "nt": pl.BlockSpec((tm, tk), lambda i, j, k: (i, k)),
              "tn": pl.BlockSpec((tk, tm), lambda i, j, k: (k, i))}[mode]
    if isinstance(b, _W):
        lay = b.layer
        if mode == "nn":
            per = b.arr.shape[3] // tn
            b_spec = pl.BlockSpec((None, None, tk, tn), lambda i, j, k: (j // per, lay, k, j % per))
        else:
            assert mode == "nt"
            per = b.arr.shape[3] // tk
            b_spec = pl.BlockSpec((None, None, tn, tk), lambda i, j, k: (k // per, lay, j, k % per))
        b = b.arr
    else:
        b_spec = {"nn": pl.BlockSpec((tk, tn), lambda i, j, k: (k, j)),
                  "nt": pl.BlockSpec((tn, tk), lambda i, j, k: (j, k)),
                  "tn": pl.BlockSpec((tk, tn), lambda i, j, k: (k, j))}[mode]
    if out_cols_by_chip:
        opc = N // N_CHIPS // tn
        out_spec = pl.BlockSpec((None, tm, tn), lambda i, j, k: (j // opc, i, j % opc))
        out_shape = jax.ShapeDtypeStruct((N_CHIPS, M, N // N_CHIPS), out_dtype)
    else:
        out_spec = pl.BlockSpec((tm, tn), lambda i, j, k: (i, j))
        out_shape = jax.ShapeDtypeStruct((M, N), out_dtype)
    in_specs, operands = [a_spec, b_spec], [a, b]
    if bias is not None:
        in_specs.append(pl.BlockSpec((1, tn), lambda i, j, k: (0, j)))
        operands.append(bias.reshape(1, N))
    if residual is not None:
        in_specs.append(pl.BlockSpec((tm, tn), lambda i, j, k: (i, j)))
        operands.append(residual)
    has_bias, has_res = bias is not None, residual is not None

    def kern(*refs):
        a_ref, b_ref = refs[0], refs[1]
        o_ref = refs[2 + has_bias + has_res]

        def finish(r):
            pos = 2
            if has_bias:
                r = r + refs[pos][...].astype(F32)
                pos += 1
            if has_res:
                r = r + refs[pos][...].astype(F32)
            o_ref[...] = r.astype(o_ref.dtype)

        part = lax.dot_general(a_ref[...].astype(MXU_DTYPE), b_ref[...].astype(MXU_DTYPE), dims,
                               preferred_element_type=F32)
        if nk == 1:
            finish(part)
            return
        acc_ref = refs[-1]
        k = pl.program_id(2)

        @pl.when(k == 0)
        def _():
            acc_ref[...] = part

        @pl.when(jnp.logical_and(k > 0, k < nk - 1))
        def _():
            acc_ref[...] += part

        @pl.when(k == nk - 1)
        def _():
            finish(acc_ref[...] + part)

    return pl.pallas_call(
        kern, name=name,
        grid=(M // tm, N // tn, nk),
        in_specs=in_specs,
        out_specs=out_spec,
        out_shape=out_shape,
        scratch_shapes=[pltpu.VMEM((tm, tn), F32)] if nk > 1 else [],
        compiler_params=_cparams(("parallel", "parallel", "arbitrary")),
    )(*operands)


def _tile_call(body, *, rows, prevs=(), nexts=(), fulls=(), row_outs=(), acc_outs=(), tm, T, name):
    M = rows[0].shape[0]
    n_tiles, tps, hb = M // tm, T // tm, tm // HALO
    n_halo_blocks = M // HALO
    nr, npv, nnx, nf, nro, nac = len(rows), len(prevs), len(nexts), len(fulls), len(row_outs), len(acc_outs)

    def kern(*refs):
        i = pl.program_id(0)
        row_refs = refs[:nr]
        prev_refs = refs[nr:nr + npv]
        next_refs = refs[nr + npv:nr + npv + nnx]
        full_refs = refs[nr + npv + nnx:nr + npv + nnx + nf]
        out_refs = refs[nr + npv + nnx + nf:nr + npv + nnx + nf + nro]
        acc_refs = refs[nr + npv + nnx + nf + nro:]
        seq_first = (i % tps) == 0
        seq_last = (i % tps) == (tps - 1)
        outs, accs = body(row_refs, prev_refs, next_refs, full_refs, seq_first, seq_last)
        for r, o in zip(out_refs, outs, strict=True):
            r[...] = o.astype(r.dtype)
        if nac:
            @pl.when(i == 0)
            def _():
                for r in acc_refs:
                    r[...] = jnp.zeros_like(r)
            for r, a in zip(acc_refs, accs, strict=True):
                r[...] += a.astype(F32)

    in_specs = [pl.BlockSpec((tm, a.shape[1]), lambda i: (i, 0)) for a in rows]
    in_specs += [pl.BlockSpec((HALO, rows[k].shape[1]), lambda i: (jnp.maximum(i * hb - 1, 0), 0)) for k in prevs]
    in_specs += [pl.BlockSpec((HALO, rows[k].shape[1]), lambda i: (jnp.minimum((i + 1) * hb, n_halo_blocks - 1), 0))
                 for k in nexts]
    in_specs += [pl.BlockSpec(f.shape, lambda i: (0, 0)) for f in fulls]
    out_specs = [pl.BlockSpec((tm, w), lambda i: (i, 0)) for (w, _) in row_outs]
    out_specs += [pl.BlockSpec(s, lambda i: (0, 0)) for s in acc_outs]
    out_shape = [jax.ShapeDtypeStruct((M, w), dt) for (w, dt) in row_outs]
    out_shape += [jax.ShapeDtypeStruct(s, F32) for s in acc_outs]
    operands = list(rows) + [rows[k] for k in prevs] + [rows[k] for k in nexts] + list(fulls)
    res = pl.pallas_call(
        kern, name=name, grid=(n_tiles,), in_specs=in_specs, out_specs=out_specs, out_shape=out_shape,
        compiler_params=_cparams(("arbitrary",)),
    )(*operands)
    return res[:nro], res[nro:]


def _f(ref):
    return ref[...].astype(F32)


def _sigmoid(x):
    return 1.0 / (1.0 + jnp.exp(-x))


def _softplus(x):
    return jnp.maximum(x, 0.0) + jnp.log(1.0 + jnp.exp(-jnp.abs(x)))


def _neg_expm1(x):
    series = -x * (1.0 + x * (0.5 + x * (1.0 / 6.0) * (1.0 + 0.25 * x)))
    return jnp.where(x > -0.01, series, 1.0 - jnp.exp(x))


def _gelu(x):
    return 0.5 * x * (1.0 + jnp.tanh(0.7978845608028654 * (x + 0.044715 * x * x * x)))


def _rms(x, g):
    return x * lax.rsqrt(jnp.mean(x * x, axis=-1, keepdims=True) + RMS_EPS) * g


@jax.custom_vjp
def _bdot(x, w):
    return jnp.dot(x.astype(MXU_DTYPE), w.astype(MXU_DTYPE), preferred_element_type=F32)


def _bdot_fwd(x, w):
    return _bdot(x, w), (x, w)


def _bdot_bwd(res, ct):
    x, w = res
    ctb = ct.astype(MXU_DTYPE)
    dx = lax.dot_general(ctb, w.astype(MXU_DTYPE), (((1,), (1,)), ((), ())), preferred_element_type=F32)
    dw = lax.dot_general(x.astype(MXU_DTYPE), ctb, (((0,), (0,)), ((), ())), preferred_element_type=F32)
    return dx.astype(x.dtype), dw.astype(w.dtype)


_bdot.defvjp(_bdot_fwd, _bdot_bwd)


@jax.custom_vjp
def _head_sum(x, e, et):
    s = jnp.dot(_split_lhs(x), e, preferred_element_type=F32)
    return jnp.dot(_split_lhs(s), et, preferred_element_type=F32)


def _head_sum_fwd(x, e, et):
    return _head_sum(x, e, et), (e, et)


def _head_sum_bwd(res, ct):
    e, et = res
    return _head_sum(ct, e, et), jnp.zeros_like(e), jnp.zeros_like(et)


_head_sum.defvjp(_head_sum_fwd, _head_sum_bwd)


def _shift_down(main, prev, s, seq_first):
    prev = jnp.where(seq_first, 0.0, prev)
    ext = jnp.concatenate([prev, main], axis=0)
    return pltpu.roll(ext, s, 0)[HALO:]


def _shift_up(main, nxt, s, seq_last):
    nxt = jnp.where(seq_last, 0.0, nxt)
    ext = jnp.concatenate([main, nxt], axis=0)
    n = ext.shape[0]
    return pltpu.roll(ext, n - s, 0)[:n - HALO]


def _colsum(x):
    return jnp.sum(x, axis=0, keepdims=True)


def _pad8(x):
    k = x.shape[0]
    return jnp.concatenate([x, jnp.zeros((8 - k, x.shape[1]), x.dtype)], axis=0) if k < 8 else x


def _rms_fwd(x, g, T, name):
    D = x.shape[1]

    def body(rows, prevs, nexts, fulls, sf, sl):
        return [_rms(_f(rows[0]), _f(fulls[0]))], []

    (h,), _ = _tile_call(body, rows=[x], fulls=[g.reshape(1, D)], row_outs=[(D, BF16)], tm=min(512, T), T=T, name=name)
    return h


def _rms_bwd(x, g, dh, dres, T, name):
    D = x.shape[1]

    def body(rows, prevs, nexts, fulls, sf, sl):
        _, vjp = jax.vjp(_rms, _f(rows[0]), _f(fulls[0]))
        dx, dg = vjp(_f(rows[1]))
        return [dx + _f(rows[2])], [dg]

    (dx,), (dg,) = _tile_call(body, rows=[x, dh, dres], fulls=[g.reshape(1, D)], row_outs=[(D, F32)],
                              acc_outs=[(1, D)], tm=min(512, T), T=T, name=name)
    return dx, dg


def _ffn_conv(u1, prev, cw, cb, sf):
    k = cw.shape[0]
    out = cb + u1 * cw[k - 1:k]
    for j in range(k - 1):
        out = out + _shift_down(u1, prev, k - 1 - j, sf) * cw[j:j + 1]
    return out


def _ffn_fwd(x, p, T, tag):
    M, D = x.shape
    F = p["w_down"].shape[0]
    hf = _rms_fwd(x, p["norm"], T, f"ffn{tag}_norm")
    uf = _matmul(hf, p["w_up"], out_dtype=BF16, name=f"ffn{tag}_up")

    def body(rows, prevs, nexts, fulls, sf, sl):
        u = rows[0]
        gate = _ffn_conv(u[:, :F].astype(F32), prevs[0][:, :F].astype(F32), _f(fulls[0]), _f(fulls[1]), sf)
        return [_gelu(gate) * u[:, F:].astype(F32)], []

    (hid,), _ = _tile_call(body, rows=[uf], prevs=[0], fulls=[p["conv_w"], p["conv_b"].reshape(1, F)],
                           row_outs=[(F, BF16)], tm=min(256, T), T=T, name=f"ffn{tag}_act")
    y = _matmul(hid, p["w_down"], residual=x, name=f"ffn{tag}_down")
    return y, (x, hf, uf, hid)


def _ffn_bwd(dy, saved, p, T, tag):
    x, hf, uf, hid = saved
    M, D = x.shape
    F = p["w_down"].shape[0]
    K = p["conv_w"].shape[0]
    d_hid = _matmul(dy, p["w_down"], mode="nt", out_dtype=BF16, name=f"ffn{tag}_down_dx")
    d_w_down = _matmul(hid, dy, mode="tn", out_dtype=BF16, name=f"ffn{tag}_down_dw")

    def body(rows, prevs, nexts, fulls, sf, sl):
        u, dh = rows
        cw, cb = _f(fulls[0]), _f(fulls[1])
        tm = u.shape[0]
        u1c, u1p, u1n = u[:, :F].astype(F32), prevs[0][:, :F].astype(F32), nexts[0][:, :F].astype(F32)
        u1 = jnp.concatenate([u1c, u1n], axis=0)
        u2 = jnp.concatenate([u[:, F:].astype(F32), nexts[0][:, F:].astype(F32)], axis=0)
        dhid = jnp.concatenate([_f(dh), _f(nexts[1])], axis=0)
        gate = _ffn_conv(u1, u1p, cw, cb, sf)
        (act, dact) = jax.jvp(_gelu, (gate,), (jnp.ones_like(gate),))
        d_gate = dhid * u2 * dact
        d_u2 = (dhid * act)[:tm]
        rowid = lax.broadcasted_iota(jnp.int32, d_gate.shape, 0)
        d_gate = jnp.where(jnp.logical_and(sl, rowid >= tm), 0.0, d_gate)
        dgc, dgn = d_gate[:tm], d_gate[tm:]
        d_u1 = dgc * cw[K - 1:K]
        dws = []
        for j in range(K - 1):
            s = K - 1 - j
            d_u1 = d_u1 + _shift_up(dgc, dgn, s, False) * cw[j:j + 1]
            dws.append(_colsum(dgc * _shift_down(u1c, u1p, s, sf)))
        dws.append(_colsum(dgc * u1c))
        d_cw = _pad8(jnp.concatenate(dws, axis=0))
        return [jnp.concatenate([d_u1, d_u2], axis=1)], [d_cw, _colsum(dgc)]

    (d_uf,), (d_cw, d_cb) = _tile_call(
        body, rows=[uf, d_hid], prevs=[0], nexts=[0, 1], fulls=[p["conv_w"], p["conv_b"].reshape(1, F)],
        row_outs=[(2 * F, BF16)], acc_outs=[(8, F), (1, F)], tm=min(256, T), T=T, name=f"ffn{tag}_act_bwd")
    d_hf = _matmul(d_uf, p["w_up"], mode="nt", name=f"ffn{tag}_up_dx")
    d_w_up = _matmul(hf, d_uf, mode="tn", out_dtype=BF16, name=f"ffn{tag}_up_dw", out_cols_by_chip=True)
    dx, d_norm = _rms_bwd(x, p["norm"], d_hf, dy, T, f"ffn{tag}_norm_bwd")
    grads = {"norm": d_norm.reshape(D), "w_up": d_w_up, "conv_w": d_cw[:K], "conv_b": d_cb.reshape(F), "w_down": d_w_down}
    return dx, grads


def _lru_conv(u2, prev, cw, cb, sf):
    return _ffn_conv(u2, prev, cw, cb, sf)


def _lru_pre(xr, wg, gb):
    D, GB = xr.shape[1], wg.shape[1] // 2
    parts = [jnp.dot(xr[:, r * GB:(r + 1) * GB].astype(MXU_DTYPE), wg[r * GB:(r + 1) * GB], preferred_element_type=F32)
             for r in range(D // GB)]
    return jnp.concatenate([q[:, :GB] for q in parts] + [q[:, GB:] for q in parts], axis=1) + gb


def _lru_pre_t(xr, dpre, wg, with_dw):
    D, GB = xr.shape[1], wg.shape[1] // 2
    dx, dw = [], []
    for r in range(D // GB):
        dp = jnp.concatenate([dpre[:, r * GB:(r + 1) * GB], dpre[:, D + r * GB:D + (r + 1) * GB]], axis=1).astype(MXU_DTYPE)
        dx.append(lax.dot_general(dp, wg[r * GB:(r + 1) * GB], (((1,), (1,)), ((), ())), preferred_element_type=F32))
        if with_dw:
            dw.append(lax.dot_general(xr[:, r * GB:(r + 1) * GB].astype(MXU_DTYPE), dp, (((0,), (0,)), ((), ())),
                                      preferred_element_type=F32))
    return jnp.concatenate(dx, axis=1), (jnp.concatenate(dw, axis=0) if with_dw else None)


def _lru_gates(xr, pre, lam):
    D = xr.shape[1]
    r_gate, i_gate = _sigmoid(pre[:, :D]), _sigmoid(pre[:, D:])
    log_a = -LRU_C * r_gate * _softplus(-lam)
    a = jnp.exp(log_a)
    mult = jnp.sqrt(_neg_expm1(2.0 * log_a))
    return a, mult * (i_gate * xr)


def _lru_scan(a, b, B, T):
    M, D = a.shape
    cw = _pick(D, 256)
    ng = T // 8

    def kern(a_ref, b_ref, o_ref):
        row = lax.broadcasted_iota(jnp.int32, (8, cw), 0)

        def step(g, carry):
            sl = pl.ds(pl.multiple_of(g * 8, 8), 8)
            a8, b8 = a_ref[sl, :], b_ref[sl, :]
            for s in (1, 2, 4):
                a_sh = jnp.where(row >= s, pltpu.roll(a8, s, 0), 1.0)
                b_sh = jnp.where(row >= s, pltpu.roll(b8, s, 0), 0.0)
                b8 = a8 * b_sh + b8
                a8 = a8 * a_sh
            h8 = a8 * carry + b8
            o_ref[sl, :] = h8
            return jnp.broadcast_to(h8[7:8, :], (8, cw))

        lax.fori_loop(0, ng, step, jnp.zeros((8, cw), F32))

    spec = pl.BlockSpec((T, cw), lambda b, c: (b, c))
    return pl.pallas_call(
        kern, name="lru_scan", grid=(B, D // cw), in_specs=[spec, spec], out_specs=spec,
        out_shape=jax.ShapeDtypeStruct((M, D), F32), compiler_params=_cparams(("parallel", "parallel")),
    )(a, b)


def _lru_scan_bwd(a, hs, dhs, B, T):
    M, D = a.shape
    cw = _pick(D, 256)
    ng = T // 8

    def kern(a_ref, h_ref, d_ref, g_ref, da_ref):
        row = lax.broadcasted_iota(jnp.int32, (8, cw), 0)

        def step(k, carry):
            g_next, a_next = carry
            g = ng - 1 - k
            sl = pl.ds(pl.multiple_of(g * 8, 8), 8)
            a8, d8, h8 = a_ref[sl, :], d_ref[sl, :], h_ref[sl, :]
            c8 = jnp.where(row < 7, pltpu.roll(a8, 7, 0), a_next)
            for s in (1, 2, 4):
                d_sh = jnp.where(row < 8 - s, pltpu.roll(d8, 8 - s, 0), 0.0)
                c_sh = jnp.where(row < 8 - s, pltpu.roll(c8, 8 - s, 0), 1.0)
                d8 = d8 + c8 * d_sh
                c8 = c8 * c_sh
            G8 = d8 + c8 * g_next
            gp = jnp.maximum(g - 1, 0)
            hp8 = h_ref[pl.ds(pl.multiple_of(gp * 8, 8), 8), :]
            hp_last = jnp.where(g > 0, jnp.broadcast_to(hp8[7:8, :], (8, cw)), 0.0)
            hprev = jnp.where(row >= 1, pltpu.roll(h8, 1, 0), hp_last)
            g_ref[sl, :] = G8
            da_ref[sl, :] = G8 * hprev
            return jnp.broadcast_to(G8[0:1, :], (8, cw)), jnp.broadcast_to(a8[0:1, :], (8, cw))

        z = jnp.zeros((8, cw), F32)
        lax.fori_loop(0, ng, step, (z, z))

    spec = pl.BlockSpec((T, cw), lambda b, c: (b, c))
    sh = jax.ShapeDtypeStruct((M, D), F32)
    return pl.pallas_call(
        kern, name="lru_scan_bwd", grid=(B, D // cw), in_specs=[spec, spec, spec], out_specs=[spec, spec],
        out_shape=[sh, sh], compiler_params=_cparams(("parallel", "parallel")),
    )(a, hs, dhs)


def _lru_fwd(x, p, B, T):
    M, D = x.shape
    h0 = _rms_fwd(x, p["norm"], T, "lru_norm")
    u0 = _matmul(h0, p["w_in"], bias=p["b_in"], name="lru_in")
    fulls = [p["conv_w"], p["conv_b"].reshape(1, D), p["wbd"], p["gate_b"].reshape(1, 2 * D), p["lam"].reshape(1, D)]

    def body(rows, prevs, nexts, fulls, sf, sl):
        xr = _lru_conv(rows[0][:, D:], prevs[0][:, D:], _f(fulls[0]), _f(fulls[1]), sf)
        a, bt = _lru_gates(xr, _lru_pre(xr, fulls[2][...], _f(fulls[3])), _f(fulls[4]))
        return [a, bt], []

    (a, bt), _ = _tile_call(body, rows=[u0], prevs=[0], fulls=fulls, row_outs=[(D, F32), (D, F32)],
                            tm=min(256, T), T=T, name="lru_gates")
    hs = _lru_scan(a, bt, B, T)

    def body2(rows, prevs, nexts, fulls, sf, sl):
        return [rows[0][...] * _gelu(rows[1][:, :D])], []

    (out,), _ = _tile_call(body2, rows=[hs, u0], row_outs=[(D, BF16)], tm=min(512, T), T=T, name="lru_mix")
    y = _matmul(out, p["w_out"], bias=p["b_out"], residual=x, name="lru_out")
    return y, (x, h0, u0, a, hs, out)


def _lru_bwd(dy, saved, p, B, T):
    x, h0, u0, a, hs, out = saved
    M, D = x.shape
    K = p["conv_w"].shape[0]
    d_out = _matmul(dy, p["w_out"], mode="nt", name="lru_out_dx")
    d_w_out = _matmul(out, dy, mode="tn", out_dtype=BF16, name="lru_out_dw")

    def body(rows, prevs, nexts, fulls, sf, sl):
        do, h, u, dyv = rows[0][...], rows[1][...], rows[2][:, :D], rows[3][...]
        act, dact = jax.jvp(_gelu, (u,), (jnp.ones_like(u),))
        return [do * act, do * h * dact], [_colsum(dyv)]

    (d_hs, d_u1), (d_b_out,) = _tile_call(body, rows=[d_out, hs, u0, dy], row_outs=[(D, F32), (D, F32)],
                                          acc_outs=[(1, D)], tm=min(512, T), T=T, name="lru_mix_bwd")
    g_b, d_a = _lru_scan_bwd(a, hs, d_hs, B, T)
    fulls = [p["conv_w"], p["conv_b"].reshape(1, D), p["wbd"], p["gate_b"].reshape(1, 2 * D), p["lam"].reshape(1, D)]

    def body3(rows, prevs, nexts, fulls, sf, sl):
        u, gb_c, da_c, du1 = rows
        cw, cb, wbd, gbias, lam = _f(fulls[0]), _f(fulls[1]), fulls[2][...], _f(fulls[3]), _f(fulls[4])
        tm = u.shape[0]
        u2c, u2p, u2n = u[:, D:], prevs[0][:, D:], nexts[0][:, D:]
        xr_c = _lru_conv(u2c, u2p, cw, cb, sf)
        xr_n = _lru_conv(u2n, u2c[tm - HALO:], cw, cb, False)
        _, vjp_c = jax.vjp(_lru_gates, xr_c, _lru_pre(xr_c, wbd, gbias), lam)
        dxr_c, dpre_c, d_lam = vjp_c((da_c[...], gb_c[...]))
        dxr_add, d_wbd = _lru_pre_t(xr_c, dpre_c, wbd, True)
        dxr_c = dxr_c + dxr_add
        d_gbias = _colsum(dpre_c)
        _, vjp_n = jax.vjp(lambda t, q: _lru_gates(t, q, lam), xr_n, _lru_pre(xr_n, wbd, gbias))
        dxr_n, dpre_n = vjp_n((nexts[2][...], nexts[1][...]))
        dxr_n = dxr_n + _lru_pre_t(xr_n, dpre_n, wbd, False)[0]
        d_u2 = dxr_c * cw[K - 1:K]
        dws = []
        for j in range(K - 1):
            s = K - 1 - j
            d_u2 = d_u2 + _shift_up(dxr_c, dxr_n, s, sl) * cw[j:j + 1]
            dws.append(_colsum(dxr_c * _shift_down(u2c, u2p, s, sf)))
        dws.append(_colsum(dxr_c * u2c))
        d_u = jnp.concatenate([du1[...], d_u2], axis=1)
        return [d_u], [_pad8(jnp.concatenate(dws, axis=0)), _colsum(dxr_c), d_wbd, d_gbias, d_lam,
                       _colsum(d_u)]

    (d_u0,), (d_cw, d_cb, d_wbd, d_gb, d_lam, d_b_in) = _tile_call(
        body3, rows=[u0, g_b, d_a, d_u1], prevs=[0], nexts=[0, 1, 2], fulls=fulls, row_outs=[(2 * D, BF16)],
        acc_outs=[(8, D), (1, D), p["wbd"].shape, (1, 2 * D), (1, D), (1, 2 * D)], tm=min(256, T), T=T,
        name="lru_gates_bwd")
    d_h0 = _matmul(d_u0, p["w_in"], mode="nt", name="lru_in_dx")
    d_w_in = _matmul(h0, d_u0, mode="tn", out_dtype=BF16, name="lru_in_dw", out_cols_by_chip=True)
    dx, d_norm = _rms_bwd(x, p["norm"], d_h0, dy, T, "lru_norm_bwd")
    grads = {"norm": d_norm.reshape(D), "w_in": d_w_in, "b_in": d_b_in.reshape(2 * D), "conv_w": d_cw[:K],
             "conv_b": d_cb.reshape(D), "wbd": d_wbd, "gate_b": d_gb.reshape(2 * D), "lam": d_lam.reshape(D),
             "w_out": d_w_out, "b_out": d_b_out.reshape(D)}
    return dx, grads


def _rwkv_mix(xc, xp, norm, mix, sf):
    h = _rms(xc, norm)
    hp = _rms(xp, norm)
    xx = _shift_down(h, hp, 1, sf) - h
    return h, xx


def _rwkv_pre(k, xw, xa, xg, w0, w1, w2, a0, a1, a2, g1, g2, k_k, k_a, e, et):
    wl = -_softplus(-(w0 + _bdot(jnp.tanh(_bdot(xw, w1)), w2))) - 0.5
    decay = jnp.exp(-jnp.exp(wl))
    a = _sigmoid(a0 + _bdot(_bdot(xa, a1), a2))
    g = _bdot(_sigmoid(_bdot(xg, g1)), g2)
    kk = k * k_k
    nrm = jnp.sqrt(_head_sum(kk * kk, e, et))
    kk = kk / jnp.maximum(nrm, 1e-12)
    k2 = k * (1.0 + (a - 1.0) * k_a)
    return decay, k2, -kk, kk * a, g


def _rwkv_post(y, r, k2, v, g, ln_w, ln_b, r_k, e, et):
    inv = 1.0 / HEAD
    mu = _head_sum(y, e, et) * inv
    yc = y - mu
    var = _head_sum(yc * yc, e, et) * inv
    yn = yc * lax.rsqrt(var + GN_EPS) * ln_w + ln_b
    bonus = _head_sum(r * k2 * r_k, e, et) * v
    return (yn + bonus) * g


def _seg_lane_sums(x, lo_mask):
    s0 = jnp.sum(jnp.where(lo_mask, x, 0.0), axis=1, keepdims=True)
    s1 = jnp.sum(jnp.where(lo_mask, 0.0, x), axis=1, keepdims=True)
    return s0, s1


def _seg_lane_sum(x, lo_mask):
    s0, s1 = _seg_lane_sums(x, lo_mask)
    return jnp.where(lo_mask, s0, s1)


def _pair_consts():
    lane = lax.broadcasted_iota(jnp.int32, (HEAD, 128), 1)
    sub = lax.broadcasted_iota(jnp.int32, (HEAD, 128), 0)
    return lane < HEAD, (jnp.bitwise_and(lane, HEAD - 1) == sub).astype(F32)


def _pair_ones():
    head = jnp.arange(128) // HEAD
    return (head[:, None] == head[None, :]).astype(MXU_DTYPE)


def _split_lhs(x):
    hi = x.astype(MXU_DTYPE)
    return jnp.concatenate([hi, (x - hi.astype(F32)).astype(MXU_DTYPE)], axis=1)


def _spread_lhs(diag, row):
    return (diag * row).astype(MXU_DTYPE)


def _rwkv_scan(r, w, k, v, a, b, B, T):
    M, D = r.shape
    HP, PG, TC, NC, chains = _scan_plan(B, T, D, pairs=2, chunk=64)
    NS = len(chains) * 8
    NG = TC // 8

    def kern(r_ref, w_ref, k_ref, v_ref, a_ref, b_ref, ones_ref, y_ref, st_ref, S_ref, lv_ref, rv_ref, ly_ref, ry_ref):
        c = pl.program_id(1)

        @pl.when(c == 0)
        def _():
            S_ref[...] = jnp.zeros_like(S_ref)

        lo, diag = _pair_consts()
        row8 = lax.broadcasted_iota(jnp.int32, (8, 128), 0)

        def blk(idx):
            return pl.ds(idx * HEAD, HEAD)

        def rows_of(gi):
            return pl.ds(pl.multiple_of(gi * 8, 8), 8)

        def spread(gi, slot):
            for ci, (bi, p) in enumerate(chains):
                v8 = v_ref[bi, rows_of(gi), p * 128:(p + 1) * 128]
                for j in range(8):
                    lv_ref[slot, blk(ci * 8 + j), :] = _spread_lhs(diag, v8[j:j + 1, :])
            rv_ref[slot] = jnp.dot(lv_ref[slot], ones_ref[...], preferred_element_type=F32)

        def recur(gi, slot):
            sl = rows_of(gi)
            tiles = [[ref[bi, sl, p * 128:(p + 1) * 128] for ref in (r_ref, w_ref, k_ref, a_ref, b_ref)]
                     for bi, p in chains]
            S = [S_ref[ci] for ci in range(len(chains))]
            for j in range(8):
                for ci, (bi, p) in enumerate(chains):
                    r8, w8, k8, a8, b8 = tiles[ci]
                    idx = ci * 8 + j
                    st_ref[p, bi, gi * 8 + j] = S[ci]
                    sa = _seg_lane_sum(S[ci] * a8[j:j + 1, :], lo)
                    S[ci] = S[ci] * w8[j:j + 1, :] + sa * b8[j:j + 1, :] + rv_ref[slot, blk(idx), :] * k8[j:j + 1, :]
                    ly_ref[slot, blk(idx), :] = (S[ci] * r8[j:j + 1, :]).astype(MXU_DTYPE)
            for ci in range(len(chains)):
                S_ref[ci] = S[ci]

        def emit(gi, slot):
            ry_ref[slot] = jnp.dot(ly_ref[slot], ones_ref[...], preferred_element_type=F32)
            for ci, (bi, p) in enumerate(chains):
                y8 = jnp.zeros((8, 128), F32)
                for j in range(8):
                    y8 = jnp.where(row8 == j, _colsum(diag * ry_ref[slot, blk(ci * 8 + j), :]), y8)
                y_ref[bi, rows_of(gi), p * 128:(p + 1) * 128] = y8

        spread(0, 0)
        ly_ref[1] = jnp.zeros_like(ly_ref[1])

        def two_groups(m, _):
            g0, g1 = 2 * m, 2 * m + 1
            spread(g1, 1)
            recur(g0, 0)
            emit(jnp.maximum(g0 - 1, 0), 1)
            spread(jnp.minimum(g1 + 1, NG - 1), 0)
            recur(g1, 1)
            emit(g0, 0)
            return 0

        lax.fori_loop(0, NG // 2, two_groups, 0)
        emit(NG - 1, 1)

    spec = pl.BlockSpec((B, TC, 128 * PG), lambda hp, c: (0, c, hp))
    st_spec = pl.BlockSpec((PG, B, TC, HEAD, 128), lambda hp, c: (hp, 0, c, 0, 0))
    y, st = pl.pallas_call(
        kern, name="rwkv_scan", grid=(HP // PG, NC),
        in_specs=[spec] * 6 + [pl.BlockSpec((128, 128), lambda hp, c: (0, 0))], out_specs=[spec, st_spec],
        out_shape=[jax.ShapeDtypeStruct((B, T, D), F32), jax.ShapeDtypeStruct((HP, B, T, HEAD, 128), F32)],
        scratch_shapes=[pltpu.VMEM((len(chains), HEAD, 128), F32),
                        pltpu.VMEM((2, NS * HEAD, 128), MXU_DTYPE), pltpu.VMEM((2, NS * HEAD, 128), F32),
                        pltpu.VMEM((2, NS * HEAD, 128), MXU_DTYPE), pltpu.VMEM((2, NS * HEAD, 128), F32)],
        compiler_params=_cparams(("parallel", "arbitrary")),
    )(*[x.reshape(B, T, D) for x in (r, w, k, v, a, b)], _pair_ones())
    return y.reshape(M, D), st


def _scan_plan(B, T, D, pairs, chunk):
    HP = D // 128
    PG = pairs if HP % pairs == 0 else 1
    TC = min(chunk, T)
    assert TC % 16 == 0 and T % TC == 0
    return HP, PG, TC, T // TC, [(bi, p) for bi in range(B) for p in range(PG)]


def _rwkv_scan_bwd(r, w, k, v, a, b, st, dy, B, T):
    M, D = r.shape
    HP, PG, TC, NC, chains = _scan_plan(B, T, D, pairs=1, chunk=128)
    NS = len(chains) * 8

    NG = TC // 8

    def kern(r_ref, w_ref, k_ref, v_ref, a_ref, b_ref, st_ref, dy_ref, ones_ref,
             dr_ref, dw_ref, dk_ref, dv_ref, da_ref, db_ref, dS_ref, lp_ref, rp_ref, lq_ref, rq_ref):
        c = pl.program_id(1)

        @pl.when(c == 0)
        def _():
            dS_ref[...] = jnp.zeros_like(dS_ref)

        lo, diag = _pair_consts()
        row8 = lax.broadcasted_iota(jnp.int32, (8, 128), 0)

        def blk(idx):
            return pl.ds(idx * HEAD, HEAD)

        def rows_of(gi):
            return pl.ds(pl.multiple_of(gi * 8, 8), 8)

        def spread(gi, slot):
            sl = rows_of(gi)
            for ci, (bi, p) in enumerate(chains):
                lanes = slice(p * 128, (p + 1) * 128)
                v8, dy8, a8 = v_ref[bi, sl, lanes], dy_ref[bi, sl, lanes], a_ref[bi, sl, lanes]
                for j in range(8):
                    idx = ci * 8 + j
                    lp_ref[slot, blk(idx), :] = _spread_lhs(diag, v8[j:j + 1, :])
                    lp_ref[slot, blk(NS + idx), :] = _spread_lhs(diag, dy8[j:j + 1, :])
                    lp_ref[slot, blk(2 * NS + idx), :] = (st_ref[p, bi, gi * 8 + j] * a8[j:j + 1, :]).astype(MXU_DTYPE)
            rp_ref[slot] = jnp.dot(lp_ref[slot], ones_ref[...], preferred_element_type=F32)

        def recur(gi, slot):
            sl = rows_of(gi)
            tiles = [[ref[bi, sl, p * 128:(p + 1) * 128] for ref in (r_ref, w_ref, k_ref, a_ref, b_ref)]
                     for bi, p in chains]
            dS = [dS_ref[ci] for ci in range(len(chains))]
            acc = [[jnp.zeros((8, 128), F32) for _ in range(5)] for _ in chains]
            St = [None] * len(chains)
            for j in range(7, -1, -1):
                for ci, (bi, p) in enumerate(chains):
                    r8, w8, k8, a8, b8 = tiles[ci]
                    rj, wj, kj, aj, bj = r8[j:j + 1, :], w8[j:j + 1, :], k8[j:j + 1, :], a8[j:j + 1, :], b8[j:j + 1, :]
                    idx = ci * 8 + j
                    Sp = st_ref[p, bi, gi * 8 + j]
                    vb, dyb, sa = rp_ref[slot, blk(idx), :], rp_ref[slot, blk(NS + idx), :], rp_ref[slot, blk(2 * NS + idx), :]
                    if j == 7:
                        St[ci] = Sp * wj + sa * bj + vb * kj
                    d = dS[ci] + dyb * rj
                    dsa = _seg_lane_sum(d * bj, lo)
                    lq_ref[slot, blk(idx), :] = (d * kj).astype(MXU_DTYPE)
                    rows = (_colsum(St[ci] * dyb), _colsum(d * Sp), _colsum(d * vb), _colsum(Sp * dsa), _colsum(d * sa))
                    acc[ci] = [jnp.where(row8 == j, rw, a8_) for rw, a8_ in zip(rows, acc[ci], strict=True)]
                    dS[ci] = d * wj + dsa * aj
                    St[ci] = Sp
            for ci, (bi, p) in enumerate(chains):
                dS_ref[ci] = dS[ci]
                for ref, a8_ in zip((dr_ref, dw_ref, dk_ref, da_ref, db_ref), acc[ci], strict=True):
                    ref[bi, sl, p * 128:(p + 1) * 128] = a8_

        def emit(gi, slot):
            rq_ref[slot] = jnp.dot(lq_ref[slot], ones_ref[...], preferred_element_type=F32)
            for ci, (bi, p) in enumerate(chains):
                dv8 = jnp.zeros((8, 128), F32)
                for j in range(8):
                    dv8 = jnp.where(row8 == j, _colsum(diag * rq_ref[slot, blk(ci * 8 + j), :]), dv8)
                dv_ref[bi, rows_of(gi), p * 128:(p + 1) * 128] = dv8

        spread(NG - 1, 0)
        lq_ref[1] = jnp.zeros_like(lq_ref[1])

        def two_groups(m, _):
            g0, g1 = NG - 1 - 2 * m, NG - 2 - 2 * m
            spread(g1, 1)
            recur(g0, 0)
            emit(jnp.minimum(g0 + 1, NG - 1), 1)
            spread(jnp.maximum(g1 - 1, 0), 0)
            recur(g1, 1)
            emit(g0, 0)
            return 0

        lax.fori_loop(0, NG // 2, two_groups, 0)
        emit(0, 1)

    spec = pl.BlockSpec((B, TC, 128 * PG), lambda hp, c: (0, NC - 1 - c, hp))
    st_spec = pl.BlockSpec((PG, B, TC, HEAD, 128), lambda hp, c: (hp, 0, NC - 1 - c, 0, 0))
    sh = jax.ShapeDtypeStruct((B, T, D), F32)
    outs = pl.pallas_call(
        kern, name="rwkv_scan_bwd", grid=(HP // PG, NC),
        in_specs=[spec] * 6 + [st_spec, spec, pl.BlockSpec((128, 128), lambda hp, c: (0, 0))], out_specs=[spec] * 6,
        out_shape=[sh] * 6,
        scratch_shapes=[pltpu.VMEM((len(chains), HEAD, 128), F32),
                        pltpu.VMEM((2, 3 * NS * HEAD, 128), MXU_DTYPE), pltpu.VMEM((2, 3 * NS * HEAD, 128), F32),
                        pltpu.VMEM((2, NS * HEAD, 128), MXU_DTYPE), pltpu.VMEM((2, NS * HEAD, 128), F32)],
        compiler_params=_cparams(("parallel", "arbitrary")),
    )(*[x.reshape(B, T, D) for x in (r, w, k, v, a, b)], st, dy.reshape(B, T, D), _pair_ones())
    return [o.reshape(M, D) for o in outs]


def _head_mats(D):
    ch = jnp.arange(D) // HEAD
    e = (ch[:, None] == jnp.arange(128)[None, :]).astype(MXU_DTYPE)
    return jnp.concatenate([e, e], axis=0), jnp.concatenate([e.T, e.T], axis=0)


def _rwkv_fwd(x, p, B, T):
    M, D = x.shape
    e, et = _head_mats(D)
    norm = p["norm"].reshape(1, D)

    def body(rows, prevs, nexts, fulls, sf, sl):
        h, xx = _rwkv_mix(rows[0][...], prevs[0][...], _f(fulls[0]), None, sf)
        mix = _f(fulls[1])
        return [h + xx * mix[i:i + 1] for i in range(6)], []

    xs, _ = _tile_call(body, rows=[x], prevs=[0], fulls=[norm, _pad8(p["mix"])], row_outs=[(D, BF16)] * 6,
                       tm=min(256, T), T=T, name="rwkv_mix")
    r = _matmul(xs[0], p["w_r"], name="rwkv_r")
    k = _matmul(xs[1], p["w_k"], name="rwkv_k")
    v = _matmul(xs[2], p["w_v"], name="rwkv_v")
    pre_fulls = [p["w0"].reshape(1, D), p["w1"], p["w2"], p["a0"].reshape(1, D), p["a1"], p["a2"], p["g1"], p["g2"],
                 p["k_k"].reshape(1, D), p["k_a"].reshape(1, D), e, et]

    def body2(rows, prevs, nexts, fulls, sf, sl):
        outs = _rwkv_pre(rows[0][...], _f(rows[1]), _f(rows[2]), _f(rows[3]), *[f[...] for f in fulls])
        return list(outs), []

    (decay, k2, kkn, bb, g), _ = _tile_call(body2, rows=[k, xs[3], xs[4], xs[5]], fulls=pre_fulls,
                                            row_outs=[(D, F32)] * 5, tm=min(256, T), T=T, name="rwkv_pre")
    y, st = _rwkv_scan(r, decay, k2, v, kkn, bb, B, T)
    post_fulls = [p["ln_w"].reshape(1, D), p["ln_b"].reshape(1, D), p["r_k"].reshape(1, D), e, et]

    def body3(rows, prevs, nexts, fulls, sf, sl):
        return [_rwkv_post(*[rr[...] for rr in rows], *[f[...] for f in fulls])], []

    (z,), _ = _tile_call(body3, rows=[y, r, k2, v, g], fulls=post_fulls, row_outs=[(D, BF16)], tm=min(256, T), T=T,
                         name="rwkv_post")
    out = _matmul(z, p["w_out"], residual=x, name="rwkv_out")
    return out, (x, xs, r, k, v, decay, k2, kkn, bb, g, y, st, z)


def _rwkv_bwd(dout, saved, p, B, T):
    x, xs, r, k, v, decay, k2, kkn, bb, g, y, st, z = saved
    M, D = x.shape
    e, et = _head_mats(D)
    d_z = _matmul(dout, p["w_out"], mode="nt", name="rwkv_out_dx")
    d_w_out = _matmul(z, dout, mode="tn", out_dtype=BF16, name="rwkv_out_dw")
    post_fulls = [p["ln_w"].reshape(1, D), p["ln_b"].reshape(1, D), p["r_k"].reshape(1, D), e, et]

    def body(rows, prevs, nexts, fulls, sf, sl):
        prim = [rr[...] for rr in rows[:5]] + [f[...] for f in fulls]
        _, vjp = jax.vjp(_rwkv_post, *prim)
        ct = vjp(rows[5][...])
        return list(ct[:5]), list(ct[5:8])

    (d_y, d_r1, d_k21, d_v1, d_g), (d_ln_w, d_ln_b, d_r_k) = _tile_call(
        body, rows=[y, r, k2, v, g, d_z], fulls=post_fulls, row_outs=[(D, F32)] * 5, acc_outs=[(1, D)] * 3,
        tm=min(256, T), T=T, name="rwkv_post_bwd")
    d_r2, d_w, d_k22, d_v2, d_kkn, d_bb = _rwkv_scan_bwd(r, decay, k2, v, kkn, bb, st, d_y, B, T)
    pre_fulls = [p["w0"].reshape(1, D), p["w1"], p["w2"], p["a0"].reshape(1, D), p["a1"], p["a2"], p["g1"], p["g2"],
                 p["k_k"].reshape(1, D), p["k_a"].reshape(1, D), e, et]

    def body2(rows, prevs, nexts, fulls, sf, sl):
        prim = [rows[0][...], _f(rows[1]), _f(rows[2]), _f(rows[3])] + [f[...] for f in fulls]
        _, vjp = jax.vjp(_rwkv_pre, *prim)
        ct = vjp((rows[4][...], rows[5][...] + rows[6][...], rows[7][...], rows[8][...], rows[9][...]))
        d_r = rows[10][...] + rows[11][...]
        d_v = rows[12][...] + rows[13][...]
        return [ct[0], ct[1], ct[2], ct[3], d_r, d_v], [c.astype(F32) for c in ct[4:14]]

    acc_shapes = [f.shape for f in pre_fulls[:10]]
    (d_k, d_xw, d_xa, d_xg, d_r, d_v), pgr = _tile_call(
        body2, rows=[k, xs[3], xs[4], xs[5], d_w, d_k21, d_k22, d_kkn, d_bb, d_g, d_r1, d_r2, d_v1, d_v2],
        fulls=pre_fulls, row_outs=[(D, BF16), (D, F32), (D, F32), (D, F32), (D, BF16), (D, BF16)], acc_outs=acc_shapes,
        tm=min(128, T), T=T, name="rwkv_pre_bwd")
    d_xr = _matmul(d_r, p["w_r"], mode="nt", name="rwkv_r_dx")
    d_xk = _matmul(d_k, p["w_k"], mode="nt", name="rwkv_k_dx")
    d_xv = _matmul(d_v, p["w_v"], mode="nt", name="rwkv_v_dx")
    d_wr = _matmul(xs[0], d_r, mode="tn", out_dtype=BF16, name="rwkv_r_dw")
    d_wk = _matmul(xs[1], d_k, mode="tn", out_dtype=BF16, name="rwkv_k_dw")
    d_wv = _matmul(xs[2], d_v, mode="tn", out_dtype=BF16, name="rwkv_v_dw")
    norm = p["norm"].reshape(1, D)

    def body3(rows, prevs, nexts, fulls, sf, sl):
        xc, xp = rows[0][...], prevs[0][...]
        nrm, mix = _f(fulls[0]), _f(fulls[1])
        h, xx = _rwkv_mix(xc, xp, nrm, None, sf)
        dxs = [rows[1 + i][...] for i in range(6)]
        dxs_n = [nexts[i][...] for i in range(6)]
        d_h = jnp.zeros_like(h)
        d_sh = jnp.zeros_like(h)
        d_sh_n = jnp.zeros_like(dxs_n[0])
        dmix = []
        for i in range(6):
            m = mix[i:i + 1]
            d_h = d_h + dxs[i] * (1.0 - m)
            d_sh = d_sh + dxs[i] * m
            d_sh_n = d_sh_n + dxs_n[i] * m
            dmix.append(_colsum(dxs[i] * xx))
        d_h = d_h + _shift_up(d_sh, d_sh_n, 1, sl)
        _, vjp = jax.vjp(_rms, xc, nrm)
        dx, dn = vjp(d_h)
        return [dx + rows[7][...]], [dn, _pad8(jnp.concatenate(dmix, axis=0))]

    (dx,), (d_norm, d_mix) = _tile_call(
        body3, rows=[x, d_xr, d_xk, d_xv, d_xw, d_xa, d_xg, dout], prevs=[0], nexts=[1, 2, 3, 4, 5, 6],
        fulls=[norm, _pad8(p["mix"])], row_outs=[(D, F32)], acc_outs=[(1, D), (8, D)], tm=min(256, T), T=T,
        name="rwkv_mix_bwd")
    names = ["w0", "w1", "w2", "a0", "a1", "a2", "g1", "g2", "k_k", "k_a"]
    grads = {n: gr.reshape(p[n].shape) for n, gr in zip(names, pgr, strict=True)}
    grads.update({"norm": d_norm.reshape(D), "mix": d_mix[:6], "w_r": d_wr, "w_k": d_wk, "w_v": d_wv,
                  "r_k": d_r_k.reshape(p["r_k"].shape), "ln_w": d_ln_w.reshape(D), "ln_b": d_ln_b.reshape(D),
                  "w_out": d_w_out})
    return dx, grads


def _loss_head(x, g, tgt, T):
    M, D = x.shape

    def body(rows, prevs, nexts, fulls, sf, sl):
        xv, gv = rows[0][...], _f(fulls[0])
        yv, vjp = jax.vjp(_rms, xv, gv)
        err = yv - rows[1][...]
        dx, dg = vjp(err * (1.0 / D))
        part = jnp.sum(_colsum(err * err), axis=1, keepdims=True) * (0.5 / D)
        return [dx], [dg, jnp.broadcast_to(part, (1, 128))]

    (dx,), (dg, loss) = _tile_call(body, rows=[x, tgt], fulls=[g.reshape(1, D)], row_outs=[(D, F32)],
                                   acc_outs=[(1, D), (1, 128)], tm=min(512, T), T=T, name="loss_head")
    return loss[0, 0], dx, dg.reshape(D)


def _local_step(x3, tgt3, P):
    B, T, D = x3.shape
    x, tgt = x3.reshape(B * T, D), tgt3.reshape(B * T, D)
    x1, s_lru = _lru_fwd(x, P["lru"], B, T)
    x2, s_f0 = _ffn_fwd(x1, P["ffn0"], T, "0")
    x3_, s_rw = _rwkv_fwd(x2, P["rwkv"], B, T)
    x4, s_f1 = _ffn_fwd(x3_, P["ffn1"], T, "1")
    loss, d4, d_fn = _loss_head(x4, P["final_norm"], tgt, T)
    d3, g_f1 = _ffn_bwd(d4, s_f1, P["ffn1"], T, "1")
    d2, g_rw = _rwkv_bwd(d3, s_rw, P["rwkv"], B, T)
    d1, g_f0 = _ffn_bwd(d2, s_f0, P["ffn0"], T, "0")
    d0, g_lru = _lru_bwd(d1, s_lru, P["lru"], B, T)
    return loss, d0.reshape(B, T, D), {"lru": g_lru, "ffn0": g_f0, "rwkv": g_rw, "ffn1": g_f1, "final_norm": d_fn}


WEIGHTS = ['lru_norm', 'lru_w_in', 'lru_b_in', 'lru_conv_w', 'lru_conv_b', 'lru_gate_w', 'lru_gate_b', 'lru_lambda',
           'lru_w_out', 'lru_b_out', 'rwkv_norm', 'rwkv_mix', 'rwkv_w_rkv', 'rwkv_w0', 'rwkv_w1', 'rwkv_w2', 'rwkv_a0',
           'rwkv_a1', 'rwkv_a2', 'rwkv_g1', 'rwkv_g2', 'rwkv_k_k', 'rwkv_k_a', 'rwkv_r_k', 'rwkv_ln_w', 'rwkv_ln_b',
           'rwkv_w_out', 'ffn_norm', 'ffn_w_up', 'ffn_conv_w', 'ffn_conv_b', 'ffn_w_down', 'final_norm']
SHARD_AXIS = {'lru_w_in': 2, 'lru_conv_w': 2, 'lru_w_out': 1, 'rwkv_norm': 1, 'rwkv_mix': 2, 'rwkv_w_rkv': 2,
              'rwkv_w0': 1, 'rwkv_w1': 1, 'rwkv_w2': 2, 'rwkv_a0': 1, 'rwkv_a1': 1, 'rwkv_a2': 2, 'rwkv_g1': 1,
              'rwkv_g2': 2, 'rwkv_k_k': 1, 'rwkv_k_a': 1, 'rwkv_ln_w': 1, 'rwkv_ln_b': 1, 'rwkv_w_out': 1,
              'ffn_w_up': 2, 'ffn_conv_w': 2, 'ffn_w_down': 1}
MXU_WEIGHTS = ('lru_w_in', 'lru_w_out', 'rwkv_w_rkv', 'rwkv_w_out', 'ffn_w_up', 'ffn_w_down')
N_CHIPS = 4
LANES = 1024


def _pack(arrs, dtype, row_mult):
    flat = jnp.concatenate([a.reshape(-1).astype(dtype) for a in arrs])
    n = flat.shape[0]
    unit = row_mult * LANES
    tot = -(-n // unit) * unit
    if tot > n:
        flat = jnp.concatenate([flat, jnp.zeros((tot - n,), dtype)])
    return flat.reshape(tot // LANES, LANES)


def _unpack(buf, shapes):
    flat = buf.reshape(-1)
    out, off = [], 0
    for s in shapes:
        n = 1
        for d in s:
            n *= d
        out.append(flat[off:off + n].reshape(s))
        off += n
    return out


def _to_shards(full, axis):
    return jnp.stack(jnp.split(full, N_CHIPS, axis=axis))


MESH_ID = pl.DeviceIdType.MESH


ANY_SPEC = pl.BlockSpec(memory_space=pl.ANY)
COMM_PARAMS = pltpu.CompilerParams(has_side_effects=True)


def _mesh_place():
    x, y, c = lax.axis_index("x"), lax.axis_index("y"), lax.axis_index("c")
    return x, y, c, 2 * x + y, [(1 - x, y), (x, 1 - y), (1 - x, 1 - y)]


def _gather_all(arrs, name):
    n = len(arrs)

    def body(*refs):
        ins, outs = refs[:n], refs[n:2 * n]
        send_sems, recv_sems, local_sems = refs[2 * n:]
        x, y, c, p, chips = _mesh_place()
        sibling = (x, y, 1 - c)

        def rows(a, which):
            h = arrs[a].shape[0] // 2
            return pl.ds(which * h, h)

        def copy(a, k, region, src, to):
            return pltpu.make_async_remote_copy(src_ref=src, dst_ref=region, send_sem=send_sems.at[a, k],
                                                recv_sem=recv_sems.at[a, k], device_id=to, device_id_type=MESH_ID)

        local = [pltpu.make_async_copy(ins[a], outs[a].at[p], local_sems.at[a]) for a in range(n)]
        for cp in local:
            cp.start()
        first = [copy(a, j, outs[a].at[p, rows(a, c)], ins[a].at[rows(a, c)], (qx, qy, c))
                 for a in range(n) for j, (qx, qy) in enumerate(chips)]
        for cp in first:
            cp.start()
        passed = []
        for a in range(n):
            for j, (qx, qy) in enumerate(chips):
                region = outs[a].at[2 * qx + qy, rows(a, c)]
                copy(a, j, region, region, (qx, qy, c)).wait_recv()
                fw = copy(a, 3 + j, region, region, sibling)
                fw.start()
                passed.append(fw)
        for a in range(n):
            for j, (qx, qy) in enumerate(chips):
                region = outs[a].at[2 * qx + qy, rows(a, 1 - c)]
                copy(a, 3 + j, region, region, sibling).wait_recv()
        for cp in first + passed:
            cp.wait_send()
        for cp in local:
            cp.wait()

    return pl.pallas_call(
        body, name=name, out_shape=[jax.ShapeDtypeStruct((N_CHIPS,) + a.shape, a.dtype) for a in arrs],
        in_specs=[ANY_SPEC] * n, out_specs=[ANY_SPEC] * n,
        scratch_shapes=[pltpu.SemaphoreType.DMA((n, 6)), pltpu.SemaphoreType.DMA((n, 6)), pltpu.SemaphoreType.DMA((n,))],
        compiler_params=COMM_PARAMS,
    )(*arrs)


def _pair_swap_all(gs, rep, name):
    n = len(gs)

    def body(*refs):
        ins, outs = refs[:n + 1], refs[n + 1:2 * n + 2]
        send_sems, recv_sems = refs[2 * n + 2:]
        x, y, c, _, _ = _mesh_place()
        copies = []
        for a in range(n + 1):
            src = ins[a]
            if a < n:
                h = gs[a].shape[1] // 2
                src = src.at[:, pl.ds((1 - c) * h, h), :]
            copies.append(pltpu.make_async_remote_copy(src_ref=src, dst_ref=outs[a], send_sem=send_sems.at[a],
                                                       recv_sem=recv_sems.at[a], device_id=(x, y, 1 - c),
                                                       device_id_type=MESH_ID))
        for cp in copies:
            cp.start()
        for cp in copies:
            cp.wait()

    shapes = [jax.ShapeDtypeStruct((N_CHIPS, g.shape[1] // 2, g.shape[2]), g.dtype) for g in gs]
    shapes.append(jax.ShapeDtypeStruct(rep.shape, rep.dtype))
    res = pl.pallas_call(
        body, name=name, out_shape=shapes, in_specs=[ANY_SPEC] * (n + 1), out_specs=[ANY_SPEC] * (n + 1),
        scratch_shapes=[pltpu.SemaphoreType.DMA((n + 1,)), pltpu.SemaphoreType.DMA((n + 1,))],
        compiler_params=COMM_PARAMS,
    )(*gs, rep)
    return res[:n], res[n]


def _chip_exchange_all(ps, rep, name):
    n = len(ps)

    def body(*refs):
        ins, outs = refs[:n + 1], refs[n + 1:2 * n + 2]
        send_sems, recv_sems, local_sems = refs[2 * n + 2:]
        x, y, c, p, chips = _mesh_place()

        def src(a, q):
            return ins[a].at[q] if a < n else ins[a]

        local = [pltpu.make_async_copy(src(a, p), outs[a].at[p], local_sems.at[a]) for a in range(n + 1)]
        for cp in local:
            cp.start()
        sends, recvs = [], []
        for a in range(n + 1):
            for j, (qx, qy) in enumerate(chips):
                q = 2 * qx + qy
                for dst, keep in ((outs[a].at[p], sends), (outs[a].at[q], recvs)):
                    keep.append(pltpu.make_async_remote_copy(
                        src_ref=src(a, q), dst_ref=dst, send_sem=send_sems.at[a, j], recv_sem=recv_sems.at[a, j],
                        device_id=(qx, qy, c), device_id_type=MESH_ID))
        for cp in sends:
            cp.start()
        for cp in recvs:
            cp.wait_recv()
        for cp in sends:
            cp.wait_send()
        for cp in local:
            cp.wait()

    shapes = [jax.ShapeDtypeStruct(g.shape, g.dtype) for g in ps]
    shapes.append(jax.ShapeDtypeStruct((N_CHIPS,) + rep.shape, rep.dtype))
    res = pl.pallas_call(
        body, name=name, out_shape=shapes, in_specs=[ANY_SPEC] * (n + 1), out_specs=[ANY_SPEC] * (n + 1),
        scratch_shapes=[pltpu.SemaphoreType.DMA((n + 1, 3)), pltpu.SemaphoreType.DMA((n + 1, 3)),
                        pltpu.SemaphoreType.DMA((n + 1,))],
        compiler_params=COMM_PARAMS,
    )(*ps, rep)
    return res[:n], res[n]


def _half_swap_all(ts, name):
    n = len(ts)

    def body(*refs):
        ins, outs = refs[:n], refs[n:2 * n]
        send_sems, recv_sems = refs[2 * n:]
        x, y, c, _, _ = _mesh_place()
        copies = [pltpu.make_async_remote_copy(src_ref=ins[a], dst_ref=outs[a], send_sem=send_sems.at[a],
                                               recv_sem=recv_sems.at[a], device_id=(x, y, 1 - c), device_id_type=MESH_ID)
                  for a in range(n)]
        for cp in copies:
            cp.start()
        for cp in copies:
            cp.wait()

    return pl.pallas_call(
        body, name=name, out_shape=[jax.ShapeDtypeStruct(t.shape, t.dtype) for t in ts],
        in_specs=[ANY_SPEC] * n, out_specs=[ANY_SPEC] * n,
        scratch_shapes=[pltpu.SemaphoreType.DMA((n,)), pltpu.SemaphoreType.DMA((n,))],
        compiler_params=COMM_PARAMS,
    )(*ts)


def _pick_rows(R, cap=256):
    for t in (512, 256, 128, 64, 32, 16, 8):
        if t <= cap and R % t == 0:
            return t
    return R


def _pair_sum(g, got, name):
    _, R, C = g.shape
    h = R // 2
    th = _pick_rows(h)

    def kern(g_ref, got_ref, o_ref):
        both = g_ref[...].astype(F32)
        mine = jnp.where(lax.axis_index("c") == 0, both[0], both[1])
        o_ref[...] = (mine + got_ref[...].astype(F32)).astype(o_ref.dtype)

    return pl.pallas_call(
        kern, name=name, grid=(N_CHIPS, h // th),
        in_specs=[pl.BlockSpec((None, 2, th, C), lambda q, i: (q, 0, i, 0)), pl.BlockSpec((None, th, C), lambda q, i: (q, i, 0))],
        out_specs=pl.BlockSpec((None, th, C), lambda q, i: (q, i, 0)),
        out_shape=jax.ShapeDtypeStruct((N_CHIPS, h, C), BF16), compiler_params=_cparams(("parallel", "parallel")),
    )(g.reshape(N_CHIPS, 2, h, C), got)


def _rep_pair_sum(rep, got, name):
    R, C = rep.shape
    tr = _pick_rows(R)

    def kern(a_ref, b_ref, o_ref):
        o_ref[...] = (a_ref[...] + b_ref[...]).astype(o_ref.dtype)

    spec = pl.BlockSpec((tr, C), lambda i: (i, 0))
    return pl.pallas_call(kern, name=name, grid=(R // tr,), in_specs=[spec, spec], out_specs=spec,
                          out_shape=jax.ShapeDtypeStruct((R, C), BF16), compiler_params=_cparams(("parallel",)))(rep, got)


def _chip_sum(arrived, name):
    _, R, C = arrived.shape
    tr = _pick_rows(R)

    def kern(a_ref, o_ref):
        acc = a_ref[0].astype(F32)
        for q in range(1, N_CHIPS):
            acc = acc + a_ref[q].astype(F32)
        o_ref[...] = acc

    return pl.pallas_call(
        kern, name=name, grid=(R // tr,), in_specs=[pl.BlockSpec((N_CHIPS, tr, C), lambda i: (0, i, 0))],
        out_specs=pl.BlockSpec((tr, C), lambda i: (i, 0)), out_shape=jax.ShapeDtypeStruct((R, C), F32),
        compiler_params=_cparams(("parallel",)),
    )(arrived)


def _adam_math(w, g, m, v):
    c1 = 1.0 / (1.0 - ADAM_B1 ** ADAM_STEP)
    c2 = 1.0 / (1.0 - ADAM_B2 ** ADAM_STEP)
    nm = ADAM_B1 * m + (1.0 - ADAM_B1) * g
    nv = ADAM_B2 * v + (1.0 - ADAM_B2) * (g * g)
    return -ADAM_LR * ((nm * c1) / (jnp.sqrt(nv * c2) + ADAM_EPS) + ADAM_WD * w), nm, nv


def _adamw_halves(w, m, v, mine, other, name):
    R, C = w.shape
    h = R // 2
    th = _pick_rows(h)
    nt = h // th

    def kern(w_ref, m_ref, v_ref, a_ref, b_ref, g_ref, d_ref, nm_ref, nv_ref):
        g = jnp.where(pl.program_id(0) == lax.axis_index("c"), a_ref[...], b_ref[...])
        d, nm, nv = _adam_math(w_ref[...], g, m_ref[...], v_ref[...])
        g_ref[...] = g
        d_ref[...] = d
        nm_ref[...] = nm
        nv_ref[...] = nv

    full = pl.BlockSpec((th, C), lambda hh, i: (hh * nt + i, 0))
    half = pl.BlockSpec((th, C), lambda hh, i: (i, 0))
    sh = jax.ShapeDtypeStruct((R, C), F32)
    return pl.pallas_call(kern, name=name, grid=(2, nt), in_specs=[full] * 3 + [half] * 2, out_specs=[full] * 4,
                          out_shape=[sh] * 4, compiler_params=_cparams(("parallel", "parallel")))(w, m, v, mine, other)


def _adamw_call(w, g, m, v, name):
    R = w.shape[0]
    tr = _pick_rows(R)
    c1 = 1.0 / (1.0 - ADAM_B1 ** ADAM_STEP)
    c2 = 1.0 / (1.0 - ADAM_B2 ** ADAM_STEP)

    def kern(w_ref, g_ref, m_ref, v_ref, d_ref, nm_ref, nv_ref):
        gv = g_ref[...]
        nm = ADAM_B1 * m_ref[...] + (1.0 - ADAM_B1) * gv
        nv = ADAM_B2 * v_ref[...] + (1.0 - ADAM_B2) * (gv * gv)
        d_ref[...] = -ADAM_LR * ((nm * c1) / (jnp.sqrt(nv * c2) + ADAM_EPS) + ADAM_WD * w_ref[...])
        nm_ref[...] = nm
        nv_ref[...] = nv

    spec = pl.BlockSpec((tr, LANES), lambda i: (i, 0))
    sh = jax.ShapeDtypeStruct((R, LANES), F32)
    return pl.pallas_call(kern, name=name, grid=(R // tr,), in_specs=[spec] * 4, out_specs=[spec] * 3,
                          out_shape=[sh] * 3, compiler_params=_cparams(("parallel",)))(w, g, m, v)


GATE_GROUP = 256


def _gate_dense(gate_w):
    _, nb, bw, _ = gate_w.shape
    D = nb * bw
    gb = min(GATE_GROUP, D)
    per = gb // bw
    w = gate_w.reshape(2, D // gb, per, bw, bw)
    eye = jnp.eye(per, dtype=gate_w.dtype)
    dense = jnp.einsum('grncd,nm->rncgmd', w, eye)
    return dense.reshape(D, 2 * gb)


def _gate_blocks(d_dense, nb):
    D, gb2 = d_dense.shape
    gb, bw = gb2 // 2, D // nb
    per = gb // bw
    g = d_dense.reshape(D // gb, per, bw, 2, per, bw)
    return jnp.einsum('rncgnd->grncd', g).reshape(2, nb, bw, bw)


def _step(W, M1, V1, x, tgt):
    sharded = [n for n in WEIGHTS if n in SHARD_AXIS]
    repl = [n for n in WEIGHTS if n not in SHARD_AXIS]
    small = [n for n in sharded if n not in MXU_WEIGHTS]
    D = x.shape[-1]

    def rows2d(a):
        return a.reshape(-1, a.shape[-1])

    small_buf = _pack([W[n] for n in small], F32, 16)
    gathered = _gather_all([rows2d(W[n]).astype(MXU_DTYPE) for n in MXU_WEIGHTS] + [small_buf], "gather_weights")
    mats = {n: g.reshape((N_CHIPS,) + W[n].shape[-3:]) for n, g in zip(MXU_WEIGHTS, gathered[:-1], strict=True)}
    per_chip = [_unpack(gathered[-1][q], [W[n].shape for n in small]) for q in range(N_CHIPS)]
    full = {n: jnp.concatenate([per_chip[q][i] for q in range(N_CHIPS)], axis=SHARD_AXIS[n]) for i, n in enumerate(small)}
    for n in repl:
        full[n] = W[n]

    def by_rows(name, layer):
        m = mats[name][:, layer]
        return m.reshape(N_CHIPS * m.shape[1], m.shape[2])

    P = {
        "lru": {"norm": full["lru_norm"][0], "w_in": _W(mats["lru_w_in"], 0), "b_in": full["lru_b_in"][0],
                "conv_w": full["lru_conv_w"][0], "conv_b": full["lru_conv_b"][0],
                "wbd": _gate_dense(full["lru_gate_w"][0]).astype(MXU_DTYPE), "gate_b": full["lru_gate_b"][0].reshape(-1),
                "lam": full["lru_lambda"][0], "w_out": by_rows("lru_w_out", 0), "b_out": full["lru_b_out"][0]},
        "rwkv": {"norm": full["rwkv_norm"][0], "mix": full["rwkv_mix"][0],
                 "w_r": by_rows("rwkv_w_rkv", 0), "w_k": by_rows("rwkv_w_rkv", 1), "w_v": by_rows("rwkv_w_rkv", 2),
                 "w0": full["rwkv_w0"][0], "w1": full["rwkv_w1"][0], "w2": full["rwkv_w2"][0], "a0": full["rwkv_a0"][0],
                 "a1": full["rwkv_a1"][0], "a2": full["rwkv_a2"][0], "g1": full["rwkv_g1"][0], "g2": full["rwkv_g2"][0],
                 "k_k": full["rwkv_k_k"][0], "k_a": full["rwkv_k_a"][0], "r_k": full["rwkv_r_k"][0],
                 "ln_w": full["rwkv_ln_w"][0], "ln_b": full["rwkv_ln_b"][0], "w_out": by_rows("rwkv_w_out", 0)},
        "final_norm": full["final_norm"],
    }
    for l in range(2):
        P[f"ffn{l}"] = {"norm": full["ffn_norm"][l], "w_up": _W(mats["ffn_w_up"], l),
                        "conv_w": full["ffn_conv_w"][l], "conv_b": full["ffn_conv_b"][l],
                        "w_down": by_rows("ffn_w_down", l)}

    loss, gx, G = _local_step(x, tgt, P)

    nb = W["lru_gate_w"].shape[2]
    gl, gr = G["lru"], G["rwkv"]
    gfull = {
        "lru_norm": gl["norm"][None], "lru_b_in": gl["b_in"][None],
        "lru_conv_w": gl["conv_w"][None], "lru_conv_b": gl["conv_b"][None], "lru_gate_w": _gate_blocks(gl["wbd"], nb)[None],
        "lru_gate_b": gl["gate_b"].reshape(W["lru_gate_b"].shape), "lru_lambda": gl["lam"][None],
        "lru_b_out": gl["b_out"][None],
        "rwkv_norm": gr["norm"][None], "rwkv_mix": gr["mix"][None],
        "rwkv_w0": gr["w0"][None], "rwkv_w1": gr["w1"][None], "rwkv_w2": gr["w2"][None], "rwkv_a0": gr["a0"][None],
        "rwkv_a1": gr["a1"][None], "rwkv_a2": gr["a2"][None], "rwkv_g1": gr["g1"][None], "rwkv_g2": gr["g2"][None],
        "rwkv_k_k": gr["k_k"][None], "rwkv_k_a": gr["k_a"][None], "rwkv_r_k": gr["r_k"][None],
        "rwkv_ln_w": gr["ln_w"][None], "rwkv_ln_b": gr["ln_b"][None],
        "final_norm": G["final_norm"],
    }
    for k in ("norm", "conv_w", "conv_b"):
        gfull["ffn_" + k] = jnp.stack([G["ffn0"][k], G["ffn1"][k]])

    small_g = jnp.stack([_pack([_to_shards(gfull[n], SHARD_AXIS[n])[q] for n in small], F32, 16) for q in range(N_CHIPS)])
    cut = lambda g: g.reshape(N_CHIPS, g.shape[0] // N_CHIPS, g.shape[1])
    pieces = [("lru_w_in", (0,), gl["w_in"]), ("lru_w_out", (0,), cut(gl["w_out"])),
              ("rwkv_w_rkv", (0, 0), cut(gr["w_r"])), ("rwkv_w_rkv", (0, 1), cut(gr["w_k"])),
              ("rwkv_w_rkv", (0, 2), cut(gr["w_v"])), ("rwkv_w_out", (0,), cut(gr["w_out"])),
              ("ffn_w_up", (0,), G["ffn0"]["w_up"]), ("ffn_w_up", (1,), G["ffn1"]["w_up"]),
              ("ffn_w_down", (0,), cut(G["ffn0"]["w_down"])), ("ffn_w_down", (1,), cut(G["ffn1"]["w_down"]))]
    gs = [g for _, _, g in pieces] + [small_g]
    grep = _pack([gfull[n] for n in repl], F32, 16)

    got, got_rep = _pair_swap_all(gs, grep, "reduce_pair_swap")
    pair = [_pair_sum(g, r, f"reduce_pair_sum{i}") for i, (g, r) in enumerate(zip(gs, got, strict=True))]
    pair_rep = _rep_pair_sum(grep, got_rep, "reduce_pair_sum_rep")
    arrived, arrived_rep = _chip_exchange_all(pair, pair_rep, "reduce_chips")
    mine = [_chip_sum(a, f"reduce_chip_sum{i}") for i, a in enumerate(arrived)]
    g_rp = _chip_sum(arrived_rep, "reduce_chip_sum_rep")
    other = _half_swap_all(mine, "reduce_half_swap")

    outs, parts = {}, {}
    for i, (n, idx, _) in enumerate(pieces):
        w, m, v = (rows2d(S[n][idx]) for S in (W, M1, V1))
        res = _adamw_halves(w, m, v, mine[i], other[i], f"adamw{i}")
        for kind, a in zip(("grad", "delta", "new_m", "new_v"), res, strict=True):
            parts.setdefault((kind, n), []).append(a)
    for (kind, n), lst in parts.items():
        a = lst[0] if len(lst) == 1 else jnp.stack(lst)
        outs[(kind, n)] = a.reshape(W[n].shape)
    wb, mb, vb = (_pack([S[n] for n in small], F32, 16) for S in (W, M1, V1))
    res = _adamw_halves(wb, mb, vb, mine[-1], other[-1], "adamw_small")
    for kind, buf in zip(("grad", "delta", "new_m", "new_v"), res, strict=True):
        for n, a in zip(small, _unpack(buf, [W[n].shape for n in small]), strict=True):
            outs[(kind, n)] = a
    wb, mb, vb = (_pack([S[n] for n in repl], F32, 16) for S in (W, M1, V1))
    d, nm, nv = _adamw_call(wb, g_rp, mb, vb, "adamw_repl")
    for kind, buf in (("grad", g_rp), ("delta", d), ("new_m", nm), ("new_v", nv)):
        for n, a in zip(repl, _unpack(buf, [W[n].shape for n in repl]), strict=True):
            outs[(kind, n)] = a
    loss = lax.psum(loss, ("x", "y", "c"))
    return (loss, gx, *[outs[(kind, n)] for kind in ("grad", "delta", "new_m", "new_v") for n in WEIGHTS])


def kernel(x, lru_norm, lru_w_in, lru_b_in, lru_conv_w, lru_conv_b, lru_gate_w, lru_gate_b, lru_lambda, lru_w_out, lru_b_out, rwkv_norm, rwkv_mix, rwkv_w_rkv, rwkv_w0, rwkv_w1, rwkv_w2, rwkv_a0, rwkv_a1, rwkv_a2, rwkv_g1, rwkv_g2, rwkv_k_k, rwkv_k_a, rwkv_r_k, rwkv_ln_w, rwkv_ln_b, rwkv_w_out, ffn_norm, ffn_w_up, ffn_conv_w, ffn_conv_b, ffn_w_down, final_norm, loss_target, m_lru_norm, m_lru_w_in, m_lru_b_in, m_lru_conv_w, m_lru_conv_b, m_lru_gate_w, m_lru_gate_b, m_lru_lambda, m_lru_w_out, m_lru_b_out, m_rwkv_norm, m_rwkv_mix, m_rwkv_w_rkv, m_rwkv_w0, m_rwkv_w1, m_rwkv_w2, m_rwkv_a0, m_rwkv_a1, m_rwkv_a2, m_rwkv_g1, m_rwkv_g2, m_rwkv_k_k, m_rwkv_k_a, m_rwkv_r_k, m_rwkv_ln_w, m_rwkv_ln_b, m_rwkv_w_out, m_ffn_norm, m_ffn_w_up, m_ffn_conv_w, m_ffn_conv_b, m_ffn_w_down, m_final_norm, v_lru_norm, v_lru_w_in, v_lru_b_in, v_lru_conv_w, v_lru_conv_b, v_lru_gate_w, v_lru_gate_b, v_lru_lambda, v_lru_w_out, v_lru_b_out, v_rwkv_norm, v_rwkv_mix, v_rwkv_w_rkv, v_rwkv_w0, v_rwkv_w1, v_rwkv_w2, v_rwkv_a0, v_rwkv_a1, v_rwkv_a2, v_rwkv_g1, v_rwkv_g2, v_rwkv_k_k, v_rwkv_k_a, v_rwkv_r_k, v_rwkv_ln_w, v_rwkv_ln_b, v_rwkv_w_out, v_ffn_norm, v_ffn_w_up, v_ffn_conv_w, v_ffn_conv_b, v_ffn_w_down, v_final_norm):
    given = dict(locals())
    W = {n: given[n] for n in WEIGHTS}
    M1 = {n: given["m_" + n] for n in WEIGHTS}
    V1 = {n: given["v_" + n] for n in WEIGHTS}
    return _step(W, M1, V1, x, loss_target)
```

```python
import functools

import jax
import jax.numpy as jnp
from jax import lax
from jax.experimental import pallas as pl
from jax.experimental.pallas import tpu as pltpu

F32 = jnp.float32
BF16 = jnp.bfloat16
MXU_DTYPE = BF16

HEAD = 64
LRU_C = 8.0
GN_EPS = 64e-5
RMS_EPS = 1e-6
HALO = 16
VMEM_LIMIT = 56 * 1024 * 1024

ADAM_LR, ADAM_B1, ADAM_B2, ADAM_EPS, ADAM_WD, ADAM_STEP = 0.001, 0.9, 0.999, 1e-08, 0.01, 10


def _cparams(sem):
    return pltpu.CompilerParams(dimension_semantics=sem, vmem_limit_bytes=VMEM_LIMIT)


def _pick(n, want):
    if n <= want:
        return n
    t = want
    while t >= 128:
        if n % t == 0:
            return t
        t -= 128
    return n


class _W:
    def __init__(self, arr, layer):
        self.arr, self.layer = arr, layer
        self.shape = (arr.shape[2], N_CHIPS * arr.shape[3])


def _matmul(a, b, mode="nn", bias=None, residual=None, out_dtype=F32, name="mm", tm=1024, tn=1024, tk=1024,
            out_cols_by_chip=False):
    bshape = b.shape
    if mode == "nn":
        (M, K), (K2, N) = a.shape, bshape
    elif mode == "nt":
        (M, K), (N, K2) = a.shape, bshape
    else:
        (K, M), (K2, N) = a.shape, bshape
    assert K == K2, (a.shape, bshape, mode)
    lim_n, lim_k = N, K
    if isinstance(b, _W):
        if mode == "nn":
            lim_n = b.arr.shape[3]
        else:
            lim_k = b.arr.shape[3]
    if out_cols_by_chip:
        lim_n = min(lim_n, N // N_CHIPS)
    if K > 2 * tk:
        tk = 2 * tk
    tm, tn, tk = _pick(M, tm), _pick(lim_n, tn), _pick(lim_k, tk)
    nk = K // tk
    dims = {"nn": (((1,), (0,)), ((), ())), "nt": (((1,), (1,)), ((), ())), "tn": (((0,), (0,)), ((), ()))}[mode]
    a_spec = {"nn": pl.BlockSpec((tm, tk), lambda i, j, k: (i, k)),
              "nt": pl.BlockSpec((tm, tk), lambda i, j, k: (i, k)),
              "tn": pl.BlockSpec((tk, tm), lambda i, j, k: (k, i))}[mode]
    if isinstance(b, _W):
        lay = b.layer
        if mode == "nn":
            per = b.arr.shape[3] // tn
            b_spec = pl.BlockSpec((None, None, tk, tn), lambda i, j, k: (j // per, lay, k, j % per))
        else:
            assert mode == "nt"
            per = b.arr.shape[3] // tk
            b_spec = pl.BlockSpec((None, None, tn, tk), lambda i, j, k: (k // per, lay, j, k % per))
        b = b.arr
    else:
        b_spec = {"nn": pl.BlockSpec((tk, tn), lambda i, j, k: (k, j)),
                  "nt": pl.BlockSpec((tn, tk), lambda i, j, k: (j, k)),
                  "tn": pl.BlockSpec((tk, tn), lambda i, j, k: (k, j))}[mode]
    if out_cols_by_chip:
        opc = N // N_CHIPS // tn
        out_spec = pl.BlockSpec((None, tm, tn), lambda i, j, k: (j // opc, i, j % opc))
        out_shape = jax.ShapeDtypeStruct((N_CHIPS, M, N // N_CHIPS), out_dtype)
    else:
        out_spec = pl.BlockSpec((tm, tn), lambda i, j, k: (i, j))
        out_shape = jax.ShapeDtypeStruct((M, N), out_dtype)
    in_specs, operands = [a_spec, b_spec], [a, b]
    if bias is not None:
        in_specs.append(pl.BlockSpec((1, tn), lambda i, j, k: (0, j)))
        operands.append(bias.reshape(1, N))
    if residual is not None:
        in_specs.append(pl.BlockSpec((tm, tn), lambda i, j, k: (i, j)))
        operands.append(residual)
    has_bias, has_res = bias is not None, residual is not None

    def kern(*refs):
        a_ref, b_ref = refs[0], refs[1]
        o_ref = refs[2 + has_bias + has_res]

        def finish(r):
            pos = 2
            if has_bias:
                r = r + refs[pos][...].astype(F32)
                pos += 1
            if has_res:
                r = r + refs[pos][...].astype(F32)
            o_ref[...] = r.astype(o_ref.dtype)

        part = lax.dot_general(a_ref[...].astype(MXU_DTYPE), b_ref[...].astype(MXU_DTYPE), dims,
                               preferred_element_type=F32)
        if nk == 1:
            finish(part)
            return
        acc_ref = refs[-1]
        k = pl.program_id(2)

        @pl.when(k == 0)
        def _():
            acc_ref[...] = part

        @pl.when(jnp.logical_and(k > 0, k < nk - 1))
        def _():
            acc_ref[...] += part

        @pl.when(k == nk - 1)
        def _():
            finish(acc_ref[...] + part)

    return pl.pallas_call(
        kern, name=name,
        grid=(M // tm, N // tn, nk),
        in_specs=in_specs,
        out_specs=out_spec,
        out_shape=out_shape,
        scratch_shapes=[pltpu.VMEM((tm, tn), F32)] if nk > 1 else [],
        compiler_params=_cparams(("parallel", "parallel", "arbitrary")),
    )(*operands)


def _tile_call(body, *, rows, prevs=(), nexts=(), fulls=(), row_outs=(), acc_outs=(), tm, T, name):
    M = rows[0].shape[0]
    n_tiles, tps, hb = M // tm, T // tm, tm // HALO
    n_halo_blocks = M // HALO
    nr, npv, nnx, nf, nro, nac = len(rows), len(prevs), len(nexts), len(fulls), len(row_outs), len(acc_outs)

    def kern(*refs):
        i = pl.program_id(0)
        row_refs = refs[:nr]
        prev_refs = refs[nr:nr + npv]
        next_refs = refs[nr + npv:nr + npv + nnx]
        full_refs = refs[nr + npv + nnx:nr + npv + nnx + nf]
        out_refs = refs[nr + npv + nnx + nf:nr + npv + nnx + nf + nro]
        acc_refs = refs[nr + npv + nnx + nf + nro:]
        seq_first = (i % tps) == 0
        seq_last = (i % tps) == (tps - 1)
        outs, accs = body(row_refs, prev_refs, next_refs, full_refs, seq_first, seq_last)
        for r, o in zip(out_refs, outs, strict=True):
            r[...] = o.astype(r.dtype)
        if nac:
            @pl.when(i == 0)
            def _():
                for r in acc_refs:
                    r[...] = jnp.zeros_like(r)
            for r, a in zip(acc_refs, accs, strict=True):
                r[...] += a.astype(F32)

    in_specs = [pl.BlockSpec((tm, a.shape[1]), lambda i: (i, 0)) for a in rows]
    in_specs += [pl.BlockSpec((HALO, rows[k].shape[1]), lambda i: (jnp.maximum(i * hb - 1, 0), 0)) for k in prevs]
    in_specs += [pl.BlockSpec((HALO, rows[k].shape[1]), lambda i: (jnp.minimum((i + 1) * hb, n_halo_blocks - 1), 0))
                 for k in nexts]
    in_specs += [pl.BlockSpec(f.shape, lambda i: (0, 0)) for f in fulls]
    out_specs = [pl.BlockSpec((tm, w), lambda i: (i, 0)) for (w, _) in row_outs]
    out_specs += [pl.BlockSpec(s, lambda i: (0, 0)) for s in acc_outs]
    out_shape = [jax.ShapeDtypeStruct((M, w), dt) for (w, dt) in row_outs]
    out_shape += [jax.ShapeDtypeStruct(s, F32) for s in acc_outs]
    operands = list(rows) + [rows[k] for k in prevs] + [rows[k] for k in nexts] + list(fulls)
    res = pl.pallas_call(
        kern, name=name, grid=(n_tiles,), in_specs=in_specs, out_specs=out_specs, out_shape=out_shape,
        compiler_params=_cparams(("arbitrary",)),
    )(*operands)
    return res[:nro], res[nro:]


def _f(ref):
    return ref[...].astype(F32)


def _sigmoid(x):
    return 1.0 / (1.0 + jnp.exp(-x))


def _softplus(x):
    return jnp.maximum(x, 0.0) + jnp.log(1.0 + jnp.exp(-jnp.abs(x)))


def _neg_expm1(x):
    series = -x * (1.0 + x * (0.5 + x * (1.0 / 6.0) * (1.0 + 0.25 * x)))
    return jnp.where(x > -0.01, series, 1.0 - jnp.exp(x))


def _gelu(x):
    return 0.5 * x * (1.0 + jnp.tanh(0.7978845608028654 * (x + 0.044715 * x * x * x)))


def _rms(x, g):
    return x * lax.rsqrt(jnp.mean(x * x, axis=-1, keepdims=True) + RMS_EPS) * g


@jax.custom_vjp
def _bdot(x, w):
    return jnp.dot(x.astype(MXU_DTYPE), w.astype(MXU_DTYPE), preferred_element_type=F32)


def _bdot_fwd(x, w):
    return _bdot(x, w), (x, w)


def _bdot_bwd(res, ct):
    x, w = res
    ctb = ct.astype(MXU_DTYPE)
    dx = lax.dot_general(ctb, w.astype(MXU_DTYPE), (((1,), (1,)), ((), ())), preferred_element_type=F32)
    dw = lax.dot_general(x.astype(MXU_DTYPE), ctb, (((0,), (0,)), ((), ())), preferred_element_type=F32)
    return dx.astype(x.dtype), dw.astype(w.dtype)


_bdot.defvjp(_bdot_fwd, _bdot_bwd)


@jax.custom_vjp
def _head_sum(x, e, et):
    s = jnp.dot(_split_lhs(x), e, preferred_element_type=F32)
    return jnp.dot(_split_lhs(s), et, preferred_element_type=F32)


def _head_sum_fwd(x, e, et):
    return _head_sum(x, e, et), (e, et)


def _head_sum_bwd(res, ct):
    e, et = res
    return _head_sum(ct, e, et), jnp.zeros_like(e), jnp.zeros_like(et)


_head_sum.defvjp(_head_sum_fwd, _head_sum_bwd)


def _shift_down(main, prev, s, seq_first):
    prev = jnp.where(seq_first, 0.0, prev)
    ext = jnp.concatenate([prev, main], axis=0)
    return pltpu.roll(ext, s, 0)[HALO:]


def _shift_up(main, nxt, s, seq_last):
    nxt = jnp.where(seq_last, 0.0, nxt)
    ext = jnp.concatenate([main, nxt], axis=0)
    n = ext.shape[0]
    return pltpu.roll(ext, n - s, 0)[:n - HALO]


def _colsum(x):
    return jnp.sum(x, axis=0, keepdims=True)


def _pad8(x):
    k = x.shape[0]
    return jnp.concatenate([x, jnp.zeros((8 - k, x.shape[1]), x.dtype)], axis=0) if k < 8 else x


def _rms_fwd(x, g, T, name):
    D = x.shape[1]

    def body(rows, prevs, nexts, fulls, sf, sl):
        return [_rms(_f(rows[0]), _f(fulls[0]))], []

    (h,), _ = _tile_call(body, rows=[x], fulls=[g.reshape(1, D)], row_outs=[(D, BF16)], tm=min(512, T), T=T, name=name)
    return h


def _rms_bwd(x, g, dh, dres, T, name):
    D = x.shape[1]

    def body(rows, prevs, nexts, fulls, sf, sl):
        _, vjp = jax.vjp(_rms, _f(rows[0]), _f(fulls[0]))
        dx, dg = vjp(_f(rows[1]))
        return [dx + _f(rows[2])], [dg]

    (dx,), (dg,) = _tile_call(body, rows=[x, dh, dres], fulls=[g.reshape(1, D)], row_outs=[(D, F32)],
                              acc_outs=[(1, D)], tm=min(512, T), T=T, name=name)
    return dx, dg


def _ffn_conv(u1, prev, cw, cb, sf):
    k = cw.shape[0]
    out = cb + u1 * cw[k - 1:k]
    for j in range(k - 1):
        out = out + _shift_down(u1, prev, k - 1 - j, sf) * cw[j:j + 1]
    return out


def _ffn_fwd(x, p, T, tag):
    M, D = x.shape
    F = p["w_down"].shape[0]
    hf = _rms_fwd(x, p["norm"], T, f"ffn{tag}_norm")
    uf = _matmul(hf, p["w_up"], out_dtype=BF16, name=f"ffn{tag}_up")

    def body(rows, prevs, nexts, fulls, sf, sl):
        u = rows[0]
        gate = _ffn_conv(u[:, :F].astype(F32), prevs[0][:, :F].astype(F32), _f(fulls[0]), _f(fulls[1]), sf)
        return [_gelu(gate) * u[:, F:].astype(F32)], []

    (hid,), _ = _tile_call(body, rows=[uf], prevs=[0], fulls=[p["conv_w"], p["conv_b"].reshape(1, F)],
                           row_outs=[(F, BF16)], tm=min(256, T), T=T, name=f"ffn{tag}_act")
    y = _matmul(hid, p["w_down"], residual=x, name=f"ffn{tag}_down")
    return y, (x, hf, uf, hid)


def _ffn_bwd(dy, saved, p, T, tag):
    x, hf, uf, hid = saved
    M, D = x.shape
    F = p["w_down"].shape[0]
    K = p["conv_w"].shape[0]
    d_hid = _matmul(dy, p["w_down"], mode="nt", out_dtype=BF16, name=f"ffn{tag}_down_dx")
    d_w_down = _matmul(hid, dy, mode="tn", out_dtype=BF16, name=f"ffn{tag}_down_dw")

    def body(rows, prevs, nexts, fulls, sf, sl):
        u, dh = rows
        cw, cb = _f(fulls[0]), _f(fulls[1])
        tm = u.shape[0]
        u1c, u1p, u1n = u[:, :F].astype(F32), prevs[0][:, :F].astype(F32), nexts[0][:, :F].astype(F32)
        u1 = jnp.concatenate([u1c, u1n], axis=0)
        u2 = jnp.concatenate([u[:, F:].astype(F32), nexts[0][:, F:].astype(F32)], axis=0)
        dhid = jnp.concatenate([_f(dh), _f(nexts[1])], axis=0)
        gate = _ffn_conv(u1, u1p, cw, cb, sf)
        (act, dact) = jax.jvp(_gelu, (gate,), (jnp.ones_like(gate),))
        d_gate = dhid * u2 * dact
        d_u2 = (dhid * act)[:tm]
        rowid = lax.broadcasted_iota(jnp.int32, d_gate.shape, 0)
        d_gate = jnp.where(jnp.logical_and(sl, rowid >= tm), 0.0, d_gate)
        dgc, dgn = d_gate[:tm], d_gate[tm:]
        d_u1 = dgc * cw[K - 1:K]
        dws = []
        for j in range(K - 1):
            s = K - 1 - j
            d_u1 = d_u1 + _shift_up(dgc, dgn, s, False) * cw[j:j + 1]
            dws.append(_colsum(dgc * _shift_down(u1c, u1p, s, sf)))
        dws.append(_colsum(dgc * u1c))
        d_cw = _pad8(jnp.concatenate(dws, axis=0))
        return [jnp.concatenate([d_u1, d_u2], axis=1)], [d_cw, _colsum(dgc)]

    (d_uf,), (d_cw, d_cb) = _tile_call(
        body, rows=[uf, d_hid], prevs=[0], nexts=[0, 1], fulls=[p["conv_w"], p["conv_b"].reshape(1, F)],
        row_outs=[(2 * F, BF16)], acc_outs=[(8, F), (1, F)], tm=min(256, T), T=T, name=f"ffn{tag}_act_bwd")
    d_hf = _matmul(d_uf, p["w_up"], mode="nt", name=f"ffn{tag}_up_dx")
    d_w_up = _matmul(hf, d_uf, mode="tn", out_dtype=BF16, name=f"ffn{tag}_up_dw", out_cols_by_chip=True)
    dx, d_norm = _rms_bwd(x, p["norm"], d_hf, dy, T, f"ffn{tag}_norm_bwd")
    grads = {"norm": d_norm.reshape(D), "w_up": d_w_up, "conv_w": d_cw[:K], "conv_b": d_cb.reshape(F), "w_down": d_w_down}
    return dx, grads


def _lru_conv(u2, prev, cw, cb, sf):
    return _ffn_conv(u2, prev, cw, cb, sf)


def _lru_pre(xr, wg, gb):
    D, GB = xr.shape[1], wg.shape[1] // 2
    parts = [jnp.dot(xr[:, r * GB:(r + 1) * GB].astype(MXU_DTYPE), wg[r * GB:(r + 1) * GB], preferred_element_type=F32)
             for r in range(D // GB)]
    return jnp.concatenate([q[:, :GB] for q in parts] + [q[:, GB:] for q in parts], axis=1) + gb


def _lru_pre_t(xr, dpre, wg, with_dw):
    D, GB = xr.shape[1], wg.shape[1] // 2
    dx, dw = [], []
    for r in range(D // GB):
        dp = jnp.concatenate([dpre[:, r * GB:(r + 1) * GB], dpre[:, D + r * GB:D + (r + 1) * GB]], axis=1).astype(MXU_DTYPE)
        dx.append(lax.dot_general(dp, wg[r * GB:(r + 1) * GB], (((1,), (1,)), ((), ())), preferred_element_type=F32))
        if with_dw:
            dw.append(lax.dot_general(xr[:, r * GB:(r + 1) * GB].astype(MXU_DTYPE), dp, (((0,), (0,)), ((), ())),
                                      preferred_element_type=F32))
    return jnp.concatenate(dx, axis=1), (jnp.concatenate(dw, axis=0) if with_dw else None)


def _lru_gates(xr, pre, lam):
    D = xr.shape[1]
    r_gate, i_gate = _sigmoid(pre[:, :D]), _sigmoid(pre[:, D:])
    log_a = -LRU_C * r_gate * _softplus(-lam)
    a = jnp.exp(log_a)
    mult = jnp.sqrt(_neg_expm1(2.0 * log_a))
    return a, mult * (i_gate * xr)


def _lru_scan(a, b, B, T):
    M, D = a.shape
    cw = _pick(D, 256)
    ng = T // 8

    def kern(a_ref, b_ref, o_ref):
        row = lax.broadcasted_iota(jnp.int32, (8, cw), 0)

        def step(g, carry):
            sl = pl.ds(pl.multiple_of(g * 8, 8), 8)
            a8, b8 = a_ref[sl, :], b_ref[sl, :]
            for s in (1, 2, 4):
                a_sh = jnp.where(row >= s, pltpu.roll(a8, s, 0), 1.0)
                b_sh = jnp.where(row >= s, pltpu.roll(b8, s, 0), 0.0)
                b8 = a8 * b_sh + b8
                a8 = a8 * a_sh
            h8 = a8 * carry + b8
            o_ref[sl, :] = h8
            return jnp.broadcast_to(h8[7:8, :], (8, cw))

        lax.fori_loop(0, ng, step, jnp.zeros((8, cw), F32))

    spec = pl.BlockSpec((T, cw), lambda b, c: (b, c))
    return pl.pallas_call(
        kern, name="lru_scan", grid=(B, D // cw), in_specs=[spec, spec], out_specs=spec,
        out_shape=jax.ShapeDtypeStruct((M, D), F32), compiler_params=_cparams(("parallel", "parallel")),
    )(a, b)


def _lru_scan_bwd(a, hs, dhs, B, T):
    M, D = a.shape
    cw = _pick(D, 256)
    ng = T // 8

    def kern(a_ref, h_ref, d_ref, g_ref, da_ref):
        row = lax.broadcasted_iota(jnp.int32, (8, cw), 0)

        def step(k, carry):
            g_next, a_next = carry
            g = ng - 1 - k
            sl = pl.ds(pl.multiple_of(g * 8, 8), 8)
            a8, d8, h8 = a_ref[sl, :], d_ref[sl, :], h_ref[sl, :]
            c8 = jnp.where(row < 7, pltpu.roll(a8, 7, 0), a_next)
            for s in (1, 2, 4):
                d_sh = jnp.where(row < 8 - s, pltpu.roll(d8, 8 - s, 0), 0.0)
                c_sh = jnp.where(row < 8 - s, pltpu.roll(c8, 8 - s, 0), 1.0)
                d8 = d8 + c8 * d_sh
                c8 = c8 * c_sh
            G8 = d8 + c8 * g_next
            gp = jnp.maximum(g - 1, 0)
            hp8 = h_ref[pl.ds(pl.multiple_of(gp * 8, 8), 8), :]
            hp_last = jnp.where(g > 0, jnp.broadcast_to(hp8[7:8, :], (8, cw)), 0.0)
            hprev = jnp.where(row >= 1, pltpu.roll(h8, 1, 0), hp_last)
            g_ref[sl, :] = G8
            da_ref[sl, :] = G8 * hprev
            return jnp.broadcast_to(G8[0:1, :], (8, cw)), jnp.broadcast_to(a8[0:1, :], (8, cw))

        z = jnp.zeros((8, cw), F32)
        lax.fori_loop(0, ng, step, (z, z))

    spec = pl.BlockSpec((T, cw), lambda b, c: (b, c))
    sh = jax.ShapeDtypeStruct((M, D), F32)
    return pl.pallas_call(
        kern, name="lru_scan_bwd", grid=(B, D // cw), in_specs=[spec, spec, spec], out_specs=[spec, spec],
        out_shape=[sh, sh], compiler_params=_cparams(("parallel", "parallel")),
    )(a, hs, dhs)


def _lru_fwd(x, p, B, T):
    M, D = x.shape
    h0 = _rms_fwd(x, p["norm"], T, "lru_norm")
    u0 = _matmul(h0, p["w_in"], bias=p["b_in"], name="lru_in")
    fulls = [p["conv_w"], p["conv_b"].reshape(1, D), p["wbd"], p["gate_b"].reshape(1, 2 * D), p["lam"].reshape(1, D)]

    def body(rows, prevs, nexts, fulls, sf, sl):
        xr = _lru_conv(rows[0][:, D:], prevs[0][:, D:], _f(fulls[0]), _f(fulls[1]), sf)
        a, bt = _lru_gates(xr, _lru_pre(xr, fulls[2][...], _f(fulls[3])), _f(fulls[4]))
        return [a, bt], []

    (a, bt), _ = _tile_call(body, rows=[u0], prevs=[0], fulls=fulls, row_outs=[(D, F32), (D, F32)],
                            tm=min(256, T), T=T, name="lru_gates")
    hs = _lru_scan(a, bt, B, T)

    def body2(rows, prevs, nexts, fulls, sf, sl):
        return [rows[0][...] * _gelu(rows[1][:, :D])], []

    (out,), _ = _tile_call(body2, rows=[hs, u0], row_outs=[(D, BF16)], tm=min(512, T), T=T, name="lru_mix")
    y = _matmul(out, p["w_out"], bias=p["b_out"], residual=x, name="lru_out")
    return y, (x, h0, u0, a, hs, out)


def _lru_bwd(dy, saved, p, B, T):
    x, h0, u0, a, hs, out = saved
    M, D = x.shape
    K = p["conv_w"].shape[0]
    d_out = _matmul(dy, p["w_out"], mode="nt", name="lru_out_dx")
    d_w_out = _matmul(out, dy, mode="tn", out_dtype=BF16, name="lru_out_dw")

    def body(rows, prevs, nexts, fulls, sf, sl):
        do, h, u, dyv = rows[0][...], rows[1][...], rows[2][:, :D], rows[3][...]
        act, dact = jax.jvp(_gelu, (u,), (jnp.ones_like(u),))
        return [do * act, do * h * dact], [_colsum(dyv)]

    (d_hs, d_u1), (d_b_out,) = _tile_call(body, rows=[d_out, hs, u0, dy], row_outs=[(D, F32), (D, F32)],
                                          acc_outs=[(1, D)], tm=min(512, T), T=T, name="lru_mix_bwd")
    g_b, d_a = _lru_scan_bwd(a, hs, d_hs, B, T)
    fulls = [p["conv_w"], p["conv_b"].reshape(1, D), p["wbd"], p["gate_b"].reshape(1, 2 * D), p["lam"].reshape(1, D)]

    def body3(rows, prevs, nexts, fulls, sf, sl):
        u, gb_c, da_c, du1 = rows
        cw, cb, wbd, gbias, lam = _f(fulls[0]), _f(fulls[1]), fulls[2][...], _f(fulls[3]), _f(fulls[4])
        tm = u.shape[0]
        u2c, u2p, u2n = u[:, D:], prevs[0][:, D:], nexts[0][:, D:]
        xr_c = _lru_conv(u2c, u2p, cw, cb, sf)
        xr_n = _lru_conv(u2n, u2c[tm - HALO:], cw, cb, False)
        _, vjp_c = jax.vjp(_lru_gates, xr_c, _lru_pre(xr_c, wbd, gbias), lam)
        dxr_c, dpre_c, d_lam = vjp_c((da_c[...], gb_c[...]))
        dxr_add, d_wbd = _lru_pre_t(xr_c, dpre_c, wbd, True)
        dxr_c = dxr_c + dxr_add
        d_gbias = _colsum(dpre_c)
        _, vjp_n = jax.vjp(lambda t, q: _lru_gates(t, q, lam), xr_n, _lru_pre(xr_n, wbd, gbias))
        dxr_n, dpre_n = vjp_n((nexts[2][...], nexts[1][...]))
        dxr_n = dxr_n + _lru_pre_t(xr_n, dpre_n, wbd, False)[0]
        d_u2 = dxr_c * cw[K - 1:K]
        dws = []
        for j in range(K - 1):
            s = K - 1 - j
            d_u2 = d_u2 + _shift_up(dxr_c, dxr_n, s, sl) * cw[j:j + 1]
            dws.append(_colsum(dxr_c * _shift_down(u2c, u2p, s, sf)))
        dws.append(_colsum(dxr_c * u2c))
        d_u = jnp.concatenate([du1[...], d_u2], axis=1)
        return [d_u], [_pad8(jnp.concatenate(dws, axis=0)), _colsum(dxr_c), d_wbd, d_gbias, d_lam,
                       _colsum(d_u)]

    (d_u0,), (d_cw, d_cb, d_wbd, d_gb, d_lam, d_b_in) = _tile_call(
        body3, rows=[u0, g_b, d_a, d_u1], prevs=[0], nexts=[0, 1, 2], fulls=fulls, row_outs=[(2 * D, BF16)],
        acc_outs=[(8, D), (1, D), p["wbd"].shape, (1, 2 * D), (1, D), (1, 2 * D)], tm=min(256, T), T=T,
        name="lru_gates_bwd")
    d_h0 = _matmul(d_u0, p["w_in"], mode="nt", name="lru_in_dx")
    d_w_in = _matmul(h0, d_u0, mode="tn", out_dtype=BF16, name="lru_in_dw", out_cols_by_chip=True)
    dx, d_norm = _rms_bwd(x, p["norm"], d_h0, dy, T, "lru_norm_bwd")
    grads = {"norm": d_norm.reshape(D), "w_in": d_w_in, "b_in": d_b_in.reshape(2 * D), "conv_w": d_cw[:K],
             "conv_b": d_cb.reshape(D), "wbd": d_wbd, "gate_b": d_gb.reshape(2 * D), "lam": d_lam.reshape(D),
             "w_out": d_w_out, "b_out": d_b_out.reshape(D)}
    return dx, grads


def _rwkv_mix(xc, xp, norm, mix, sf):
    h = _rms(xc, norm)
    hp = _rms(xp, norm)
    xx = _shift_down(h, hp, 1, sf) - h
    return h, xx


def _rwkv_pre(k, xw, xa, xg, w0, w1, w2, a0, a1, a2, g1, g2, k_k, k_a, e, et):
    wl = -_softplus(-(w0 + _bdot(jnp.tanh(_bdot(xw, w1)), w2))) - 0.5
    decay = jnp.exp(-jnp.exp(wl))
    a = _sigmoid(a0 + _bdot(_bdot(xa, a1), a2))
    g = _bdot(_sigmoid(_bdot(xg, g1)), g2)
    kk = k * k_k
    nrm = jnp.sqrt(_head_sum(kk * kk, e, et))
    kk = kk / jnp.maximum(nrm, 1e-12)
    k2 = k * (1.0 + (a - 1.0) * k_a)
    return decay, k2, -kk, kk * a, g


def _rwkv_post(y, r, k2, v, g, ln_w, ln_b, r_k, e, et):
    inv = 1.0 / HEAD
    mu = _head_sum(y, e, et) * inv
    yc = y - mu
    var = _head_sum(yc * yc, e, et) * inv
    yn = yc * lax.rsqrt(var + GN_EPS) * ln_w + ln_b
    bonus = _head_sum(r * k2 * r_k, e, et) * v
    return (yn + bonus) * g


def _seg_lane_sums(x, lo_mask):
    s0 = jnp.sum(jnp.where(lo_mask, x, 0.0), axis=1, keepdims=True)
    s1 = jnp.sum(jnp.where(lo_mask, 0.0, x), axis=1, keepdims=True)
    return s0, s1


def _seg_lane_sum(x, lo_mask):
    s0, s1 = _seg_lane_sums(x, lo_mask)
    return jnp.where(lo_mask, s0, s1)


def _pair_consts():
    lane = lax.broadcasted_iota(jnp.int32, (HEAD, 128), 1)
    sub = lax.broadcasted_iota(jnp.int32, (HEAD, 128), 0)
    return lane < HEAD, (jnp.bitwise_and(lane, HEAD - 1) == sub).astype(F32)


def _pair_ones():
    head = jnp.arange(128) // HEAD
    return (head[:, None] == head[None, :]).astype(MXU_DTYPE)


def _split_lhs(x):
    hi = x.astype(MXU_DTYPE)
    return jnp.concatenate([hi, (x - hi.astype(F32)).astype(MXU_DTYPE)], axis=1)


def _spread_lhs(diag, row):
    return (diag * row).astype(MXU_DTYPE)


def _rwkv_scan(r, w, k, v, a, b, B, T):
    M, D = r.shape
    HP, PG, TC, NC, chains = _scan_plan(B, T, D, pairs=2, chunk=64)
    NS = len(chains) * 8
    NG = TC // 8

    def kern(r_ref, w_ref, k_ref, v_ref, a_ref, b_ref, ones_ref, y_ref, st_ref, S_ref, lv_ref, rv_ref, ly_ref, ry_ref):
        c = pl.program_id(1)

        @pl.when(c == 0)
        def _():
            S_ref[...] = jnp.zeros_like(S_ref)

        lo, diag = _pair_consts()
        row8 = lax.broadcasted_iota(jnp.int32, (8, 128), 0)

        def blk(idx):
            return pl.ds(idx * HEAD, HEAD)

        def rows_of(gi):
            return pl.ds(pl.multiple_of(gi * 8, 8), 8)

        def spread(gi, slot):
            for ci, (bi, p) in enumerate(chains):
                v8 = v_ref[bi, rows_of(gi), p * 128:(p + 1) * 128]
                for j in range(8):
                    lv_ref[slot, blk(ci * 8 + j), :] = _spread_lhs(diag, v8[j:j + 1, :])
            rv_ref[slot] = jnp.dot(lv_ref[slot], ones_ref[...], preferred_element_type=F32)

        def recur(gi, slot):
            sl = rows_of(gi)
            tiles = [[ref[bi, sl, p * 128:(p + 1) * 128] for ref in (r_ref, w_ref, k_ref, a_ref, b_ref)]
                     for bi, p in chains]
            S = [S_ref[ci] for ci in range(len(chains))]
            for j in range(8):
                for ci, (bi, p) in enumerate(chains):
                    r8, w8, k8, a8, b8 = tiles[ci]
                    idx = ci * 8 + j
                    st_ref[p, bi, gi * 8 + j] = S[ci]
                    sa = _seg_lane_sum(S[ci] * a8[j:j + 1, :], lo)
                    S[ci] = S[ci] * w8[j:j + 1, :] + sa * b8[j:j + 1, :] + rv_ref[slot, blk(idx), :] * k8[j:j + 1, :]
                    ly_ref[slot, blk(idx), :] = (S[ci] * r8[j:j + 1, :]).astype(MXU_DTYPE)
            for ci in range(len(chains)):
                S_ref[ci] = S[ci]

        def emit(gi, slot):
            ry_ref[slot] = jnp.dot(ly_ref[slot], ones_ref[...], preferred_element_type=F32)
            for ci, (bi, p) in enumerate(chains):
                y8 = jnp.zeros((8, 128), F32)
                for j in range(8):
                    y8 = jnp.where(row8 == j, _colsum(diag * ry_ref[slot, blk(ci * 8 + j), :]), y8)
                y_ref[bi, rows_of(gi), p * 128:(p + 1) * 128] = y8

        spread(0, 0)
        ly_ref[1] = jnp.zeros_like(ly_ref[1])

        def two_groups(m, _):
            g0, g1 = 2 * m, 2 * m + 1
            spread(g1, 1)
            recur(g0, 0)
            emit(jnp.maximum(g0 - 1, 0), 1)
            spread(jnp.minimum(g1 + 1, NG - 1), 0)
            recur(g1, 1)
            emit(g0, 0)
            return 0

        lax.fori_loop(0, NG // 2, two_groups, 0)
        emit(NG - 1, 1)

    spec = pl.BlockSpec((B, TC, 128 * PG), lambda hp, c: (0, c, hp))
    st_spec = pl.BlockSpec((PG, B, TC, HEAD, 128), lambda hp, c: (hp, 0, c, 0, 0))
    y, st = pl.pallas_call(
        kern, name="rwkv_scan", grid=(HP // PG, NC),
        in_specs=[spec] * 6 + [pl.BlockSpec((128, 128), lambda hp, c: (0, 0))], out_specs=[spec, st_spec],
        out_shape=[jax.ShapeDtypeStruct((B, T, D), F32), jax.ShapeDtypeStruct((HP, B, T, HEAD, 128), F32)],
        scratch_shapes=[pltpu.VMEM((len(chains), HEAD, 128), F32),
                        pltpu.VMEM((2, NS * HEAD, 128), MXU_DTYPE), pltpu.VMEM((2, NS * HEAD, 128), F32),
                        pltpu.VMEM((2, NS * HEAD, 128), MXU_DTYPE), pltpu.VMEM((2, NS * HEAD, 128), F32)],
        compiler_params=_cparams(("parallel", "arbitrary")),
    )(*[x.reshape(B, T, D) for x in (r, w, k, v, a, b)], _pair_ones())
    return y.reshape(M, D), st


def _scan_plan(B, T, D, pairs, chunk):
    HP = D // 128
    PG = pairs if HP % pairs == 0 else 1
    TC = min(chunk, T)
    assert TC % 16 == 0 and T % TC == 0
    return HP, PG, TC, T // TC, [(bi, p) for bi in range(B) for p in range(PG)]


def _rwkv_scan_bwd(r, w, k, v, a, b, st, dy, B, T):
    M, D = r.shape
    HP, PG, TC, NC, chains = _scan_plan(B, T, D, pairs=1, chunk=128)
    NS = len(chains) * 8

    NG = TC // 8

    def kern(r_ref, w_ref, k_ref, v_ref, a_ref, b_ref, st_ref, dy_ref, ones_ref,
             dr_ref, dw_ref, dk_ref, dv_ref, da_ref, db_ref, dS_ref, lp_ref, rp_ref, lq_ref, rq_ref):
        c = pl.program_id(1)

        @pl.when(c == 0)
        def _():
            dS_ref[...] = jnp.zeros_like(dS_ref)

        lo, diag = _pair_consts()
        row8 = lax.broadcasted_iota(jnp.int32, (8, 128), 0)

        def blk(idx):
            return pl.ds(idx * HEAD, HEAD)

        def rows_of(gi):
            return pl.ds(pl.multiple_of(gi * 8, 8), 8)

        def spread(gi, slot):
            sl = rows_of(gi)
            for ci, (bi, p) in enumerate(chains):
                lanes = slice(p * 128, (p + 1) * 128)
                v8, dy8, a8 = v_ref[bi, sl, lanes], dy_ref[bi, sl, lanes], a_ref[bi, sl, lanes]
                for j in range(8):
                    idx = ci * 8 + j
                    lp_ref[slot, blk(idx), :] = _spread_lhs(diag, v8[j:j + 1, :])
                    lp_ref[slot, blk(NS + idx), :] = _spread_lhs(diag, dy8[j:j + 1, :])
                    lp_ref[slot, blk(2 * NS + idx), :] = (st_ref[p, bi, gi * 8 + j] * a8[j:j + 1, :]).astype(MXU_DTYPE)
            rp_ref[slot] = jnp.dot(lp_ref[slot], ones_ref[...], preferred_element_type=F32)

        def recur(gi, slot):
            sl = rows_of(gi)
            tiles = [[ref[bi, sl, p * 128:(p + 1) * 128] for ref in (r_ref, w_ref, k_ref, a_ref, b_ref)]
                     for bi, p in chains]
            dS = [dS_ref[ci] for ci in range(len(chains))]
            acc = [[jnp.zeros((8, 128), F32) for _ in range(5)] for _ in chains]
            St = [None] * len(chains)
            for j in range(7, -1, -1):
                for ci, (bi, p) in enumerate(chains):
                    r8, w8, k8, a8, b8 = tiles[ci]
                    rj, wj, kj, aj, bj = r8[j:j + 1, :], w8[j:j + 1, :], k8[j:j + 1, :], a8[j:j + 1, :], b8[j:j + 1, :]
                    idx = ci * 8 + j
                    Sp = st_ref[p, bi, gi * 8 + j]
                    vb, dyb, sa = rp_ref[slot, blk(idx), :], rp_ref[slot, blk(NS + idx), :], rp_ref[slot, blk(2 * NS + idx), :]
                    if j == 7:
                        St[ci] = Sp * wj + sa * bj + vb * kj
                    d = dS[ci] + dyb * rj
                    dsa = _seg_lane_sum(d * bj, lo)
                    lq_ref[slot, blk(idx), :] = (d * kj).astype(MXU_DTYPE)
                    rows = (_colsum(St[ci] * dyb), _colsum(d * Sp), _colsum(d * vb), _colsum(Sp * dsa), _colsum(d * sa))
                    acc[ci] = [jnp.where(row8 == j, rw, a8_) for rw, a8_ in zip(rows, acc[ci], strict=True)]
                    dS[ci] = d * wj + dsa * aj
                    St[ci] = Sp
            for ci, (bi, p) in enumerate(chains):
                dS_ref[ci] = dS[ci]
                for ref, a8_ in zip((dr_ref, dw_ref, dk_ref, da_ref, db_ref), acc[ci], strict=True):
                    ref[bi, sl, p * 128:(p + 1) * 128] = a8_

        def emit(gi, slot):
            rq_ref[slot] = jnp.dot(lq_ref[slot], ones_ref[...], preferred_element_type=F32)
            for ci, (bi, p) in enumerate(chains):
                dv8 = jnp.zeros((8, 128), F32)
                for j in range(8):
                    dv8 = jnp.where(row8 == j, _colsum(diag * rq_ref[slot, blk(ci * 8 + j), :]), dv8)
                dv_ref[bi, rows_of(gi), p * 128:(p + 1) * 128] = dv8

        spread(NG - 1, 0)
        lq_ref[1] = jnp.zeros_like(lq_ref[1])

        def two_groups(m, _):
            g0, g1 = NG - 1 - 2 * m, NG - 2 - 2 * m
            spread(g1, 1)
            recur(g0, 0)
            emit(jnp.minimum(g0 + 1, NG - 1), 1)
            spread(jnp.maximum(g1 - 1, 0), 0)
            recur(g1, 1)
            emit(g0, 0)
            return 0

        lax.fori_loop(0, NG // 2, two_groups, 0)
        emit(0, 1)

    spec = pl.BlockSpec((B, TC, 128 * PG), lambda hp, c: (0, NC - 1 - c, hp))
    st_spec = pl.BlockSpec((PG, B, TC, HEAD, 128), lambda hp, c: (hp, 0, NC - 1 - c, 0, 0))
    sh = jax.ShapeDtypeStruct((B, T, D), F32)
    outs = pl.pallas_call(
        kern, name="rwkv_scan_bwd", grid=(HP // PG, NC),
        in_specs=[spec] * 6 + [st_spec, spec, pl.BlockSpec((128, 128), lambda hp, c: (0, 0))], out_specs=[spec] * 6,
        out_shape=[sh] * 6,
        scratch_shapes=[pltpu.VMEM((len(chains), HEAD, 128), F32),
                        pltpu.VMEM((2, 3 * NS * HEAD, 128), MXU_DTYPE), pltpu.VMEM((2, 3 * NS * HEAD, 128), F32),
                        pltpu.VMEM((2, NS * HEAD, 128), MXU_DTYPE), pltpu.VMEM((2, NS * HEAD, 128), F32)],
        compiler_params=_cparams(("parallel", "arbitrary")),
    )(*[x.reshape(B, T, D) for x in (r, w, k, v, a, b)], st, dy.reshape(B, T, D), _pair_ones())
    return [o.reshape(M, D) for o in outs]


def _head_mats(D):
    ch = jnp.arange(D) // HEAD
    e = (ch[:, None] == jnp.arange(128)[None, :]).astype(MXU_DTYPE)
    return jnp.concatenate([e, e], axis=0), jnp.concatenate([e.T, e.T], axis=0)


def _rwkv_fwd(x, p, B, T):
    M, D = x.shape
    e, et = _head_mats(D)
    norm = p["norm"].reshape(1, D)

    def body(rows, prevs, nexts, fulls, sf, sl):
        h, xx = _rwkv_mix(rows[0][...], prevs[0][...], _f(fulls[0]), None, sf)
        mix = _f(fulls[1])
        return [h + xx * mix[i:i + 1] for i in range(6)], []

    xs, _ = _tile_call(body, rows=[x], prevs=[0], fulls=[norm, _pad8(p["mix"])], row_outs=[(D, BF16)] * 6,
                       tm=min(256, T), T=T, name="rwkv_mix")
    r = _matmul(xs[0], p["w_r"], name="rwkv_r")
    k = _matmul(xs[1], p["w_k"], name="rwkv_k")
    v = _matmul(xs[2], p["w_v"], name="rwkv_v")
    pre_fulls = [p["w0"].reshape(1, D), p["w1"], p["w2"], p["a0"].reshape(1, D), p["a1"], p["a2"], p["g1"], p["g2"],
                 p["k_k"].reshape(1, D), p["k_a"].reshape(1, D), e, et]

    def body2(rows, prevs, nexts, fulls, sf, sl):
        outs = _rwkv_pre(rows[0][...], _f(rows[1]), _f(rows[2]), _f(rows[3]), *[f[...] for f in fulls])
        return list(outs), []

    (decay, k2, kkn, bb, g), _ = _tile_call(body2, rows=[k, xs[3], xs[4], xs[5]], fulls=pre_fulls,
                                            row_outs=[(D, F32)] * 5, tm=min(256, T), T=T, name="rwkv_pre")
    y, st = _rwkv_scan(r, decay, k2, v, kkn, bb, B, T)
    post_fulls = [p["ln_w"].reshape(1, D), p["ln_b"].reshape(1, D), p["r_k"].reshape(1, D), e, et]

    def body3(rows, prevs, nexts, fulls, sf, sl):
        return [_rwkv_post(*[rr[...] for rr in rows], *[f[...] for f in fulls])], []

    (z,), _ = _tile_call(body3, rows=[y, r, k2, v, g], fulls=post_fulls, row_outs=[(D, BF16)], tm=min(256, T), T=T,
                         name="rwkv_post")
    out = _matmul(z, p["w_out"], residual=x, name="rwkv_out")
    return out, (x, xs, r, k, v, decay, k2, kkn, bb, g, y, st, z)


def _rwkv_bwd(dout, saved, p, B, T):
    x, xs, r, k, v, decay, k2, kkn, bb, g, y, st, z = saved
    M, D = x.shape
    e, et = _head_mats(D)
    d_z = _matmul(dout, p["w_out"], mode="nt", name="rwkv_out_dx")
    d_w_out = _matmul(z, dout, mode="tn", out_dtype=BF16, name="rwkv_out_dw")
    post_fulls = [p["ln_w"].reshape(1, D), p["ln_b"].reshape(1, D), p["r_k"].reshape(1, D), e, et]

    def body(rows, prevs, nexts, fulls, sf, sl):
        prim = [rr[...] for rr in rows[:5]] + [f[...] for f in fulls]
        _, vjp = jax.vjp(_rwkv_post, *prim)
        ct = vjp(rows[5][...])
        return list(ct[:5]), list(ct[5:8])

    (d_y, d_r1, d_k21, d_v1, d_g), (d_ln_w, d_ln_b, d_r_k) = _tile_call(
        body, rows=[y, r, k2, v, g, d_z], fulls=post_fulls, row_outs=[(D, F32)] * 5, acc_outs=[(1, D)] * 3,
        tm=min(256, T), T=T, name="rwkv_post_bwd")
    d_r2, d_w, d_k22, d_v2, d_kkn, d_bb = _rwkv_scan_bwd(r, decay, k2, v, kkn, bb, st, d_y, B, T)
    pre_fulls = [p["w0"].reshape(1, D), p["w1"], p["w2"], p["a0"].reshape(1, D), p["a1"], p["a2"], p["g1"], p["g2"],
                 p["k_k"].reshape(1, D), p["k_a"].reshape(1, D), e, et]

    def body2(rows, prevs, nexts, fulls, sf, sl):
        prim = [rows[0][...], _f(rows[1]), _f(rows[2]), _f(rows[3])] + [f[...] for f in fulls]
        _, vjp = jax.vjp(_rwkv_pre, *prim)
        ct = vjp((rows[4][...], rows[5][...] + rows[6][...], rows[7][...], rows[8][...], rows[9][...]))
        d_r = rows[10][...] + rows[11][...]
        d_v = rows[12][...] + rows[13][...]
        return [ct[0], ct[1], ct[2], ct[3], d_r, d_v], [c.astype(F32) for c in ct[4:14]]

    acc_shapes = [f.shape for f in pre_fulls[:10]]
    (d_k, d_xw, d_xa, d_xg, d_r, d_v), pgr = _tile_call(
        body2, rows=[k, xs[3], xs[4], xs[5], d_w, d_k21, d_k22, d_kkn, d_bb, d_g, d_r1, d_r2, d_v1, d_v2],
        fulls=pre_fulls, row_outs=[(D, BF16), (D, F32), (D, F32), (D, F32), (D, BF16), (D, BF16)], acc_outs=acc_shapes,
        tm=min(128, T), T=T, name="rwkv_pre_bwd")
    d_xr = _matmul(d_r, p["w_r"], mode="nt", name="rwkv_r_dx")
    d_xk = _matmul(d_k, p["w_k"], mode="nt", name="rwkv_k_dx")
    d_xv = _matmul(d_v, p["w_v"], mode="nt", name="rwkv_v_dx")
    d_wr = _matmul(xs[0], d_r, mode="tn", out_dtype=BF16, name="rwkv_r_dw")
    d_wk = _matmul(xs[1], d_k, mode="tn", out_dtype=BF16, name="rwkv_k_dw")
    d_wv = _matmul(xs[2], d_v, mode="tn", out_dtype=BF16, name="rwkv_v_dw")
    norm = p["norm"].reshape(1, D)

    def body3(rows, prevs, nexts, fulls, sf, sl):
        xc, xp = rows[0][...], prevs[0][...]
        nrm, mix = _f(fulls[0]), _f(fulls[1])
        h, xx = _rwkv_mix(xc, xp, nrm, None, sf)
        dxs = [rows[1 + i][...] for i in range(6)]
        dxs_n = [nexts[i][...] for i in range(6)]
        d_h = jnp.zeros_like(h)
        d_sh = jnp.zeros_like(h)
        d_sh_n = jnp.zeros_like(dxs_n[0])
        dmix = []
        for i in range(6):
            m = mix[i:i + 1]
            d_h = d_h + dxs[i] * (1.0 - m)
            d_sh = d_sh + dxs[i] * m
            d_sh_n = d_sh_n + dxs_n[i] * m
            dmix.append(_colsum(dxs[i] * xx))
        d_h = d_h + _shift_up(d_sh, d_sh_n, 1, sl)
        _, vjp = jax.vjp(_rms, xc, nrm)
        dx, dn = vjp(d_h)
        return [dx + rows[7][...]], [dn, _pad8(jnp.concatenate(dmix, axis=0))]

    (dx,), (d_norm, d_mix) = _tile_call(
        body3, rows=[x, d_xr, d_xk, d_xv, d_xw, d_xa, d_xg, dout], prevs=[0], nexts=[1, 2, 3, 4, 5, 6],
        fulls=[norm, _pad8(p["mix"])], row_outs=[(D, F32)], acc_outs=[(1, D), (8, D)], tm=min(256, T), T=T,
        name="rwkv_mix_bwd")
    names = ["w0", "w1", "w2", "a0", "a1", "a2", "g1", "g2", "k_k", "k_a"]
    grads = {n: gr.reshape(p[n].shape) for n, gr in zip(names, pgr, strict=True)}
    grads.update({"norm": d_norm.reshape(D), "mix": d_mix[:6], "w_r": d_wr, "w_k": d_wk, "w_v": d_wv,
                  "r_k": d_r_k.reshape(p["r_k"].shape), "ln_w": d_ln_w.reshape(D), "ln_b": d_ln_b.reshape(D),
                  "w_out": d_w_out})
    return dx, grads


def _loss_head(x, g, tgt, T):
    M, D = x.shape

    def body(rows, prevs, nexts, fulls, sf, sl):
        xv, gv = rows[0][...], _f(fulls[0])
        yv, vjp = jax.vjp(_rms, xv, gv)
        err = yv - rows[1][...]
        dx, dg = vjp(err * (1.0 / D))
        part = jnp.sum(_colsum(err * err), axis=1, keepdims=True) * (0.5 / D)
        return [dx], [dg, jnp.broadcast_to(part, (1, 128))]

    (dx,), (dg, loss) = _tile_call(body, rows=[x, tgt], fulls=[g.reshape(1, D)], row_outs=[(D, F32)],
                                   acc_outs=[(1, D), (1, 128)], tm=min(512, T), T=T, name="loss_head")
    return loss[0, 0], dx, dg.reshape(D)


def _local_step(x3, tgt3, P):
    B, T, D = x3.shape
    x, tgt = x3.reshape(B * T, D), tgt3.reshape(B * T, D)
    x1, s_lru = _lru_fwd(x, P["lru"], B, T)
    x2, s_f0 = _ffn_fwd(x1, P["ffn0"], T, "0")
    x3_, s_rw = _rwkv_fwd(x2, P["rwkv"], B, T)
    x4, s_f1 = _ffn_fwd(x3_, P["ffn1"], T, "1")
    loss, d4, d_fn = _loss_head(x4, P["final_norm"], tgt, T)
    d3, g_f1 = _ffn_bwd(d4, s_f1, P["ffn1"], T, "1")
    d2, g_rw = _rwkv_bwd(d3, s_rw, P["rwkv"], B, T)
    d1, g_f0 = _ffn_bwd(d2, s_f0, P["ffn0"], T, "0")
    d0, g_lru = _lru_bwd(d1, s_lru, P["lru"], B, T)
    return loss, d0.reshape(B, T, D), {"lru": g_lru, "ffn0": g_f0, "rwkv": g_rw, "ffn1": g_f1, "final_norm": d_fn}


WEIGHTS = ['lru_norm', 'lru_w_in', 'lru_b_in', 'lru_conv_w', 'lru_conv_b', 'lru_gate_w', 'lru_gate_b', 'lru_lambda',
           'lru_w_out', 'lru_b_out', 'rwkv_norm', 'rwkv_mix', 'rwkv_w_rkv', 'rwkv_w0', 'rwkv_w1', 'rwkv_w2', 'rwkv_a0',
           'rwkv_a1', 'rwkv_a2', 'rwkv_g1', 'rwkv_g2', 'rwkv_k_k', 'rwkv_k_a', 'rwkv_r_k', 'rwkv_ln_w', 'rwkv_ln_b',
           'rwkv_w_out', 'ffn_norm', 'ffn_w_up', 'ffn_conv_w', 'ffn_conv_b', 'ffn_w_down', 'final_norm']
SHARD_AXIS = {'lru_w_in': 2, 'lru_conv_w': 2, 'lru_w_out': 1, 'rwkv_norm': 1, 'rwkv_mix': 2, 'rwkv_w_rkv': 2,
              'rwkv_w0': 1, 'rwkv_w1': 1, 'rwkv_w2': 2, 'rwkv_a0': 1, 'rwkv_a1': 1, 'rwkv_a2': 2, 'rwkv_g1': 1,
              'rwkv_g2': 2, 'rwkv_k_k': 1, 'rwkv_k_a': 1, 'rwkv_ln_w': 1, 'rwkv_ln_b': 1, 'rwkv_w_out': 1,
              'ffn_w_up': 2, 'ffn_conv_w': 2, 'ffn_w_down': 1}
MXU_WEIGHTS = ('lru_w_in', 'lru_w_out', 'rwkv_w_rkv', 'rwkv_w_out', 'ffn_w_up', 'ffn_w_down')
N_CHIPS = 4
LANES = 1024


def _pack(arrs, dtype, row_mult):
    flat = jnp.concatenate([a.reshape(-1).astype(dtype) for a in arrs])
    n = flat.shape[0]
    unit = row_mult * LANES
    tot = -(-n // unit) * unit
    if tot > n:
        flat = jnp.concatenate([flat, jnp.zeros((tot - n,), dtype)])
    return flat.reshape(tot // LANES, LANES)


def _unpack(buf, shapes):
    flat = buf.reshape(-1)
    out, off = [], 0
    for s in shapes:
        n = 1
        for d in s:
            n *= d
        out.append(flat[off:off + n].reshape(s))
        off += n
    return out


def _to_shards(full, axis):
    return jnp.stack(jnp.split(full, N_CHIPS, axis=axis))


MESH_ID = pl.DeviceIdType.MESH


ANY_SPEC = pl.BlockSpec(memory_space=pl.ANY)
COMM_PARAMS = pltpu.CompilerParams(has_side_effects=True)


def _mesh_place():
    x, y, c = lax.axis_index("x"), lax.axis_index("y"), lax.axis_index("c")
    return x, y, c, 2 * x + y, [(1 - x, y), (x, 1 - y), (1 - x, 1 - y)]


def _gather_all(arrs, name):
    n = len(arrs)

    def body(*refs):
        ins, outs = refs[:n], refs[n:2 * n]
        send_sems, recv_sems, local_sems = refs[2 * n:]
        x, y, c, p, chips = _mesh_place()
        sibling = (x, y, 1 - c)

        def rows(a, which):
            h = arrs[a].shape[0] // 2
            return pl.ds(which * h, h)

        def copy(a, k, region, src, to):
            return pltpu.make_async_remote_copy(src_ref=src, dst_ref=region, send_sem=send_sems.at[a, k],
                                                recv_sem=recv_sems.at[a, k], device_id=to, device_id_type=MESH_ID)

        local = [pltpu.make_async_copy(ins[a], outs[a].at[p], local_sems.at[a]) for a in range(n)]
        for cp in local:
            cp.start()
        first = [copy(a, j, outs[a].at[p, rows(a, c)], ins[a].at[rows(a, c)], (qx, qy, c))
                 for a in range(n) for j, (qx, qy) in enumerate(chips)]
        for cp in first:
            cp.start()
        passed = []
        for a in range(n):
            for j, (qx, qy) in enumerate(chips):
                region = outs[a].at[2 * qx + qy, rows(a, c)]
                copy(a, j, region, region, (qx, qy, c)).wait_recv()
                fw = copy(a, 3 + j, region, region, sibling)
                fw.start()
                passed.append(fw)
        for a in range(n):
            for j, (qx, qy) in enumerate(chips):
                region = outs[a].at[2 * qx + qy, rows(a, 1 - c)]
                copy(a, 3 + j, region, region, sibling).wait_recv()
        for cp in first + passed:
            cp.wait_send()
        for cp in local:
            cp.wait()

    return pl.pallas_call(
        body, name=name, out_shape=[jax.ShapeDtypeStruct((N_CHIPS,) + a.shape, a.dtype) for a in arrs],
        in_specs=[ANY_SPEC] * n, out_specs=[ANY_SPEC] * n,
        scratch_shapes=[pltpu.SemaphoreType.DMA((n, 6)), pltpu.SemaphoreType.DMA((n, 6)), pltpu.SemaphoreType.DMA((n,))],
        compiler_params=COMM_PARAMS,
    )(*arrs)


def _pair_swap_all(gs, rep, name):
    n = len(gs)

    def body(*refs):
        ins, outs = refs[:n + 1], refs[n + 1:2 * n + 2]
        send_sems, recv_sems = refs[2 * n + 2:]
        x, y, c, _, _ = _mesh_place()
        copies = []
        for a in range(n + 1):
            src = ins[a]
            if a < n:
                h = gs[a].shape[1] // 2
                src = src.at[:, pl.ds((1 - c) * h, h), :]
            copies.append(pltpu.make_async_remote_copy(src_ref=src, dst_ref=outs[a], send_sem=send_sems.at[a],
                                                       recv_sem=recv_sems.at[a], device_id=(x, y, 1 - c),
                                                       device_id_type=MESH_ID))
        for cp in copies:
            cp.start()
        for cp in copies:
            cp.wait()

    shapes = [jax.ShapeDtypeStruct((N_CHIPS, g.shape[1] // 2, g.shape[2]), g.dtype) for g in gs]
    shapes.append(jax.ShapeDtypeStruct(rep.shape, rep.dtype))
    res = pl.pallas_call(
        body, name=name, out_shape=shapes, in_specs=[ANY_SPEC] * (n + 1), out_specs=[ANY_SPEC] * (n + 1),
        scratch_shapes=[pltpu.SemaphoreType.DMA((n + 1,)), pltpu.SemaphoreType.DMA((n + 1,))],
        compiler_params=COMM_PARAMS,
    )(*gs, rep)
    return res[:n], res[n]


def _chip_exchange_all(ps, rep, name):
    n = len(ps)

    def body(*refs):
        ins, outs = refs[:n + 1], refs[n + 1:2 * n + 2]
        send_sems, recv_sems, local_sems = refs[2 * n + 2:]
        x, y, c, p, chips = _mesh_place()

        def src(a, q):
            return ins[a].at[q] if a < n else ins[a]

        local = [pltpu.make_async_copy(src(a, p), outs[a].at[p], local_sems.at[a]) for a in range(n + 1)]
        for cp in local:
            cp.start()
        sends, recvs = [], []
        for a in range(n + 1):
            for j, (qx, qy) in enumerate(chips):
                q = 2 * qx + qy
                for dst, keep in ((outs[a].at[p], sends), (outs[a].at[q], recvs)):
                    keep.append(pltpu.make_async_remote_copy(
                        src_ref=src(a, q), dst_ref=dst, send_sem=send_sems.at[a, j], recv_sem=recv_sems.at[a, j],
                        device_id=(qx, qy, c), device_id_type=MESH_ID))
        for cp in sends:
            cp.start()
        for cp in recvs:
            cp.wait_recv()
        for cp in sends:
            cp.wait_send()
        for cp in local:
            cp.wait()

    shapes = [jax.ShapeDtypeStruct(g.shape, g.dtype) for g in ps]
    shapes.append(jax.ShapeDtypeStruct((N_CHIPS,) + rep.shape, rep.dtype))
    res = pl.pallas_call(
        body, name=name, out_shape=shapes, in_specs=[ANY_SPEC] * (n + 1), out_specs=[ANY_SPEC] * (n + 1),
        scratch_shapes=[pltpu.SemaphoreType.DMA((n + 1, 3)), pltpu.SemaphoreType.DMA((n + 1, 3)),
                        pltpu.SemaphoreType.DMA((n + 1,))],
        compiler_params=COMM_PARAMS,
    )(*ps, rep)
    return res[:n], res[n]


def _half_swap_all(ts, name):
    n = len(ts)

    def body(*refs):
        ins, outs = refs[:n], refs[n:2 * n]
        send_sems, recv_sems = refs[2 * n:]
        x, y, c, _, _ = _mesh_place()
        copies = [pltpu.make_async_remote_copy(src_ref=ins[a], dst_ref=outs[a], send_sem=send_sems.at[a],
                                               recv_sem=recv_sems.at[a], device_id=(x, y, 1 - c), device_id_type=MESH_ID)
                  for a in range(n)]
        for cp in copies:
            cp.start()
        for cp in copies:
            cp.wait()

    return pl.pallas_call(
        body, name=name, out_shape=[jax.ShapeDtypeStruct(t.shape, t.dtype) for t in ts],
        in_specs=[ANY_SPEC] * n, out_specs=[ANY_SPEC] * n,
        scratch_shapes=[pltpu.SemaphoreType.DMA((n,)), pltpu.SemaphoreType.DMA((n,))],
        compiler_params=COMM_PARAMS,
    )(*ts)


def _pick_rows(R, cap=256):
    for t in (512, 256, 128, 64, 32, 16, 8):
        if t <= cap and R % t == 0:
            return t
    return R


def _pair_sum(g, got, name):
    _, R, C = g.shape
    h = R // 2
    th = _pick_rows(h)

    def kern(g_ref, got_ref, o_ref):
        both = g_ref[...].astype(F32)
        mine = jnp.where(lax.axis_index("c") == 0, both[0], both[1])
        o_ref[...] = (mine + got_ref[...].astype(F32)).astype(o_ref.dtype)

    return pl.pallas_call(
        kern, name=name, grid=(N_CHIPS, h // th),
        in_specs=[pl.BlockSpec((None, 2, th, C), lambda q, i: (q, 0, i, 0)), pl.BlockSpec((None, th, C), lambda q, i: (q, i, 0))],
        out_specs=pl.BlockSpec((None, th, C), lambda q, i: (q, i, 0)),
        out_shape=jax.ShapeDtypeStruct((N_CHIPS, h, C), BF16), compiler_params=_cparams(("parallel", "parallel")),
    )(g.reshape(N_CHIPS, 2, h, C), got)


def _rep_pair_sum(rep, got, name):
    R, C = rep.shape
    tr = _pick_rows(R)

    def kern(a_ref, b_ref, o_ref):
        o_ref[...] = (a_ref[...] + b_ref[...]).astype(o_ref.dtype)

    spec = pl.BlockSpec((tr, C), lambda i: (i, 0))
    return pl.pallas_call(kern, name=name, grid=(R // tr,), in_specs=[spec, spec], out_specs=spec,
                          out_shape=jax.ShapeDtypeStruct((R, C), BF16), compiler_params=_cparams(("parallel",)))(rep, got)


def _chip_sum(arrived, name):
    _, R, C = arrived.shape
    tr = _pick_rows(R)

    def kern(a_ref, o_ref):
        acc = a_ref[0].astype(F32)
        for q in range(1, N_CHIPS):
            acc = acc + a_ref[q].astype(F32)
        o_ref[...] = acc

    return pl.pallas_call(
        kern, name=name, grid=(R // tr,), in_specs=[pl.BlockSpec((N_CHIPS, tr, C), lambda i: (0, i, 0))],
        out_specs=pl.BlockSpec((tr, C), lambda i: (i, 0)), out_shape=jax.ShapeDtypeStruct((R, C), F32),
        compiler_params=_cparams(("parallel",)),
    )(arrived)


def _adam_math(w, g, m, v):
    c1 = 1.0 / (1.0 - ADAM_B1 ** ADAM_STEP)
    c2 = 1.0 / (1.0 - ADAM_B2 ** ADAM_STEP)
    nm = ADAM_B1 * m + (1.0 - ADAM_B1) * g
    nv = ADAM_B2 * v + (1.0 - ADAM_B2) * (g * g)
    return -ADAM_LR * ((nm * c1) / (jnp.sqrt(nv * c2) + ADAM_EPS) + ADAM_WD * w), nm, nv


def _adamw_halves(w, m, v, mine, other, name):
    R, C = w.shape
    h = R // 2
    th = _pick_rows(h)
    nt = h // th

    def kern(w_ref, m_ref, v_ref, a_ref, b_ref, g_ref, d_ref, nm_ref, nv_ref):
        g = jnp.where(pl.program_id(0) == lax.axis_index("c"), a_ref[...], b_ref[...])
        d, nm, nv = _adam_math(w_ref[...], g, m_ref[...], v_ref[...])
        g_ref[...] = g
        d_ref[...] = d
        nm_ref[...] = nm
        nv_ref[...] = nv

    full = pl.BlockSpec((th, C), lambda hh, i: (hh * nt + i, 0))
    half = pl.BlockSpec((th, C), lambda hh, i: (i, 0))
    sh = jax.ShapeDtypeStruct((R, C), F32)
    return pl.pallas_call(kern, name=name, grid=(2, nt), in_specs=[full] * 3 + [half] * 2, out_specs=[full] * 4,
                          out_shape=[sh] * 4, compiler_params=_cparams(("parallel", "parallel")))(w, m, v, mine, other)


def _adamw_call(w, g, m, v, name):
    R = w.shape[0]
    tr = _pick_rows(R)
    c1 = 1.0 / (1.0 - ADAM_B1 ** ADAM_STEP)
    c2 = 1.0 / (1.0 - ADAM_B2 ** ADAM_STEP)

    def kern(w_ref, g_ref, m_ref, v_ref, d_ref, nm_ref, nv_ref):
        gv = g_ref[...]
        nm = ADAM_B1 * m_ref[...] + (1.0 - ADAM_B1) * gv
        nv = ADAM_B2 * v_ref[...] + (1.0 - ADAM_B2) * (gv * gv)
        d_ref[...] = -ADAM_LR * ((nm * c1) / (jnp.sqrt(nv * c2) + ADAM_EPS) + ADAM_WD * w_ref[...])
        nm_ref[...] = nm
        nv_ref[...] = nv

    spec = pl.BlockSpec((tr, LANES), lambda i: (i, 0))
    sh = jax.ShapeDtypeStruct((R, LANES), F32)
    return pl.pallas_call(kern, name=name, grid=(R // tr,), in_specs=[spec] * 4, out_specs=[spec] * 3,
                          out_shape=[sh] * 3, compiler_params=_cparams(("parallel",)))(w, g, m, v)


GATE_GROUP = 256


def _gate_dense(gate_w):
    _, nb, bw, _ = gate_w.shape
    D = nb * bw
    gb = min(GATE_GROUP, D)
    per = gb // bw
    w = gate_w.reshape(2, D // gb, per, bw, bw)
    eye = jnp.eye(per, dtype=gate_w.dtype)
    dense = jnp.einsum('grncd,nm->rncgmd', w, eye)
    return dense.reshape(D, 2 * gb)


def _gate_blocks(d_dense, nb):
    D, gb2 = d_dense.shape
    gb, bw = gb2 // 2, D // nb
    per = gb // bw
    g = d_dense.reshape(D // gb, per, bw, 2, per, bw)
    return jnp.einsum('rncgnd->grncd', g).reshape(2, nb, bw, bw)


def _step(W, M1, V1, x, tgt):
    sharded = [n for n in WEIGHTS if n in SHARD_AXIS]
    repl = [n for n in WEIGHTS if n not in SHARD_AXIS]
    small = [n for n in sharded if n not in MXU_WEIGHTS]
    D = x.shape[-1]

    def rows2d(a):
        return a.reshape(-1, a.shape[-1])

    small_buf = _pack([W[n] for n in small], F32, 16)
    gathered = _gather_all([rows2d(W[n]).astype(MXU_DTYPE) for n in MXU_WEIGHTS] + [small_buf], "gather_weights")
    mats = {n: g.reshape((N_CHIPS,) + W[n].shape[-3:]) for n, g in zip(MXU_WEIGHTS, gathered[:-1], strict=True)}
    per_chip = [_unpack(gathered[-1][q], [W[n].shape for n in small]) for q in range(N_CHIPS)]
    full = {n: jnp.concatenate([per_chip[q][i] for q in range(N_CHIPS)], axis=SHARD_AXIS[n]) for i, n in enumerate(small)}
    for n in repl:
        full[n] = W[n]

    def by_rows(name, layer):
        m = mats[name][:, layer]
        return m.reshape(N_CHIPS * m.shape[1], m.shape[2])

    P = {
        "lru": {"norm": full["lru_norm"][0], "w_in": _W(mats["lru_w_in"], 0), "b_in": full["lru_b_in"][0],
                "conv_w": full["lru_conv_w"][0], "conv_b": full["lru_conv_b"][0],
                "wbd": _gate_dense(full["lru_gate_w"][0]).astype(MXU_DTYPE), "gate_b": full["lru_gate_b"][0].reshape(-1),
                "lam": full["lru_lambda"][0], "w_out": by_rows("lru_w_out", 0), "b_out": full["lru_b_out"][0]},
        "rwkv": {"norm": full["rwkv_norm"][0], "mix": full["rwkv_mix"][0],
                 "w_r": by_rows("rwkv_w_rkv", 0), "w_k": by_rows("rwkv_w_rkv", 1), "w_v": by_rows("rwkv_w_rkv", 2),
                 "w0": full["rwkv_w0"][0], "w1": full["rwkv_w1"][0], "w2": full["rwkv_w2"][0], "a0": full["rwkv_a0"][0],
                 "a1": full["rwkv_a1"][0], "a2": full["rwkv_a2"][0], "g1": full["rwkv_g1"][0], "g2": full["rwkv_g2"][0],
                 "k_k": full["rwkv_k_k"][0], "k_a": full["rwkv_k_a"][0], "r_k": full["rwkv_r_k"][0],
                 "ln_w": full["rwkv_ln_w"][0], "ln_b": full["rwkv_ln_b"][0], "w_out": by_rows("rwkv_w_out", 0)},
        "final_norm": full["final_norm"],
    }
    for l in range(2):
        P[f"ffn{l}"] = {"norm": full["ffn_norm"][l], "w_up": _W(mats["ffn_w_up"], l),
                        "conv_w": full["ffn_conv_w"][l], "conv_b": full["ffn_conv_b"][l],
                        "w_down": by_rows("ffn_w_down", l)}

    loss, gx, G = _local_step(x, tgt, P)

    nb = W["lru_gate_w"].shape[2]
    gl, gr = G["lru"], G["rwkv"]
    gfull = {
        "lru_norm": gl["norm"][None], "lru_b_in": gl["b_in"][None],
        "lru_conv_w": gl["conv_w"][None], "lru_conv_b": gl["conv_b"][None], "lru_gate_w": _gate_blocks(gl["wbd"], nb)[None],
        "lru_gate_b": gl["gate_b"].reshape(W["lru_gate_b"].shape), "lru_lambda": gl["lam"][None],
        "lru_b_out": gl["b_out"][None],
        "rwkv_norm": gr["norm"][None], "rwkv_mix": gr["mix"][None],
        "rwkv_w0": gr["w0"][None], "rwkv_w1": gr["w1"][None], "rwkv_w2": gr["w2"][None], "rwkv_a0": gr["a0"][None],
        "rwkv_a1": gr["a1"][None], "rwkv_a2": gr["a2"][None], "rwkv_g1": gr["g1"][None], "rwkv_g2": gr["g2"][None],
        "rwkv_k_k": gr["k_k"][None], "rwkv_k_a": gr["k_a"][None], "rwkv_r_k": gr["r_k"][None],
        "rwkv_ln_w": gr["ln_w"][None], "rwkv_ln_b": gr["ln_b"][None],
        "final_norm": G["final_norm"],
    }
    for k in ("norm", "conv_w", "conv_b"):
        gfull["ffn_" + k] = jnp.stack([G["ffn0"][k], G["ffn1"][k]])

    small_g = jnp.stack([_pack([_to_shards(gfull[n], SHARD_AXIS[n])[q] for n in small], F32, 16) for q in range(N_CHIPS)])
    cut = lambda g: g.reshape(N_CHIPS, g.shape[0] // N_CHIPS, g.shape[1])
    pieces = [("lru_w_in", (0,), gl["w_in"]), ("lru_w_out", (0,), cut(gl["w_out"])),
              ("rwkv_w_rkv", (0, 0), cut(gr["w_r"])), ("rwkv_w_rkv", (0, 1), cut(gr["w_k"])),
              ("rwkv_w_rkv", (0, 2), cut(gr["w_v"])), ("rwkv_w_out", (0,), cut(gr["w_out"])),
              ("ffn_w_up", (0,), G["ffn0"]["w_up"]), ("ffn_w_up", (1,), G["ffn1"]["w_up"]),
              ("ffn_w_down", (0,), cut(G["ffn0"]["w_down"])), ("ffn_w_down", (1,), cut(G["ffn1"]["w_down"]))]
    gs = [g for _, _, g in pieces] + [small_g]
    grep = _pack([gfull[n] for n in repl], F32, 16)

    got, got_rep = _pair_swap_all(gs, grep, "reduce_pair_swap")
    pair = [_pair_sum(g, r, f"reduce_pair_sum{i}") for i, (g, r) in enumerate(zip(gs, got, strict=True))]
    pair_rep = _rep_pair_sum(grep, got_rep, "reduce_pair_sum_rep")
    arrived, arrived_rep = _chip_exchange_all(pair, pair_rep, "reduce_chips")
    mine = [_chip_sum(a, f"reduce_chip_sum{i}") for i, a in enumerate(arrived)]
    g_rp = _chip_sum(arrived_rep, "reduce_chip_sum_rep")
    other = _half_swap_all(mine, "reduce_half_swap")

    outs, parts = {}, {}
    for i, (n, idx, _) in enumerate(pieces):
        w, m, v = (rows2d(S[n][idx]) for S in (W, M1, V1))
        res = _adamw_halves(w, m, v, mine[i], other[i], f"adamw{i}")
        for kind, a in zip(("grad", "delta", "new_m", "new_v"), res, strict=True):
            parts.setdefault((kind, n), []).append(a)
    for (kind, n), lst in parts.items():
        a = lst[0] if len(lst) == 1 else jnp.stack(lst)
        outs[(kind, n)] = a.reshape(W[n].shape)
    wb, mb, vb = (_pack([S[n] for n in small], F32, 16) for S in (W, M1, V1))
    res = _adamw_halves(wb, mb, vb, mine[-1], other[-1], "adamw_small")
    for kind, buf in zip(("grad", "delta", "new_m", "new_v"), res, strict=True):
        for n, a in zip(small, _unpack(buf, [W[n].shape for n in small]), strict=True):
            outs[(kind, n)] = a
    wb, mb, vb = (_pack([S[n] for n in repl], F32, 16) for S in (W, M1, V1))
    d, nm, nv = _adamw_call(wb, g_rp, mb, vb, "adamw_repl")
    for kind, buf in (("grad", g_rp), ("delta", d), ("new_m", nm), ("new_v", nv)):
        for n, a in zip(repl, _unpack(buf, [W[n].shape for n in repl]), strict=True):
            outs[(kind, n)] = a
    loss = lax.psum(loss, ("x", "y", "c"))
    return (loss, gx, *[outs[(kind, n)] for kind in ("grad", "delta", "new_m", "new_v") for n in WEIGHTS])


def kernel(x, lru_norm, lru_w_in, lru_b_in, lru_conv_w, lru_conv_b, lru_gate_w, lru_gate_b, lru_lambda, lru_w_out, lru_b_out, rwkv_norm, rwkv_mix, rwkv_w_rkv, rwkv_w0, rwkv_w1, rwkv_w2, rwkv_a0, rwkv_a1, rwkv_a2, rwkv_g1, rwkv_g2, rwkv_k_k, rwkv_k_a, rwkv_r_k, rwkv_ln_w, rwkv_ln_b, rwkv_w_out, ffn_norm, ffn_w_up, ffn_conv_w, ffn_conv_b, ffn_w_down, final_norm, loss_target, m_lru_norm, m_lru_w_in, m_lru_b_in, m_lru_conv_w, m_lru_conv_b, m_lru_gate_w, m_lru_gate_b, m_lru_lambda, m_lru_w_out, m_lru_b_out, m_rwkv_norm, m_rwkv_mix, m_rwkv_w_rkv, m_rwkv_w0, m_rwkv_w1, m_rwkv_w2, m_rwkv_a0, m_rwkv_a1, m_rwkv_a2, m_rwkv_g1, m_rwkv_g2, m_rwkv_k_k, m_rwkv_k_a, m_rwkv_r_k, m_rwkv_ln_w, m_rwkv_ln_b, m_rwkv_w_out, m_ffn_norm, m_ffn_w_up, m_ffn_conv_w, m_ffn_conv_b, m_ffn_w_down, m_final_norm, v_lru_norm, v_lru_w_in, v_lru_b_in, v_lru_conv_w, v_lru_conv_b, v_lru_gate_w, v_lru_gate_b, v_lru_lambda, v_lru_w_out, v_lru_b_out, v_rwkv_norm, v_rwkv_mix, v_rwkv_w_rkv, v_rwkv_w0, v_rwkv_w1, v_rwkv_w2, v_rwkv_a0, v_rwkv_a1, v_rwkv_a2, v_rwkv_g1, v_rwkv_g2, v_rwkv_k_k, v_rwkv_k_a, v_rwkv_r_k, v_rwkv_ln_w, v_rwkv_ln_b, v_rwkv_w_out, v_ffn_norm, v_ffn_w_up, v_ffn_conv_w, v_ffn_conv_b, v_ffn_w_down, v_final_norm):
    given = dict(locals())
    W = {n: given[n] for n in WEIGHTS}
    M1 = {n: given["m_" + n] for n in WEIGHTS}
    V1 = {n: given["v_" + n] for n in WEIGHTS}
    return _step(W, M1, V1, x, loss_target)
```

```python
import functools

import jax
import jax.numpy as jnp
from jax import lax
from jax.experimental import pallas as pl
from jax.experimental.pallas import tpu as pltpu

F32 = jnp.float32
BF16 = jnp.bfloat16
MXU_DTYPE = BF16

HEAD = 64
LRU_C = 8.0
GN_EPS = 64e-5
RMS_EPS = 1e-6
HALO = 16
VMEM_LIMIT = 56 * 1024 * 1024

ADAM_LR, ADAM_B1, ADAM_B2, ADAM_EPS, ADAM_WD, ADAM_STEP = 0.001, 0.9, 0.999, 1e-08, 0.01, 10


def _cparams(sem):
    return pltpu.CompilerParams(dimension_semantics=sem, vmem_limit_bytes=VMEM_LIMIT)


def _pick(n, want):
    if n <= want:
        return n
    t = want
    while t >= 128:
        if n % t == 0:
            return t
        t -= 128
    return n


class _W:
    def __init__(self, arr, layer):
        self.arr, self.layer = arr, layer
        self.shape = (arr.shape[2], N_CHIPS * arr.shape[3])


def _matmul(a, b, mode="nn", bias=None, residual=None, out_dtype=F32, name="mm", tm=1024, tn=1536, tk=1024,
            out_cols_by_chip=False):
    bshape = b.shape
    if mode == "nn":
        (M, K), (K2, N) = a.shape, bshape
    elif mode == "nt":
        (M, K), (N, K2) = a.shape, bshape
    else:
        (K, M), (K2, N) = a.shape, bshape
    assert K == K2, (a.shape, bshape, mode)
    lim_n, lim_k = N, K
    if isinstance(b, _W):
        if mode == "nn":
            lim_n = b.arr.shape[3]
        else:
            lim_k = b.arr.shape[3]
    if out_cols_by_chip:
        lim_n = min(lim_n, N // N_CHIPS)
    if K > 2 * tk:
        tk = 2 * tk
    tm, tn, tk = _pick(M, tm), _pick(lim_n, tn), _pick(lim_k, tk)
    nk = K // tk
    dims = {"nn": (((1,), (0,)), ((), ())), "nt": (((1,), (1,)), ((), ())), "tn": (((0,), (0,)), ((), ()))}[mode]
    a_spec = {"nn": pl.BlockSpec((tm, tk), lambda i, j, k: (i, k)),
              "nt": pl.BlockSpec((tm, tk), lambda i, j, k: (i, k)),
              "tn": pl.BlockSpec((tk, tm), lambda i, j, k: (k, i))}[mode]
    if isinstance(b, _W):
        lay = b.layer
        if mode == "nn":
            per = b.arr.shape[3] // tn
            b_spec = pl.BlockSpec((None, None, tk, tn), lambda i, j, k: (j // per, lay, k, j % per))
        else:
            assert mode == "nt"
            per = b.arr.shape[3] // tk
            b_spec = pl.BlockSpec((None, None, tn, tk), lambda i, j, k: (k // per, lay, j, k % per))
        b = b.arr
    else:
        b_spec = {"nn": pl.BlockSpec((tk, tn), lambda i, j, k: (k, j)),
                  "nt": pl.BlockSpec((tn, tk), lambda i, j, k: (j, k)),
                  "tn": pl.BlockSpec((tk, tn), lambda i, j, k: (k, j))}[mode]
    if out_cols_by_chip:
        opc = N // N_CHIPS // tn
        out_spec = pl.BlockSpec((None, tm, tn), lambda i, j, k: (j // opc, i, j % opc))
        out_shape = jax.ShapeDtypeStruct((N_CHIPS, M, N // N_CHIPS), out_dtype)
    else:
        out_spec = pl.BlockSpec((tm, tn), lambda i, j, k: (i, j))
        out_shape = jax.ShapeDtypeStruct((M, N), out_dtype)
    in_specs, operands = [a_spec, b_spec], [a, b]
    if bias is not None:
        in_specs.append(pl.BlockSpec((1, tn), lambda i, j, k: (0, j)))
        operands.append(bias.reshape(1, N))
    if residual is not None:
        in_specs.append(pl.BlockSpec((tm, tn), lambda i, j, k: (i, j)))
        operands.append(residual)
    has_bias, has_res = bias is not None, residual is not None

    def kern(*refs):
        a_ref, b_ref = refs[0], refs[1]
        o_ref = refs[2 + has_bias + has_res]

        def finish(r):
            pos = 2
            if has_bias:
                r = r + refs[pos][...].astype(F32)
                pos += 1
            if has_res:
                r = r + refs[pos][...].astype(F32)
            o_ref[...] = r.astype(o_ref.dtype)

        part = lax.dot_general(a_ref[...].astype(MXU_DTYPE), b_ref[...].astype(MXU_DTYPE), dims,
                               preferred_element_type=F32)
        if nk == 1:
            finish(part)
            return
        acc_ref = refs[-1]
        k = pl.program_id(2)

        @pl.when(k == 0)
        def _():
            acc_ref[...] = part

        @pl.when(jnp.logical_and(k > 0, k < nk - 1))
        def _():
            acc_ref[...] += part

        @pl.when(k == nk - 1)
        def _():
            finish(acc_ref[...] + part)

    return pl.pallas_call(
        kern, name=name,
        grid=(M // tm, N // tn, nk),
        in_specs=in_specs,
        out_specs=out_spec,
        out_shape=out_shape,
        scratch_shapes=[pltpu.VMEM((tm, tn), F32)] if nk > 1 else [],
        compiler_params=_cparams(("parallel", "parallel", "arbitrary")),
    )(*operands)


def _tile_call(body, *, rows, prevs=(), nexts=(), fulls=(), row_outs=(), acc_outs=(), tm, T, name):
    M = rows[0].shape[0]
    n_tiles, tps, hb = M // tm, T // tm, tm // HALO
    n_halo_blocks = M // HALO
    nr, npv, nnx, nf, nro, nac = len(rows), len(prevs), len(nexts), len(fulls), len(row_outs), len(acc_outs)

    def kern(*refs):
        i = pl.program_id(0)
        row_refs = refs[:nr]
        prev_refs = refs[nr:nr + npv]
        next_refs = refs[nr + npv:nr + npv + nnx]
        full_refs = refs[nr + npv + nnx:nr + npv + nnx + nf]
        out_refs = refs[nr + npv + nnx + nf:nr + npv + nnx + nf + nro]
        acc_refs = refs[nr + npv + nnx + nf + nro:]
        seq_first = (i % tps) == 0
        seq_last = (i % tps) == (tps - 1)
        outs, accs = body(row_refs, prev_refs, next_refs, full_refs, seq_first, seq_last)
        for r, o in zip(out_refs, outs, strict=True):
            r[...] = o.astype(r.dtype)
        if nac:
            @pl.when(i == 0)
            def _():
                for r in acc_refs:
                    r[...] = jnp.zeros_like(r)
            for r, a in zip(acc_refs, accs, strict=True):
                r[...] += a.astype(F32)

    in_specs = [pl.BlockSpec((tm, a.shape[1]), lambda i: (i, 0)) for a in rows]
    in_specs += [pl.BlockSpec((HALO, rows[k].shape[1]), lambda i: (jnp.maximum(i * hb - 1, 0), 0)) for k in prevs]
    in_specs += [pl.BlockSpec((HALO, rows[k].shape[1]), lambda i: (jnp.minimum((i + 1) * hb, n_halo_blocks - 1), 0))
                 for k in nexts]
    in_specs += [pl.BlockSpec(f.shape, lambda i: (0, 0)) for f in fulls]
    out_specs = [pl.BlockSpec((tm, w), lambda i: (i, 0)) for (w, _) in row_outs]
    out_specs += [pl.BlockSpec(s, lambda i: (0, 0)) for s in acc_outs]
    out_shape = [jax.ShapeDtypeStruct((M, w), dt) for (w, dt) in row_outs]
    out_shape += [jax.ShapeDtypeStruct(s, F32) for s in acc_outs]
    operands = list(rows) + [rows[k] for k in prevs] + [rows[k] for k in nexts] + list(fulls)
    res = pl.pallas_call(
        kern, name=name, grid=(n_tiles,), in_specs=in_specs, out_specs=out_specs, out_shape=out_shape,
        compiler_params=_cparams(("arbitrary",)),
    )(*operands)
    return res[:nro], res[nro:]


def _f(ref):
    return ref[...].astype(F32)


def _sigmoid(x):
    return 1.0 / (1.0 + jnp.exp(-x))


def _softplus(x):
    return jnp.maximum(x, 0.0) + jnp.log(1.0 + jnp.exp(-jnp.abs(x)))


def _neg_expm1(x):
    series = -x * (1.0 + x * (0.5 + x * (1.0 / 6.0) * (1.0 + 0.25 * x)))
    return jnp.where(x > -0.01, series, 1.0 - jnp.exp(x))


def _gelu(x):
    return 0.5 * x * (1.0 + jnp.tanh(0.7978845608028654 * (x + 0.044715 * x * x * x)))


def _rms(x, g):
    return x * lax.rsqrt(jnp.mean(x * x, axis=-1, keepdims=True) + RMS_EPS) * g


@jax.custom_vjp
def _bdot(x, w):
    return jnp.dot(x.astype(MXU_DTYPE), w.astype(MXU_DTYPE), preferred_element_type=F32)


def _bdot_fwd(x, w):
    return _bdot(x, w), (x, w)


def _bdot_bwd(res, ct):
    x, w = res
    ctb = ct.astype(MXU_DTYPE)
    dx = lax.dot_general(ctb, w.astype(MXU_DTYPE), (((1,), (1,)), ((), ())), preferred_element_type=F32)
    dw = lax.dot_general(x.astype(MXU_DTYPE), ctb, (((0,), (0,)), ((), ())), preferred_element_type=F32)
    return dx.astype(x.dtype), dw.astype(w.dtype)


_bdot.defvjp(_bdot_fwd, _bdot_bwd)


@jax.custom_vjp
def _head_sum(x, e, et):
    s = jnp.dot(_split_lhs(x), e, preferred_element_type=F32)
    return jnp.dot(_split_lhs(s), et, preferred_element_type=F32)


def _head_sum_fwd(x, e, et):
    return _head_sum(x, e, et), (e, et)


def _head_sum_bwd(res, ct):
    e, et = res
    return _head_sum(ct, e, et), jnp.zeros_like(e), jnp.zeros_like(et)


_head_sum.defvjp(_head_sum_fwd, _head_sum_bwd)


def _shift_down(main, prev, s, seq_first):
    prev = jnp.where(seq_first, 0.0, prev)
    ext = jnp.concatenate([prev, main], axis=0)
    return pltpu.roll(ext, s, 0)[HALO:]


def _shift_up(main, nxt, s, seq_last):
    nxt = jnp.where(seq_last, 0.0, nxt)
    ext = jnp.concatenate([main, nxt], axis=0)
    n = ext.shape[0]
    return pltpu.roll(ext, n - s, 0)[:n - HALO]


def _colsum(x):
    return jnp.sum(x, axis=0, keepdims=True)


def _pad8(x):
    k = x.shape[0]
    return jnp.concatenate([x, jnp.zeros((8 - k, x.shape[1]), x.dtype)], axis=0) if k < 8 else x


def _rms_fwd(x, g, T, name):
    D = x.shape[1]

    def body(rows, prevs, nexts, fulls, sf, sl):
        return [_rms(_f(rows[0]), _f(fulls[0]))], []

    (h,), _ = _tile_call(body, rows=[x], fulls=[g.reshape(1, D)], row_outs=[(D, BF16)], tm=min(512, T), T=T, name=name)
    return h


def _rms_bwd(x, g, dh, dres, T, name):
    D = x.shape[1]

    def body(rows, prevs, nexts, fulls, sf, sl):
        _, vjp = jax.vjp(_rms, _f(rows[0]), _f(fulls[0]))
        dx, dg = vjp(_f(rows[1]))
        return [dx + _f(rows[2])], [dg]

    (dx,), (dg,) = _tile_call(body, rows=[x, dh, dres], fulls=[g.reshape(1, D)], row_outs=[(D, F32)],
                              acc_outs=[(1, D)], tm=min(512, T), T=T, name=name)
    return dx, dg


def _ffn_conv(u1, prev, cw, cb, sf):
    k = cw.shape[0]
    out = cb + u1 * cw[k - 1:k]
    for j in range(k - 1):
        out = out + _shift_down(u1, prev, k - 1 - j, sf) * cw[j:j + 1]
    return out


def _ffn_fwd(x, p, T, tag):
    M, D = x.shape
    F = p["w_down"].shape[0]
    hf = _rms_fwd(x, p["norm"], T, f"ffn{tag}_norm")
    uf = _matmul(hf, p["w_up"], out_dtype=BF16, name=f"ffn{tag}_up")

    def body(rows, prevs, nexts, fulls, sf, sl):
        u = rows[0]
        gate = _ffn_conv(u[:, :F].astype(F32), prevs[0][:, :F].astype(F32), _f(fulls[0]), _f(fulls[1]), sf)
        return [_gelu(gate) * u[:, F:].astype(F32)], []

    (hid,), _ = _tile_call(body, rows=[uf], prevs=[0], fulls=[p["conv_w"], p["conv_b"].reshape(1, F)],
                           row_outs=[(F, BF16)], tm=min(256, T), T=T, name=f"ffn{tag}_act")
    y = _matmul(hid, p["w_down"], residual=x, name=f"ffn{tag}_down")
    return y, (x, hf, uf, hid)


def _ffn_bwd(dy, saved, p, T, tag):
    x, hf, uf, hid = saved
    M, D = x.shape
    F = p["w_down"].shape[0]
    K = p["conv_w"].shape[0]
    d_hid = _matmul(dy, p["w_down"], mode="nt", out_dtype=BF16, name=f"ffn{tag}_down_dx")
    d_w_down = _matmul(hid, dy, mode="tn", out_dtype=BF16, name=f"ffn{tag}_down_dw")

    def body(rows, prevs, nexts, fulls, sf, sl):
        u, dh = rows
        cw, cb = _f(fulls[0]), _f(fulls[1])
        tm = u.shape[0]
        u1c, u1p, u1n = u[:, :F].astype(F32), prevs[0][:, :F].astype(F32), nexts[0][:, :F].astype(F32)
        u1 = jnp.concatenate([u1c, u1n], axis=0)
        u2 = jnp.concatenate([u[:, F:].astype(F32), nexts[0][:, F:].astype(F32)], axis=0)
        dhid = jnp.concatenate([_f(dh), _f(nexts[1])], axis=0)
        gate = _ffn_conv(u1, u1p, cw, cb, sf)
        (act, dact) = jax.jvp(_gelu, (gate,), (jnp.ones_like(gate),))
        d_gate = dhid * u2 * dact
        d_u2 = (dhid * act)[:tm]
        rowid = lax.broadcasted_iota(jnp.int32, d_gate.shape, 0)
        d_gate = jnp.where(jnp.logical_and(sl, rowid >= tm), 0.0, d_gate)
        dgc, dgn = d_gate[:tm], d_gate[tm:]
        d_u1 = dgc * cw[K - 1:K]
        dws = []
        for j in range(K - 1):
            s = K - 1 - j
            d_u1 = d_u1 + _shift_up(dgc, dgn, s, False) * cw[j:j + 1]
            dws.append(_colsum(dgc * _shift_down(u1c, u1p, s, sf)))
        dws.append(_colsum(dgc * u1c))
        d_cw = _pad8(jnp.concatenate(dws, axis=0))
        return [jnp.concatenate([d_u1, d_u2], axis=1)], [d_cw, _colsum(dgc)]

    (d_uf,), (d_cw, d_cb) = _tile_call(
        body, rows=[uf, d_hid], prevs=[0], nexts=[0, 1], fulls=[p["conv_w"], p["conv_b"].reshape(1, F)],
        row_outs=[(2 * F, BF16)], acc_outs=[(8, F), (1, F)], tm=min(256, T), T=T, name=f"ffn{tag}_act_bwd")
    d_hf = _matmul(d_uf, p["w_up"], mode="nt", name=f"ffn{tag}_up_dx")
    d_w_up = _matmul(hf, d_uf, mode="tn", out_dtype=BF16, name=f"ffn{tag}_up_dw", out_cols_by_chip=True)
    dx, d_norm = _rms_bwd(x, p["norm"], d_hf, dy, T, f"ffn{tag}_norm_bwd")
    grads = {"norm": d_norm.reshape(D), "w_up": d_w_up, "conv_w": d_cw[:K], "conv_b": d_cb.reshape(F), "w_down": d_w_down}
    return dx, grads


def _lru_conv(u2, prev, cw, cb, sf):
    return _ffn_conv(u2, prev, cw, cb, sf)


def _lru_pre(xr, wg, gb):
    D, GB = xr.shape[1], wg.shape[1] // 2
    parts = [jnp.dot(xr[:, r * GB:(r + 1) * GB].astype(MXU_DTYPE), wg[r * GB:(r + 1) * GB], preferred_element_type=F32)
             for r in range(D // GB)]
    return jnp.concatenate([q[:, :GB] for q in parts] + [q[:, GB:] for q in parts], axis=1) + gb


def _lru_pre_t(xr, dpre, wg, with_dw):
    D, GB = xr.shape[1], wg.shape[1] // 2
    dx, dw = [], []
    for r in range(D // GB):
        dp = jnp.concatenate([dpre[:, r * GB:(r + 1) * GB], dpre[:, D + r * GB:D + (r + 1) * GB]], axis=1).astype(MXU_DTYPE)
        dx.append(lax.dot_general(dp, wg[r * GB:(r + 1) * GB], (((1,), (1,)), ((), ())), preferred_element_type=F32))
        if with_dw:
            dw.append(lax.dot_general(xr[:, r * GB:(r + 1) * GB].astype(MXU_DTYPE), dp, (((0,), (0,)), ((), ())),
                                      preferred_element_type=F32))
    return jnp.concatenate(dx, axis=1), (jnp.concatenate(dw, axis=0) if with_dw else None)


def _lru_gates(xr, pre, lam):
    D = xr.shape[1]
    r_gate, i_gate = _sigmoid(pre[:, :D]), _sigmoid(pre[:, D:])
    log_a = -LRU_C * r_gate * _softplus(-lam)
    a = jnp.exp(log_a)
    mult = jnp.sqrt(_neg_expm1(2.0 * log_a))
    return a, mult * (i_gate * xr)


def _lru_scan(a, b, B, T):
    M, D = a.shape
    cw = _pick(D, 256)
    ng = T // 8

    def kern(a_ref, b_ref, o_ref):
        row = lax.broadcasted_iota(jnp.int32, (8, cw), 0)

        def step(g, carry):
            sl = pl.ds(pl.multiple_of(g * 8, 8), 8)
            a8, b8 = a_ref[sl, :], b_ref[sl, :]
            for s in (1, 2, 4):
                a_sh = jnp.where(row >= s, pltpu.roll(a8, s, 0), 1.0)
                b_sh = jnp.where(row >= s, pltpu.roll(b8, s, 0), 0.0)
                b8 = a8 * b_sh + b8
                a8 = a8 * a_sh
            h8 = a8 * carry + b8
            o_ref[sl, :] = h8
            return jnp.broadcast_to(h8[7:8, :], (8, cw))

        lax.fori_loop(0, ng, step, jnp.zeros((8, cw), F32))

    spec = pl.BlockSpec((T, cw), lambda b, c: (b, c))
    return pl.pallas_call(
        kern, name="lru_scan", grid=(B, D // cw), in_specs=[spec, spec], out_specs=spec,
        out_shape=jax.ShapeDtypeStruct((M, D), F32), compiler_params=_cparams(("parallel", "parallel")),
    )(a, b)


def _lru_scan_bwd(a, hs, dhs, B, T):
    M, D = a.shape
    cw = _pick(D, 256)
    ng = T // 8

    def kern(a_ref, h_ref, d_ref, g_ref, da_ref):
        row = lax.broadcasted_iota(jnp.int32, (8, cw), 0)

        def step(k, carry):
            g_next, a_next = carry
            g = ng - 1 - k
            sl = pl.ds(pl.multiple_of(g * 8, 8), 8)
            a8, d8, h8 = a_ref[sl, :], d_ref[sl, :], h_ref[sl, :]
            c8 = jnp.where(row < 7, pltpu.roll(a8, 7, 0), a_next)
            for s in (1, 2, 4):
                d_sh = jnp.where(row < 8 - s, pltpu.roll(d8, 8 - s, 0), 0.0)
                c_sh = jnp.where(row < 8 - s, pltpu.roll(c8, 8 - s, 0), 1.0)
                d8 = d8 + c8 * d_sh
                c8 = c8 * c_sh
            G8 = d8 + c8 * g_next
            gp = jnp.maximum(g - 1, 0)
            hp8 = h_ref[pl.ds(pl.multiple_of(gp * 8, 8), 8), :]
            hp_last = jnp.where(g > 0, jnp.broadcast_to(hp8[7:8, :], (8, cw)), 0.0)
            hprev = jnp.where(row >= 1, pltpu.roll(h8, 1, 0), hp_last)
            g_ref[sl, :] = G8
            da_ref[sl, :] = G8 * hprev
            return jnp.broadcast_to(G8[0:1, :], (8, cw)), jnp.broadcast_to(a8[0:1, :], (8, cw))

        z = jnp.zeros((8, cw), F32)
        lax.fori_loop(0, ng, step, (z, z))

    spec = pl.BlockSpec((T, cw), lambda b, c: (b, c))
    sh = jax.ShapeDtypeStruct((M, D), F32)
    return pl.pallas_call(
        kern, name="lru_scan_bwd", grid=(B, D // cw), in_specs=[spec, spec, spec], out_specs=[spec, spec],
        out_shape=[sh, sh], compiler_params=_cparams(("parallel", "parallel")),
    )(a, hs, dhs)


def _lru_fwd(x, p, B, T):
    M, D = x.shape
    h0 = _rms_fwd(x, p["norm"], T, "lru_norm")
    u0 = _matmul(h0, p["w_in"], bias=p["b_in"], name="lru_in")
    fulls = [p["conv_w"], p["conv_b"].reshape(1, D), p["wbd"], p["gate_b"].reshape(1, 2 * D), p["lam"].reshape(1, D)]

    def body(rows, prevs, nexts, fulls, sf, sl):
        xr = _lru_conv(rows[0][:, D:], prevs[0][:, D:], _f(fulls[0]), _f(fulls[1]), sf)
        a, bt = _lru_gates(xr, _lru_pre(xr, fulls[2][...], _f(fulls[3])), _f(fulls[4]))
        return [a, bt], []

    (a, bt), _ = _tile_call(body, rows=[u0], prevs=[0], fulls=fulls, row_outs=[(D, F32), (D, F32)],
                            tm=min(256, T), T=T, name="lru_gates")
    hs = _lru_scan(a, bt, B, T)

    def body2(rows, prevs, nexts, fulls, sf, sl):
        return [rows[0][...] * _gelu(rows[1][:, :D])], []

    (out,), _ = _tile_call(body2, rows=[hs, u0], row_outs=[(D, BF16)], tm=min(512, T), T=T, name="lru_mix")
    y = _matmul(out, p["w_out"], bias=p["b_out"], residual=x, name="lru_out")
    return y, (x, h0, u0, a, hs, out)


def _lru_bwd(dy, saved, p, B, T):
    x, h0, u0, a, hs, out = saved
    M, D = x.shape
    K = p["conv_w"].shape[0]
    d_out = _matmul(dy, p["w_out"], mode="nt", name="lru_out_dx")
    d_w_out = _matmul(out, dy, mode="tn", out_dtype=BF16, name="lru_out_dw")

    def body(rows, prevs, nexts, fulls, sf, sl):
        do, h, u, dyv = rows[0][...], rows[1][...], rows[2][:, :D], rows[3][...]
        act, dact = jax.jvp(_gelu, (u,), (jnp.ones_like(u),))
        return [do * act, do * h * dact], [_colsum(dyv)]

    (d_hs, d_u1), (d_b_out,) = _tile_call(body, rows=[d_out, hs, u0, dy], row_outs=[(D, F32), (D, F32)],
                                          acc_outs=[(1, D)], tm=min(512, T), T=T, name="lru_mix_bwd")
    g_b, d_a = _lru_scan_bwd(a, hs, d_hs, B, T)
    fulls = [p["conv_w"], p["conv_b"].reshape(1, D), p["wbd"], p["gate_b"].reshape(1, 2 * D), p["lam"].reshape(1, D)]

    def body3(rows, prevs, nexts, fulls, sf, sl):
        u, gb_c, da_c, du1 = rows
        cw, cb, wbd, gbias, lam = _f(fulls[0]), _f(fulls[1]), fulls[2][...], _f(fulls[3]), _f(fulls[4])
        tm = u.shape[0]
        u2c, u2p, u2n = u[:, D:], prevs[0][:, D:], nexts[0][:, D:]
        xr_c = _lru_conv(u2c, u2p, cw, cb, sf)
        xr_n = _lru_conv(u2n, u2c[tm - HALO:], cw, cb, False)
        _, vjp_c = jax.vjp(_lru_gates, xr_c, _lru_pre(xr_c, wbd, gbias), lam)
        dxr_c, dpre_c, d_lam = vjp_c((da_c[...], gb_c[...]))
        dxr_add, d_wbd = _lru_pre_t(xr_c, dpre_c, wbd, True)
        dxr_c = dxr_c + dxr_add
        d_gbias = _colsum(dpre_c)
        _, vjp_n = jax.vjp(lambda t, q: _lru_gates(t, q, lam), xr_n, _lru_pre(xr_n, wbd, gbias))
        dxr_n, dpre_n = vjp_n((nexts[2][...], nexts[1][...]))
        dxr_n = dxr_n + _lru_pre_t(xr_n, dpre_n, wbd, False)[0]
        d_u2 = dxr_c * cw[K - 1:K]
        dws = []
        for j in range(K - 1):
            s = K - 1 - j
            d_u2 = d_u2 + _shift_up(dxr_c, dxr_n, s, sl) * cw[j:j + 1]
            dws.append(_colsum(dxr_c * _shift_down(u2c, u2p, s, sf)))
        dws.append(_colsum(dxr_c * u2c))
        d_u = jnp.concatenate([du1[...], d_u2], axis=1)
        return [d_u], [_pad8(jnp.concatenate(dws, axis=0)), _colsum(dxr_c), d_wbd, d_gbias, d_lam,
                       _colsum(d_u)]

    (d_u0,), (d_cw, d_cb, d_wbd, d_gb, d_lam, d_b_in) = _tile_call(
        body3, rows=[u0, g_b, d_a, d_u1], prevs=[0], nexts=[0, 1, 2], fulls=fulls, row_outs=[(2 * D, BF16)],
        acc_outs=[(8, D), (1, D), p["wbd"].shape, (1, 2 * D), (1, D), (1, 2 * D)], tm=min(256, T), T=T,
        name="lru_gates_bwd")
    d_h0 = _matmul(d_u0, p["w_in"], mode="nt", name="lru_in_dx")
    d_w_in = _matmul(h0, d_u0, mode="tn", out_dtype=BF16, name="lru_in_dw", out_cols_by_chip=True)
    dx, d_norm = _rms_bwd(x, p["norm"], d_h0, dy, T, "lru_norm_bwd")
    grads = {"norm": d_norm.reshape(D), "w_in": d_w_in, "b_in": d_b_in.reshape(2 * D), "conv_w": d_cw[:K],
             "conv_b": d_cb.reshape(D), "wbd": d_wbd, "gate_b": d_gb.reshape(2 * D), "lam": d_lam.reshape(D),
             "w_out": d_w_out, "b_out": d_b_out.reshape(D)}
    return dx, grads


def _rwkv_mix(xc, xp, norm, mix, sf):
    h = _rms(xc, norm)
    hp = _rms(xp, norm)
    xx = _shift_down(h, hp, 1, sf) - h
    return h, xx


def _rwkv_pre(k, xw, xa, xg, w0, w1, w2, a0, a1, a2, g1, g2, k_k, k_a, e, et):
    wl = -_softplus(-(w0 + _bdot(jnp.tanh(_bdot(xw, w1)), w2))) - 0.5
    decay = jnp.exp(-jnp.exp(wl))
    a = _sigmoid(a0 + _bdot(_bdot(xa, a1), a2))
    g = _bdot(_sigmoid(_bdot(xg, g1)), g2)
    kk = k * k_k
    nrm = jnp.sqrt(_head_sum(kk * kk, e, et))
    kk = kk / jnp.maximum(nrm, 1e-12)
    k2 = k * (1.0 + (a - 1.0) * k_a)
    return decay, k2, -kk, kk * a, g


def _rwkv_post(y, r, k2, v, g, ln_w, ln_b, r_k, e, et):
    inv = 1.0 / HEAD
    mu = _head_sum(y, e, et) * inv
    yc = y - mu
    var = _head_sum(yc * yc, e, et) * inv
    yn = yc * lax.rsqrt(var + GN_EPS) * ln_w + ln_b
    bonus = _head_sum(r * k2 * r_k, e, et) * v
    return (yn + bonus) * g


def _seg_lane_sums(x, lo_mask):
    s0 = jnp.sum(jnp.where(lo_mask, x, 0.0), axis=1, keepdims=True)
    s1 = jnp.sum(jnp.where(lo_mask, 0.0, x), axis=1, keepdims=True)
    return s0, s1


def _seg_lane_sum(x, lo_mask):
    s0, s1 = _seg_lane_sums(x, lo_mask)
    return jnp.where(lo_mask, s0, s1)


def _pair_consts():
    lane = lax.broadcasted_iota(jnp.int32, (HEAD, 128), 1)
    sub = lax.broadcasted_iota(jnp.int32, (HEAD, 128), 0)
    return lane < HEAD, (jnp.bitwise_and(lane, HEAD - 1) == sub).astype(F32)


def _pair_ones():
    head = jnp.arange(128) // HEAD
    return (head[:, None] == head[None, :]).astype(MXU_DTYPE)


def _split_lhs(x):
    hi = x.astype(MXU_DTYPE)
    return jnp.concatenate([hi, (x - hi.astype(F32)).astype(MXU_DTYPE)], axis=1)


def _spread_lhs(diag, row):
    return (diag * row).astype(MXU_DTYPE)


def _rwkv_scan(r, w, k, v, a, b, B, T):
    M, D = r.shape
    HP, PG, TC, NC, chains = _scan_plan(B, T, D, pairs=2, chunk=64)
    NS = len(chains) * 8
    NG = TC // 8

    def kern(r_ref, w_ref, k_ref, v_ref, a_ref, b_ref, ones_ref, y_ref, st_ref, S_ref, lv_ref, rv_ref, ly_ref, ry_ref):
        c = pl.program_id(1)

        @pl.when(c == 0)
        def _():
            S_ref[...] = jnp.zeros_like(S_ref)

        lo, diag = _pair_consts()
        row8 = lax.broadcasted_iota(jnp.int32, (8, 128), 0)

        def blk(idx):
            return pl.ds(idx * HEAD, HEAD)

        def rows_of(gi):
            return pl.ds(pl.multiple_of(gi * 8, 8), 8)

        def spread(gi, slot):
            for ci, (bi, p) in enumerate(chains):
                v8 = v_ref[bi, rows_of(gi), p * 128:(p + 1) * 128]
                for j in range(8):
                    lv_ref[slot, blk(ci * 8 + j), :] = _spread_lhs(diag, v8[j:j + 1, :])
            rv_ref[slot] = jnp.dot(lv_ref[slot], ones_ref[...], preferred_element_type=F32)

        def recur(gi, slot):
            sl = rows_of(gi)
            tiles = [[ref[bi, sl, p * 128:(p + 1) * 128] for ref in (r_ref, w_ref, k_ref, a_ref, b_ref)]
                     for bi, p in chains]
            S = [S_ref[ci] for ci in range(len(chains))]
            for j in range(8):
                for ci, (bi, p) in enumerate(chains):
                    r8, w8, k8, a8, b8 = tiles[ci]
                    idx = ci * 8 + j
                    st_ref[p, bi, gi * 8 + j] = S[ci]
                    sa = _seg_lane_sum(S[ci] * a8[j:j + 1, :], lo)
                    S[ci] = S[ci] * w8[j:j + 1, :] + sa * b8[j:j + 1, :] + rv_ref[slot, blk(idx), :] * k8[j:j + 1, :]
                    ly_ref[slot, blk(idx), :] = (S[ci] * r8[j:j + 1, :]).astype(MXU_DTYPE)
            for ci in range(len(chains)):
                S_ref[ci] = S[ci]

        def emit(gi, slot):
            ry_ref[slot] = jnp.dot(ly_ref[slot], ones_ref[...], preferred_element_type=F32)
            for ci, (bi, p) in enumerate(chains):
                y8 = jnp.zeros((8, 128), F32)
                for j in range(8):
                    y8 = jnp.where(row8 == j, _colsum(diag * ry_ref[slot, blk(ci * 8 + j), :]), y8)
                y_ref[bi, rows_of(gi), p * 128:(p + 1) * 128] = y8

        spread(0, 0)
        ly_ref[1] = jnp.zeros_like(ly_ref[1])

        def two_groups(m, _):
            g0, g1 = 2 * m, 2 * m + 1
            spread(g1, 1)
            recur(g0, 0)
            emit(jnp.maximum(g0 - 1, 0), 1)
            spread(jnp.minimum(g1 + 1, NG - 1), 0)
            recur(g1, 1)
            emit(g0, 0)
            return 0

        lax.fori_loop(0, NG // 2, two_groups, 0)
        emit(NG - 1, 1)

    spec = pl.BlockSpec((B, TC, 128 * PG), lambda hp, c: (0, c, hp))
    st_spec = pl.BlockSpec((PG, B, TC, HEAD, 128), lambda hp, c: (hp, 0, c, 0, 0))
    y, st = pl.pallas_call(
        kern, name="rwkv_scan", grid=(HP // PG, NC),
        in_specs=[spec] * 6 + [pl.BlockSpec((128, 128), lambda hp, c: (0, 0))], out_specs=[spec, st_spec],
        out_shape=[jax.ShapeDtypeStruct((B, T, D), F32), jax.ShapeDtypeStruct((HP, B, T, HEAD, 128), F32)],
        scratch_shapes=[pltpu.VMEM((len(chains), HEAD, 128), F32),
                        pltpu.VMEM((2, NS * HEAD, 128), MXU_DTYPE), pltpu.VMEM((2, NS * HEAD, 128), F32),
                        pltpu.VMEM((2, NS * HEAD, 128), MXU_DTYPE), pltpu.VMEM((2, NS * HEAD, 128), F32)],
        compiler_params=_cparams(("parallel", "arbitrary")),
    )(*[x.reshape(B, T, D) for x in (r, w, k, v, a, b)], _pair_ones())
    return y.reshape(M, D), st


def _scan_plan(B, T, D, pairs, chunk):
    HP = D // 128
    PG = pairs if HP % pairs == 0 else 1
    TC = min(chunk, T)
    assert TC % 16 == 0 and T % TC == 0
    return HP, PG, TC, T // TC, [(bi, p) for bi in range(B) for p in range(PG)]


def _rwkv_scan_bwd(r, w, k, v, a, b, st, dy, B, T):
    M, D = r.shape
    HP, PG, TC, NC, chains = _scan_plan(B, T, D, pairs=1, chunk=128)
    NS = len(chains) * 8

    NG = TC // 8

    def kern(r_ref, w_ref, k_ref, v_ref, a_ref, b_ref, st_ref, dy_ref, ones_ref,
             dr_ref, dw_ref, dk_ref, dv_ref, da_ref, db_ref, dS_ref, lp_ref, rp_ref, lq_ref, rq_ref):
        c = pl.program_id(1)

        @pl.when(c == 0)
        def _():
            dS_ref[...] = jnp.zeros_like(dS_ref)

        lo, diag = _pair_consts()
        row8 = lax.broadcasted_iota(jnp.int32, (8, 128), 0)

        def blk(idx):
            return pl.ds(idx * HEAD, HEAD)

        def rows_of(gi):
            return pl.ds(pl.multiple_of(gi * 8, 8), 8)

        def spread(gi, slot):
            sl = rows_of(gi)
            for ci, (bi, p) in enumerate(chains):
                lanes = slice(p * 128, (p + 1) * 128)
                v8, dy8, a8 = v_ref[bi, sl, lanes], dy_ref[bi, sl, lanes], a_ref[bi, sl, lanes]
                for j in range(8):
                    idx = ci * 8 + j
                    lp_ref[slot, blk(idx), :] = _spread_lhs(diag, v8[j:j + 1, :])
                    lp_ref[slot, blk(NS + idx), :] = _spread_lhs(diag, dy8[j:j + 1, :])
                    lp_ref[slot, blk(2 * NS + idx), :] = (st_ref[p, bi, gi * 8 + j] * a8[j:j + 1, :]).astype(MXU_DTYPE)
            rp_ref[slot] = jnp.dot(lp_ref[slot], ones_ref[...], preferred_element_type=F32)

        def recur(gi, slot):
            sl = rows_of(gi)
            tiles = [[ref[bi, sl, p * 128:(p + 1) * 128] for ref in (r_ref, w_ref, k_ref, a_ref, b_ref)]
                     for bi, p in chains]
            dS = [dS_ref[ci] for ci in range(len(chains))]
            acc = [[jnp.zeros((8, 128), F32) for _ in range(5)] for _ in chains]
            St = [None] * len(chains)
            for j in range(7, -1, -1):
                for ci, (bi, p) in enumerate(chains):
                    r8, w8, k8, a8, b8 = tiles[ci]
                    rj, wj, kj, aj, bj = r8[j:j + 1, :], w8[j:j + 1, :], k8[j:j + 1, :], a8[j:j + 1, :], b8[j:j + 1, :]
                    idx = ci * 8 + j
                    Sp = st_ref[p, bi, gi * 8 + j]
                    vb, dyb, sa = rp_ref[slot, blk(idx), :], rp_ref[slot, blk(NS + idx), :], rp_ref[slot, blk(2 * NS + idx), :]
                    if j == 7:
                        St[ci] = Sp * wj + sa * bj + vb * kj
                    d = dS[ci] + dyb * rj
                    dsa = _seg_lane_sum(d * bj, lo)
                    lq_ref[slot, blk(idx), :] = (d * kj).astype(MXU_DTYPE)
                    rows = (_colsum(St[ci] * dyb), _colsum(d * Sp), _colsum(d * vb), _colsum(Sp * dsa), _colsum(d * sa))
                    acc[ci] = [jnp.where(row8 == j, rw, a8_) for rw, a8_ in zip(rows, acc[ci], strict=True)]
                    dS[ci] = d * wj + dsa * aj
                    St[ci] = Sp
            for ci, (bi, p) in enumerate(chains):
                dS_ref[ci] = dS[ci]
                for ref, a8_ in zip((dr_ref, dw_ref, dk_ref, da_ref, db_ref), acc[ci], strict=True):
                    ref[bi, sl, p * 128:(p + 1) * 128] = a8_

        def emit(gi, slot):
            rq_ref[slot] = jnp.dot(lq_ref[slot], ones_ref[...], preferred_element_type=F32)
            for ci, (bi, p) in enumerate(chains):
                dv8 = jnp.zeros((8, 128), F32)
                for j in range(8):
                    dv8 = jnp.where(row8 == j, _colsum(diag * rq_ref[slot, blk(ci * 8 + j), :]), dv8)
                dv_ref[bi, rows_of(gi), p * 128:(p + 1) * 128] = dv8

        spread(NG - 1, 0)
        lq_ref[1] = jnp.zeros_like(lq_ref[1])

        def two_groups(m, _):
            g0, g1 = NG - 1 - 2 * m, NG - 2 - 2 * m
            spread(g1, 1)
            recur(g0, 0)
            emit(jnp.minimum(g0 + 1, NG - 1), 1)
            spread(jnp.maximum(g1 - 1, 0), 0)
            recur(g1, 1)
            emit(g0, 0)
            return 0

        lax.fori_loop(0, NG // 2, two_groups, 0)
        emit(0, 1)

    spec = pl.BlockSpec((B, TC, 128 * PG), lambda hp, c: (0, NC - 1 - c, hp))
    st_spec = pl.BlockSpec((PG, B, TC, HEAD, 128), lambda hp, c: (hp, 0, NC - 1 - c, 0, 0))
    sh = jax.ShapeDtypeStruct((B, T, D), F32)
    outs = pl.pallas_call(
        kern, name="rwkv_scan_bwd", grid=(HP // PG, NC),
        in_specs=[spec] * 6 + [st_spec, spec, pl.BlockSpec((128, 128), lambda hp, c: (0, 0))], out_specs=[spec] * 6,
        out_shape=[sh] * 6,
        scratch_shapes=[pltpu.VMEM((len(chains), HEAD, 128), F32),
                        pltpu.VMEM((2, 3 * NS * HEAD, 128), MXU_DTYPE), pltpu.VMEM((2, 3 * NS * HEAD, 128), F32),
                        pltpu.VMEM((2, NS * HEAD, 128), MXU_DTYPE), pltpu.VMEM((2, NS * HEAD, 128), F32)],
        compiler_params=_cparams(("parallel", "arbitrary")),
    )(*[x.reshape(B, T, D) for x in (r, w, k, v, a, b)], st, dy.reshape(B, T, D), _pair_ones())
    return [o.reshape(M, D) for o in outs]


def _head_mats(D):
    ch = jnp.arange(D) // HEAD
    e = (ch[:, None] == jnp.arange(128)[None, :]).astype(MXU_DTYPE)
    return jnp.concatenate([e, e], axis=0), jnp.concatenate([e.T, e.T], axis=0)


def _rwkv_fwd(x, p, B, T):
    M, D = x.shape
    e, et = _head_mats(D)
    norm = p["norm"].reshape(1, D)

    def body(rows, prevs, nexts, fulls, sf, sl):
        h, xx = _rwkv_mix(rows[0][...], prevs[0][...], _f(fulls[0]), None, sf)
        mix = _f(fulls[1])
        return [h + xx * mix[i:i + 1] for i in range(6)], []

    xs, _ = _tile_call(body, rows=[x], prevs=[0], fulls=[norm, _pad8(p["mix"])], row_outs=[(D, BF16)] * 6,
                       tm=min(256, T), T=T, name="rwkv_mix")
    r = _matmul(xs[0], p["w_r"], name="rwkv_r")
    k = _matmul(xs[1], p["w_k"], name="rwkv_k")
    v = _matmul(xs[2], p["w_v"], name="rwkv_v")
    pre_fulls = [p["w0"].reshape(1, D), p["w1"], p["w2"], p["a0"].reshape(1, D), p["a1"], p["a2"], p["g1"], p["g2"],
                 p["k_k"].reshape(1, D), p["k_a"].reshape(1, D), e, et]

    def body2(rows, prevs, nexts, fulls, sf, sl):
        outs = _rwkv_pre(rows[0][...], _f(rows[1]), _f(rows[2]), _f(rows[3]), *[f[...] for f in fulls])
        return list(outs), []

    (decay, k2, kkn, bb, g), _ = _tile_call(body2, rows=[k, xs[3], xs[4], xs[5]], fulls=pre_fulls,
                                            row_outs=[(D, F32)] * 5, tm=min(256, T), T=T, name="rwkv_pre")
    y, st = _rwkv_scan(r, decay, k2, v, kkn, bb, B, T)
    post_fulls = [p["ln_w"].reshape(1, D), p["ln_b"].reshape(1, D), p["r_k"].reshape(1, D), e, et]

    def body3(rows, prevs, nexts, fulls, sf, sl):
        return [_rwkv_post(*[rr[...] for rr in rows], *[f[...] for f in fulls])], []

    (z,), _ = _tile_call(body3, rows=[y, r, k2, v, g], fulls=post_fulls, row_outs=[(D, BF16)], tm=min(256, T), T=T,
                         name="rwkv_post")
    out = _matmul(z, p["w_out"], residual=x, name="rwkv_out")
    return out, (x, xs, r, k, v, decay, k2, kkn, bb, g, y, st, z)


def _rwkv_bwd(dout, saved, p, B, T):
    x, xs, r, k, v, decay, k2, kkn, bb, g, y, st, z = saved
    M, D = x.shape
    e, et = _head_mats(D)
    d_z = _matmul(dout, p["w_out"], mode="nt", name="rwkv_out_dx")
    d_w_out = _matmul(z, dout, mode="tn", out_dtype=BF16, name="rwkv_out_dw")
    post_fulls = [p["ln_w"].reshape(1, D), p["ln_b"].reshape(1, D), p["r_k"].reshape(1, D), e, et]

    def body(rows, prevs, nexts, fulls, sf, sl):
        prim = [rr[...] for rr in rows[:5]] + [f[...] for f in fulls]
        _, vjp = jax.vjp(_rwkv_post, *prim)
        ct = vjp(rows[5][...])
        return list(ct[:5]), list(ct[5:8])

    (d_y, d_r1, d_k21, d_v1, d_g), (d_ln_w, d_ln_b, d_r_k) = _tile_call(
        body, rows=[y, r, k2, v, g, d_z], fulls=post_fulls, row_outs=[(D, F32)] * 5, acc_outs=[(1, D)] * 3,
        tm=min(256, T), T=T, name="rwkv_post_bwd")
    d_r2, d_w, d_k22, d_v2, d_kkn, d_bb = _rwkv_scan_bwd(r, decay, k2, v, kkn, bb, st, d_y, B, T)
    pre_fulls = [p["w0"].reshape(1, D), p["w1"], p["w2"], p["a0"].reshape(1, D), p["a1"], p["a2"], p["g1"], p["g2"],
                 p["k_k"].reshape(1, D), p["k_a"].reshape(1, D), e, et]

    def body2(rows, prevs, nexts, fulls, sf, sl):
        prim = [rows[0][...], _f(rows[1]), _f(rows[2]), _f(rows[3])] + [f[...] for f in fulls]
        _, vjp = jax.vjp(_rwkv_pre, *prim)
        ct = vjp((rows[4][...], rows[5][...] + rows[6][...], rows[7][...], rows[8][...], rows[9][...]))
        d_r = rows[10][...] + rows[11][...]
        d_v = rows[12][...] + rows[13][...]
        return [ct[0], ct[1], ct[2], ct[3], d_r, d_v], [c.astype(F32) for c in ct[4:14]]

    acc_shapes = [f.shape for f in pre_fulls[:10]]
    (d_k, d_xw, d_xa, d_xg, d_r, d_v), pgr = _tile_call(
        body2, rows=[k, xs[3], xs[4], xs[5], d_w, d_k21, d_k22, d_kkn, d_bb, d_g, d_r1, d_r2, d_v1, d_v2],
        fulls=pre_fulls, row_outs=[(D, BF16), (D, F32), (D, F32), (D, F32), (D, BF16), (D, BF16)], acc_outs=acc_shapes,
        tm=min(128, T), T=T, name="rwkv_pre_bwd")
    d_xr = _matmul(d_r, p["w_r"], mode="nt", name="rwkv_r_dx")
    d_xk = _matmul(d_k, p["w_k"], mode="nt", name="rwkv_k_dx")
    d_xv = _matmul(d_v, p["w_v"], mode="nt", name="rwkv_v_dx")
    d_wr = _matmul(xs[0], d_r, mode="tn", out_dtype=BF16, name="rwkv_r_dw")
    d_wk = _matmul(xs[1], d_k, mode="tn", out_dtype=BF16, name="rwkv_k_dw")
    d_wv = _matmul(xs[2], d_v, mode="tn", out_dtype=BF16, name="rwkv_v_dw")
    norm = p["norm"].reshape(1, D)

    def body3(rows, prevs, nexts, fulls, sf, sl):
        xc, xp = rows[0][...], prevs[0][...]
        nrm, mix = _f(fulls[0]), _f(fulls[1])
        h, xx = _rwkv_mix(xc, xp, nrm, None, sf)
        dxs = [rows[1 + i][...] for i in range(6)]
        dxs_n = [nexts[i][...] for i in range(6)]
        d_h = jnp.zeros_like(h)
        d_sh = jnp.zeros_like(h)
        d_sh_n = jnp.zeros_like(dxs_n[0])
        dmix = []
        for i in range(6):
            m = mix[i:i + 1]
            d_h = d_h + dxs[i] * (1.0 - m)
            d_sh = d_sh + dxs[i] * m
            d_sh_n = d_sh_n + dxs_n[i] * m
            dmix.append(_colsum(dxs[i] * xx))
        d_h = d_h + _shift_up(d_sh, d_sh_n, 1, sl)
        _, vjp = jax.vjp(_rms, xc, nrm)
        dx, dn = vjp(d_h)
        return [dx + rows[7][...]], [dn, _pad8(jnp.concatenate(dmix, axis=0))]

    (dx,), (d_norm, d_mix) = _tile_call(
        body3, rows=[x, d_xr, d_xk, d_xv, d_xw, d_xa, d_xg, dout], prevs=[0], nexts=[1, 2, 3, 4, 5, 6],
        fulls=[norm, _pad8(p["mix"])], row_outs=[(D, F32)], acc_outs=[(1, D), (8, D)], tm=min(256, T), T=T,
        name="rwkv_mix_bwd")
    names = ["w0", "w1", "w2", "a0", "a1", "a2", "g1", "g2", "k_k", "k_a"]
    grads = {n: gr.reshape(p[n].shape) for n, gr in zip(names, pgr, strict=True)}
    grads.update({"norm": d_norm.reshape(D), "mix": d_mix[:6], "w_r": d_wr, "w_k": d_wk, "w_v": d_wv,
                  "r_k": d_r_k.reshape(p["r_k"].shape), "ln_w": d_ln_w.reshape(D), "ln_b": d_ln_b.reshape(D),
                  "w_out": d_w_out})
    return dx, grads


def _loss_head(x, g, tgt, T):
    M, D = x.shape

    def body(rows, prevs, nexts, fulls, sf, sl):
        xv, gv = rows[0][...], _f(fulls[0])
        yv, vjp = jax.vjp(_rms, xv, gv)
        err = yv - rows[1][...]
        dx, dg = vjp(err * (1.0 / D))
        part = jnp.sum(_colsum(err * err), axis=1, keepdims=True) * (0.5 / D)
        return [dx], [dg, jnp.broadcast_to(part, (1, 128))]

    (dx,), (dg, loss) = _tile_call(body, rows=[x, tgt], fulls=[g.reshape(1, D)], row_outs=[(D, F32)],
                                   acc_outs=[(1, D), (1, 128)], tm=min(512, T), T=T, name="loss_head")
    return loss[0, 0], dx, dg.reshape(D)


def _local_step(x3, tgt3, P):
    B, T, D = x3.shape
    x, tgt = x3.reshape(B * T, D), tgt3.reshape(B * T, D)
    x1, s_lru = _lru_fwd(x, P["lru"], B, T)
    x2, s_f0 = _ffn_fwd(x1, P["ffn0"], T, "0")
    x3_, s_rw = _rwkv_fwd(x2, P["rwkv"], B, T)
    x4, s_f1 = _ffn_fwd(x3_, P["ffn1"], T, "1")
    loss, d4, d_fn = _loss_head(x4, P["final_norm"], tgt, T)
    d3, g_f1 = _ffn_bwd(d4, s_f1, P["ffn1"], T, "1")
    d2, g_rw = _rwkv_bwd(d3, s_rw, P["rwkv"], B, T)
    d1, g_f0 = _ffn_bwd(d2, s_f0, P["ffn0"], T, "0")
    d0, g_lru = _lru_bwd(d1, s_lru, P["lru"], B, T)
    return loss, d0.reshape(B, T, D), {"lru": g_lru, "ffn0": g_f0, "rwkv": g_rw, "ffn1": g_f1, "final_norm": d_fn}


WEIGHTS = ['lru_norm', 'lru_w_in', 'lru_b_in', 'lru_conv_w', 'lru_conv_b', 'lru_gate_w', 'lru_gate_b', 'lru_lambda',
           'lru_w_out', 'lru_b_out', 'rwkv_norm', 'rwkv_mix', 'rwkv_w_rkv', 'rwkv_w0', 'rwkv_w1', 'rwkv_w2', 'rwkv_a0',
           'rwkv_a1', 'rwkv_a2', 'rwkv_g1', 'rwkv_g2', 'rwkv_k_k', 'rwkv_k_a', 'rwkv_r_k', 'rwkv_ln_w', 'rwkv_ln_b',
           'rwkv_w_out', 'ffn_norm', 'ffn_w_up', 'ffn_conv_w', 'ffn_conv_b', 'ffn_w_down', 'final_norm']
SHARD_AXIS = {'lru_w_in': 2, 'lru_conv_w': 2, 'lru_w_out': 1, 'rwkv_norm': 1, 'rwkv_mix': 2, 'rwkv_w_rkv': 2,
              'rwkv_w0': 1, 'rwkv_w1': 1, 'rwkv_w2': 2, 'rwkv_a0': 1, 'rwkv_a1': 1, 'rwkv_a2': 2, 'rwkv_g1': 1,
              'rwkv_g2': 2, 'rwkv_k_k': 1, 'rwkv_k_a': 1, 'rwkv_ln_w': 1, 'rwkv_ln_b': 1, 'rwkv_w_out': 1,
              'ffn_w_up': 2, 'ffn_conv_w': 2, 'ffn_w_down': 1}
MXU_WEIGHTS = ('lru_w_in', 'lru_w_out', 'rwkv_w_rkv', 'rwkv_w_out', 'ffn_w_up', 'ffn_w_down')
N_CHIPS = 4
LANES = 1024


def _pack(arrs, dtype, row_mult):
    flat = jnp.concatenate([a.reshape(-1).astype(dtype) for a in arrs])
    n = flat.shape[0]
    unit = row_mult * LANES
    tot = -(-n // unit) * unit
    if tot > n:
        flat = jnp.concatenate([flat, jnp.zeros((tot - n,), dtype)])
    return flat.reshape(tot // LANES, LANES)


def _unpack(buf, shapes):
    flat = buf.reshape(-1)
    out, off = [], 0
    for s in shapes:
        n = 1
        for d in s:
            n *= d
        out.append(flat[off:off + n].reshape(s))
        off += n
    return out


def _to_shards(full, axis):
    return jnp.stack(jnp.split(full, N_CHIPS, axis=axis))


MESH_ID = pl.DeviceIdType.MESH


ANY_SPEC = pl.BlockSpec(memory_space=pl.ANY)
COMM_PARAMS = pltpu.CompilerParams(has_side_effects=True)


def _mesh_place():
    x, y, c = lax.axis_index("x"), lax.axis_index("y"), lax.axis_index("c")
    return x, y, c, 2 * x + y, [(1 - x, y), (x, 1 - y), (1 - x, 1 - y)]


def _gather_all(arrs, name):
    n = len(arrs)

    def body(*refs):
        ins, outs = refs[:n], refs[n:2 * n]
        send_sems, recv_sems, local_sems = refs[2 * n:]
        x, y, c, p, chips = _mesh_place()
        sibling = (x, y, 1 - c)

        def rows(a, which):
            h = arrs[a].shape[0] // 2
            return pl.ds(which * h, h)

        def copy(a, k, region, src, to):
            return pltpu.make_async_remote_copy(src_ref=src, dst_ref=region, send_sem=send_sems.at[a, k],
                                                recv_sem=recv_sems.at[a, k], device_id=to, device_id_type=MESH_ID)

        local = [pltpu.make_async_copy(ins[a], outs[a].at[p], local_sems.at[a]) for a in range(n)]
        for cp in local:
            cp.start()
        first = [copy(a, j, outs[a].at[p, rows(a, c)], ins[a].at[rows(a, c)], (qx, qy, c))
                 for a in range(n) for j, (qx, qy) in enumerate(chips)]
        for cp in first:
            cp.start()
        passed = []
        for a in range(n):
            for j, (qx, qy) in enumerate(chips):
                region = outs[a].at[2 * qx + qy, rows(a, c)]
                copy(a, j, region, region, (qx, qy, c)).wait_recv()
                fw = copy(a, 3 + j, region, region, sibling)
                fw.start()
                passed.append(fw)
        for a in range(n):
            for j, (qx, qy) in enumerate(chips):
                region = outs[a].at[2 * qx + qy, rows(a, 1 - c)]
                copy(a, 3 + j, region, region, sibling).wait_recv()
        for cp in first + passed:
            cp.wait_send()
        for cp in local:
            cp.wait()

    return pl.pallas_call(
        body, name=name, out_shape=[jax.ShapeDtypeStruct((N_CHIPS,) + a.shape, a.dtype) for a in arrs],
        in_specs=[ANY_SPEC] * n, out_specs=[ANY_SPEC] * n,
        scratch_shapes=[pltpu.SemaphoreType.DMA((n, 6)), pltpu.SemaphoreType.DMA((n, 6)), pltpu.SemaphoreType.DMA((n,))],
        compiler_params=COMM_PARAMS,
    )(*arrs)


def _pair_swap_all(gs, rep, name):
    n = len(gs)

    def body(*refs):
        ins, outs = refs[:n + 1], refs[n + 1:2 * n + 2]
        send_sems, recv_sems = refs[2 * n + 2:]
        x, y, c, _, _ = _mesh_place()
        copies = []
        for a in range(n + 1):
            src = ins[a]
            if a < n:
                h = gs[a].shape[1] // 2
                src = src.at[:, pl.ds((1 - c) * h, h), :]
            copies.append(pltpu.make_async_remote_copy(src_ref=src, dst_ref=outs[a], send_sem=send_sems.at[a],
                                                       recv_sem=recv_sems.at[a], device_id=(x, y, 1 - c),
                                                       device_id_type=MESH_ID))
        for cp in copies:
            cp.start()
        for cp in copies:
            cp.wait()

    shapes = [jax.ShapeDtypeStruct((N_CHIPS, g.shape[1] // 2, g.shape[2]), g.dtype) for g in gs]
    shapes.append(jax.ShapeDtypeStruct(rep.shape, rep.dtype))
    res = pl.pallas_call(
        body, name=name, out_shape=shapes, in_specs=[ANY_SPEC] * (n + 1), out_specs=[ANY_SPEC] * (n + 1),
        scratch_shapes=[pltpu.SemaphoreType.DMA((n + 1,)), pltpu.SemaphoreType.DMA((n + 1,))],
        compiler_params=COMM_PARAMS,
    )(*gs, rep)
    return res[:n], res[n]


def _chip_exchange_all(ps, rep, name):
    n = len(ps)

    def body(*refs):
        ins, outs = refs[:n + 1], refs[n + 1:2 * n + 2]
        send_sems, recv_sems, local_sems = refs[2 * n + 2:]
        x, y, c, p, chips = _mesh_place()

        def src(a, q):
            return ins[a].at[q] if a < n else ins[a]

        local = [pltpu.make_async_copy(src(a, p), outs[a].at[p], local_sems.at[a]) for a in range(n + 1)]
        for cp in local:
            cp.start()
        sends, recvs = [], []
        for a in range(n + 1):
            for j, (qx, qy) in enumerate(chips):
                q = 2 * qx + qy
                for dst, keep in ((outs[a].at[p], sends), (outs[a].at[q], recvs)):
                    keep.append(pltpu.make_async_remote_copy(
                        src_ref=src(a, q), dst_ref=dst, send_sem=send_sems.at[a, j], recv_sem=recv_sems.at[a, j],
                        device_id=(qx, qy, c), device_id_type=MESH_ID))
        for cp in sends:
            cp.start()
        for cp in recvs:
            cp.wait_recv()
        for cp in sends:
            cp.wait_send()
        for cp in local:
            cp.wait()

    shapes = [jax.ShapeDtypeStruct(g.shape, g.dtype) for g in ps]
    shapes.append(jax.ShapeDtypeStruct((N_CHIPS,) + rep.shape, rep.dtype))
    res = pl.pallas_call(
        body, name=name, out_shape=shapes, in_specs=[ANY_SPEC] * (n + 1), out_specs=[ANY_SPEC] * (n + 1),
        scratch_shapes=[pltpu.SemaphoreType.DMA((n + 1, 3)), pltpu.SemaphoreType.DMA((n + 1, 3)),
                        pltpu.SemaphoreType.DMA((n + 1,))],
        compiler_params=COMM_PARAMS,
    )(*ps, rep)
    return res[:n], res[n]


def _half_swap_all(ts, name):
    n = len(ts)

    def body(*refs):
        ins, outs = refs[:n], refs[n:2 * n]
        send_sems, recv_sems = refs[2 * n:]
        x, y, c, _, _ = _mesh_place()
        copies = [pltpu.make_async_remote_copy(src_ref=ins[a], dst_ref=outs[a], send_sem=send_sems.at[a],
                                               recv_sem=recv_sems.at[a], device_id=(x, y, 1 - c), device_id_type=MESH_ID)
                  for a in range(n)]
        for cp in copies:
            cp.start()
        for cp in copies:
            cp.wait()

    return pl.pallas_call(
        body, name=name, out_shape=[jax.ShapeDtypeStruct(t.shape, t.dtype) for t in ts],
        in_specs=[ANY_SPEC] * n, out_specs=[ANY_SPEC] * n,
        scratch_shapes=[pltpu.SemaphoreType.DMA((n,)), pltpu.SemaphoreType.DMA((n,))],
        compiler_params=COMM_PARAMS,
    )(*ts)


def _pick_rows(R, cap=256):
    for t in (512, 256, 128, 64, 32, 16, 8):
        if t <= cap and R % t == 0:
            return t
    return R


def _pair_sum(g, got, name):
    _, R, C = g.shape
    h = R // 2
    th = _pick_rows(h)

    def kern(g_ref, got_ref, o_ref):
        both = g_ref[...].astype(F32)
        mine = jnp.where(lax.axis_index("c") == 0, both[0], both[1])
        o_ref[...] = (mine + got_ref[...].astype(F32)).astype(o_ref.dtype)

    return pl.pallas_call(
        kern, name=name, grid=(N_CHIPS, h // th),
        in_specs=[pl.BlockSpec((None, 2, th, C), lambda q, i: (q, 0, i, 0)), pl.BlockSpec((None, th, C), lambda q, i: (q, i, 0))],
        out_specs=pl.BlockSpec((None, th, C), lambda q, i: (q, i, 0)),
        out_shape=jax.ShapeDtypeStruct((N_CHIPS, h, C), BF16), compiler_params=_cparams(("parallel", "parallel")),
    )(g.reshape(N_CHIPS, 2, h, C), got)


def _rep_pair_sum(rep, got, name):
    R, C = rep.shape
    tr = _pick_rows(R)

    def kern(a_ref, b_ref, o_ref):
        o_ref[...] = (a_ref[...] + b_ref[...]).astype(o_ref.dtype)

    spec = pl.BlockSpec((tr, C), lambda i: (i, 0))
    return pl.pallas_call(kern, name=name, grid=(R // tr,), in_specs=[spec, spec], out_specs=spec,
                          out_shape=jax.ShapeDtypeStruct((R, C), BF16), compiler_params=_cparams(("parallel",)))(rep, got)


def _chip_sum(arrived, name):
    _, R, C = arrived.shape
    tr = _pick_rows(R)

    def kern(a_ref, o_ref):
        acc = a_ref[0].astype(F32)
        for q in range(1, N_CHIPS):
            acc = acc + a_ref[q].astype(F32)
        o_ref[...] = acc

    return pl.pallas_call(
        kern, name=name, grid=(R // tr,), in_specs=[pl.BlockSpec((N_CHIPS, tr, C), lambda i: (0, i, 0))],
        out_specs=pl.BlockSpec((tr, C), lambda i: (i, 0)), out_shape=jax.ShapeDtypeStruct((R, C), F32),
        compiler_params=_cparams(("parallel",)),
    )(arrived)


def _adam_math(w, g, m, v):
    c1 = 1.0 / (1.0 - ADAM_B1 ** ADAM_STEP)
    c2 = 1.0 / (1.0 - ADAM_B2 ** ADAM_STEP)
    nm = ADAM_B1 * m + (1.0 - ADAM_B1) * g
    nv = ADAM_B2 * v + (1.0 - ADAM_B2) * (g * g)
    return -ADAM_LR * ((nm * c1) / (jnp.sqrt(nv * c2) + ADAM_EPS) + ADAM_WD * w), nm, nv


def _adamw_halves(w, m, v, mine, other, name):
    R, C = w.shape
    h = R // 2
    th = _pick_rows(h)
    nt = h // th

    def kern(w_ref, m_ref, v_ref, a_ref, b_ref, g_ref, d_ref, nm_ref, nv_ref):
        g = jnp.where(pl.program_id(0) == lax.axis_index("c"), a_ref[...], b_ref[...])
        d, nm, nv = _adam_math(w_ref[...], g, m_ref[...], v_ref[...])
        g_ref[...] = g
        d_ref[...] = d
        nm_ref[...] = nm
        nv_ref[...] = nv

    full = pl.BlockSpec((th, C), lambda hh, i: (hh * nt + i, 0))
    half = pl.BlockSpec((th, C), lambda hh, i: (i, 0))
    sh = jax.ShapeDtypeStruct((R, C), F32)
    return pl.pallas_call(kern, name=name, grid=(2, nt), in_specs=[full] * 3 + [half] * 2, out_specs=[full] * 4,
                          out_shape=[sh] * 4, compiler_params=_cparams(("parallel", "parallel")))(w, m, v, mine, other)


def _adamw_call(w, g, m, v, name):
    R = w.shape[0]
    tr = _pick_rows(R)
    c1 = 1.0 / (1.0 - ADAM_B1 ** ADAM_STEP)
    c2 = 1.0 / (1.0 - ADAM_B2 ** ADAM_STEP)

    def kern(w_ref, g_ref, m_ref, v_ref, d_ref, nm_ref, nv_ref):
        gv = g_ref[...]
        nm = ADAM_B1 * m_ref[...] + (1.0 - ADAM_B1) * gv
        nv = ADAM_B2 * v_ref[...] + (1.0 - ADAM_B2) * (gv * gv)
        d_ref[...] = -ADAM_LR * ((nm * c1) / (jnp.sqrt(nv * c2) + ADAM_EPS) + ADAM_WD * w_ref[...])
        nm_ref[...] = nm
        nv_ref[...] = nv

    spec = pl.BlockSpec((tr, LANES), lambda i: (i, 0))
    sh = jax.ShapeDtypeStruct((R, LANES), F32)
    return pl.pallas_call(kern, name=name, grid=(R // tr,), in_specs=[spec] * 4, out_specs=[spec] * 3,
                          out_shape=[sh] * 3, compiler_params=_cparams(("parallel",)))(w, g, m, v)


GATE_GROUP = 256


def _gate_dense(gate_w):
    _, nb, bw, _ = gate_w.shape
    D = nb * bw
    gb = min(GATE_GROUP, D)
    per = gb // bw
    w = gate_w.reshape(2, D // gb, per, bw, bw)
    eye = jnp.eye(per, dtype=gate_w.dtype)
    dense = jnp.einsum('grncd,nm->rncgmd', w, eye)
    return dense.reshape(D, 2 * gb)


def _gate_blocks(d_dense, nb):
    D, gb2 = d_dense.shape
    gb, bw = gb2 // 2, D // nb
    per = gb // bw
    g = d_dense.reshape(D // gb, per, bw, 2, per, bw)
    return jnp.einsum('rncgnd->grncd', g).reshape(2, nb, bw, bw)


def _step(W, M1, V1, x, tgt):
    sharded = [n for n in WEIGHTS if n in SHARD_AXIS]
    repl = [n for n in WEIGHTS if n not in SHARD_AXIS]
    small = [n for n in sharded if n not in MXU_WEIGHTS]
    D = x.shape[-1]

    def rows2d(a):
        return a.reshape(-1, a.shape[-1])

    small_buf = _pack([W[n] for n in small], F32, 16)
    gathered = _gather_all([rows2d(W[n]).astype(MXU_DTYPE) for n in MXU_WEIGHTS] + [small_buf], "gather_weights")
    mats = {n: g.reshape((N_CHIPS,) + W[n].shape[-3:]) for n, g in zip(MXU_WEIGHTS, gathered[:-1], strict=True)}
    per_chip = [_unpack(gathered[-1][q], [W[n].shape for n in small]) for q in range(N_CHIPS)]
    full = {n: jnp.concatenate([per_chip[q][i] for q in range(N_CHIPS)], axis=SHARD_AXIS[n]) for i, n in enumerate(small)}
    for n in repl:
        full[n] = W[n]

    def by_rows(name, layer):
        m = mats[name][:, layer]
        return m.reshape(N_CHIPS * m.shape[1], m.shape[2])

    P = {
        "lru": {"norm": full["lru_norm"][0], "w_in": _W(mats["lru_w_in"], 0), "b_in": full["lru_b_in"][0],
                "conv_w": full["lru_conv_w"][0], "conv_b": full["lru_conv_b"][0],
                "wbd": _gate_dense(full["lru_gate_w"][0]).astype(MXU_DTYPE), "gate_b": full["lru_gate_b"][0].reshape(-1),
                "lam": full["lru_lambda"][0], "w_out": by_rows("lru_w_out", 0), "b_out": full["lru_b_out"][0]},
        "rwkv": {"norm": full["rwkv_norm"][0], "mix": full["rwkv_mix"][0],
                 "w_r": by_rows("rwkv_w_rkv", 0), "w_k": by_rows("rwkv_w_rkv", 1), "w_v": by_rows("rwkv_w_rkv", 2),
                 "w0": full["rwkv_w0"][0], "w1": full["rwkv_w1"][0], "w2": full["rwkv_w2"][0], "a0": full["rwkv_a0"][0],
                 "a1": full["rwkv_a1"][0], "a2": full["rwkv_a2"][0], "g1": full["rwkv_g1"][0], "g2": full["rwkv_g2"][0],
                 "k_k": full["rwkv_k_k"][0], "k_a": full["rwkv_k_a"][0], "r_k": full["rwkv_r_k"][0],
                 "ln_w": full["rwkv_ln_w"][0], "ln_b": full["rwkv_ln_b"][0], "w_out": by_rows("rwkv_w_out", 0)},
        "final_norm": full["final_norm"],
    }
    for l in range(2):
        P[f"ffn{l}"] = {"norm": full["ffn_norm"][l], "w_up": _W(mats["ffn_w_up"], l),
                        "conv_w": full["ffn_conv_w"][l], "conv_b": full["ffn_conv_b"][l],
                        "w_down": by_rows("ffn_w_down", l)}

    loss, gx, G = _local_step(x, tgt, P)

    nb = W["lru_gate_w"].shape[2]
    gl, gr = G["lru"], G["rwkv"]
    gfull = {
        "lru_norm": gl["norm"][None], "lru_b_in": gl["b_in"][None],
        "lru_conv_w": gl["conv_w"][None], "lru_conv_b": gl["conv_b"][None], "lru_gate_w": _gate_blocks(gl["wbd"], nb)[None],
        "lru_gate_b": gl["gate_b"].reshape(W["lru_gate_b"].shape), "lru_lambda": gl["lam"][None],
        "lru_b_out": gl["b_out"][None],
        "rwkv_norm": gr["norm"][None], "rwkv_mix": gr["mix"][None],
        "rwkv_w0": gr["w0"][None], "rwkv_w1": gr["w1"][None], "rwkv_w2": gr["w2"][None], "rwkv_a0": gr["a0"][None],
        "rwkv_a1": gr["a1"][None], "rwkv_a2": gr["a2"][None], "rwkv_g1": gr["g1"][None], "rwkv_g2": gr["g2"][None],
        "rwkv_k_k": gr["k_k"][None], "rwkv_k_a": gr["k_a"][None], "rwkv_r_k": gr["r_k"][None],
        "rwkv_ln_w": gr["ln_w"][None], "rwkv_ln_b": gr["ln_b"][None],
        "final_norm": G["final_norm"],
    }
    for k in ("norm", "conv_w", "conv_b"):
        gfull["ffn_" + k] = jnp.stack([G["ffn0"][k], G["ffn1"][k]])

    small_g = jnp.stack([_pack([_to_shards(gfull[n], SHARD_AXIS[n])[q] for n in small], F32, 16) for q in range(N_CHIPS)])
    cut = lambda g: g.reshape(N_CHIPS, g.shape[0] // N_CHIPS, g.shape[1])
    pieces = [("lru_w_in", (0,), gl["w_in"]), ("lru_w_out", (0,), cut(gl["w_out"])),
              ("rwkv_w_rkv", (0, 0), cut(gr["w_r"])), ("rwkv_w_rkv", (0, 1), cut(gr["w_k"])),
              ("rwkv_w_rkv", (0, 2), cut(gr["w_v"])), ("rwkv_w_out", (0,), cut(gr["w_out"])),
              ("ffn_w_up", (0,), G["ffn0"]["w_up"]), ("ffn_w_up", (1,), G["ffn1"]["w_up"]),
              ("ffn_w_down", (0,), cut(G["ffn0"]["w_down"])), ("ffn_w_down", (1,), cut(G["ffn1"]["w_down"]))]
    gs = [g for _, _, g in pieces] + [small_g]
    grep = _pack([gfull[n] for n in repl], F32, 16)

    got, got_rep = _pair_swap_all(gs, grep, "reduce_pair_swap")
    pair = [_pair_sum(g, r, f"reduce_pair_sum{i}") for i, (g, r) in enumerate(zip(gs, got, strict=True))]
    pair_rep = _rep_pair_sum(grep, got_rep, "reduce_pair_sum_rep")
    arrived, arrived_rep = _chip_exchange_all(pair, pair_rep, "reduce_chips")
    mine = [_chip_sum(a, f"reduce_chip_sum{i}") for i, a in enumerate(arrived)]
    g_rp = _chip_sum(arrived_rep, "reduce_chip_sum_rep")
    other = _half_swap_all(mine, "reduce_half_swap")

    outs, parts = {}, {}
    for i, (n, idx, _) in enumerate(pieces):
        w, m, v = (rows2d(S[n][idx]) for S in (W, M1, V1))
        res = _adamw_halves(w, m, v, mine[i], other[i], f"adamw{i}")
        for kind, a in zip(("grad", "delta", "new_m", "new_v"), res, strict=True):
            parts.setdefault((kind, n), []).append(a)
    for (kind, n), lst in parts.items():
        a = lst[0] if len(lst) == 1 else jnp.stack(lst)
        outs[(kind, n)] = a.reshape(W[n].shape)
    wb, mb, vb = (_pack([S[n] for n in small], F32, 16) for S in (W, M1, V1))
    res = _adamw_halves(wb, mb, vb, mine[-1], other[-1], "adamw_small")
    for kind, buf in zip(("grad", "delta", "new_m", "new_v"), res, strict=True):
        for n, a in zip(small, _unpack(buf, [W[n].shape for n in small]), strict=True):
            outs[(kind, n)] = a
    wb, mb, vb = (_pack([S[n] for n in repl], F32, 16) for S in (W, M1, V1))
    d, nm, nv = _adamw_call(wb, g_rp, mb, vb, "adamw_repl")
    for kind, buf in (("grad", g_rp), ("delta", d), ("new_m", nm), ("new_v", nv)):
        for n, a in zip(repl, _unpack(buf, [W[n].shape for n in repl]), strict=True):
            outs[(kind, n)] = a
    loss = lax.psum(loss, ("x", "y", "c"))
    return (loss, gx, *[outs[(kind, n)] for kind in ("grad", "delta", "new_m", "new_v") for n in WEIGHTS])


def kernel(x, lru_norm, lru_w_in, lru_b_in, lru_conv_w, lru_conv_b, lru_gate_w, lru_gate_b, lru_lambda, lru_w_out, lru_b_out, rwkv_norm, rwkv_mix, rwkv_w_rkv, rwkv_w0, rwkv_w1, rwkv_w2, rwkv_a0, rwkv_a1, rwkv_a2, rwkv_g1, rwkv_g2, rwkv_k_k, rwkv_k_a, rwkv_r_k, rwkv_ln_w, rwkv_ln_b, rwkv_w_out, ffn_norm, ffn_w_up, ffn_conv_w, ffn_conv_b, ffn_w_down, final_norm, loss_target, m_lru_norm, m_lru_w_in, m_lru_b_in, m_lru_conv_w, m_lru_conv_b, m_lru_gate_w, m_lru_gate_b, m_lru_lambda, m_lru_w_out, m_lru_b_out, m_rwkv_norm, m_rwkv_mix, m_rwkv_w_rkv, m_rwkv_w0, m_rwkv_w1, m_rwkv_w2, m_rwkv_a0, m_rwkv_a1, m_rwkv_a2, m_rwkv_g1, m_rwkv_g2, m_rwkv_k_k, m_rwkv_k_a, m_rwkv_r_k, m_rwkv_ln_w, m_rwkv_ln_b, m_rwkv_w_out, m_ffn_norm, m_ffn_w_up, m_ffn_conv_w, m_ffn_conv_b, m_ffn_w_down, m_final_norm, v_lru_norm, v_lru_w_in, v_lru_b_in, v_lru_conv_w, v_lru_conv_b, v_lru_gate_w, v_lru_gate_b, v_lru_lambda, v_lru_w_out, v_lru_b_out, v_rwkv_norm, v_rwkv_mix, v_rwkv_w_rkv, v_rwkv_w0, v_rwkv_w1, v_rwkv_w2, v_rwkv_a0, v_rwkv_a1, v_rwkv_a2, v_rwkv_g1, v_rwkv_g2, v_rwkv_k_k, v_rwkv_k_a, v_rwkv_r_k, v_rwkv_ln_w, v_rwkv_ln_b, v_rwkv_w_out, v_ffn_norm, v_ffn_w_up, v_ffn_conv_w, v_ffn_conv_b, v_ffn_w_down, v_final_norm):
    given = dict(locals())
    W = {n: given[n] for n in WEIGHTS}
    M1 = {n: given["m_" + n] for n in WEIGHTS}
    V1 = {n: given["v_" + n] for n in WEIGHTS}
    return _step(W, M1, V1, x, loss_target)
```

```python
import functools

import jax
import jax.numpy as jnp
from jax import lax
from jax.experimental import pallas as pl
from jax.experimental.pallas import tpu as pltpu

F32 = jnp.float32
BF16 = jnp.bfloat16
MXU_DTYPE = BF16

HEAD = 64
LRU_C = 8.0
GN_EPS = 64e-5
RMS_EPS = 1e-6
HALO = 16
VMEM_LIMIT = 56 * 1024 * 1024

ADAM_LR, ADAM_B1, ADAM_B2, ADAM_EPS, ADAM_WD, ADAM_STEP = 0.001, 0.9, 0.999, 1e-08, 0.01, 10


def _cparams(sem):
    return pltpu.CompilerParams(dimension_semantics=sem, vmem_limit_bytes=VMEM_LIMIT)


def _pick(n, want):
    if n <= want:
        return n
    t = want
    while t >= 128:
        if n % t == 0:
            return t
        t -= 128
    return n


class _W:
    def __init__(self, arr, layer):
        self.arr, self.layer = arr, layer
        self.shape = (arr.shape[2], N_CHIPS * arr.shape[3])


def _matmul(a, b, mode="nn", bias=None, residual=None, out_dtype=F32, name="mm", tm=1024, tn=1536, tk=1024,
            out_cols_by_chip=False):
    bshape = b.shape
    if mode == "nn":
        (M, K), (K2, N) = a.shape, bshape
    elif mode == "nt":
        (M, K), (N, K2) = a.shape, bshape
    else:
        (K, M), (K2, N) = a.shape, bshape
    assert K == K2, (a.shape, bshape, mode)
    lim_n, lim_k = N, K
    if isinstance(b, _W):
        if mode == "nn":
            lim_n = b.arr.shape[3]
        else:
            lim_k = b.arr.shape[3]
    if out_cols_by_chip:
        lim_n = min(lim_n, N // N_CHIPS)
    if K > 2 * tk:
        tk = 2 * tk
    tm, tn, tk = _pick(M, tm), _pick(lim_n, tn), _pick(lim_k, tk)
    nk = K // tk
    dims = {"nn": (((1,), (0,)), ((), ())), "nt": (((1,), (1,)), ((), ())), "tn": (((0,), (0,)), ((), ()))}[mode]
    a_spec = {"nn": pl.BlockSpec((tm, tk), lambda i, j, k: (i, k)),
              "nt": pl.BlockSpec((tm, tk), lambda i, j, k: (i, k)),
              "tn": pl.BlockSpec((tk, tm), lambda i, j, k: (k, i))}[mode]
    if isinstance(b, _W):
        lay = b.layer
        if mode == "nn":
            per = b.arr.shape[3] // tn
            b_spec = pl.BlockSpec((None, None, tk, tn), lambda i, j, k: (j // per, lay, k, j % per))
        else:
            assert mode == "nt"
            per = b.arr.shape[3] // tk
            b_spec = pl.BlockSpec((None, None, tn, tk), lambda i, j, k: (k // per, lay, j, k % per))
        b = b.arr
    else:
        b_spec = {"nn": pl.BlockSpec((tk, tn), lambda i, j, k: (k, j)),
                  "nt": pl.BlockSpec((tn, tk), lambda i, j, k: (j, k)),
                  "tn": pl.BlockSpec((tk, tn), lambda i, j, k: (k, j))}[mode]
    if out_cols_by_chip:
        opc = N // N_CHIPS // tn
        out_spec = pl.BlockSpec((None, tm, tn), lambda i, j, k: (j // opc, i, j % opc))
        out_shape = jax.ShapeDtypeStruct((N_CHIPS, M, N // N_CHIPS), out_dtype)
    else:
        out_spec = pl.BlockSpec((tm, tn), lambda i, j, k: (i, j))
        out_shape = jax.ShapeDtypeStruct((M, N), out_dtype)
    in_specs, operands = [a_spec, b_spec], [a, b]
    if bias is not None:
        in_specs.append(pl.BlockSpec((1, tn), lambda i, j, k: (0, j)))
        operands.append(bias.reshape(1, N))
    if residual is not None:
        in_specs.append(pl.BlockSpec((tm, tn), lambda i, j, k: (i, j)))
        operands.append(residual)
    has_bias, has_res = bias is not None, residual is not None

    def kern(*refs):
        a_ref, b_ref = refs[0], refs[1]
        o_ref = refs[2 + has_bias + has_res]

        def finish(r):
            pos = 2
            if has_bias:
                r = r + refs[pos][...].astype(F32)
                pos += 1
            if has_res:
                r = r + refs[pos][...].astype(F32)
            o_ref[...] = r.astype(o_ref.dtype)

        part = lax.dot_general(a_ref[...].astype(MXU_DTYPE), b_ref[...].astype(MXU_DTYPE), dims,
                               preferred_element_type=F32)
        if nk == 1:
            finish(part)
            return
        acc_ref = refs[-1]
        k = pl.program_id(2)

        @pl.when(k == 0)
        def _():
            acc_ref[...] = part

        @pl.when(jnp.logical_and(k > 0, k < nk - 1))
        def _():
            acc_ref[...] += part

        @pl.when(k == nk - 1)
        def _():
            finish(acc_ref[...] + part)

    return pl.pallas_call(
        kern, name=name,
        grid=(M // tm, N // tn, nk),
        in_specs=in_specs,
        out_specs=out_spec,
        out_shape=out_shape,
        scratch_shapes=[pltpu.VMEM((tm, tn), F32)] if nk > 1 else [],
        compiler_params=_cparams(("parallel", "parallel", "arbitrary")),
    )(*operands)


def _tile_call(body, *, rows, prevs=(), nexts=(), fulls=(), row_outs=(), acc_outs=(), tm, T, name):
    M = rows[0].shape[0]
    n_tiles, tps, hb = M // tm, T // tm, tm // HALO
    n_halo_blocks = M // HALO
    nr, npv, nnx, nf, nro, nac = len(rows), len(prevs), len(nexts), len(fulls), len(row_outs), len(acc_outs)

    def kern(*refs):
        i = pl.program_id(0)
        row_refs = refs[:nr]
        prev_refs = refs[nr:nr + npv]
        next_refs = refs[nr + npv:nr + npv + nnx]
        full_refs = refs[nr + npv + nnx:nr + npv + nnx + nf]
        out_refs = refs[nr + npv + nnx + nf:nr + npv + nnx + nf + nro]
        acc_refs = refs[nr + npv + nnx + nf + nro:]
        seq_first = (i % tps) == 0
        seq_last = (i % tps) == (tps - 1)
        outs, accs = body(row_refs, prev_refs, next_refs, full_refs, seq_first, seq_last)
        for r, o in zip(out_refs, outs, strict=True):
            r[...] = o.astype(r.dtype)
        if nac:
            @pl.when(i == 0)
            def _():
                for r in acc_refs:
                    r[...] = jnp.zeros_like(r)
            for r, a in zip(acc_refs, accs, strict=True):
                r[...] += a.astype(F32)

    in_specs = [pl.BlockSpec((tm, a.shape[1]), lambda i: (i, 0)) for a in rows]
    in_specs += [pl.BlockSpec((HALO, rows[k].shape[1]), lambda i: (jnp.maximum(i * hb - 1, 0), 0)) for k in prevs]
    in_specs += [pl.BlockSpec((HALO, rows[k].shape[1]), lambda i: (jnp.minimum((i + 1) * hb, n_halo_blocks - 1), 0))
                 for k in nexts]
    in_specs += [pl.BlockSpec(f.shape, lambda i: (0, 0)) for f in fulls]
    out_specs = [pl.BlockSpec((tm, w), lambda i: (i, 0)) for (w, _) in row_outs]
    out_specs += [pl.BlockSpec(s, lambda i: (0, 0)) for s in acc_outs]
    out_shape = [jax.ShapeDtypeStruct((M, w), dt) for (w, dt) in row_outs]
    out_shape += [jax.ShapeDtypeStruct(s, F32) for s in acc_outs]
    operands = list(rows) + [rows[k] for k in prevs] + [rows[k] for k in nexts] + list(fulls)
    res = pl.pallas_call(
        kern, name=name, grid=(n_tiles,), in_specs=in_specs, out_specs=out_specs, out_shape=out_shape,
        compiler_params=_cparams(("arbitrary",)),
    )(*operands)
    return res[:nro], res[nro:]


def _f(ref):
    return ref[...].astype(F32)


def _sigmoid(x):
    return 1.0 / (1.0 + jnp.exp(-x))


def _softplus(x):
    return jnp.maximum(x, 0.0) + jnp.log(1.0 + jnp.exp(-jnp.abs(x)))


def _neg_expm1(x):
    series = -x * (1.0 + x * (0.5 + x * (1.0 / 6.0) * (1.0 + 0.25 * x)))
    return jnp.where(x > -0.01, series, 1.0 - jnp.exp(x))


def _gelu(x):
    return 0.5 * x * (1.0 + jnp.tanh(0.7978845608028654 * (x + 0.044715 * x * x * x)))


def _rms(x, g):
    return x * lax.rsqrt(jnp.mean(x * x, axis=-1, keepdims=True) + RMS_EPS) * g


@jax.custom_vjp
def _bdot(x, w):
    return jnp.dot(x.astype(MXU_DTYPE), w.astype(MXU_DTYPE), preferred_element_type=F32)


def _bdot_fwd(x, w):
    return _bdot(x, w), (x, w)


def _bdot_bwd(res, ct):
    x, w = res
    ctb = ct.astype(MXU_DTYPE)
    dx = lax.dot_general(ctb, w.astype(MXU_DTYPE), (((1,), (1,)), ((), ())), preferred_element_type=F32)
    dw = lax.dot_general(x.astype(MXU_DTYPE), ctb, (((0,), (0,)), ((), ())), preferred_element_type=F32)
    return dx.astype(x.dtype), dw.astype(w.dtype)


_bdot.defvjp(_bdot_fwd, _bdot_bwd)


@jax.custom_vjp
def _head_sum(x, e, et):
    s = jnp.dot(_split_lhs(x), e, preferred_element_type=F32)
    return jnp.dot(_split_lhs(s), et, preferred_element_type=F32)


def _head_sum_fwd(x, e, et):
    return _head_sum(x, e, et), (e, et)


def _head_sum_bwd(res, ct):
    e, et = res
    return _head_sum(ct, e, et), jnp.zeros_like(e), jnp.zeros_like(et)


_head_sum.defvjp(_head_sum_fwd, _head_sum_bwd)


def _shift_down(main, prev, s, seq_first):
    prev = jnp.where(seq_first, 0.0, prev)
    ext = jnp.concatenate([prev, main], axis=0)
    return pltpu.roll(ext, s, 0)[HALO:]


def _shift_up(main, nxt, s, seq_last):
    nxt = jnp.where(seq_last, 0.0, nxt)
    ext = jnp.concatenate([main, nxt], axis=0)
    n = ext.shape[0]
    return pltpu.roll(ext, n - s, 0)[:n - HALO]


def _colsum(x):
    return jnp.sum(x, axis=0, keepdims=True)


def _pad8(x):
    k = x.shape[0]
    return jnp.concatenate([x, jnp.zeros((8 - k, x.shape[1]), x.dtype)], axis=0) if k < 8 else x


def _rms_fwd(x, g, T, name):
    D = x.shape[1]

    def body(rows, prevs, nexts, fulls, sf, sl):
        return [_rms(_f(rows[0]), _f(fulls[0]))], []

    (h,), _ = _tile_call(body, rows=[x], fulls=[g.reshape(1, D)], row_outs=[(D, BF16)], tm=min(512, T), T=T, name=name)
    return h


def _rms_bwd(x, g, dh, dres, T, name):
    D = x.shape[1]

    def body(rows, prevs, nexts, fulls, sf, sl):
        _, vjp = jax.vjp(_rms, _f(rows[0]), _f(fulls[0]))
        dx, dg = vjp(_f(rows[1]))
        return [dx + _f(rows[2])], [dg]

    (dx,), (dg,) = _tile_call(body, rows=[x, dh, dres], fulls=[g.reshape(1, D)], row_outs=[(D, F32)],
                              acc_outs=[(1, D)], tm=min(512, T), T=T, name=name)
    return dx, dg


def _ffn_conv(u1, prev, cw, cb, sf):
    k = cw.shape[0]
    out = cb + u1 * cw[k - 1:k]
    for j in range(k - 1):
        out = out + _shift_down(u1, prev, k - 1 - j, sf) * cw[j:j + 1]
    return out


def _ffn_fwd(x, p, T, tag):
    M, D = x.shape
    F = p["w_down"].shape[0]
    hf = _rms_fwd(x, p["norm"], T, f"ffn{tag}_norm")
    uf = _matmul(hf, p["w_up"], out_dtype=BF16, name=f"ffn{tag}_up")

    def body(rows, prevs, nexts, fulls, sf, sl):
        u = rows[0]
        gate = _ffn_conv(u[:, :F].astype(F32), prevs[0][:, :F].astype(F32), _f(fulls[0]), _f(fulls[1]), sf)
        return [_gelu(gate) * u[:, F:].astype(F32)], []

    (hid,), _ = _tile_call(body, rows=[uf], prevs=[0], fulls=[p["conv_w"], p["conv_b"].reshape(1, F)],
                           row_outs=[(F, BF16)], tm=min(256, T), T=T, name=f"ffn{tag}_act")
    y = _matmul(hid, p["w_down"], residual=x, name=f"ffn{tag}_down")
    return y, (x, hf, uf, hid)


def _ffn_bwd(dy, saved, p, T, tag):
    x, hf, uf, hid = saved
    M, D = x.shape
    F = p["w_down"].shape[0]
    K = p["conv_w"].shape[0]
    dy_mxu = dy.astype(BF16)
    d_hid = _matmul(dy_mxu, p["w_down"], mode="nt", out_dtype=BF16, name=f"ffn{tag}_down_dx")
    d_w_down = _matmul(hid, dy_mxu, mode="tn", out_dtype=BF16, name=f"ffn{tag}_down_dw")

    def body(rows, prevs, nexts, fulls, sf, sl):
        u, dh = rows
        cw, cb = _f(fulls[0]), _f(fulls[1])
        tm = u.shape[0]
        u1c, u1p, u1n = u[:, :F].astype(F32), prevs[0][:, :F].astype(F32), nexts[0][:, :F].astype(F32)
        u1 = jnp.concatenate([u1c, u1n], axis=0)
        u2 = jnp.concatenate([u[:, F:].astype(F32), nexts[0][:, F:].astype(F32)], axis=0)
        dhid = jnp.concatenate([_f(dh), _f(nexts[1])], axis=0)
        gate = _ffn_conv(u1, u1p, cw, cb, sf)
        (act, dact) = jax.jvp(_gelu, (gate,), (jnp.ones_like(gate),))
        d_gate = dhid * u2 * dact
        d_u2 = (dhid * act)[:tm]
        rowid = lax.broadcasted_iota(jnp.int32, d_gate.shape, 0)
        d_gate = jnp.where(jnp.logical_and(sl, rowid >= tm), 0.0, d_gate)
        dgc, dgn = d_gate[:tm], d_gate[tm:]
        d_u1 = dgc * cw[K - 1:K]
        dws = []
        for j in range(K - 1):
            s = K - 1 - j
            d_u1 = d_u1 + _shift_up(dgc, dgn, s, False) * cw[j:j + 1]
            dws.append(_colsum(dgc * _shift_down(u1c, u1p, s, sf)))
        dws.append(_colsum(dgc * u1c))
        d_cw = _pad8(jnp.concatenate(dws, axis=0))
        return [jnp.concatenate([d_u1, d_u2], axis=1)], [d_cw, _colsum(dgc)]

    (d_uf,), (d_cw, d_cb) = _tile_call(
        body, rows=[uf, d_hid], prevs=[0], nexts=[0, 1], fulls=[p["conv_w"], p["conv_b"].reshape(1, F)],
        row_outs=[(2 * F, BF16)], acc_outs=[(8, F), (1, F)], tm=min(256, T), T=T, name=f"ffn{tag}_act_bwd")
    d_hf = _matmul(d_uf, p["w_up"], mode="nt", name=f"ffn{tag}_up_dx")
    d_w_up = _matmul(hf, d_uf, mode="tn", out_dtype=BF16, name=f"ffn{tag}_up_dw", out_cols_by_chip=True)
    dx, d_norm = _rms_bwd(x, p["norm"], d_hf, dy, T, f"ffn{tag}_norm_bwd")
    grads = {"norm": d_norm.reshape(D), "w_up": d_w_up, "conv_w": d_cw[:K], "conv_b": d_cb.reshape(F), "w_down": d_w_down}
    return dx, grads


def _lru_conv(u2, prev, cw, cb, sf):
    return _ffn_conv(u2, prev, cw, cb, sf)


def _lru_pre(xr, wg, gb):
    D, GB = xr.shape[1], wg.shape[1] // 2
    parts = [jnp.dot(xr[:, r * GB:(r + 1) * GB].astype(MXU_DTYPE), wg[r * GB:(r + 1) * GB], preferred_element_type=F32)
             for r in range(D // GB)]
    return jnp.concatenate([q[:, :GB] for q in parts] + [q[:, GB:] for q in parts], axis=1) + gb


def _lru_pre_t(xr, dpre, wg, with_dw):
    D, GB = xr.shape[1], wg.shape[1] // 2
    dx, dw = [], []
    for r in range(D // GB):
        dp = jnp.concatenate([dpre[:, r * GB:(r + 1) * GB], dpre[:, D + r * GB:D + (r + 1) * GB]], axis=1).astype(MXU_DTYPE)
        dx.append(lax.dot_general(dp, wg[r * GB:(r + 1) * GB], (((1,), (1,)), ((), ())), preferred_element_type=F32))
        if with_dw:
            dw.append(lax.dot_general(xr[:, r * GB:(r + 1) * GB].astype(MXU_DTYPE), dp, (((0,), (0,)), ((), ())),
                                      preferred_element_type=F32))
    return jnp.concatenate(dx, axis=1), (jnp.concatenate(dw, axis=0) if with_dw else None)


def _lru_gates(xr, pre, lam):
    D = xr.shape[1]
    r_gate, i_gate = _sigmoid(pre[:, :D]), _sigmoid(pre[:, D:])
    log_a = -LRU_C * r_gate * _softplus(-lam)
    a = jnp.exp(log_a)
    mult = jnp.sqrt(_neg_expm1(2.0 * log_a))
    return a, mult * (i_gate * xr)


def _lru_scan(a, b, B, T):
    M, D = a.shape
    cw = _pick(D, 256)
    ng = T // 8

    def kern(a_ref, b_ref, o_ref):
        row = lax.broadcasted_iota(jnp.int32, (8, cw), 0)

        def step(g, carry):
            sl = pl.ds(pl.multiple_of(g * 8, 8), 8)
            a8, b8 = a_ref[sl, :], b_ref[sl, :]
            for s in (1, 2, 4):
                a_sh = jnp.where(row >= s, pltpu.roll(a8, s, 0), 1.0)
                b_sh = jnp.where(row >= s, pltpu.roll(b8, s, 0), 0.0)
                b8 = a8 * b_sh + b8
                a8 = a8 * a_sh
            h8 = a8 * carry + b8
            o_ref[sl, :] = h8
            return jnp.broadcast_to(h8[7:8, :], (8, cw))

        lax.fori_loop(0, ng, step, jnp.zeros((8, cw), F32))

    spec = pl.BlockSpec((T, cw), lambda b, c: (b, c))
    return pl.pallas_call(
        kern, name="lru_scan", grid=(B, D // cw), in_specs=[spec, spec], out_specs=spec,
        out_shape=jax.ShapeDtypeStruct((M, D), F32), compiler_params=_cparams(("parallel", "parallel")),
    )(a, b)


def _lru_scan_bwd(a, hs, dhs, B, T):
    M, D = a.shape
    cw = _pick(D, 256)
    ng = T // 8

    def kern(a_ref, h_ref, d_ref, g_ref, da_ref):
        row = lax.broadcasted_iota(jnp.int32, (8, cw), 0)

        def step(k, carry):
            g_next, a_next = carry
            g = ng - 1 - k
            sl = pl.ds(pl.multiple_of(g * 8, 8), 8)
            a8, d8, h8 = a_ref[sl, :], d_ref[sl, :], h_ref[sl, :]
            c8 = jnp.where(row < 7, pltpu.roll(a8, 7, 0), a_next)
            for s in (1, 2, 4):
                d_sh = jnp.where(row < 8 - s, pltpu.roll(d8, 8 - s, 0), 0.0)
                c_sh = jnp.where(row < 8 - s, pltpu.roll(c8, 8 - s, 0), 1.0)
                d8 = d8 + c8 * d_sh
                c8 = c8 * c_sh
            G8 = d8 + c8 * g_next
            gp = jnp.maximum(g - 1, 0)
            hp8 = h_ref[pl.ds(pl.multiple_of(gp * 8, 8), 8), :]
            hp_last = jnp.where(g > 0, jnp.broadcast_to(hp8[7:8, :], (8, cw)), 0.0)
            hprev = jnp.where(row >= 1, pltpu.roll(h8, 1, 0), hp_last)
            g_ref[sl, :] = G8
            da_ref[sl, :] = G8 * hprev
            return jnp.broadcast_to(G8[0:1, :], (8, cw)), jnp.broadcast_to(a8[0:1, :], (8, cw))

        z = jnp.zeros((8, cw), F32)
        lax.fori_loop(0, ng, step, (z, z))

    spec = pl.BlockSpec((T, cw), lambda b, c: (b, c))
    sh = jax.ShapeDtypeStruct((M, D), F32)
    return pl.pallas_call(
        kern, name="lru_scan_bwd", grid=(B, D // cw), in_specs=[spec, spec, spec], out_specs=[spec, spec],
        out_shape=[sh, sh], compiler_params=_cparams(("parallel", "parallel")),
    )(a, hs, dhs)


def _lru_fwd(x, p, B, T):
    M, D = x.shape
    h0 = _rms_fwd(x, p["norm"], T, "lru_norm")
    u0 = _matmul(h0, p["w_in"], bias=p["b_in"], name="lru_in")
    fulls = [p["conv_w"], p["conv_b"].reshape(1, D), p["wbd"], p["gate_b"].reshape(1, 2 * D), p["lam"].reshape(1, D)]

    def body(rows, prevs, nexts, fulls, sf, sl):
        xr = _lru_conv(rows[0][:, D:], prevs[0][:, D:], _f(fulls[0]), _f(fulls[1]), sf)
        a, bt = _lru_gates(xr, _lru_pre(xr, fulls[2][...], _f(fulls[3])), _f(fulls[4]))
        return [a, bt], []

    (a, bt), _ = _tile_call(body, rows=[u0], prevs=[0], fulls=fulls, row_outs=[(D, F32), (D, F32)],
                            tm=min(256, T), T=T, name="lru_gates")
    hs = _lru_scan(a, bt, B, T)

    def body2(rows, prevs, nexts, fulls, sf, sl):
        return [rows[0][...] * _gelu(rows[1][:, :D])], []

    (out,), _ = _tile_call(body2, rows=[hs, u0], row_outs=[(D, BF16)], tm=min(512, T), T=T, name="lru_mix")
    y = _matmul(out, p["w_out"], bias=p["b_out"], residual=x, name="lru_out")
    return y, (x, h0, u0, a, hs, out)


def _lru_bwd(dy, saved, p, B, T):
    x, h0, u0, a, hs, out = saved
    M, D = x.shape
    K = p["conv_w"].shape[0]
    dy_mxu = dy.astype(BF16)
    d_out = _matmul(dy_mxu, p["w_out"], mode="nt", name="lru_out_dx")
    d_w_out = _matmul(out, dy_mxu, mode="tn", out_dtype=BF16, name="lru_out_dw")

    def body(rows, prevs, nexts, fulls, sf, sl):
        do, h, u, dyv = rows[0][...], rows[1][...], rows[2][:, :D], rows[3][...]
        act, dact = jax.jvp(_gelu, (u,), (jnp.ones_like(u),))
        return [do * act, do * h * dact], [_colsum(dyv)]

    (d_hs, d_u1), (d_b_out,) = _tile_call(body, rows=[d_out, hs, u0, dy], row_outs=[(D, F32), (D, F32)],
                                          acc_outs=[(1, D)], tm=min(512, T), T=T, name="lru_mix_bwd")
    g_b, d_a = _lru_scan_bwd(a, hs, d_hs, B, T)
    fulls = [p["conv_w"], p["conv_b"].reshape(1, D), p["wbd"], p["gate_b"].reshape(1, 2 * D), p["lam"].reshape(1, D)]

    def body3(rows, prevs, nexts, fulls, sf, sl):
        u, gb_c, da_c, du1 = rows
        cw, cb, wbd, gbias, lam = _f(fulls[0]), _f(fulls[1]), fulls[2][...], _f(fulls[3]), _f(fulls[4])
        tm = u.shape[0]
        u2c, u2p, u2n = u[:, D:], prevs[0][:, D:], nexts[0][:, D:]
        xr_c = _lru_conv(u2c, u2p, cw, cb, sf)
        xr_n = _lru_conv(u2n, u2c[tm - HALO:], cw, cb, False)
        _, vjp_c = jax.vjp(_lru_gates, xr_c, _lru_pre(xr_c, wbd, gbias), lam)
        dxr_c, dpre_c, d_lam = vjp_c((da_c[...], gb_c[...]))
        dxr_add, d_wbd = _lru_pre_t(xr_c, dpre_c, wbd, True)
        dxr_c = dxr_c + dxr_add
        d_gbias = _colsum(dpre_c)
        _, vjp_n = jax.vjp(lambda t, q: _lru_gates(t, q, lam), xr_n, _lru_pre(xr_n, wbd, gbias))
        dxr_n, dpre_n = vjp_n((nexts[2][...], nexts[1][...]))
        dxr_n = dxr_n + _lru_pre_t(xr_n, dpre_n, wbd, False)[0]
        d_u2 = dxr_c * cw[K - 1:K]
        dws = []
        for j in range(K - 1):
            s = K - 1 - j
            d_u2 = d_u2 + _shift_up(dxr_c, dxr_n, s, sl) * cw[j:j + 1]
            dws.append(_colsum(dxr_c * _shift_down(u2c, u2p, s, sf)))
        dws.append(_colsum(dxr_c * u2c))
        d_u = jnp.concatenate([du1[...], d_u2], axis=1)
        return [d_u], [_pad8(jnp.concatenate(dws, axis=0)), _colsum(dxr_c), d_wbd, d_gbias, d_lam,
                       _colsum(d_u)]

    (d_u0,), (d_cw, d_cb, d_wbd, d_gb, d_lam, d_b_in) = _tile_call(
        body3, rows=[u0, g_b, d_a, d_u1], prevs=[0], nexts=[0, 1, 2], fulls=fulls, row_outs=[(2 * D, BF16)],
        acc_outs=[(8, D), (1, D), p["wbd"].shape, (1, 2 * D), (1, D), (1, 2 * D)], tm=min(256, T), T=T,
        name="lru_gates_bwd")
    d_h0 = _matmul(d_u0, p["w_in"], mode="nt", name="lru_in_dx")
    d_w_in = _matmul(h0, d_u0, mode="tn", out_dtype=BF16, name="lru_in_dw", out_cols_by_chip=True)
    dx, d_norm = _rms_bwd(x, p["norm"], d_h0, dy, T, "lru_norm_bwd")
    grads = {"norm": d_norm.reshape(D), "w_in": d_w_in, "b_in": d_b_in.reshape(2 * D), "conv_w": d_cw[:K],
             "conv_b": d_cb.reshape(D), "wbd": d_wbd, "gate_b": d_gb.reshape(2 * D), "lam": d_lam.reshape(D),
             "w_out": d_w_out, "b_out": d_b_out.reshape(D)}
    return dx, grads


def _rwkv_mix(xc, xp, norm, mix, sf):
    h = _rms(xc, norm)
    hp = _rms(xp, norm)
    xx = _shift_down(h, hp, 1, sf) - h
    return h, xx


def _rwkv_pre(k, xw, xa, xg, w0, w1, w2, a0, a1, a2, g1, g2, k_k, k_a, e, et):
    wl = -_softplus(-(w0 + _bdot(jnp.tanh(_bdot(xw, w1)), w2))) - 0.5
    decay = jnp.exp(-jnp.exp(wl))
    a = _sigmoid(a0 + _bdot(_bdot(xa, a1), a2))
    g = _bdot(_sigmoid(_bdot(xg, g1)), g2)
    kk = k * k_k
    nrm = jnp.sqrt(_head_sum(kk * kk, e, et))
    kk = kk / jnp.maximum(nrm, 1e-12)
    k2 = k * (1.0 + (a - 1.0) * k_a)
    return decay, k2, -kk, kk * a, g


def _rwkv_post(y, r, k2, v, g, ln_w, ln_b, r_k, e, et):
    inv = 1.0 / HEAD
    mu = _head_sum(y, e, et) * inv
    yc = y - mu
    var = _head_sum(yc * yc, e, et) * inv
    yn = yc * lax.rsqrt(var + GN_EPS) * ln_w + ln_b
    bonus = _head_sum(r * k2 * r_k, e, et) * v
    return (yn + bonus) * g


def _seg_lane_sums(x, lo_mask):
    s0 = jnp.sum(jnp.where(lo_mask, x, 0.0), axis=1, keepdims=True)
    s1 = jnp.sum(jnp.where(lo_mask, 0.0, x), axis=1, keepdims=True)
    return s0, s1


def _seg_lane_sum(x, lo_mask):
    s0, s1 = _seg_lane_sums(x, lo_mask)
    return jnp.where(lo_mask, s0, s1)


def _pair_consts():
    lane = lax.broadcasted_iota(jnp.int32, (HEAD, 128), 1)
    sub = lax.broadcasted_iota(jnp.int32, (HEAD, 128), 0)
    return lane < HEAD, (jnp.bitwise_and(lane, HEAD - 1) == sub).astype(F32)


def _pair_ones():
    head = jnp.arange(128) // HEAD
    return (head[:, None] == head[None, :]).astype(MXU_DTYPE)


def _split_lhs(x):
    hi = x.astype(MXU_DTYPE)
    return jnp.concatenate([hi, (x - hi.astype(F32)).astype(MXU_DTYPE)], axis=1)


def _spread_lhs(diag, row):
    return (diag * row).astype(MXU_DTYPE)


def _rwkv_scan(r, w, k, v, a, b, B, T):
    M, D = r.shape
    HP, PG, TC, NC, chains = _scan_plan(B, T, D, pairs=2, chunk=64)
    NS = len(chains) * 8
    NG = TC // 8

    def kern(r_ref, w_ref, k_ref, v_ref, a_ref, b_ref, ones_ref, y_ref, st_ref, S_ref, lv_ref, rv_ref, ly_ref, ry_ref):
        c = pl.program_id(1)

        @pl.when(c == 0)
        def _():
            S_ref[...] = jnp.zeros_like(S_ref)

        lo, diag = _pair_consts()
        row8 = lax.broadcasted_iota(jnp.int32, (8, 128), 0)

        def blk(idx):
            return pl.ds(idx * HEAD, HEAD)

        def rows_of(gi):
            return pl.ds(pl.multiple_of(gi * 8, 8), 8)

        def spread(gi, slot):
            for ci, (bi, p) in enumerate(chains):
                v8 = v_ref[bi, rows_of(gi), p * 128:(p + 1) * 128]
                for j in range(8):
                    lv_ref[slot, blk(ci * 8 + j), :] = _spread_lhs(diag, v8[j:j + 1, :])
            rv_ref[slot] = jnp.dot(lv_ref[slot], ones_ref[...], preferred_element_type=F32)

        def recur(gi, slot):
            sl = rows_of(gi)
            tiles = [[ref[bi, sl, p * 128:(p + 1) * 128] for ref in (r_ref, w_ref, k_ref, a_ref, b_ref)]
                     for bi, p in chains]
            S = [S_ref[ci] for ci in range(len(chains))]
            for j in range(8):
                for ci, (bi, p) in enumerate(chains):
                    r8, w8, k8, a8, b8 = tiles[ci]
                    idx = ci * 8 + j
                    st_ref[p, bi, gi * 8 + j] = S[ci]
                    sa = _seg_lane_sum(S[ci] * a8[j:j + 1, :], lo)
                    S[ci] = S[ci] * w8[j:j + 1, :] + sa * b8[j:j + 1, :] + rv_ref[slot, blk(idx), :] * k8[j:j + 1, :]
                    ly_ref[slot, blk(idx), :] = (S[ci] * r8[j:j + 1, :]).astype(MXU_DTYPE)
            for ci in range(len(chains)):
                S_ref[ci] = S[ci]

        def emit(gi, slot):
            ry_ref[slot] = jnp.dot(ly_ref[slot], ones_ref[...], preferred_element_type=F32)
            for ci, (bi, p) in enumerate(chains):
                y8 = jnp.zeros((8, 128), F32)
                for j in range(8):
                    y8 = jnp.where(row8 == j, _colsum(diag * ry_ref[slot, blk(ci * 8 + j), :]), y8)
                y_ref[bi, rows_of(gi), p * 128:(p + 1) * 128] = y8

        spread(0, 0)
        ly_ref[1] = jnp.zeros_like(ly_ref[1])

        def two_groups(m, _):
            g0, g1 = 2 * m, 2 * m + 1
            spread(g1, 1)
            recur(g0, 0)
            emit(jnp.maximum(g0 - 1, 0), 1)
            spread(jnp.minimum(g1 + 1, NG - 1), 0)
            recur(g1, 1)
            emit(g0, 0)
            return 0

        lax.fori_loop(0, NG // 2, two_groups, 0)
        emit(NG - 1, 1)

    spec = pl.BlockSpec((B, TC, 128 * PG), lambda hp, c: (0, c, hp))
    st_spec = pl.BlockSpec((PG, B, TC, HEAD, 128), lambda hp, c: (hp, 0, c, 0, 0))
    y, st = pl.pallas_call(
        kern, name="rwkv_scan", grid=(HP // PG, NC),
        in_specs=[spec] * 6 + [pl.BlockSpec((128, 128), lambda hp, c: (0, 0))], out_specs=[spec, st_spec],
        out_shape=[jax.ShapeDtypeStruct((B, T, D), F32), jax.ShapeDtypeStruct((HP, B, T, HEAD, 128), F32)],
        scratch_shapes=[pltpu.VMEM((len(chains), HEAD, 128), F32),
                        pltpu.VMEM((2, NS * HEAD, 128), MXU_DTYPE), pltpu.VMEM((2, NS * HEAD, 128), F32),
                        pltpu.VMEM((2, NS * HEAD, 128), MXU_DTYPE), pltpu.VMEM((2, NS * HEAD, 128), F32)],
        compiler_params=_cparams(("parallel", "arbitrary")),
    )(*[x.reshape(B, T, D) for x in (r, w, k, v, a, b)], _pair_ones())
    return y.reshape(M, D), st


def _scan_plan(B, T, D, pairs, chunk):
    HP = D // 128
    PG = pairs if HP % pairs == 0 else 1
    TC = min(chunk, T)
    assert TC % 16 == 0 and T % TC == 0
    return HP, PG, TC, T // TC, [(bi, p) for bi in range(B) for p in range(PG)]


def _rwkv_scan_bwd(r, w, k, v, a, b, st, dy, B, T):
    M, D = r.shape
    HP, PG, TC, NC, chains = _scan_plan(B, T, D, pairs=1, chunk=128)
    NS = len(chains) * 8

    NG = TC // 8

    def kern(r_ref, w_ref, k_ref, v_ref, a_ref, b_ref, st_ref, dy_ref, ones_ref,
             dr_ref, dw_ref, dk_ref, dv_ref, da_ref, db_ref, dS_ref, lp_ref, rp_ref, lq_ref, rq_ref):
        c = pl.program_id(1)

        @pl.when(c == 0)
        def _():
            dS_ref[...] = jnp.zeros_like(dS_ref)

        lo, diag = _pair_consts()
        row8 = lax.broadcasted_iota(jnp.int32, (8, 128), 0)

        def blk(idx):
            return pl.ds(idx * HEAD, HEAD)

        def rows_of(gi):
            return pl.ds(pl.multiple_of(gi * 8, 8), 8)

        def spread(gi, slot):
            sl = rows_of(gi)
            for ci, (bi, p) in enumerate(chains):
                lanes = slice(p * 128, (p + 1) * 128)
                v8, dy8, a8 = v_ref[bi, sl, lanes], dy_ref[bi, sl, lanes], a_ref[bi, sl, lanes]
                for j in range(8):
                    idx = ci * 8 + j
                    lp_ref[slot, blk(idx), :] = _spread_lhs(diag, v8[j:j + 1, :])
                    lp_ref[slot, blk(NS + idx), :] = _spread_lhs(diag, dy8[j:j + 1, :])
                    lp_ref[slot, blk(2 * NS + idx), :] = (st_ref[p, bi, gi * 8 + j] * a8[j:j + 1, :]).astype(MXU_DTYPE)
            rp_ref[slot] = jnp.dot(lp_ref[slot], ones_ref[...], preferred_element_type=F32)

        def recur(gi, slot):
            sl = rows_of(gi)
            tiles = [[ref[bi, sl, p * 128:(p + 1) * 128] for ref in (r_ref, w_ref, k_ref, a_ref, b_ref)]
                     for bi, p in chains]
            dS = [dS_ref[ci] for ci in range(len(chains))]
            acc = [[jnp.zeros((8, 128), F32) for _ in range(5)] for _ in chains]
            St = [None] * len(chains)
            for j in range(7, -1, -1):
                for ci, (bi, p) in enumerate(chains):
                    r8, w8, k8, a8, b8 = tiles[ci]
                    rj, wj, kj, aj, bj = r8[j:j + 1, :], w8[j:j + 1, :], k8[j:j + 1, :], a8[j:j + 1, :], b8[j:j + 1, :]
                    idx = ci * 8 + j
                    Sp = st_ref[p, bi, gi * 8 + j]
                    vb, dyb, sa = rp_ref[slot, blk(idx), :], rp_ref[slot, blk(NS + idx), :], rp_ref[slot, blk(2 * NS + idx), :]
                    if j == 7:
                        St[ci] = Sp * wj + sa * bj + vb * kj
                    d = dS[ci] + dyb * rj
                    dsa = _seg_lane_sum(d * bj, lo)
                    lq_ref[slot, blk(idx), :] = (d * kj).astype(MXU_DTYPE)
                    rows = (_colsum(St[ci] * dyb), _colsum(d * Sp), _colsum(d * vb), _colsum(Sp * dsa), _colsum(d * sa))
                    acc[ci] = [jnp.where(row8 == j, rw, a8_) for rw, a8_ in zip(rows, acc[ci], strict=True)]
                    dS[ci] = d * wj + dsa * aj
                    St[ci] = Sp
            for ci, (bi, p) in enumerate(chains):
                dS_ref[ci] = dS[ci]
                for ref, a8_ in zip((dr_ref, dw_ref, dk_ref, da_ref, db_ref), acc[ci], strict=True):
                    ref[bi, sl, p * 128:(p + 1) * 128] = a8_

        def emit(gi, slot):
            rq_ref[slot] = jnp.dot(lq_ref[slot], ones_ref[...], preferred_element_type=F32)
            for ci, (bi, p) in enumerate(chains):
                dv8 = jnp.zeros((8, 128), F32)
                for j in range(8):
                    dv8 = jnp.where(row8 == j, _colsum(diag * rq_ref[slot, blk(ci * 8 + j), :]), dv8)
                dv_ref[bi, rows_of(gi), p * 128:(p + 1) * 128] = dv8

        spread(NG - 1, 0)
        lq_ref[1] = jnp.zeros_like(lq_ref[1])

        def two_groups(m, _):
            g0, g1 = NG - 1 - 2 * m, NG - 2 - 2 * m
            spread(g1, 1)
            recur(g0, 0)
            emit(jnp.minimum(g0 + 1, NG - 1), 1)
            spread(jnp.maximum(g1 - 1, 0), 0)
            recur(g1, 1)
            emit(g0, 0)
            return 0

        lax.fori_loop(0, NG // 2, two_groups, 0)
        emit(0, 1)

    spec = pl.BlockSpec((B, TC, 128 * PG), lambda hp, c: (0, NC - 1 - c, hp))
    st_spec = pl.BlockSpec((PG, B, TC, HEAD, 128), lambda hp, c: (hp, 0, NC - 1 - c, 0, 0))
    sh = jax.ShapeDtypeStruct((B, T, D), F32)
    outs = pl.pallas_call(
        kern, name="rwkv_scan_bwd", grid=(HP // PG, NC),
        in_specs=[spec] * 6 + [st_spec, spec, pl.BlockSpec((128, 128), lambda hp, c: (0, 0))], out_specs=[spec] * 6,
        out_shape=[sh] * 6,
        scratch_shapes=[pltpu.VMEM((len(chains), HEAD, 128), F32),
                        pltpu.VMEM((2, 3 * NS * HEAD, 128), MXU_DTYPE), pltpu.VMEM((2, 3 * NS * HEAD, 128), F32),
                        pltpu.VMEM((2, NS * HEAD, 128), MXU_DTYPE), pltpu.VMEM((2, NS * HEAD, 128), F32)],
        compiler_params=_cparams(("parallel", "arbitrary")),
    )(*[x.reshape(B, T, D) for x in (r, w, k, v, a, b)], st, dy.reshape(B, T, D), _pair_ones())
    return [o.reshape(M, D) for o in outs]


def _head_mats(D):
    ch = jnp.arange(D) // HEAD
    e = (ch[:, None] == jnp.arange(128)[None, :]).astype(MXU_DTYPE)
    return jnp.concatenate([e, e], axis=0), jnp.concatenate([e.T, e.T], axis=0)


def _rwkv_fwd(x, p, B, T):
    M, D = x.shape
    e, et = _head_mats(D)
    norm = p["norm"].reshape(1, D)

    def body(rows, prevs, nexts, fulls, sf, sl):
        h, xx = _rwkv_mix(rows[0][...], prevs[0][...], _f(fulls[0]), None, sf)
        mix = _f(fulls[1])
        return [h + xx * mix[i:i + 1] for i in range(6)], []

    xs, _ = _tile_call(body, rows=[x], prevs=[0], fulls=[norm, _pad8(p["mix"])], row_outs=[(D, BF16)] * 6,
                       tm=min(256, T), T=T, name="rwkv_mix")
    r = _matmul(xs[0], p["w_r"], name="rwkv_r")
    k = _matmul(xs[1], p["w_k"], name="rwkv_k")
    v = _matmul(xs[2], p["w_v"], name="rwkv_v")
    pre_fulls = [p["w0"].reshape(1, D), p["w1"], p["w2"], p["a0"].reshape(1, D), p["a1"], p["a2"], p["g1"], p["g2"],
                 p["k_k"].reshape(1, D), p["k_a"].reshape(1, D), e, et]

    def body2(rows, prevs, nexts, fulls, sf, sl):
        outs = _rwkv_pre(rows[0][...], _f(rows[1]), _f(rows[2]), _f(rows[3]), *[f[...] for f in fulls])
        return list(outs), []

    (decay, k2, kkn, bb, g), _ = _tile_call(body2, rows=[k, xs[3], xs[4], xs[5]], fulls=pre_fulls,
                                            row_outs=[(D, F32)] * 5, tm=min(256, T), T=T, name="rwkv_pre")
    y, st = _rwkv_scan(r, decay, k2, v, kkn, bb, B, T)
    post_fulls = [p["ln_w"].reshape(1, D), p["ln_b"].reshape(1, D), p["r_k"].reshape(1, D), e, et]

    def body3(rows, prevs, nexts, fulls, sf, sl):
        return [_rwkv_post(*[rr[...] for rr in rows], *[f[...] for f in fulls])], []

    (z,), _ = _tile_call(body3, rows=[y, r, k2, v, g], fulls=post_fulls, row_outs=[(D, BF16)], tm=min(256, T), T=T,
                         name="rwkv_post")
    out = _matmul(z, p["w_out"], residual=x, name="rwkv_out")
    return out, (x, xs, r, k, v, decay, k2, kkn, bb, g, y, st, z)


def _rwkv_bwd(dout, saved, p, B, T):
    x, xs, r, k, v, decay, k2, kkn, bb, g, y, st, z = saved
    M, D = x.shape
    e, et = _head_mats(D)
    dout_mxu = dout.astype(BF16)
    d_z = _matmul(dout_mxu, p["w_out"], mode="nt", name="rwkv_out_dx")
    d_w_out = _matmul(z, dout_mxu, mode="tn", out_dtype=BF16, name="rwkv_out_dw")
    post_fulls = [p["ln_w"].reshape(1, D), p["ln_b"].reshape(1, D), p["r_k"].reshape(1, D), e, et]

    def body(rows, prevs, nexts, fulls, sf, sl):
        prim = [rr[...] for rr in rows[:5]] + [f[...] for f in fulls]
        _, vjp = jax.vjp(_rwkv_post, *prim)
        ct = vjp(rows[5][...])
        return list(ct[:5]), list(ct[5:8])

    (d_y, d_r1, d_k21, d_v1, d_g), (d_ln_w, d_ln_b, d_r_k) = _tile_call(
        body, rows=[y, r, k2, v, g, d_z], fulls=post_fulls, row_outs=[(D, F32)] * 5, acc_outs=[(1, D)] * 3,
        tm=min(256, T), T=T, name="rwkv_post_bwd")
    d_r2, d_w, d_k22, d_v2, d_kkn, d_bb = _rwkv_scan_bwd(r, decay, k2, v, kkn, bb, st, d_y, B, T)
    pre_fulls = [p["w0"].reshape(1, D), p["w1"], p["w2"], p["a0"].reshape(1, D), p["a1"], p["a2"], p["g1"], p["g2"],
                 p["k_k"].reshape(1, D), p["k_a"].reshape(1, D), e, et]

    def body2(rows, prevs, nexts, fulls, sf, sl):
        prim = [rows[0][...], _f(rows[1]), _f(rows[2]), _f(rows[3])] + [f[...] for f in fulls]
        _, vjp = jax.vjp(_rwkv_pre, *prim)
        ct = vjp((rows[4][...], rows[5][...] + rows[6][...], rows[7][...], rows[8][...], rows[9][...]))
        d_r = rows[10][...] + rows[11][...]
        d_v = rows[12][...] + rows[13][...]
        return [ct[0], ct[1], ct[2], ct[3], d_r, d_v], [c.astype(F32) for c in ct[4:14]]

    acc_shapes = [f.shape for f in pre_fulls[:10]]
    (d_k, d_xw, d_xa, d_xg, d_r, d_v), pgr = _tile_call(
        body2, rows=[k, xs[3], xs[4], xs[5], d_w, d_k21, d_k22, d_kkn, d_bb, d_g, d_r1, d_r2, d_v1, d_v2],
        fulls=pre_fulls, row_outs=[(D, BF16), (D, F32), (D, F32), (D, F32), (D, BF16), (D, BF16)], acc_outs=acc_shapes,
        tm=min(128, T), T=T, name="rwkv_pre_bwd")
    d_xr = _matmul(d_r, p["w_r"], mode="nt", name="rwkv_r_dx")
    d_xk = _matmul(d_k, p["w_k"], mode="nt", name="rwkv_k_dx")
    d_xv = _matmul(d_v, p["w_v"], mode="nt", name="rwkv_v_dx")
    d_wr = _matmul(xs[0], d_r, mode="tn", out_dtype=BF16, name="rwkv_r_dw")
    d_wk = _matmul(xs[1], d_k, mode="tn", out_dtype=BF16, name="rwkv_k_dw")
    d_wv = _matmul(xs[2], d_v, mode="tn", out_dtype=BF16, name="rwkv_v_dw")
    norm = p["norm"].reshape(1, D)

    def body3(rows, prevs, nexts, fulls, sf, sl):
        xc, xp = rows[0][...], prevs[0][...]
        nrm, mix = _f(fulls[0]), _f(fulls[1])
        h, xx = _rwkv_mix(xc, xp, nrm, None, sf)
        dxs = [rows[1 + i][...] for i in range(6)]
        dxs_n = [nexts[i][...] for i in range(6)]
        d_h = jnp.zeros_like(h)
        d_sh = jnp.zeros_like(h)
        d_sh_n = jnp.zeros_like(dxs_n[0])
        dmix = []
        for i in range(6):
            m = mix[i:i + 1]
            d_h = d_h + dxs[i] * (1.0 - m)
            d_sh = d_sh + dxs[i] * m
            d_sh_n = d_sh_n + dxs_n[i] * m
            dmix.append(_colsum(dxs[i] * xx))
        d_h = d_h + _shift_up(d_sh, d_sh_n, 1, sl)
        _, vjp = jax.vjp(_rms, xc, nrm)
        dx, dn = vjp(d_h)
        return [dx + rows[7][...]], [dn, _pad8(jnp.concatenate(dmix, axis=0))]

    (dx,), (d_norm, d_mix) = _tile_call(
        body3, rows=[x, d_xr, d_xk, d_xv, d_xw, d_xa, d_xg, dout], prevs=[0], nexts=[1, 2, 3, 4, 5, 6],
        fulls=[norm, _pad8(p["mix"])], row_outs=[(D, F32)], acc_outs=[(1, D), (8, D)], tm=min(256, T), T=T,
        name="rwkv_mix_bwd")
    names = ["w0", "w1", "w2", "a0", "a1", "a2", "g1", "g2", "k_k", "k_a"]
    grads = {n: gr.reshape(p[n].shape) for n, gr in zip(names, pgr, strict=True)}
    grads.update({"norm": d_norm.reshape(D), "mix": d_mix[:6], "w_r": d_wr, "w_k": d_wk, "w_v": d_wv,
                  "r_k": d_r_k.reshape(p["r_k"].shape), "ln_w": d_ln_w.reshape(D), "ln_b": d_ln_b.reshape(D),
                  "w_out": d_w_out})
    return dx, grads


def _loss_head(x, g, tgt, T):
    M, D = x.shape

    def body(rows, prevs, nexts, fulls, sf, sl):
        xv, gv = rows[0][...], _f(fulls[0])
        yv, vjp = jax.vjp(_rms, xv, gv)
        err = yv - rows[1][...]
        dx, dg = vjp(err * (1.0 / D))
        part = jnp.sum(_colsum(err * err), axis=1, keepdims=True) * (0.5 / D)
        return [dx], [dg, jnp.broadcast_to(part, (1, 128))]

    (dx,), (dg, loss) = _tile_call(body, rows=[x, tgt], fulls=[g.reshape(1, D)], row_outs=[(D, F32)],
                                   acc_outs=[(1, D), (1, 128)], tm=min(512, T), T=T, name="loss_head")
    return loss[0, 0], dx, dg.reshape(D)


def _local_step(x3, tgt3, P):
    B, T, D = x3.shape
    x, tgt = x3.reshape(B * T, D), tgt3.reshape(B * T, D)
    x1, s_lru = _lru_fwd(x, P["lru"], B, T)
    x2, s_f0 = _ffn_fwd(x1, P["ffn0"], T, "0")
    x3_, s_rw = _rwkv_fwd(x2, P["rwkv"], B, T)
    x4, s_f1 = _ffn_fwd(x3_, P["ffn1"], T, "1")
    loss, d4, d_fn = _loss_head(x4, P["final_norm"], tgt, T)
    d3, g_f1 = _ffn_bwd(d4, s_f1, P["ffn1"], T, "1")
    d2, g_rw = _rwkv_bwd(d3, s_rw, P["rwkv"], B, T)
    d1, g_f0 = _ffn_bwd(d2, s_f0, P["ffn0"], T, "0")
    d0, g_lru = _lru_bwd(d1, s_lru, P["lru"], B, T)
    return loss, d0.reshape(B, T, D), {"lru": g_lru, "ffn0": g_f0, "rwkv": g_rw, "ffn1": g_f1, "final_norm": d_fn}


WEIGHTS = ['lru_norm', 'lru_w_in', 'lru_b_in', 'lru_conv_w', 'lru_conv_b', 'lru_gate_w', 'lru_gate_b', 'lru_lambda',
           'lru_w_out', 'lru_b_out', 'rwkv_norm', 'rwkv_mix', 'rwkv_w_rkv', 'rwkv_w0', 'rwkv_w1', 'rwkv_w2', 'rwkv_a0',
           'rwkv_a1', 'rwkv_a2', 'rwkv_g1', 'rwkv_g2', 'rwkv_k_k', 'rwkv_k_a', 'rwkv_r_k', 'rwkv_ln_w', 'rwkv_ln_b',
           'rwkv_w_out', 'ffn_norm', 'ffn_w_up', 'ffn_conv_w', 'ffn_conv_b', 'ffn_w_down', 'final_norm']
SHARD_AXIS = {'lru_w_in': 2, 'lru_conv_w': 2, 'lru_w_out': 1, 'rwkv_norm': 1, 'rwkv_mix': 2, 'rwkv_w_rkv': 2,
              'rwkv_w0': 1, 'rwkv_w1': 1, 'rwkv_w2': 2, 'rwkv_a0': 1, 'rwkv_a1': 1, 'rwkv_a2': 2, 'rwkv_g1': 1,
              'rwkv_g2': 2, 'rwkv_k_k': 1, 'rwkv_k_a': 1, 'rwkv_ln_w': 1, 'rwkv_ln_b': 1, 'rwkv_w_out': 1,
              'ffn_w_up': 2, 'ffn_conv_w': 2, 'ffn_w_down': 1}
MXU_WEIGHTS = ('lru_w_in', 'lru_w_out', 'rwkv_w_rkv', 'rwkv_w_out', 'ffn_w_up', 'ffn_w_down')
N_CHIPS = 4
LANES = 1024


def _pack(arrs, dtype, row_mult):
    flat = jnp.concatenate([a.reshape(-1).astype(dtype) for a in arrs])
    n = flat.shape[0]
    unit = row_mult * LANES
    tot = -(-n // unit) * unit
    if tot > n:
        flat = jnp.concatenate([flat, jnp.zeros((tot - n,), dtype)])
    return flat.reshape(tot // LANES, LANES)


def _unpack(buf, shapes):
    flat = buf.reshape(-1)
    out, off = [], 0
    for s in shapes:
        n = 1
        for d in s:
            n *= d
        out.append(flat[off:off + n].reshape(s))
        off += n
    return out


def _to_shards(full, axis):
    return jnp.stack(jnp.split(full, N_CHIPS, axis=axis))


MESH_ID = pl.DeviceIdType.MESH


ANY_SPEC = pl.BlockSpec(memory_space=pl.ANY)
COMM_PARAMS = pltpu.CompilerParams(has_side_effects=True)


def _mesh_place():
    x, y, c = lax.axis_index("x"), lax.axis_index("y"), lax.axis_index("c")
    return x, y, c, 2 * x + y, [(1 - x, y), (x, 1 - y), (1 - x, 1 - y)]


def _gather_all(arrs, name):
    n = len(arrs)

    def body(*refs):
        ins, outs = refs[:n], refs[n:2 * n]
        send_sems, recv_sems, local_sems = refs[2 * n:]
        x, y, c, p, chips = _mesh_place()
        sibling = (x, y, 1 - c)

        def rows(a, which):
            h = arrs[a].shape[0] // 2
            return pl.ds(which * h, h)

        def copy(a, k, region, src, to):
            return pltpu.make_async_remote_copy(src_ref=src, dst_ref=region, send_sem=send_sems.at[a, k],
                                                recv_sem=recv_sems.at[a, k], device_id=to, device_id_type=MESH_ID)

        local = [pltpu.make_async_copy(ins[a], outs[a].at[p], local_sems.at[a]) for a in range(n)]
        for cp in local:
            cp.start()
        first = [copy(a, j, outs[a].at[p, rows(a, c)], ins[a].at[rows(a, c)], (qx, qy, c))
                 for a in range(n) for j, (qx, qy) in enumerate(chips)]
        for cp in first:
            cp.start()
        passed = []
        for a in range(n):
            for j, (qx, qy) in enumerate(chips):
                region = outs[a].at[2 * qx + qy, rows(a, c)]
                copy(a, j, region, region, (qx, qy, c)).wait_recv()
                fw = copy(a, 3 + j, region, region, sibling)
                fw.start()
                passed.append(fw)
        for a in range(n):
            for j, (qx, qy) in enumerate(chips):
                region = outs[a].at[2 * qx + qy, rows(a, 1 - c)]
                copy(a, 3 + j, region, region, sibling).wait_recv()
        for cp in first + passed:
            cp.wait_send()
        for cp in local:
            cp.wait()

    return pl.pallas_call(
        body, name=name, out_shape=[jax.ShapeDtypeStruct((N_CHIPS,) + a.shape, a.dtype) for a in arrs],
        in_specs=[ANY_SPEC] * n, out_specs=[ANY_SPEC] * n,
        scratch_shapes=[pltpu.SemaphoreType.DMA((n, 6)), pltpu.SemaphoreType.DMA((n, 6)), pltpu.SemaphoreType.DMA((n,))],
        compiler_params=COMM_PARAMS,
    )(*arrs)


def _pair_swap_all(gs, rep, name):
    n = len(gs)

    def body(*refs):
        ins, outs = refs[:n + 1], refs[n + 1:2 * n + 2]
        send_sems, recv_sems = refs[2 * n + 2:]
        x, y, c, _, _ = _mesh_place()
        copies = []
        for a in range(n + 1):
            src = ins[a]
            if a < n:
                h = gs[a].shape[1] // 2
                src = src.at[:, pl.ds((1 - c) * h, h), :]
            copies.append(pltpu.make_async_remote_copy(src_ref=src, dst_ref=outs[a], send_sem=send_sems.at[a],
                                                       recv_sem=recv_sems.at[a], device_id=(x, y, 1 - c),
                                                       device_id_type=MESH_ID))
        for cp in copies:
            cp.start()
        for cp in copies:
            cp.wait()

    shapes = [jax.ShapeDtypeStruct((N_CHIPS, g.shape[1] // 2, g.shape[2]), g.dtype) for g in gs]
    shapes.append(jax.ShapeDtypeStruct(rep.shape, rep.dtype))
    res = pl.pallas_call(
        body, name=name, out_shape=shapes, in_specs=[ANY_SPEC] * (n + 1), out_specs=[ANY_SPEC] * (n + 1),
        scratch_shapes=[pltpu.SemaphoreType.DMA((n + 1,)), pltpu.SemaphoreType.DMA((n + 1,))],
        compiler_params=COMM_PARAMS,
    )(*gs, rep)
    return res[:n], res[n]


def _chip_exchange_all(ps, rep, name):
    n = len(ps)

    def body(*refs):
        ins, outs = refs[:n + 1], refs[n + 1:2 * n + 2]
        send_sems, recv_sems, local_sems = refs[2 * n + 2:]
        x, y, c, p, chips = _mesh_place()

        def src(a, q):
            return ins[a].at[q] if a < n else ins[a]

        local = [pltpu.make_async_copy(src(a, p), outs[a].at[p], local_sems.at[a]) for a in range(n + 1)]
        for cp in local:
            cp.start()
        sends, recvs = [], []
        for a in range(n + 1):
            for j, (qx, qy) in enumerate(chips):
                q = 2 * qx + qy
                for dst, keep in ((outs[a].at[p], sends), (outs[a].at[q], recvs)):
                    keep.append(pltpu.make_async_remote_copy(
                        src_ref=src(a, q), dst_ref=dst, send_sem=send_sems.at[a, j], recv_sem=recv_sems.at[a, j],
                        device_id=(qx, qy, c), device_id_type=MESH_ID))
        for cp in sends:
            cp.start()
        for cp in recvs:
            cp.wait_recv()
        for cp in sends:
            cp.wait_send()
        for cp in local:
            cp.wait()

    shapes = [jax.ShapeDtypeStruct(g.shape, g.dtype) for g in ps]
    shapes.append(jax.ShapeDtypeStruct((N_CHIPS,) + rep.shape, rep.dtype))
    res = pl.pallas_call(
        body, name=name, out_shape=shapes, in_specs=[ANY_SPEC] * (n + 1), out_specs=[ANY_SPEC] * (n + 1),
        scratch_shapes=[pltpu.SemaphoreType.DMA((n + 1, 3)), pltpu.SemaphoreType.DMA((n + 1, 3)),
                        pltpu.SemaphoreType.DMA((n + 1,))],
        compiler_params=COMM_PARAMS,
    )(*ps, rep)
    return res[:n], res[n]


def _half_swap_all(ts, name):
    n = len(ts)

    def body(*refs):
        ins, outs = refs[:n], refs[n:2 * n]
        send_sems, recv_sems = refs[2 * n:]
        x, y, c, _, _ = _mesh_place()
        copies = [pltpu.make_async_remote_copy(src_ref=ins[a], dst_ref=outs[a], send_sem=send_sems.at[a],
                                               recv_sem=recv_sems.at[a], device_id=(x, y, 1 - c), device_id_type=MESH_ID)
                  for a in range(n)]
        for cp in copies:
            cp.start()
        for cp in copies:
            cp.wait()

    return pl.pallas_call(
        body, name=name, out_shape=[jax.ShapeDtypeStruct(t.shape, t.dtype) for t in ts],
        in_specs=[ANY_SPEC] * n, out_specs=[ANY_SPEC] * n,
        scratch_shapes=[pltpu.SemaphoreType.DMA((n,)), pltpu.SemaphoreType.DMA((n,))],
        compiler_params=COMM_PARAMS,
    )(*ts)


def _pick_rows(R, cap=256):
    for t in (512, 256, 128, 64, 32, 16, 8):
        if t <= cap and R % t == 0:
            return t
    return R


def _pair_sum(g, got, name):
    _, R, C = g.shape
    h = R // 2
    th = _pick_rows(h)

    def kern(g_ref, got_ref, o_ref):
        both = g_ref[...].astype(F32)
        mine = jnp.where(lax.axis_index("c") == 0, both[0], both[1])
        o_ref[...] = (mine + got_ref[...].astype(F32)).astype(o_ref.dtype)

    return pl.pallas_call(
        kern, name=name, grid=(N_CHIPS, h // th),
        in_specs=[pl.BlockSpec((None, 2, th, C), lambda q, i: (q, 0, i, 0)), pl.BlockSpec((None, th, C), lambda q, i: (q, i, 0))],
        out_specs=pl.BlockSpec((None, th, C), lambda q, i: (q, i, 0)),
        out_shape=jax.ShapeDtypeStruct((N_CHIPS, h, C), BF16), compiler_params=_cparams(("parallel", "parallel")),
    )(g.reshape(N_CHIPS, 2, h, C), got)


def _rep_pair_sum(rep, got, name):
    R, C = rep.shape
    tr = _pick_rows(R)

    def kern(a_ref, b_ref, o_ref):
        o_ref[...] = (a_ref[...] + b_ref[...]).astype(o_ref.dtype)

    spec = pl.BlockSpec((tr, C), lambda i: (i, 0))
    return pl.pallas_call(kern, name=name, grid=(R // tr,), in_specs=[spec, spec], out_specs=spec,
                          out_shape=jax.ShapeDtypeStruct((R, C), BF16), compiler_params=_cparams(("parallel",)))(rep, got)


def _chip_sum(arrived, name):
    _, R, C = arrived.shape
    tr = _pick_rows(R)

    def kern(a_ref, o_ref):
        acc = a_ref[0].astype(F32)
        for q in range(1, N_CHIPS):
            acc = acc + a_ref[q].astype(F32)
        o_ref[...] = acc

    return pl.pallas_call(
        kern, name=name, grid=(R // tr,), in_specs=[pl.BlockSpec((N_CHIPS, tr, C), lambda i: (0, i, 0))],
        out_specs=pl.BlockSpec((tr, C), lambda i: (i, 0)), out_shape=jax.ShapeDtypeStruct((R, C), F32),
        compiler_params=_cparams(("parallel",)),
    )(arrived)


def _adam_math(w, g, m, v):
    c1 = 1.0 / (1.0 - ADAM_B1 ** ADAM_STEP)
    c2 = 1.0 / (1.0 - ADAM_B2 ** ADAM_STEP)
    nm = ADAM_B1 * m + (1.0 - ADAM_B1) * g
    nv = ADAM_B2 * v + (1.0 - ADAM_B2) * (g * g)
    return -ADAM_LR * ((nm * c1) / (jnp.sqrt(nv * c2) + ADAM_EPS) + ADAM_WD * w), nm, nv


def _adamw_halves(w, m, v, mine, other, name):
    R, C = w.shape
    h = R // 2
    th = _pick_rows(h)
    nt = h // th

    def kern(w_ref, m_ref, v_ref, a_ref, b_ref, g_ref, d_ref, nm_ref, nv_ref):
        g = jnp.where(pl.program_id(0) == lax.axis_index("c"), a_ref[...], b_ref[...])
        d, nm, nv = _adam_math(w_ref[...], g, m_ref[...], v_ref[...])
        g_ref[...] = g
        d_ref[...] = d
        nm_ref[...] = nm
        nv_ref[...] = nv

    full = pl.BlockSpec((th, C), lambda hh, i: (hh * nt + i, 0))
    half = pl.BlockSpec((th, C), lambda hh, i: (i, 0))
    sh = jax.ShapeDtypeStruct((R, C), F32)
    return pl.pallas_call(kern, name=name, grid=(2, nt), in_specs=[full] * 3 + [half] * 2, out_specs=[full] * 4,
                          out_shape=[sh] * 4, compiler_params=_cparams(("parallel", "parallel")))(w, m, v, mine, other)


def _adamw_call(w, g, m, v, name):
    R = w.shape[0]
    tr = _pick_rows(R)
    c1 = 1.0 / (1.0 - ADAM_B1 ** ADAM_STEP)
    c2 = 1.0 / (1.0 - ADAM_B2 ** ADAM_STEP)

    def kern(w_ref, g_ref, m_ref, v_ref, d_ref, nm_ref, nv_ref):
        gv = g_ref[...]
        nm = ADAM_B1 * m_ref[...] + (1.0 - ADAM_B1) * gv
        nv = ADAM_B2 * v_ref[...] + (1.0 - ADAM_B2) * (gv * gv)
        d_ref[...] = -ADAM_LR * ((nm * c1) / (jnp.sqrt(nv * c2) + ADAM_EPS) + ADAM_WD * w_ref[...])
        nm_ref[...] = nm
        nv_ref[...] = nv

    spec = pl.BlockSpec((tr, LANES), lambda i: (i, 0))
    sh = jax.ShapeDtypeStruct((R, LANES), F32)
    return pl.pallas_call(kern, name=name, grid=(R // tr,), in_specs=[spec] * 4, out_specs=[spec] * 3,
                          out_shape=[sh] * 3, compiler_params=_cparams(("parallel",)))(w, g, m, v)


GATE_GROUP = 256


def _gate_dense(gate_w):
    _, nb, bw, _ = gate_w.shape
    D = nb * bw
    gb = min(GATE_GROUP, D)
    per = gb // bw
    w = gate_w.reshape(2, D // gb, per, bw, bw)
    eye = jnp.eye(per, dtype=gate_w.dtype)
    dense = jnp.einsum('grncd,nm->rncgmd', w, eye)
    return dense.reshape(D, 2 * gb)


def _gate_blocks(d_dense, nb):
    D, gb2 = d_dense.shape
    gb, bw = gb2 // 2, D // nb
    per = gb // bw
    g = d_dense.reshape(D // gb, per, bw, 2, per, bw)
    return jnp.einsum('rncgnd->grncd', g).reshape(2, nb, bw, bw)


def _step(W, M1, V1, x, tgt):
    sharded = [n for n in WEIGHTS if n in SHARD_AXIS]
    repl = [n for n in WEIGHTS if n not in SHARD_AXIS]
    small = [n for n in sharded if n not in MXU_WEIGHTS]
    D = x.shape[-1]

    def rows2d(a):
        return a.reshape(-1, a.shape[-1])

    small_buf = _pack([W[n] for n in small], F32, 16)
    gathered = _gather_all([rows2d(W[n]).astype(MXU_DTYPE) for n in MXU_WEIGHTS] + [small_buf], "gather_weights")
    mats = {n: g.reshape((N_CHIPS,) + W[n].shape[-3:]) for n, g in zip(MXU_WEIGHTS, gathered[:-1], strict=True)}
    per_chip = [_unpack(gathered[-1][q], [W[n].shape for n in small]) for q in range(N_CHIPS)]
    full = {n: jnp.concatenate([per_chip[q][i] for q in range(N_CHIPS)], axis=SHARD_AXIS[n]) for i, n in enumerate(small)}
    for n in repl:
        full[n] = W[n]

    def by_rows(name, layer):
        m = mats[name][:, layer]
        return m.reshape(N_CHIPS * m.shape[1], m.shape[2])

    P = {
        "lru": {"norm": full["lru_norm"][0], "w_in": _W(mats["lru_w_in"], 0), "b_in": full["lru_b_in"][0],
                "conv_w": full["lru_conv_w"][0], "conv_b": full["lru_conv_b"][0],
                "wbd": _gate_dense(full["lru_gate_w"][0]).astype(MXU_DTYPE), "gate_b": full["lru_gate_b"][0].reshape(-1),
                "lam": full["lru_lambda"][0], "w_out": by_rows("lru_w_out", 0), "b_out": full["lru_b_out"][0]},
        "rwkv": {"norm": full["rwkv_norm"][0], "mix": full["rwkv_mix"][0],
                 "w_r": by_rows("rwkv_w_rkv", 0), "w_k": by_rows("rwkv_w_rkv", 1), "w_v": by_rows("rwkv_w_rkv", 2),
                 "w0": full["rwkv_w0"][0], "w1": full["rwkv_w1"][0], "w2": full["rwkv_w2"][0], "a0": full["rwkv_a0"][0],
                 "a1": full["rwkv_a1"][0], "a2": full["rwkv_a2"][0], "g1": full["rwkv_g1"][0], "g2": full["rwkv_g2"][0],
                 "k_k": full["rwkv_k_k"][0], "k_a": full["rwkv_k_a"][0], "r_k": full["rwkv_r_k"][0],
                 "ln_w": full["rwkv_ln_w"][0], "ln_b": full["rwkv_ln_b"][0], "w_out": by_rows("rwkv_w_out", 0)},
        "final_norm": full["final_norm"],
    }
    for l in range(2):
        P[f"ffn{l}"] = {"norm": full["ffn_norm"][l], "w_up": _W(mats["ffn_w_up"], l),
                        "conv_w": full["ffn_conv_w"][l], "conv_b": full["ffn_conv_b"][l],
                        "w_down": by_rows("ffn_w_down", l)}

    loss, gx, G = _local_step(x, tgt, P)

    nb = W["lru_gate_w"].shape[2]
    gl, gr = G["lru"], G["rwkv"]
    gfull = {
        "lru_norm": gl["norm"][None], "lru_b_in": gl["b_in"][None],
        "lru_conv_w": gl["conv_w"][None], "lru_conv_b": gl["conv_b"][None], "lru_gate_w": _gate_blocks(gl["wbd"], nb)[None],
        "lru_gate_b": gl["gate_b"].reshape(W["lru_gate_b"].shape), "lru_lambda": gl["lam"][None],
        "lru_b_out": gl["b_out"][None],
        "rwkv_norm": gr["norm"][None], "rwkv_mix": gr["mix"][None],
        "rwkv_w0": gr["w0"][None], "rwkv_w1": gr["w1"][None], "rwkv_w2": gr["w2"][None], "rwkv_a0": gr["a0"][None],
        "rwkv_a1": gr["a1"][None], "rwkv_a2": gr["a2"][None], "rwkv_g1": gr["g1"][None], "rwkv_g2": gr["g2"][None],
        "rwkv_k_k": gr["k_k"][None], "rwkv_k_a": gr["k_a"][None], "rwkv_r_k": gr["r_k"][None],
        "rwkv_ln_w": gr["ln_w"][None], "rwkv_ln_b": gr["ln_b"][None],
        "final_norm": G["final_norm"],
    }
    for k in ("norm", "conv_w", "conv_b"):
        gfull["ffn_" + k] = jnp.stack([G["ffn0"][k], G["ffn1"][k]])

    small_g = jnp.stack([_pack([_to_shards(gfull[n], SHARD_AXIS[n])[q] for n in small], F32, 16) for q in range(N_CHIPS)])
    cut = lambda g: g.reshape(N_CHIPS, g.shape[0] // N_CHIPS, g.shape[1])
    pieces = [("lru_w_in", (0,), gl["w_in"]), ("lru_w_out", (0,), cut(gl["w_out"])),
              ("rwkv_w_rkv", (0, 0), cut(gr["w_r"])), ("rwkv_w_rkv", (0, 1), cut(gr["w_k"])),
              ("rwkv_w_rkv", (0, 2), cut(gr["w_v"])), ("rwkv_w_out", (0,), cut(gr["w_out"])),
              ("ffn_w_up", (0,), G["ffn0"]["w_up"]), ("ffn_w_up", (1,), G["ffn1"]["w_up"]),
              ("ffn_w_down", (0,), cut(G["ffn0"]["w_down"])), ("ffn_w_down", (1,), cut(G["ffn1"]["w_down"]))]
    gs = [g for _, _, g in pieces] + [small_g]
    grep = _pack([gfull[n] for n in repl], F32, 16)

    got, got_rep = _pair_swap_all(gs, grep, "reduce_pair_swap")
    pair = [_pair_sum(g, r, f"reduce_pair_sum{i}") for i, (g, r) in enumerate(zip(gs, got, strict=True))]
    pair_rep = _rep_pair_sum(grep, got_rep, "reduce_pair_sum_rep")
    arrived, arrived_rep = _chip_exchange_all(pair, pair_rep, "reduce_chips")
    mine = [_chip_sum(a, f"reduce_chip_sum{i}") for i, a in enumerate(arrived)]
    g_rp = _chip_sum(arrived_rep, "reduce_chip_sum_rep")
    other = _half_swap_all(mine, "reduce_half_swap")

    outs, parts = {}, {}
    for i, (n, idx, _) in enumerate(pieces):
        w, m, v = (rows2d(S[n][idx]) for S in (W, M1, V1))
        res = _adamw_halves(w, m, v, mine[i], other[i], f"adamw{i}")
        for kind, a in zip(("grad", "delta", "new_m", "new_v"), res, strict=True):
            parts.setdefault((kind, n), []).append(a)
    for (kind, n), lst in parts.items():
        a = lst[0] if len(lst) == 1 else jnp.stack(lst)
        outs[(kind, n)] = a.reshape(W[n].shape)
    wb, mb, vb = (_pack([S[n] for n in small], F32, 16) for S in (W, M1, V1))
    res = _adamw_halves(wb, mb, vb, mine[-1], other[-1], "adamw_small")
    for kind, buf in zip(("grad", "delta", "new_m", "new_v"), res, strict=True):
        for n, a in zip(small, _unpack(buf, [W[n].shape for n in small]), strict=True):
            outs[(kind, n)] = a
    wb, mb, vb = (_pack([S[n] for n in repl], F32, 16) for S in (W, M1, V1))
    d, nm, nv = _adamw_call(wb, g_rp, mb, vb, "adamw_repl")
    for kind, buf in (("grad", g_rp), ("delta", d), ("new_m", nm), ("new_v", nv)):
        for n, a in zip(repl, _unpack(buf, [W[n].shape for n in repl]), strict=True):
            outs[(kind, n)] = a
    loss = lax.psum(loss, ("x", "y", "c"))
    return (loss, gx, *[outs[(kind, n)] for kind in ("grad", "delta", "new_m", "new_v") for n in WEIGHTS])


def kernel(x, lru_norm, lru_w_in, lru_b_in, lru_conv_w, lru_conv_b, lru_gate_w, lru_gate_b, lru_lambda, lru_w_out, lru_b_out, rwkv_norm, rwkv_mix, rwkv_w_rkv, rwkv_w0, rwkv_w1, rwkv_w2, rwkv_a0, rwkv_a1, rwkv_a2, rwkv_g1, rwkv_g2, rwkv_k_k, rwkv_k_a, rwkv_r_k, rwkv_ln_w, rwkv_ln_b, rwkv_w_out, ffn_norm, ffn_w_up, ffn_conv_w, ffn_conv_b, ffn_w_down, final_norm, loss_target, m_lru_norm, m_lru_w_in, m_lru_b_in, m_lru_conv_w, m_lru_conv_b, m_lru_gate_w, m_lru_gate_b, m_lru_lambda, m_lru_w_out, m_lru_b_out, m_rwkv_norm, m_rwkv_mix, m_rwkv_w_rkv, m_rwkv_w0, m_rwkv_w1, m_rwkv_w2, m_rwkv_a0, m_rwkv_a1, m_rwkv_a2, m_rwkv_g1, m_rwkv_g2, m_rwkv_k_k, m_rwkv_k_a, m_rwkv_r_k, m_rwkv_ln_w, m_rwkv_ln_b, m_rwkv_w_out, m_ffn_norm, m_ffn_w_up, m_ffn_conv_w, m_ffn_conv_b, m_ffn_w_down, m_final_norm, v_lru_norm, v_lru_w_in, v_lru_b_in, v_lru_conv_w, v_lru_conv_b, v_lru_gate_w, v_lru_gate_b, v_lru_lambda, v_lru_w_out, v_lru_b_out, v_rwkv_norm, v_rwkv_mix, v_rwkv_w_rkv, v_rwkv_w0, v_rwkv_w1, v_rwkv_w2, v_rwkv_a0, v_rwkv_a1, v_rwkv_a2, v_rwkv_g1, v_rwkv_g2, v_rwkv_k_k, v_rwkv_k_a, v_rwkv_r_k, v_rwkv_ln_w, v_rwkv_ln_b, v_rwkv_w_out, v_ffn_norm, v_ffn_w_up, v_ffn_conv_w, v_ffn_conv_b, v_ffn_w_down, v_final_norm):
    given = dict(locals())
    W = {n: given[n] for n in WEIGHTS}
    M1 = {n: given["m_" + n] for n in WEIGHTS}
    V1 = {n: given["v_" + n] for n in WEIGHTS}
    return _step(W, M1, V1, x, loss_target)
```
